```python
import jax, jax.numpy as jnp
from jax import lax
import numpy as np

D_MODEL = 2048
BATCH = 32
SEQ = 256
DEPTH = 4
DEC_BATCH = 4
DEC_SEQ = 2048
PAST_LEN = 256

GRID_W = 64
N_MIXERS = 3
N_MLSTM_LAYERS = (DEPTH + 2) // 3
N_GQA_LAYERS = (DEPTH + 1) // 3
N_NA_LAYERS = DEPTH // 3
D_FF = 4 * D_MODEL
EPS = 1e-6
MLSTM_HEADS = 8
MLSTM_DK = D_MODEL // (2 * MLSTM_HEADS)
MLSTM_DV = D_MODEL // MLSTM_HEADS
MLSTM_IN = 2 * MLSTM_HEADS * MLSTM_DK + 2 * MLSTM_HEADS * MLSTM_DV
MLSTM_CHUNK = 128
GATE_SOFTCAP = 15.0
HEAD_DIM = 128
GQA_HEADS = D_MODEL // HEAD_DIM
GQA_KV_HEADS = 4
GQA_GROUP = GQA_HEADS // GQA_KV_HEADS
Q_BLOCK = 128
ROPE_THETA = 10000.0
NA_HEADS = D_MODEL // HEAD_DIM
WIN_ROWS = 8
WIN_COLS = 16

kernel_name = 'hybrid_mlstm_gqa_natten_dit_step'


def rms_norm(x, g):
    xf = x.astype(jnp.float32)
    y = xf * lax.rsqrt(jnp.mean(xf * xf, axis=-1, keepdims=True) + EPS)
    return (y * g.astype(jnp.float32)).astype(x.dtype)


def adaln(cond, w, b):
    return jnp.split(jax.nn.silu(cond) @ w + b, 6, axis=-1)


def modulate(x, g, shift, scale):
    return rms_norm(x, g) * (1.0 + scale) + shift


def sq_relu_mlp(h, w1, w2):
    return jnp.square(jax.nn.relu(h @ w1)) @ w2


def grid_rope(x):
    B, T, H, hd = x.shape
    nf = hd // 4
    t = jnp.arange(T)
    inv = ROPE_THETA ** (-jnp.arange(nf, dtype=jnp.float32) / nf)
    pos = jnp.stack([t // GRID_W, t % GRID_W], axis=-1).astype(jnp.float32)
    ang = pos[:, :, None] * inv
    cos = jnp.cos(ang)[None, :, None]
    sin = jnp.sin(ang)[None, :, None]
    xf = x.astype(jnp.float32).reshape(B, T, H, 2, 2, nf)
    x1, x2 = xf[..., 0, :], xf[..., 1, :]
    out = jnp.stack([x1 * cos - x2 * sin, x2 * cos + x1 * sin], axis=-2)
    return out.reshape(B, T, H, hd).astype(x.dtype)


def qkv_heads(h, w_qkv, n_q, n_kv, q_g, k_g):
    B, T, _ = h.shape
    q, k, v = jnp.split(h @ w_qkv, [n_q * HEAD_DIM, (n_q + n_kv) * HEAD_DIM], axis=-1)
    q = rms_norm(q.reshape(B, T, n_q, HEAD_DIM), q_g)
    k = rms_norm(k.reshape(B, T, n_kv, HEAD_DIM), k_g)
    return q, k, v.reshape(B, T, n_kv, HEAD_DIM)


def blocked_attention(q, k, v):
    B, T, KV, G, hd = q.shape
    nb = T // Q_BLOCK
    scale = hd ** -0.5
    qb = jnp.moveaxis(q.reshape(B, nb, Q_BLOCK, KV, G, hd), 1, 0)

    def one_block(qi):
        s = jnp.einsum('bqhgd,bshd->bhgqs', qi, k).astype(jnp.float32) * scale
        p = jax.nn.softmax(s, axis=-1).astype(v.dtype)
        return jnp.einsum('bhgqs,bshd->bqhgd', p, v)

    o = lax.map(one_block, qb)
    return jnp.moveaxis(o, 0, 1).reshape(B, T, KV * G * hd)


def neighbourhood_attention(q, k, v, k_ctx, v_ctx, rpb):
    B, T, H, hd = q.shape
    rows = T // GRID_W
    wr = min(WIN_ROWS, rows)
    n_loc = wr * GRID_W
    scale = hd ** -0.5
    kg = k.reshape(B, rows, GRID_W, H, hd)
    vg = v.reshape(B, rows, GRID_W, H, hd)
    qg = jnp.moveaxis(q.reshape(B, rows, GRID_W, H, hd), 1, 0)
    col = jnp.arange(GRID_W)
    cs = jnp.clip(col - WIN_COLS // 2, 0, GRID_W - WIN_COLS)
    col_ok = (col[None, :] >= cs[:, None]) & (col[None, :] < cs[:, None] + WIN_COLS)
    dc_idx = jnp.clip(col[None, :] - col[:, None] + WIN_COLS - 1, 0, 2 * WIN_COLS - 2)

    def row_block(args):
        r, qr = args
        rs = jnp.clip(r - wr // 2, 0, rows - wr)
        kb = lax.dynamic_slice_in_dim(kg, rs, wr, axis=1)
        vb = lax.dynamic_slice_in_dim(vg, rs, wr, axis=1)
        s_loc = jnp.einsum('bqhd,bjkhd->bhqjk', qr, kb).astype(jnp.float32) * scale
        dr_idx = rs + jnp.arange(wr) - r + WIN_ROWS - 1
        bias = rpb[:, dr_idx[None, :, None], dc_idx[:, None, :]].astype(jnp.float32)
        s_loc = jnp.where(col_ok[:, None, :], s_loc + bias, -jnp.inf)
        s_ctx = jnp.einsum('bqhd,bshd->bhqs', qr, k_ctx).astype(jnp.float32) * scale
        s = jnp.concatenate([s_loc.reshape(B, H, GRID_W, n_loc), s_ctx], axis=-1)
        p = jax.nn.softmax(s, axis=-1).astype(v.dtype)
        return (jnp.einsum('bhqn,bnhd->bqhd', p[..., :n_loc], vb.reshape(B, n_loc, H, hd))
                + jnp.einsum('bhqs,bshd->bqhd', p[..., n_loc:], v_ctx))

    o = lax.map(row_block, (jnp.arange(rows), qg))
    return jnp.moveaxis(o, 0, 1).reshape(B, T, H * hd)


def mlstm_scan(q, k, v, log_i, log_f, C0, n0, m0):
    B, H, T, _ = q.shape
    L = MLSTM_CHUNK
    nc = T // L

    def chunks(a):
        return jnp.moveaxis(a.reshape(a.shape[:2] + (nc, L) + a.shape[3:]), 2, 0)

    causal = jnp.tril(jnp.ones((L, L), dtype=bool))

    def step(carry, inp):
        C, n, m = carry
        qc, kc, vc, li, lf = inp
        b = jnp.cumsum(lf, axis=-1)
        log_d = jnp.where(causal, b[..., :, None] - b[..., None, :] + li[..., None, :], -jnp.inf)
        inter = b + m[..., None]
        m_j = jnp.maximum(inter, jnp.max(log_d, axis=-1))
        d_mat = jnp.exp(log_d - m_j[..., None])
        w_inter = jnp.exp(inter - m_j)
        s = jnp.einsum('bhjd,bhsd->bhjs', qc, kc) * d_mat
        num = (w_inter[..., None] * jnp.einsum('bhjd,bhde->bhje', qc, C)
               + jnp.einsum('bhjs,bhse->bhje', s, vc))
        den = w_inter * jnp.einsum('bhjd,bhd->bhj', qc, n) + jnp.sum(s, axis=-1)
        h = num / jnp.maximum(jnp.abs(den), jnp.exp(-m_j))[..., None]
        b_last = b[..., -1]
        log_w = b_last[..., None] - b + li
        m_new = jnp.maximum(b_last + m, jnp.max(log_w, axis=-1))
        wt = jnp.exp(log_w - m_new[..., None])
        decay = jnp.exp(b_last + m - m_new)
        C_new = decay[..., None, None] * C + jnp.einsum('bhs,bhsd,bhse->bhde', wt, kc, vc)
        n_new = decay[..., None] * n + jnp.einsum('bhs,bhsd->bhd', wt, kc)
        return (C_new, n_new, m_new), h

    (C, n, m), h = lax.scan(step, (C0, n0, m0),
                            (chunks(q), chunks(k), chunks(v), chunks(log_i), chunks(log_f)))
    return jnp.moveaxis(h, 0, 2).reshape(B, H, T, -1), (C, n, m)


def mlstm_mixer(h, w_in, w_gate, b_gate, head_g, w_out, C0, n0, m0):
    B, T, _ = h.shape
    Hh, dk, dv = MLSTM_HEADS, MLSTM_DK, MLSTM_DV
    q, k, v, o = jnp.split(h @ w_in, [Hh * dk, 2 * Hh * dk, 2 * Hh * dk + Hh * dv], axis=-1)

    def heads(a, d):
        return jnp.moveaxis(a.reshape(B, T, Hh, d), 1, 2).astype(jnp.float32)

    q = heads(q, dk) * (dk ** -0.5)
    k = heads(k, dk)
    v = heads(v, dv)
    gates = (h @ w_gate + b_gate).astype(jnp.float32)
    gates = GATE_SOFTCAP * jnp.tanh(gates / GATE_SOFTCAP)
    gates = jnp.moveaxis(gates.reshape(B, T, 4, Hh), 1, 3)
    log_f = jax.nn.log_sigmoid(gates[:, 1::2])
    h_f, st_f = mlstm_scan(q, k, v, gates[:, 0], log_f[:, 0],
                           C0[:, 0].astype(jnp.float32), n0[:, 0].astype(jnp.float32), m0[:, 0].astype(jnp.float32))
    rev = lambda a: jnp.flip(a, axis=2)
    h_b, st_b = mlstm_scan(rev(q), rev(k), rev(v), rev(gates[:, 2]), rev(log_f[:, 1]),
                           C0[:, 1].astype(jnp.float32), n0[:, 1].astype(jnp.float32), m0[:, 1].astype(jnp.float32))
    h_sum = jnp.moveaxis(h_f + rev(h_b), 1, 2)
    hn = rms_norm(h_sum, head_g.reshape(Hh, dv))
    y = (hn * jax.nn.sigmoid(o.astype(jnp.float32)).reshape(B, T, Hh, dv)).reshape(B, T, Hh * dv)
    y = y.astype(h.dtype) @ w_out
    C_fin = jnp.stack([st_f[0], st_b[0]], axis=1)
    n_fin = jnp.stack([st_f[1], st_b[1]], axis=1)
    m_fin = jnp.stack([st_f[2], st_b[2]], axis=1)
    return y, (C_fin, n_fin, m_fin)


def setup_inputs(seed: int = 0) -> dict:
    key = jax.random.key(seed)
    keys = iter(jax.random.split(key, 40))

    def nrm(shape, scale):
        return jax.random.normal(next(keys), shape, jnp.float32) * scale

    D = D_MODEL
    Hm, DK, DV = MLSTM_HEADS, MLSTM_DK, MLSTM_DV
    f_bias = jnp.linspace(3.0, 6.0, Hm, dtype=jnp.float32)
    z_bias = jnp.zeros((Hm,), jnp.float32)
    gate_offset = jnp.concatenate([z_bias, f_bias, z_bias, f_bias])
    return {
        'x_prompt': nrm((BATCH, SEQ, D), 1.0),
        'x_sample': nrm((DEC_BATCH, DEC_SEQ, D), 1.0),
        'state_mlstm_C': nrm((DEC_BATCH, N_MLSTM_LAYERS, 2, Hm, DK, DV), 0.5),
        'state_mlstm_n': nrm((DEC_BATCH, N_MLSTM_LAYERS, 2, Hm, DK), 0.5),
        'state_mlstm_m': nrm((DEC_BATCH, N_MLSTM_LAYERS, 2, Hm), 1.0),
        'cache_gqa_k': nrm((DEC_BATCH, N_GQA_LAYERS, PAST_LEN, GQA_KV_HEADS, HEAD_DIM), 1.0),
        'cache_gqa_v': nrm((DEC_BATCH, N_GQA_LAYERS, PAST_LEN, GQA_KV_HEADS, HEAD_DIM), 1.0),
        'cache_na_k': nrm((DEC_BATCH, N_NA_LAYERS, PAST_LEN, NA_HEADS, HEAD_DIM), 1.0),
        'cache_na_v': nrm((DEC_BATCH, N_NA_LAYERS, PAST_LEN, NA_HEADS, HEAD_DIM), 1.0),
        'c': nrm((DEC_BATCH, D), 1.0),
        'c_ctx': nrm((D,), 1.0),
        'norm1_g': 1.0 + nrm((DEPTH, D), 0.05),
        'norm2_g': 1.0 + nrm((DEPTH, D), 0.05),
        'w_ada': nrm((DEPTH, D, 6 * D), D ** -0.5),
        'b_ada': nrm((DEPTH, 6 * D), 0.01),
        'w_mlp1': nrm((DEPTH, D, D_FF), D ** -0.5),
        'w_mlp2': nrm((DEPTH, D_FF, D), D_FF ** -0.5),
        'mlstm_w_in': nrm((N_MLSTM_LAYERS, D, MLSTM_IN), D ** -0.5),
        'mlstm_w_gate': nrm((N_MLSTM_LAYERS, D, 4 * Hm), D ** -0.5),
        'mlstm_b_gate': nrm((N_MLSTM_LAYERS, 4 * Hm), 0.1) + gate_offset,
        'mlstm_head_g': 1.0 + nrm((N_MLSTM_LAYERS, Hm * DV), 0.05),
        'mlstm_w_out': nrm((N_MLSTM_LAYERS, Hm * DV, D), (Hm * DV) ** -0.5),
        'gqa_w_qkv': nrm((N_GQA_LAYERS, D, (GQA_HEADS + 2 * GQA_KV_HEADS) * HEAD_DIM), D ** -0.5),
        'gqa_q_g': 1.0 + nrm((N_GQA_LAYERS, HEAD_DIM), 0.05),
        'gqa_k_g': 1.0 + nrm((N_GQA_LAYERS, HEAD_DIM), 0.05),
        'gqa_w_o': nrm((N_GQA_LAYERS, GQA_HEADS * HEAD_DIM, D), (GQA_HEADS * HEAD_DIM) ** -0.5),
        'na_w_qkv': nrm((N_NA_LAYERS, D, 3 * NA_HEADS * HEAD_DIM), D ** -0.5),
        'na_q_g': 1.0 + nrm((N_NA_LAYERS, HEAD_DIM), 0.05),
        'na_k_g': 1.0 + nrm((N_NA_LAYERS, HEAD_DIM), 0.05),
        'na_rpb': nrm((N_NA_LAYERS, NA_HEADS, 2 * WIN_ROWS - 1, 2 * WIN_COLS - 1), 0.5),
        'na_w_o': nrm((N_NA_LAYERS, NA_HEADS * HEAD_DIM, D), (NA_HEADS * HEAD_DIM) ** -0.5),
    }


def reference(x_prompt, x_sample, state_mlstm_C, state_mlstm_n, state_mlstm_m, cache_gqa_k, cache_gqa_v,
              cache_na_k, cache_na_v, c, c_ctx, norm1_g, norm2_g, w_ada, b_ada, w_mlp1, w_mlp2,
              mlstm_w_in, mlstm_w_gate, mlstm_b_gate, mlstm_head_g, mlstm_w_out,
              gqa_w_qkv, gqa_q_g, gqa_k_g, gqa_w_o, na_w_qkv, na_q_g, na_k_g, na_rpb, na_w_o):
    xp = x_prompt
    bp, tp, _ = xp.shape
    C_l, n_l, m_l, gk_l, gv_l, nk_l, nv_l = [], [], [], [], [], [], []
    for i in range(DEPTH):
        kind, j = i % N_MIXERS, i // N_MIXERS
        sh1, sc1, g1, sh2, sc2, g2 = adaln(c_ctx, w_ada[i], b_ada[i])
        h = modulate(xp, norm1_g[i], sh1, sc1)
        if kind == 0:
            C0 = jnp.zeros((bp, 2, MLSTM_HEADS, MLSTM_DK, MLSTM_DV), jnp.float32)
            n0 = jnp.zeros((bp, 2, MLSTM_HEADS, MLSTM_DK), jnp.float32)
            m0 = jnp.zeros((bp, 2, MLSTM_HEADS), jnp.float32)
            out, (C_f, n_f, m_f) = mlstm_mixer(h, mlstm_w_in[j], mlstm_w_gate[j], mlstm_b_gate[j],
                                               mlstm_head_g[j], mlstm_w_out[j], C0, n0, m0)
            C_l.append(C_f)
            n_l.append(n_f)
            m_l.append(m_f)
        elif kind == 1:
            q, k, v = qkv_heads(h, gqa_w_qkv[j], GQA_HEADS, GQA_KV_HEADS, gqa_q_g[j], gqa_k_g[j])
            out = blocked_attention(q.reshape(bp, tp, GQA_KV_HEADS, GQA_GROUP, HEAD_DIM), k, v) @ gqa_w_o[j]
            gk_l.append(k)
            gv_l.append(v)
        else:
            q, k, v = qkv_heads(h, na_w_qkv[j], NA_HEADS, NA_HEADS, na_q_g[j], na_k_g[j])
            out = blocked_attention(q[:, :, :, None, :], k, v) @ na_w_o[j]
            nk_l.append(k)
            nv_l.append(v)
        xp = xp + g1 * out
        xp = xp + g2 * sq_relu_mlp(modulate(xp, norm2_g[i], sh2, sc2), w_mlp1[i], w_mlp2[i])
    y_prompt = xp

    xs = x_sample
    bs, ts, _ = xs.shape
    cond = c[:, None, :]
    for i in range(DEPTH):
        kind, j = i % N_MIXERS, i // N_MIXERS
        sh1, sc1, g1, sh2, sc2, g2 = adaln(cond, w_ada[i], b_ada[i])
        h = modulate(xs, norm1_g[i], sh1, sc1)
        if kind == 0:
            out, _ = mlstm_mixer(h, mlstm_w_in[j], mlstm_w_gate[j], mlstm_b_gate[j], mlstm_head_g[j],
                                 mlstm_w_out[j], state_mlstm_C[:, j], state_mlstm_n[:, j], state_mlstm_m[:, j])
        elif kind == 1:
            q, k, v = qkv_heads(h, gqa_w_qkv[j], GQA_HEADS, GQA_KV_HEADS, gqa_q_g[j], gqa_k_g[j])
            q, k = grid_rope(q), grid_rope(k)
            kk = jnp.concatenate([k, cache_gqa_k[:, j].astype(k.dtype)], axis=1)
            vv = jnp.concatenate([v, cache_gqa_v[:, j].astype(v.dtype)], axis=1)
            out = blocked_attention(q.reshape(bs, ts, GQA_KV_HEADS, GQA_GROUP, HEAD_DIM), kk, vv) @ gqa_w_o[j]
        else:
            q, k, v = qkv_heads(h, na_w_qkv[j], NA_HEADS, NA_HEADS, na_q_g[j], na_k_g[j])
            out = neighbourhood_attention(q, k, v, cache_na_k[:, j].astype(k.dtype),
                                          cache_na_v[:, j].astype(v.dtype), na_rpb[j]) @ na_w_o[j]
        xs = xs + g1 * out
        xs = xs + g2 * sq_relu_mlp(modulate(xs, norm2_g[i], sh2, sc2), w_mlp1[i], w_mlp2[i])
    y_sample = xs

    return (y_prompt, y_sample, jnp.stack(C_l, axis=1), jnp.stack(n_l, axis=1), jnp.stack(m_l, axis=1),
            jnp.stack(gk_l, axis=1), jnp.stack(gv_l, axis=1), jnp.stack(nk_l, axis=1), jnp.stack(nv_l, axis=1))
```

```python
import functools

import jax
import jax.numpy as jnp
from jax import lax
from jax.experimental import pallas as pl
from jax.experimental.pallas import tpu as pltpu

EPS = 1e-6
HEAD_DIM = 128
MLSTM_DK = 128
MLSTM_DV = 256
MLSTM_CHUNK = 128
GATE_SOFTCAP = 15.0
GRID_W = 64
WIN_ROWS = 8
WIN_COLS = 16
ROPE_THETA = 10000.0
N_MIXERS = 3

VMEM_LIMIT_BYTES = 56 * 1024 * 1024
NORM_ROW_CHUNK = 16

F32 = jnp.float32
BF16 = jnp.bfloat16
NEG_INF = float("-inf")


def _params(sem):
    return pltpu.CompilerParams(dimension_semantics=sem, vmem_limit_bytes=VMEM_LIMIT_BYTES)


def _nt(a, b):
    return lax.dot_general(a, b, (((1,), (1,)), ((), ())), preferred_element_type=F32)


def _mm(a, b):
    return jnp.dot(a, b, preferred_element_type=F32)


def _split2(x):
    hi = x.astype(BF16)
    lo = (x - hi.astype(F32)).astype(BF16)
    return hi, lo


def _split3(x):
    hi = x.astype(BF16)
    r = x - hi.astype(F32)
    mid = r.astype(BF16)
    lo = (r - mid.astype(F32)).astype(BF16)
    return hi, mid, lo


def _adaln_kernel(cond_ref, w_ref, b_ref, out_ref):
    c = cond_ref[...]
    s = c * jax.nn.sigmoid(c)
    sh, sl = _split2(s)
    w = w_ref[...]
    wh, wl = _split2(w)
    lhs = jnp.concatenate([sh, sl], axis=0)
    r = _mm(lhs, wh)
    acc = r[:8] + r[8:] + _mm(sh, wl)
    out_ref[...] = acc + b_ref[...]


def adaln_all(cond8, w_ada, b_ada, tn=1024):
    nl, d, n = w_ada.shape
    return pl.pallas_call(
        _adaln_kernel,
        out_shape=jax.ShapeDtypeStruct((nl, 8, n), F32),
        grid=(nl, n // tn),
        in_specs=[
            pl.BlockSpec((8, d), lambda l, j: (0, 0)),
            pl.BlockSpec((None, d, tn), lambda l, j: (l, 0, j)),
            pl.BlockSpec((None, 1, tn), lambda l, j: (l, 0, j)),
        ],
        out_specs=pl.BlockSpec((None, 8, tn), lambda l, j: (l, 0, j)),
        compiler_params=_params(("parallel", "parallel")),
    )(cond8, w_ada, b_ada.reshape(nl, 1, n))


def _modulate_rows(x_ref, g_ref, sh_ref, sc_ref, dst_ref, tm):
    g = g_ref[...]
    mul = 1.0 + sc_ref[...]
    sh = sh_ref[...]

    def body(i, _):
        r0 = pl.multiple_of(i * NORM_ROW_CHUNK, NORM_ROW_CHUNK)
        xf = x_ref[pl.ds(r0, NORM_ROW_CHUNK), :]
        y = xf * lax.rsqrt(jnp.mean(xf * xf, axis=-1, keepdims=True) + EPS)
        dst_ref[pl.ds(r0, NORM_ROW_CHUNK), :] = ((y * g) * mul + sh).astype(dst_ref.dtype)
        return 0

    lax.fori_loop(0, tm // NORM_ROW_CHUNK, body, 0)


def _mod_index(layer, which, rows_per_batch, tm, row0):
    def f(i, *_):
        b = row0 + (i * tm) // rows_per_batch
        return (layer * 48 + b * 6 + which, 0, 0)
    return f


def _modmm_kernel(x_ref, g_ref, sh_ref, sc_ref, w_ref, out_ref, xn_ref, *, tm, relu2):
    @pl.when(pl.program_id(1) == 0)
    def _():
        _modulate_rows(x_ref, g_ref, sh_ref, sc_ref, xn_ref, tm)

    acc = _mm(xn_ref[...], w_ref[...])
    if relu2:
        acc = jnp.square(jnp.maximum(acc, 0.0))
    out_ref[...] = acc.astype(out_ref.dtype)


def modmm(x, g, mod, w, *, layer, which_shift, which_scale, rows_per_batch, row0,
          relu2=False, out_dtype=F32, tm=1024, tn=512):
    m, d = x.shape
    n = w.shape[1]
    tm = min(tm, rows_per_batch)
    kern = functools.partial(_modmm_kernel, tm=tm, relu2=relu2)
    return pl.pallas_call(
        kern,
        out_shape=jax.ShapeDtypeStruct((m, n), out_dtype),
        grid=(m // tm, n // tn),
        in_specs=[
            pl.BlockSpec((tm, d), lambda i, j: (i, 0)),
            pl.BlockSpec((None, 1, d), lambda i, j: (layer, 0, 0)),
            pl.BlockSpec((None, 1, d), _mod_index(layer, which_shift, rows_per_batch, tm, row0)),
            pl.BlockSpec((None, 1, d), _mod_index(layer, which_scale, rows_per_batch, tm, row0)),
            pl.BlockSpec((d, tn), lambda i, j: (0, j)),
        ],
        out_specs=pl.BlockSpec((tm, tn), lambda i, j: (i, j)),
        scratch_shapes=[pltpu.VMEM((tm, d), BF16)],
        compiler_params=_params(("parallel", "arbitrary")),
    )(x, g, mod, mod, w)


def _mmres_kernel(a_ref, w_ref, res_ref, gate_ref, out_ref):
    acc = _mm(a_ref[...], w_ref[...])
    out_ref[...] = res_ref[...] + gate_ref[...] * acc


def mm_res(a, w, res, mod, *, layer, which_gate, rows_per_batch, row0, tm=512, tn=512):
    m, k = a.shape
    n = w.shape[1]
    tm = min(tm, rows_per_batch)

    def gate_index(i, j):
        b = row0 + (i * tm) // rows_per_batch
        return (layer * 48 + b * 6 + which_gate, 0, j)

    return pl.pallas_call(
        _mmres_kernel,
        out_shape=jax.ShapeDtypeStruct((m, n), F32),
        grid=(m // tm, n // tn),
        in_specs=[
            pl.BlockSpec((tm, k), lambda i, j: (i, 0)),
            pl.BlockSpec((k, tn), lambda i, j: (0, j)),
            pl.BlockSpec((tm, tn), lambda i, j: (i, j)),
            pl.BlockSpec((None, 1, tn), gate_index),
        ],
        out_specs=pl.BlockSpec((tm, tn), lambda i, j: (i, j)),
        compiler_params=_params(("parallel", "arbitrary")),
    )(a, w, res, mod)


def _head_norm(x, g):
    return x * lax.rsqrt(jnp.mean(x * x, axis=-1, keepdims=True) + EPS) * g


def _softmax_pv(s, v):
    mx = jnp.max(s, axis=-1, keepdims=True)
    p = jnp.exp(s - mx)
    l = jnp.sum(p, axis=-1, keepdims=True)
    return _mm(p.astype(BF16), v) / l


def _rope(x, cos, sin_signed, lane_is_first):
    swapped = jnp.where(lane_is_first, pltpu.roll(x, 96, 1), pltpu.roll(x, 32, 1))
    return x * cos + swapped * sin_signed


def _ctx_attn_kernel(q_ref, k_ref, v_ref, qg_ref, kg_ref, o_ref, kn_ref, vo_ref, *, groups):
    scale = HEAD_DIM ** -0.5
    kn = _head_norm(k_ref[...], kg_ref[...])
    kn_ref[...] = kn
    v = v_ref[...]
    vo_ref[...] = v
    kb = kn.astype(BF16)
    vb = v.astype(BF16)
    for g in range(groups):
        qn = _head_norm(q_ref[:, g * HEAD_DIM:(g + 1) * HEAD_DIM], qg_ref[...]).astype(BF16)
        s = _nt(qn, kb) * scale
        o_ref[:, g * HEAD_DIM:(g + 1) * HEAD_DIM] = _softmax_pv(s, vb).astype(o_ref.dtype)


def ctx_attention(qkv, q_g, k_g, *, batch, seq, n_q, n_kv):
    m = qkv.shape[0]
    groups = n_q // n_kv
    hd = HEAD_DIM
    kern = functools.partial(_ctx_attn_kernel, groups=groups)
    return pl.pallas_call(
        kern,
        out_shape=(jax.ShapeDtypeStruct((m, n_q * hd), BF16),
                   jax.ShapeDtypeStruct((m, n_kv * hd), F32),
                   jax.ShapeDtypeStruct((m, n_kv * hd), F32)),
        grid=(batch, n_kv),
        in_specs=[
            pl.BlockSpec((seq, groups * hd), lambda b, h: (b, h)),
            pl.BlockSpec((seq, hd), lambda b, h: (b, n_q + h)),
            pl.BlockSpec((seq, hd), lambda b, h: (b, n_q + n_kv + h)),
            pl.BlockSpec((1, hd), lambda b, h: (0, 0)),
            pl.BlockSpec((1, hd), lambda b, h: (0, 0)),
        ],
        out_specs=(pl.BlockSpec((seq, groups * hd), lambda b, h: (b, h)),
                   pl.BlockSpec((seq, hd), lambda b, h: (b, h)),
                   pl.BlockSpec((seq, hd), lambda b, h: (b, h))),
        compiler_params=_params(("parallel", "parallel")),
    )(qkv, qkv, qkv, q_g.reshape(1, hd), k_g.reshape(1, hd))


def _gqa_kernel(q_ref, k_ref, v_ref, ck_ref, cv_ref, qg_ref, kg_ref, cosq_ref, sinq_ref, cosk_ref, sink_ref,
                o_ref, kall_ref, vall_ref, *, groups, seq, past):
    scale = HEAD_DIM ** -0.5
    lane = lax.broadcasted_iota(jnp.int32, (1, HEAD_DIM), 1)
    first = (lane % 64) < 32

    @pl.when(pl.program_id(2) == 0)
    def _():
        kn = _head_norm(k_ref[...], kg_ref[...])
        kall_ref[0:seq, :] = _rope(kn, cosk_ref[...], sink_ref[...], first).astype(BF16)
        kall_ref[seq:seq + past, :] = ck_ref[...].astype(BF16)
        vall_ref[0:seq, :] = v_ref[...].astype(BF16)
        vall_ref[seq:seq + past, :] = cv_ref[...].astype(BF16)

    kb = kall_ref[...]
    vb = vall_ref[...]
    cos = cosq_ref[...]
    sin = sinq_ref[...]
    for g in range(groups):
        qn = _head_norm(q_ref[:, g * HEAD_DIM:(g + 1) * HEAD_DIM], qg_ref[...])
        qr = _rope(qn, cos, sin, first).astype(BF16)
        s = _nt(qr, kb) * scale
        o_ref[:, g * HEAD_DIM:(g + 1) * HEAD_DIM] = _softmax_pv(s, vb).astype(o_ref.dtype)


def gqa_attention(qkv, cache_k, cache_v, q_g, k_g, cos, sin, *, batch, seq, past, n_q, n_kv, tq=256):
    m = qkv.shape[0]
    groups = n_q // n_kv
    hd = HEAD_DIM
    nqb = seq // tq
    kern = functools.partial(_gqa_kernel, groups=groups, seq=seq, past=past)
    return pl.pallas_call(
        kern,
        out_shape=jax.ShapeDtypeStruct((m, n_q * hd), BF16),
        grid=(batch, n_kv, nqb),
        in_specs=[
            pl.BlockSpec((tq, groups * hd), lambda b, h, i: (b * nqb + i, h)),
            pl.BlockSpec((seq, hd), lambda b, h, i: (b, n_q + h)),
            pl.BlockSpec((seq, hd), lambda b, h, i: (b, n_q + n_kv + h)),
            pl.BlockSpec((past, hd), lambda b, h, i: (b, h)),
            pl.BlockSpec((past, hd), lambda b, h, i: (b, h)),
            pl.BlockSpec((1, hd), lambda b, h, i: (0, 0)),
            pl.BlockSpec((1, hd), lambda b, h, i: (0, 0)),
            pl.BlockSpec((tq, hd), lambda b, h, i: (i, 0)),
            pl.BlockSpec((tq, hd), lambda b, h, i: (i, 0)),
            pl.BlockSpec((seq, hd), lambda b, h, i: (0, 0)),
            pl.BlockSpec((seq, hd), lambda b, h, i: (0, 0)),
        ],
        out_specs=pl.BlockSpec((tq, groups * hd), lambda b, h, i: (b * nqb + i, h)),
        scratch_shapes=[pltpu.VMEM((seq + past, hd), BF16), pltpu.VMEM((seq + past, hd), BF16)],
        compiler_params=_params(("parallel", "parallel", "arbitrary")),
    )(qkv, qkv, qkv, cache_k, cache_v, q_g.reshape(1, hd), k_g.reshape(1, hd), cos, sin, cos, sin)


def _na_kernel(rpb_ref, q_ref, k_ref, v_ref, ck_ref, cv_ref, qg_ref, kg_ref, o_ref,
               qn_ref, kn_ref, vb_ref, ckb_ref, cvb_ref, tile_ref, pair_ref, *, rows, wr, n_dr, n_dc):
    scale = HEAD_DIM ** -0.5
    h = pl.program_id(1)
    w = GRID_W

    qn_ref[...] = _head_norm(q_ref[...], qg_ref[...]).astype(BF16)
    kn_ref[...] = _head_norm(k_ref[...], kg_ref[...]).astype(BF16)
    vb_ref[...] = v_ref[...].astype(BF16)
    ckb_ref[...] = ck_ref[...].astype(BF16)
    cvb_ref[...] = cv_ref[...].astype(BF16)

    qc = lax.broadcasted_iota(jnp.int32, (w, 2 * w), 0)
    lane = lax.broadcasted_iota(jnp.int32, (w, 2 * w), 1)
    kc = lane % w
    cs = jnp.clip(qc - WIN_COLS // 2, 0, w - WIN_COLS)
    col_ok = (kc >= cs) & (kc < cs + WIN_COLS)
    dc = jnp.clip(kc - qc + WIN_COLS - 1, 0, n_dc - 1)
    for dr in range(n_dr):
        tile_ref[dr] = jnp.zeros((w, 2 * w), F32)
    for d in range(n_dc):
        sel = dc == d
        for dr in range(n_dr):
            tile_ref[dr] = jnp.where(sel, rpb_ref[h, dr * n_dc + d], tile_ref[dr])
    for dr in range(n_dr):
        tile_ref[dr] = jnp.where(col_ok, tile_ref[dr], NEG_INF)
    for dr in range(n_dr - 1):
        pair_ref[dr] = jnp.where(lane < w, tile_ref[dr], tile_ref[dr + 1])

    ckb = ckb_ref[...]
    cvb = cvb_ref[...]

    def body(r, _):
        rs = jnp.clip(r - wr // 2, 0, rows - wr)
        dr0 = rs - r + WIN_ROWS - 1
        q_r = qn_ref[pl.ds(pl.multiple_of(r * w, w), w), :]
        k0 = pl.multiple_of(rs * w, w)
        kw = kn_ref[pl.ds(k0, wr * w), :]
        vw = vb_ref[pl.ds(k0, wr * w), :]
        bias = jnp.concatenate([pair_ref[dr0 + 2 * i] for i in range(wr // 2)], axis=1)
        s_loc = _nt(q_r, kw) * scale + bias
        s_ctx = _nt(q_r, ckb) * scale
        mx = jnp.maximum(jnp.max(s_loc, axis=-1, keepdims=True), jnp.max(s_ctx, axis=-1, keepdims=True))
        p_loc = jnp.exp(s_loc - mx)
        p_ctx = jnp.exp(s_ctx - mx)
        l = jnp.sum(p_loc, axis=-1, keepdims=True) + jnp.sum(p_ctx, axis=-1, keepdims=True)
        o = (_mm(p_loc.astype(BF16), vw) + _mm(p_ctx.astype(BF16), cvb)) / l
        o_ref[pl.ds(pl.multiple_of(r * w, w), w), :] = o.astype(o_ref.dtype)
        return 0

    lax.fori_loop(0, rows, body, 0)


def na_attention(qkv, cache_k, cache_v, q_g, k_g, rpb, *, batch, seq, past, n_heads):
    m = qkv.shape[0]
    hd = HEAD_DIM
    rows = seq // GRID_W
    wr = min(WIN_ROWS, rows)
    n_dr, n_dc = rpb.shape[1], rpb.shape[2]
    assert wr == WIN_ROWS and wr % 2 == 0
    kern = functools.partial(_na_kernel, rows=rows, wr=wr, n_dr=n_dr, n_dc=n_dc)
    return pl.pallas_call(
        kern,
        out_shape=jax.ShapeDtypeStruct((m, n_heads * hd), BF16),
        grid=(batch, n_heads),
        in_specs=[
            pl.BlockSpec(memory_space=pltpu.SMEM),
            pl.BlockSpec((seq, hd), lambda b, h: (b, h)),
            pl.BlockSpec((seq, hd), lambda b, h: (b, n_heads + h)),
            pl.BlockSpec((seq, hd), lambda b, h: (b, 2 * n_heads + h)),
            pl.BlockSpec((past, hd), lambda b, h: (b, h)),
            pl.BlockSpec((past, hd), lambda b, h: (b, h)),
            pl.BlockSpec((1, hd), lambda b, h: (0, 0)),
            pl.BlockSpec((1, hd), lambda b, h: (0, 0)),
        ],
        out_specs=pl.BlockSpec((seq, hd), lambda b, h: (b, h)),
        scratch_shapes=[
            pltpu.VMEM((seq, hd), BF16), pltpu.VMEM((seq, hd), BF16), pltpu.VMEM((seq, hd), BF16),
            pltpu.VMEM((past, hd), BF16), pltpu.VMEM((past, hd), BF16),
            pltpu.VMEM((n_dr, GRID_W, 2 * GRID_W), F32),
            pltpu.VMEM((n_dr - 1, GRID_W, 2 * GRID_W), F32),
        ],
        compiler_params=_params(("parallel", "parallel")),
    )(rpb.reshape(n_heads, n_dr * n_dc), qkv, qkv, qkv, cache_k, cache_v,
      q_g.reshape(1, hd), k_g.reshape(1, hd))


def _gates_kernel(x_ref, g_ref, sh_ref, sc_ref, w_ref, b_ref, out_ref, xn_ref, *, tm, n_heads):
    _modulate_rows(x_ref, g_ref, sh_ref, sc_ref, xn_ref, tm)
    xh, xl = _split2(xn_ref[...])
    wh, wl = _split2(w_ref[...])
    pre = _nt(wh, xh) + _nt(wl, xh) + _nt(wh, xl) + b_ref[...]
    capped = GATE_SOFTCAP * jnp.tanh(pre / GATE_SOFTCAP)
    row = lax.broadcasted_iota(jnp.int32, capped.shape, 0)
    is_forget = (row % 2) == 1
    gates = jnp.where(is_forget, jax.nn.log_sigmoid(capped), capped)
    for c in range(tm // MLSTM_CHUNK):
        out_ref[c] = gates[:, c * MLSTM_CHUNK:(c + 1) * MLSTM_CHUNK]


def mlstm_gates(x, g, mod, w_gate_t, b_gate_t, *, layer, rows_per_batch, row0, tm=512):
    m, d = x.shape
    gh = w_gate_t.shape[0]
    tm = min(tm, rows_per_batch)
    kern = functools.partial(_gates_kernel, tm=tm, n_heads=gh // 8)
    return pl.pallas_call(
        kern,
        out_shape=jax.ShapeDtypeStruct((m // MLSTM_CHUNK, gh, MLSTM_CHUNK), F32),
        grid=(m // tm,),
        in_specs=[
            pl.BlockSpec((tm, d), lambda i: (i, 0)),
            pl.BlockSpec((None, 1, d), lambda i: (layer, 0, 0)),
            pl.BlockSpec((None, 1, d), _mod_index(layer, 0, rows_per_batch, tm, row0)),
            pl.BlockSpec((None, 1, d), _mod_index(layer, 1, rows_per_batch, tm, row0)),
            pl.BlockSpec((gh, d), lambda i: (0, 0)),
            pl.BlockSpec((gh, 1), lambda i: (0, 0)),
        ],
        out_specs=pl.BlockSpec((tm // MLSTM_CHUNK, gh, MLSTM_CHUNK), lambda i: (i, 0, 0)),
        scratch_shapes=[pltpu.VMEM((tm, d), F32)],
        compiler_params=_params(("parallel",)),
    )(x, g, mod, mod, w_gate_t, b_gate_t)


def _exact_bcast_products(lf_row, tri_jt, tri_ts):
    L = lf_row.shape[1]
    lfb = jnp.broadcast_to(lf_row, (L, L))
    pieces = _split3(lfb)
    bmat = sum(_nt(tri_jt, p) for p in pieces)
    brow = sum(_mm(p, tri_ts) for p in pieces)
    return bmat, brow


def _mlstm_chunk(q, k, vaug, li, lf, caug, m, tri_jt, tri_ts, mask):
    dv = MLSTM_DV
    bmat, brow = _exact_bcast_products(lf, tri_jt, tri_ts)
    log_d = jnp.where(mask, bmat - brow + li, NEG_INF)
    inter = bmat[:, :1] + m
    m_j = jnp.maximum(inter, jnp.max(log_d, axis=-1, keepdims=True))
    d_mat = jnp.exp(log_d - m_j)
    w_inter = jnp.exp(inter - m_j)
    kb = k.astype(BF16)
    s = _nt(q, kb) * d_mat
    nd = w_inter * _mm(q, caug.astype(BF16)) + _mm(s.astype(BF16), vaug)
    num = nd[:, :dv]
    den = nd[:, dv:dv + 1]
    h = num / jnp.maximum(jnp.abs(den), jnp.exp(-m_j))

    tot = jnp.sum(lf, axis=-1, keepdims=True)
    log_w = tot - brow[:1] + li
    m_new = jnp.maximum(tot + m, jnp.max(log_w, axis=-1, keepdims=True))
    wt = jnp.exp(log_w - m_new)
    decay = jnp.exp(tot + m - m_new)
    ktw = (k.T * wt).astype(BF16)
    caug_new = decay * caug + _mm(ktw, vaug)
    return h, caug_new, m_new


def _mlstm_kernel(*refs, n_chunks, zero_init, write_state):
    it = iter(refs)
    q_ref, k_ref, v_ref, o_ref, gates_ref, hg_ref = (next(it) for _ in range(6))
    if not zero_init:
        c0_ref, n0_ref, m0_ref = (next(it) for _ in range(3))
    y_ref = next(it)
    if write_state:
        cf_ref, nf_ref, mf_ref = (next(it) for _ in range(3))
    hsum_ref = next(it)

    L, dk, dv = MLSTM_CHUNK, MLSTM_DK, MLSTM_DV
    ri = lax.broadcasted_iota(jnp.int32, (L, L), 0)
    ci = lax.broadcasted_iota(jnp.int32, (L, L), 1)
    lower = ri >= ci
    upper = ri <= ci
    eye = ri == ci
    lower_b = jnp.where(lower, 1.0, 0.0).astype(BF16)
    upper_b = jnp.where(upper, 1.0, 0.0).astype(BF16)
    lane_v = lax.broadcasted_iota(jnp.int32, (L, 128), 1)
    ones_col = jnp.where(lane_v == 0, 1.0, 0.0).astype(BF16)
    qscale = dk ** -0.5

    for direction in range(2):
        tri_jt = lower_b if direction == 0 else upper_b
        tri_ts = upper_b if direction == 0 else lower_b
        mask = lower if direction == 0 else upper
        if zero_init:
            caug0 = jnp.zeros((dk, dv + 128), F32)
            m_init = jnp.zeros((1, 1), F32)
        else:
            n_row = jnp.broadcast_to(n0_ref[direction], (dk, dk))
            n_col = jnp.sum(jnp.where(eye, n_row, 0.0), axis=-1, keepdims=True)
            n_pad = jnp.where(lane_v == 0, n_col, 0.0)
            caug0 = jnp.concatenate([c0_ref[direction], n_pad], axis=1)
            m_init = m0_ref[direction]

        def body(step, carry, direction=direction, tri_jt=tri_jt, tri_ts=tri_ts, mask=mask):
            caug, m = carry
            c = step if direction == 0 else n_chunks - 1 - step
            r0 = pl.multiple_of(c * L, L)
            q = (q_ref[pl.ds(r0, L), :] * qscale).astype(BF16)
            k = k_ref[pl.ds(r0, L), :]
            vaug = jnp.concatenate([v_ref[pl.ds(r0, L), :].astype(BF16), ones_col], axis=1)
            gt = gates_ref[c]
            li = gt[2 * direction:2 * direction + 1]
            lf = gt[2 * direction + 1:2 * direction + 2]
            h, caug_new, m_new = _mlstm_chunk(q, k, vaug, li, lf, caug, m, tri_jt, tri_ts, mask)
            if direction == 0:
                hsum_ref[pl.ds(r0, L), :] = h
            else:
                hs = hsum_ref[pl.ds(r0, L), :] + h
                hn = hs * lax.rsqrt(jnp.mean(hs * hs, axis=-1, keepdims=True) + EPS) * hg_ref[...]
                y = hn * jax.nn.sigmoid(o_ref[pl.ds(r0, L), :])
                y_ref[pl.ds(r0, L), :] = y.astype(y_ref.dtype)
            return caug_new, m_new

        caug_f, m_f = lax.fori_loop(0, n_chunks, body, (caug0, m_init))
        if write_state:
            cf_ref[direction] = caug_f[:, :dv]
            n_b = jnp.broadcast_to(caug_f[:, dv:dv + 1], (dk, dk))
            nf_ref[direction] = jnp.sum(jnp.where(eye, n_b, 0.0), axis=0, keepdims=True)
            mf_ref[direction] = m_f


def mlstm_scan(proj, gates, head_g, state=None, *, batch, seq, n_heads, write_state):
    m = proj.shape[0]
    dk, dv, L = MLSTM_DK, MLSTM_DV, MLSTM_CHUNK
    H = n_heads
    n_chunks = seq // L
    zero_init = state is None
    kern = functools.partial(_mlstm_kernel, n_chunks=n_chunks, zero_init=zero_init, write_state=write_state)
    in_specs = [
        pl.BlockSpec((seq, dk), lambda b, h: (b, h)),
        pl.BlockSpec((seq, dk), lambda b, h: (b, H + h)),
        pl.BlockSpec((seq, dv), lambda b, h: (b, H + h)),
        pl.BlockSpec((seq, dv), lambda b, h: (b, 2 * H + h)),
        pl.BlockSpec((n_chunks, 8, L), lambda b, h: (b, h, 0)),
        pl.BlockSpec((None, 1, dv), lambda b, h: (h, 0, 0)),
    ]
    args = [proj, proj, proj, proj, gates, head_g.reshape(H, 1, dv)]
    if not zero_init:
        c0, n0, m0 = state
        in_specs += [
            pl.BlockSpec((None, 2, None, dk, dv), lambda b, h: (b, 0, h, 0, 0)),
            pl.BlockSpec((None, 2, None, 1, dk), lambda b, h: (b, 0, h, 0, 0)),
            pl.BlockSpec((None, 2, None, 1, 1), lambda b, h: (b, 0, h, 0, 0)),
        ]
        args += [c0.astype(F32), n0.astype(F32).reshape(batch, 2, H, 1, dk), m0.astype(F32).reshape(batch, 2, H, 1, 1)]
    out_shape = [jax.ShapeDtypeStruct((m, H * dv), BF16)]
    out_specs = [pl.BlockSpec((seq, dv), lambda b, h: (b, h))]
    if write_state:
        out_shape += [jax.ShapeDtypeStruct((batch, 2, H, dk, dv), F32),
                      jax.ShapeDtypeStruct((batch, 2, H, 1, dk), F32),
                      jax.ShapeDtypeStruct((batch, 2, H, 1, 1), F32)]
        out_specs += [pl.BlockSpec((None, 2, None, dk, dv), lambda b, h: (b, 0, h, 0, 0)),
                      pl.BlockSpec((None, 2, None, 1, dk), lambda b, h: (b, 0, h, 0, 0)),
                      pl.BlockSpec((None, 2, None, 1, 1), lambda b, h: (b, 0, h, 0, 0))]
    outs = pl.pallas_call(
        kern,
        out_shape=tuple(out_shape),
        grid=(batch, H),
        in_specs=in_specs,
        out_specs=tuple(out_specs),
        scratch_shapes=[pltpu.VMEM((seq, dv), F32)],
        compiler_params=_params(("parallel", "parallel")),
    )(*args)
    if write_state:
        y, cf, nf, mf = outs
        return y, (cf, nf.reshape(batch, 2, H, dk), mf.reshape(batch, 2, H))
    return outs[0], None


def _gate_weights_head_major(w_gate, b_gate, n_heads):
    d = w_gate.shape[0]
    wt = jnp.transpose(w_gate.reshape(d, 4, n_heads), (2, 1, 0))
    wt = jnp.concatenate([wt, jnp.zeros_like(wt)], axis=1).reshape(8 * n_heads, d)
    bt = jnp.transpose(b_gate.reshape(4, n_heads), (1, 0))
    bt = jnp.concatenate([bt, jnp.zeros_like(bt)], axis=1).reshape(8 * n_heads, 1)
    return wt.astype(F32), bt.astype(F32)


def _rope_tables(seq):
    nf = HEAD_DIM // 4
    t = jnp.arange(seq)
    inv = ROPE_THETA ** (-jnp.arange(nf, dtype=F32) / nf)
    pos = jnp.stack([t // GRID_W, t % GRID_W], axis=-1).astype(F32)
    ang = pos[:, :, None] * inv
    cos = jnp.cos(ang)
    sin = jnp.sin(ang)
    cos_full = jnp.stack([cos, cos], axis=2).reshape(seq, HEAD_DIM)
    sin_full = jnp.stack([-sin, sin], axis=2).reshape(seq, HEAD_DIM)
    return cos_full, sin_full


def kernel(x_prompt, x_sample, state_mlstm_C, state_mlstm_n, state_mlstm_m, cache_gqa_k, cache_gqa_v,
           cache_na_k, cache_na_v, c, c_ctx, norm1_g, norm2_g, w_ada, b_ada, w_mlp1, w_mlp2,
           mlstm_w_in, mlstm_w_gate, mlstm_b_gate, mlstm_head_g, mlstm_w_out,
           gqa_w_qkv, gqa_q_g, gqa_k_g, gqa_w_o, na_w_qkv, na_q_g, na_k_g, na_rpb, na_w_o):
    bp, tp, d = x_prompt.shape
    bs, ts, _ = x_sample.shape
    depth = w_ada.shape[0]
    past = cache_gqa_k.shape[2]
    mh = mlstm_w_gate.shape[-1] // 4
    gqa_kv = cache_gqa_k.shape[3]
    gqa_q = gqa_w_o.shape[1] // HEAD_DIM
    na_h = cache_na_k.shape[3]
    assert bs + 1 <= 8

    cond8 = jnp.zeros((8, d), F32).at[:bs].set(c).at[bs].set(c_ctx)
    mod = adaln_all(cond8, w_ada, b_ada).reshape(depth * 48, 1, d)
    g1 = norm1_g.reshape(depth, 1, d)
    g2 = norm2_g.reshape(depth, 1, d)
    cos, sin = _rope_tables(ts)

    groups = [
        dict(x=x_prompt.reshape(bp * tp, d), batch=bp, seq=tp, rpb_=bp * tp, row0=bs, ctx=True),
        dict(x=x_sample.reshape(bs * ts, d), batch=bs, seq=ts, rpb_=ts, row0=0, ctx=False),
    ]
    C_l, n_l, m_l, gk_l, gv_l, nk_l, nv_l = [], [], [], [], [], [], []

    for i in range(depth):
        kind, j = i % N_MIXERS, i // N_MIXERS
        w1 = w_mlp1[i].astype(BF16)
        w2 = w_mlp2[i].astype(BF16)
        if kind == 0:
            w_in = mlstm_w_in[j].astype(BF16)
            w_out = mlstm_w_out[j].astype(BF16)
            wg_t, bg_t = _gate_weights_head_major(mlstm_w_gate[j], mlstm_b_gate[j], mh)
        elif kind == 1:
            w_qkv = gqa_w_qkv[j].astype(BF16)
            w_o = gqa_w_o[j].astype(BF16)
        else:
            w_qkv = na_w_qkv[j].astype(BF16)
            w_o = na_w_o[j].astype(BF16)

        for grp in groups:
            x, batch, seq, ctx = grp["x"], grp["batch"], grp["seq"], grp["ctx"]
            mk = dict(layer=i, rows_per_batch=grp["rpb_"], row0=grp["row0"])
            if kind == 0:
                proj = modmm(x, g1, mod, w_in, which_shift=0, which_scale=1, **mk)
                gates = mlstm_gates(x, g1, mod, wg_t, bg_t, **mk)
                if ctx:
                    y, (cf, nf, mf) = mlstm_scan(proj, gates, mlstm_head_g[j], None, batch=batch, seq=seq,
                                                 n_heads=mh, write_state=True)
                    C_l.append(cf)
                    n_l.append(nf)
                    m_l.append(mf)
                else:
                    st = (state_mlstm_C[:, j], state_mlstm_n[:, j], state_mlstm_m[:, j])
                    y, _ = mlstm_scan(proj, gates, mlstm_head_g[j], st, batch=batch, seq=seq,
                                      n_heads=mh, write_state=False)
                x = mm_res(y, w_out, x, mod, which_gate=2, **mk)
            elif kind == 1:
                qkv = modmm(x, g1, mod, w_qkv, which_shift=0, which_scale=1, **mk)
                if ctx:
                    a, kn, v = ctx_attention(qkv, gqa_q_g[j], gqa_k_g[j], batch=batch, seq=seq, n_q=gqa_q, n_kv=gqa_kv)
                    gk_l.append(kn.reshape(batch, seq, gqa_kv, HEAD_DIM))
                    gv_l.append(v.reshape(batch, seq, gqa_kv, HEAD_DIM))
                else:
                    ck = cache_gqa_k[:, j].astype(F32).reshape(batch * past, gqa_kv * HEAD_DIM)
                    cv = cache_gqa_v[:, j].astype(F32).reshape(batch * past, gqa_kv * HEAD_DIM)
                    a = gqa_attention(qkv, ck, cv, gqa_q_g[j], gqa_k_g[j], cos, sin, batch=batch, seq=seq,
                                      past=past, n_q=gqa_q, n_kv=gqa_kv)
                x = mm_res(a, w_o, x, mod, which_gate=2, **mk)
            else:
                qkv = modmm(x, g1, mod, w_qkv, which_shift=0, which_scale=1, **mk)
                if ctx:
                    a, kn, v = ctx_attention(qkv, na_q_g[j], na_k_g[j], batch=batch, seq=seq, n_q=na_h, n_kv=na_h)
                    nk_l.append(kn.reshape(batch, seq, na_h, HEAD_DIM))
                    nv_l.append(v.reshape(batch, seq, na_h, HEAD_DIM))
                else:
                    ck = cache_na_k[:, j].astype(F32).reshape(batch * past, na_h * HEAD_DIM)
                    cv = cache_na_v[:, j].astype(F32).reshape(batch * past, na_h * HEAD_DIM)
                    a = na_attention(qkv, ck, cv, na_q_g[j], na_k_g[j], na_rpb[j], batch=batch, seq=seq,
                                     past=past, n_heads=na_h)
                x = mm_res(a, w_o, x, mod, which_gate=2, **mk)
            hid = modmm(x, g2, mod, w1, which_shift=3, which_scale=4, relu2=True, out_dtype=BF16, **mk)
            x = mm_res(hid, w2, x, mod, which_gate=5, **mk)
            grp["x"] = x

    y_prompt = groups[0]["x"].reshape(bp, tp, d)
    y_sample = groups[1]["x"].reshape(bs, ts, d)
    return (y_prompt, y_sample, jnp.stack(C_l, axis=1), jnp.stack(n_l, axis=1), jnp.stack(m_l, axis=1),
            jnp.stack(gk_l, axis=1), jnp.stack(gv_l, axis=1), jnp.stack(nk_l, axis=1), jnp.stack(nv_l, axis=1))
```

```python
import functools
import math

import jax
import jax.numpy as jnp
from jax import lax
from jax.experimental import pallas as pl
from jax.experimental.pallas import tpu as pltpu

EPS = 1e-6
HEAD_DIM = 128
MLSTM_DK = 128
MLSTM_DV = 256
MLSTM_CHUNK = 128
GATE_SOFTCAP = 15.0
GRID_W = 64
WIN_ROWS = 8
WIN_COLS = 16
ROPE_THETA = 10000.0
N_MIXERS = 3
LOG2E = math.log2(math.e)

VMEM_LIMIT_BYTES = 56 * 1024 * 1024
NORM_ROW_CHUNK = 16

F32 = jnp.float32
BF16 = jnp.bfloat16
NEG_INF = float("-inf")


def _params(sem):
    return pltpu.CompilerParams(dimension_semantics=sem, vmem_limit_bytes=VMEM_LIMIT_BYTES)


def _nt(a, b):
    return lax.dot_general(a, b, (((1,), (1,)), ((), ())), preferred_element_type=F32)


def _mm(a, b):
    return jnp.dot(a, b, preferred_element_type=F32)


def _split2(x):
    hi = x.astype(BF16)
    lo = (x - hi.astype(F32)).astype(BF16)
    return hi, lo


def _split3(x):
    hi = x.astype(BF16)
    r = x - hi.astype(F32)
    mid = r.astype(BF16)
    lo = (r - mid.astype(F32)).astype(BF16)
    return hi, mid, lo


def _mod_row(layer, which, rows_per_batch, tm, row0):
    def f(i, *_):
        b = row0 + (i * tm) // rows_per_batch
        return (layer * 48 + b * 6 + which, 0, 0)
    return f


def _adaln_kernel(cond_ref, w_ref, b_ref, out_ref):
    c = cond_ref[...]
    s = c * jax.nn.sigmoid(c)
    sh, sl = _split2(s)
    wh, wl = _split2(w_ref[...])
    lhs = jnp.concatenate([sh, sl], axis=0)
    r = _mm(lhs, wh)
    out_ref[...] = r[:8] + r[8:] + _mm(sh, wl) + b_ref[...]


def adaln_all(cond8, w_ada, b_ada, tn=1024):
    nl, d, n = w_ada.shape
    return pl.pallas_call(
        _adaln_kernel,
        out_shape=jax.ShapeDtypeStruct((nl, 8, n), F32),
        grid=(nl, n // tn),
        in_specs=[
            pl.BlockSpec((8, d), lambda l, j: (0, 0)),
            pl.BlockSpec((None, d, tn), lambda l, j: (l, 0, j)),
            pl.BlockSpec((None, 1, tn), lambda l, j: (l, 0, j)),
        ],
        out_specs=pl.BlockSpec((None, 8, tn), lambda l, j: (l, 0, j)),
        compiler_params=_params(("parallel", "parallel")),
        name="adaln",
    )(cond8, w_ada, b_ada.reshape(nl, 1, n))


def _modulate_rows(src_ref, dst_ref, g, mul, sh, r0, n_rows):
    for c in range(n_rows // NORM_ROW_CHUNK):
        start = r0 + c * NORM_ROW_CHUNK
        if not isinstance(start, int):
            start = pl.multiple_of(start, NORM_ROW_CHUNK)
        rows = pl.ds(start, NORM_ROW_CHUNK)
        xf = src_ref[rows, :]
        y = xf * lax.rsqrt(jnp.mean(xf * xf, axis=-1, keepdims=True) + EPS)
        dst_ref[rows, :] = ((y * g) * mul + sh).astype(dst_ref.dtype)


def _modulate_kernel(x_ref, g_ref, sh_ref, sc_ref, out_ref, *, tm):
    g = g_ref[...]
    mul = 1.0 + sc_ref[...]
    sh = sh_ref[...]
    unroll = 4 * NORM_ROW_CHUNK

    def body(i, _):
        _modulate_rows(x_ref, out_ref, g, mul, sh, pl.multiple_of(i * unroll, unroll), unroll)
        return 0

    lax.fori_loop(0, tm // unroll, body, 0)


def modulate(x, g, mod, *, layer, which_shift, which_scale, rows_per_batch, row0, tm=256):
    m, d = x.shape
    tm = min(tm, rows_per_batch)
    return pl.pallas_call(
        functools.partial(_modulate_kernel, tm=tm),
        out_shape=jax.ShapeDtypeStruct((m, d), BF16),
        grid=(m // tm,),
        in_specs=[
            pl.BlockSpec((tm, d), lambda i: (i, 0)),
            pl.BlockSpec((None, 1, d), lambda i: (layer, 0, 0)),
            pl.BlockSpec((None, 1, d), _mod_row(layer, which_shift, rows_per_batch, tm, row0)),
            pl.BlockSpec((None, 1, d), _mod_row(layer, which_scale, rows_per_batch, tm, row0)),
        ],
        out_specs=pl.BlockSpec((tm, d), lambda i: (i, 0)),
        compiler_params=_params(("parallel",)),
        name="modulate",
    )(x, g, mod, mod)


def _mm_kernel(a_ref, w_ref, out_ref):
    out_ref[...] = _mm(a_ref[...], w_ref[...]).astype(out_ref.dtype)


def matmul(a, w, *, out_dtype=F32, tm=1024, tn=1024, name="proj"):
    m, k = a.shape
    n = w.shape[1]
    return pl.pallas_call(
        _mm_kernel,
        out_shape=jax.ShapeDtypeStruct((m, n), out_dtype),
        grid=(m // tm, n // tn),
        in_specs=[
            pl.BlockSpec((tm, k), lambda i, j: (i, 0)),
            pl.BlockSpec((k, tn), lambda i, j: (0, j)),
        ],
        out_specs=pl.BlockSpec((tm, tn), lambda i, j: (i, j)),
        compiler_params=_params(("parallel", "arbitrary")),
        name=name,
    )(a, w)


def _mmres_kernel(a_ref, w_ref, res_ref, gate_ref, g_ref, sh_ref, sc_ref, x_ref, xn_ref, *, tm, sub):
    gate = gate_ref[...]
    g = g_ref[...]
    mul = 1.0 + sc_ref[...]
    sh = sh_ref[...]
    for s in range(tm // sub):
        rows = pl.ds(s * sub, sub)
        x_ref[rows, :] = res_ref[rows, :] + gate * _mm(a_ref[rows, :], w_ref[...])
        _modulate_rows(x_ref, xn_ref, g, mul, sh, s * sub, sub)


def mm_res_norm(a, w, res, mod, g_next, *, layer, which_gate, next_layer, next_shift, next_scale,
                rows_per_batch, row0, tm=512, sub=128):
    m, k = a.shape
    d = w.shape[1]
    tm = min(tm, rows_per_batch)
    mk = (rows_per_batch, tm, row0)
    return pl.pallas_call(
        functools.partial(_mmres_kernel, tm=tm, sub=sub),
        out_shape=(jax.ShapeDtypeStruct((m, d), F32), jax.ShapeDtypeStruct((m, d), BF16)),
        grid=(m // tm,),
        in_specs=[
            pl.BlockSpec((tm, k), lambda i: (i, 0)),
            pl.BlockSpec((k, d), lambda i: (0, 0), pipeline_mode=pl.Buffered(1)),
            pl.BlockSpec((tm, d), lambda i: (i, 0)),
            pl.BlockSpec((None, 1, d), _mod_row(layer, which_gate, *mk)),
            pl.BlockSpec((None, 1, d), lambda i: (next_layer, 0, 0)),
            pl.BlockSpec((None, 1, d), _mod_row(next_layer, next_shift, *mk)),
            pl.BlockSpec((None, 1, d), _mod_row(next_layer, next_scale, *mk)),
        ],
        out_specs=(pl.BlockSpec((tm, d), lambda i: (i, 0)), pl.BlockSpec((tm, d), lambda i: (i, 0))),
        compiler_params=_params(("parallel",)),
        name="outproj_res_norm",
    )(a, w, res, mod, g_next, mod, mod)


def _mlp_kernel(xn_ref, w1_ref, w2_ref, res_ref, gate_ref, g_ref, sh_ref, sc_ref, x_ref, *rest, tm, emit_next):
    xno_ref = rest[0] if emit_next else None
    acc_ref = rest[-1]
    f = pl.program_id(1)

    @pl.when(f == 0)
    def _():
        acc_ref[...] = jnp.zeros_like(acc_ref)

    h = _mm(xn_ref[...], w1_ref[...])
    h = jnp.square(jnp.maximum(h, 0.0)).astype(BF16)
    acc_ref[...] += _mm(h, w2_ref[...])

    @pl.when(f == pl.num_programs(1) - 1)
    def _():
        gate = gate_ref[...]
        g = g_ref[...]
        mul = 1.0 + sc_ref[...]
        sh = sh_ref[...]
        unroll = 4 * NORM_ROW_CHUNK

        def body(i, _):
            r0 = pl.multiple_of(i * unroll, unroll)
            rows = pl.ds(r0, unroll)
            x_ref[rows, :] = res_ref[rows, :] + gate * acc_ref[rows, :]
            if emit_next:
                _modulate_rows(x_ref, xno_ref, g, mul, sh, r0, unroll)
            return 0

        lax.fori_loop(0, tm // unroll, body, 0)


def fused_mlp(xn, w1, w2, res, mod, g_next, *, layer, next_layer, rows_per_batch, row0, emit_next,
              tm=1024, fc=512):
    m, d = xn.shape
    ff = w1.shape[1]
    tm = min(tm, rows_per_batch)
    mk = (rows_per_batch, tm, row0)
    once = pl.Buffered(1)
    row_tile = pl.BlockSpec((tm, d), lambda i, f: (i, 0), pipeline_mode=once)
    out_shape = [jax.ShapeDtypeStruct((m, d), F32)] + ([jax.ShapeDtypeStruct((m, d), BF16)] if emit_next else [])
    outs = pl.pallas_call(
        functools.partial(_mlp_kernel, tm=tm, emit_next=emit_next),
        out_shape=tuple(out_shape),
        grid=(m // tm, ff // fc),
        in_specs=[
            pl.BlockSpec((tm, d), lambda i, f: (i, 0)),
            pl.BlockSpec((d, fc), lambda i, f: (0, f)),
            pl.BlockSpec((fc, d), lambda i, f: (f, 0)),
            row_tile,
            pl.BlockSpec((None, 1, d), _mod_row(layer, 5, *mk)),
            pl.BlockSpec((None, 1, d), lambda i, f: (next_layer, 0, 0)),
            pl.BlockSpec((None, 1, d), _mod_row(next_layer, 0, *mk)),
            pl.BlockSpec((None, 1, d), _mod_row(next_layer, 1, *mk)),
        ],
        out_specs=tuple(row_tile for _ in out_shape),
        scratch_shapes=[pltpu.VMEM((tm, d), F32)],
        compiler_params=_params(("parallel", "arbitrary")),
        name="fused_mlp",
    )(xn, w1, w2, res, mod, g_next, mod, mod)
    return (outs[0], outs[1]) if emit_next else (outs[0], None)


def _head_norm(x, g):
    return x * lax.rsqrt(jnp.mean(x * x, axis=-1, keepdims=True) + EPS) * g


_Q_SCALE = HEAD_DIM ** -0.5 * LOG2E


def _with_ones_column(v):
    lane = lax.broadcasted_iota(jnp.int32, v.shape, 1)
    return jnp.concatenate([v.astype(BF16), jnp.where(lane == 0, 1.0, 0.0).astype(BF16)], axis=1)


def _softmax2_pv(s, vaug):
    p = jnp.exp2(s - jnp.max(s, axis=-1, keepdims=True)).astype(BF16)
    pv = _mm(p, vaug)
    return pv[:, :HEAD_DIM] / pv[:, HEAD_DIM:HEAD_DIM + 1]


def _rope(x, cos, sin_signed, lane_is_first):
    swapped = jnp.where(lane_is_first, pltpu.roll(x, 96, 1), pltpu.roll(x, 32, 1))
    return x * cos + swapped * sin_signed


def _ctx_attn_kernel(q_ref, k_ref, v_ref, qg_ref, kg_ref, o_ref, kn_ref, vo_ref, *, groups, heads):
    hd = HEAD_DIM
    qg = qg_ref[...] * _Q_SCALE
    for h in range(heads):
        hs = slice(h * hd, (h + 1) * hd)
        kn = _head_norm(k_ref[:, hs], kg_ref[...])
        kn_ref[:, hs] = kn
        v = v_ref[:, hs]
        vo_ref[:, hs] = v
        kb = kn.astype(BF16)
        vb = v.astype(BF16)
        for g in range(groups):
            qs = slice((h * groups + g) * hd, (h * groups + g + 1) * hd)
            qn = _head_norm(q_ref[:, qs], qg).astype(BF16)
            s = _nt(qn, kb)
            p = jnp.exp2(s - jnp.max(s, axis=-1, keepdims=True))
            o_ref[:, qs] = (_mm(p.astype(BF16), vb) / jnp.sum(p, axis=-1, keepdims=True)).astype(o_ref.dtype)


def ctx_attention(qkv, q_g, k_g, *, batch, seq, n_q, n_kv, heads=4):
    m = qkv.shape[0]
    groups = n_q // n_kv
    hd = HEAD_DIM
    kw = heads * hd
    qw = heads * groups * hd
    assert (n_q * hd) % kw == 0 and ((n_q + n_kv) * hd) % kw == 0
    k_blk0 = n_q * hd // kw
    v_blk0 = (n_q + n_kv) * hd // kw
    kern = functools.partial(_ctx_attn_kernel, groups=groups, heads=heads)
    return pl.pallas_call(
        kern,
        out_shape=(jax.ShapeDtypeStruct((m, n_q * hd), BF16),
                   jax.ShapeDtypeStruct((m, n_kv * hd), F32),
                   jax.ShapeDtypeStruct((m, n_kv * hd), F32)),
        grid=(batch, n_kv // heads),
        in_specs=[
            pl.BlockSpec((seq, qw), lambda b, h: (b, h)),
            pl.BlockSpec((seq, kw), lambda b, h: (b, k_blk0 + h)),
            pl.BlockSpec((seq, kw), lambda b, h: (b, v_blk0 + h)),
            pl.BlockSpec((1, hd), lambda b, h: (0, 0)),
            pl.BlockSpec((1, hd), lambda b, h: (0, 0)),
        ],
        out_specs=(pl.BlockSpec((seq, qw), lambda b, h: (b, h)),
                   pl.BlockSpec((seq, kw), lambda b, h: (b, h)),
                   pl.BlockSpec((seq, kw), lambda b, h: (b, h))),
        compiler_params=_params(("parallel", "parallel")),
        name="ctx_attention",
    )(qkv, qkv, qkv, q_g.reshape(1, hd), k_g.reshape(1, hd))


def _gqa_kernel(q_ref, k_ref, v_ref, ck_ref, cv_ref, qg_ref, kg_ref, cosq_ref, sinq_ref, cosk_ref, sink_ref,
                o_ref, kall_ref, vall_ref, *, groups, seq, past):
    lane = lax.broadcasted_iota(jnp.int32, (1, HEAD_DIM), 1)
    first = (lane % 64) < 32

    @pl.when(pl.program_id(2) == 0)
    def _():
        kn = _head_norm(k_ref[...], kg_ref[...])
        kall_ref[0:seq, :] = _rope(kn, cosk_ref[...], sink_ref[...], first).astype(BF16)
        kall_ref[seq:seq + past, :] = ck_ref[...].astype(BF16)
        vall_ref[0:seq, :] = _with_ones_column(v_ref[...])
        vall_ref[seq:seq + past, :] = _with_ones_column(cv_ref[...])

    cos = cosq_ref[...]
    sin = sinq_ref[...]
    qg = qg_ref[...] * _Q_SCALE
    for g in range(groups):
        qn = _head_norm(q_ref[:, g * HEAD_DIM:(g + 1) * HEAD_DIM], qg)
        qr = _rope(qn, cos, sin, first).astype(BF16)
        o_ref[:, g * HEAD_DIM:(g + 1) * HEAD_DIM] = _softmax2_pv(_nt(qr, kall_ref[...]), vall_ref[...]).astype(o_ref.dtype)


def gqa_attention(qkv, cache_k, cache_v, q_g, k_g, cos, sin, *, batch, seq, past, n_q, n_kv, tq=256):
    m = qkv.shape[0]
    groups = n_q // n_kv
    hd = HEAD_DIM
    nqb = seq // tq
    kern = functools.partial(_gqa_kernel, groups=groups, seq=seq, past=past)
    return pl.pallas_call(
        kern,
        out_shape=jax.ShapeDtypeStruct((m, n_q * hd), BF16),
        grid=(batch, n_kv, nqb),
        in_specs=[
            pl.BlockSpec((tq, groups * hd), lambda b, h, i: (b * nqb + i, h)),
            pl.BlockSpec((seq, hd), lambda b, h, i: (b, n_q + h)),
            pl.BlockSpec((seq, hd), lambda b, h, i: (b, n_q + n_kv + h)),
            pl.BlockSpec((past, hd), lambda b, h, i: (b, h)),
            pl.BlockSpec((past, hd), lambda b, h, i: (b, h)),
            pl.BlockSpec((1, hd), lambda b, h, i: (0, 0)),
            pl.BlockSpec((1, hd), lambda b, h, i: (0, 0)),
            pl.BlockSpec((tq, hd), lambda b, h, i: (i, 0)),
            pl.BlockSpec((tq, hd), lambda b, h, i: (i, 0)),
            pl.BlockSpec((seq, hd), lambda b, h, i: (0, 0)),
            pl.BlockSpec((seq, hd), lambda b, h, i: (0, 0)),
        ],
        out_specs=pl.BlockSpec((tq, groups * hd), lambda b, h, i: (b * nqb + i, h)),
        scratch_shapes=[pltpu.VMEM((seq + past, hd), BF16), pltpu.VMEM((seq + past, 2 * hd), BF16)],
        compiler_params=_params(("parallel", "parallel", "arbitrary")),
        name="gqa_attention",
    )(qkv, qkv, qkv, cache_k, cache_v, q_g.reshape(1, hd), k_g.reshape(1, hd), cos, sin, cos, sin)


def _na_kernel(rpb_ref, q_ref, k_ref, v_ref, ck_ref, cv_ref, qg_ref, kg_ref, o_ref,
               qn_ref, kn_ref, vb_ref, ckb_ref, cvb_ref, tile_ref, pair_ref, *, rows, wr, n_dr, n_dc, unroll):
    h = pl.program_id(0)
    w = GRID_W

    qn_ref[...] = _head_norm(q_ref[...], qg_ref[...] * _Q_SCALE).astype(BF16)
    kn_ref[...] = _head_norm(k_ref[...], kg_ref[...]).astype(BF16)
    vb_ref[...] = _with_ones_column(v_ref[...])
    ckb_ref[...] = ck_ref[...].astype(BF16)
    cvb_ref[...] = _with_ones_column(cv_ref[...])

    @pl.when(pl.program_id(1) == 0)
    def _():
        qc = lax.broadcasted_iota(jnp.int32, (w, 2 * w), 0)
        lane = lax.broadcasted_iota(jnp.int32, (w, 2 * w), 1)
        kc = lane % w
        cs = jnp.clip(qc - WIN_COLS // 2, 0, w - WIN_COLS)
        col_ok = (kc >= cs) & (kc < cs + WIN_COLS)
        dc = jnp.clip(kc - qc + WIN_COLS - 1, 0, n_dc - 1)
        tiles = [jnp.zeros((w, 2 * w), F32) for _ in range(n_dr)]
        for d in range(n_dc):
            sel = dc == d
            for dr in range(n_dr):
                tiles[dr] = jnp.where(sel, rpb_ref[h, dr * n_dc + d], tiles[dr])
        for dr in range(n_dr):
            tile_ref[dr] = jnp.where(col_ok, tiles[dr] * LOG2E, NEG_INF)
        for dr in range(n_dr - 1):
            pair_ref[dr] = jnp.where(lane < w, tile_ref[dr], tile_ref[dr + 1])

    zero_bias = jnp.zeros((w, ckb_ref.shape[0]), F32)

    def body(i, _):
        rws = [i * unroll + u for u in range(unroll)]
        rss = [jnp.clip(r - wr // 2, 0, rows - wr) for r in rws]
        k0s = [pl.multiple_of(rs * w, w) for rs in rss]
        scores = []
        for r, rs, k0 in zip(rws, rss, k0s):
            dr0 = rs - r + WIN_ROWS - 1
            q_r = qn_ref[pl.ds(pl.multiple_of(r * w, w), w), :]
            kcat = jnp.concatenate([kn_ref[pl.ds(k0, wr * w), :], ckb_ref[...]], axis=0)
            bias = jnp.concatenate([pair_ref[dr0 + 2 * j] for j in range(wr // 2)] + [zero_bias], axis=1)
            scores.append(_nt(q_r, kcat) + bias)
        probs = [jnp.exp2(s - jnp.max(s, axis=-1, keepdims=True)).astype(BF16) for s in scores]
        for r, k0, p in zip(rws, k0s, probs):
            vcat = jnp.concatenate([vb_ref[pl.ds(k0, wr * w), :], cvb_ref[...]], axis=0)
            pv = _mm(p, vcat)
            o = pv[:, :HEAD_DIM] / pv[:, HEAD_DIM:HEAD_DIM + 1]
            o_ref[pl.ds(pl.multiple_of(r * w, w), w), :] = o.astype(o_ref.dtype)
        return 0

    lax.fori_loop(0, rows // unroll, body, 0)


def na_attention(qkv, cache_k, cache_v, q_g, k_g, rpb, *, batch, seq, past, n_heads, unroll=4):
    m = qkv.shape[0]
    hd = HEAD_DIM
    rows = seq // GRID_W
    wr = min(WIN_ROWS, rows)
    n_dr, n_dc = rpb.shape[1], rpb.shape[2]
    assert wr == WIN_ROWS and wr % 2 == 0 and rows % unroll == 0
    kern = functools.partial(_na_kernel, rows=rows, wr=wr, n_dr=n_dr, n_dc=n_dc, unroll=unroll)
    return pl.pallas_call(
        kern,
        out_shape=jax.ShapeDtypeStruct((m, n_heads * hd), BF16),
        grid=(n_heads, batch),
        in_specs=[
            pl.BlockSpec(memory_space=pltpu.SMEM),
            pl.BlockSpec((seq, hd), lambda h, b: (b, h)),
            pl.BlockSpec((seq, hd), lambda h, b: (b, n_heads + h)),
            pl.BlockSpec((seq, hd), lambda h, b: (b, 2 * n_heads + h)),
            pl.BlockSpec((past, hd), lambda h, b: (b, h)),
            pl.BlockSpec((past, hd), lambda h, b: (b, h)),
            pl.BlockSpec((1, hd), lambda h, b: (0, 0)),
            pl.BlockSpec((1, hd), lambda h, b: (0, 0)),
        ],
        out_specs=pl.BlockSpec((seq, hd), lambda h, b: (b, h)),
        scratch_shapes=[
            pltpu.VMEM((seq, hd), BF16), pltpu.VMEM((seq, hd), BF16), pltpu.VMEM((seq, 2 * hd), BF16),
            pltpu.VMEM((past, hd), BF16), pltpu.VMEM((past, 2 * hd), BF16),
            pltpu.VMEM((n_dr, GRID_W, 2 * GRID_W), F32),
            pltpu.VMEM((n_dr - 1, GRID_W, 2 * GRID_W), F32),
        ],
        compiler_params=_params(("parallel", "arbitrary")),
        name="na_attention",
    )(rpb.reshape(n_heads, n_dr * n_dc), qkv, qkv, qkv, cache_k, cache_v,
      q_g.reshape(1, hd), k_g.reshape(1, hd))


def _gates_kernel(xn_ref, w_ref, b_ref, out_ref, *, tm):
    L = MLSTM_CHUNK
    wh, wl = _split2(w_ref[...])
    xn = xn_ref[...]
    pre = _nt(wh, xn) + _nt(wl, xn) + b_ref[...]
    capped = GATE_SOFTCAP * jnp.tanh(pre / GATE_SOFTCAP)
    row = lax.broadcasted_iota(jnp.int32, (capped.shape[0], L), 0) % 8
    is_input = (row == 0) | (row == 2)
    ri = lax.broadcasted_iota(jnp.int32, (L, L), 0)
    ci = lax.broadcasted_iota(jnp.int32, (L, L), 1)
    upper = jnp.where(ri <= ci, 1.0, 0.0).astype(BF16)
    lower = jnp.where(ri >= ci, 1.0, 0.0).astype(BF16)
    for c in range(tm // L):
        cap = capped[:, c * L:(c + 1) * L]
        gates = jnp.where(is_input, cap, jax.nn.log_sigmoid(cap))
        pieces = _split3(gates)
        prefix = sum(_mm(p, upper) for p in pieces)
        suffix = sum(_mm(p, lower) for p in pieces)
        out_ref[c] = jnp.where(row == 4, prefix, jnp.where(row == 5, suffix, gates))


def mlstm_gates(xn, w_gate_t, b_gate_t, *, tm=512):
    m, d = xn.shape
    gh = w_gate_t.shape[0]
    return pl.pallas_call(
        functools.partial(_gates_kernel, tm=tm),
        out_shape=jax.ShapeDtypeStruct((m // MLSTM_CHUNK, gh, MLSTM_CHUNK), F32),
        grid=(m // tm,),
        in_specs=[
            pl.BlockSpec((tm, d), lambda i: (i, 0)),
            pl.BlockSpec((gh, d), lambda i: (0, 0)),
            pl.BlockSpec((gh, 1), lambda i: (0, 0)),
        ],
        out_specs=pl.BlockSpec((tm // MLSTM_CHUNK, gh, MLSTM_CHUNK), lambda i: (i, 0, 0)),
        compiler_params=_params(("parallel",)),
        name="mlstm_gates",
    )(xn, w_gate_t, b_gate_t)


def _mlstm_chunk(q, k, vaug, li, lf, brow, caug, m, mask):
    dv = MLSTM_DV
    L = brow.shape[1]
    bmat = jnp.broadcast_to(brow, (L, L)).T
    log_d = jnp.where(mask, bmat - brow + li, NEG_INF)
    inter = bmat[:, :1] + m
    m_j = jnp.maximum(inter, jnp.max(log_d, axis=-1, keepdims=True))
    d_mat = jnp.exp(log_d - m_j)
    w_inter = jnp.exp(inter - m_j)
    s = _nt(q, k.astype(BF16)) * d_mat
    nd = w_inter * _mm(q, caug.astype(BF16)) + _mm(s.astype(BF16), vaug)
    h = nd[:, :dv] / jnp.maximum(jnp.abs(nd[:, dv:dv + 1]), jnp.exp(-m_j))

    tot = jnp.sum(lf, axis=-1, keepdims=True)
    log_w = tot - brow + li
    m_new = jnp.maximum(tot + m, jnp.max(log_w, axis=-1, keepdims=True))
    wt = jnp.exp(log_w - m_new)
    decay = jnp.exp(tot + m - m_new)
    ktw = (k.T * wt).astype(BF16)
    caug_new = decay * caug + _mm(ktw, vaug)
    return h, caug_new, m_new


def _mlstm_kernel(*refs, n_chunks, heads, zero_init, write_state):
    it = iter(refs)
    q_ref, k_ref, v_ref, o_ref, gates_ref, hg_ref = (next(it) for _ in range(6))
    if not zero_init:
        c0_ref, n0_ref, m0_ref = (next(it) for _ in range(3))
    y_ref = next(it)
    if write_state:
        cf_ref, nf_ref, mf_ref = (next(it) for _ in range(3))
    hdir_ref, cst_ref = next(it), next(it)

    L, dk, dv = MLSTM_CHUNK, MLSTM_DK, MLSTM_DV
    ri = lax.broadcasted_iota(jnp.int32, (L, L), 0)
    ci = lax.broadcasted_iota(jnp.int32, (L, L), 1)
    masks = (ri >= ci, ri <= ci)
    eye = ri == ci
    lane_v = lax.broadcasted_iota(jnp.int32, (L, 128), 1)
    ones_col = jnp.where(lane_v == 0, 1.0, 0.0).astype(BF16)
    qscale = dk ** -0.5

    m_init = []
    for hh in range(heads):
        for direction in range(2):
            idx = hh * 2 + direction
            if zero_init:
                cst_ref[idx] = jnp.zeros((dk, dv + 128), F32)
                m_init.append(jnp.zeros((1, 1), F32))
            else:
                n_row = jnp.broadcast_to(n0_ref[direction, hh], (dk, dk))
                n_col = jnp.sum(jnp.where(eye, n_row, 0.0), axis=-1, keepdims=True)
                cst_ref[idx, :, 0:dv] = c0_ref[direction, hh]
                cst_ref[idx, :, dv:dv + 128] = jnp.where(lane_v == 0, n_col, 0.0)
                m_init.append(m0_ref[direction, hh])

    def body(step, ms):
        new_ms = []
        for hh in range(heads):
            for direction in range(2):
                idx = hh * 2 + direction
                c = step if direction == 0 else n_chunks - 1 - step
                r0 = pl.multiple_of(c * L, L)
                q = (q_ref[pl.ds(r0, L), hh * dk:(hh + 1) * dk] * qscale).astype(BF16)
                k = k_ref[pl.ds(r0, L), hh * dk:(hh + 1) * dk]
                vaug = jnp.concatenate([v_ref[pl.ds(r0, L), hh * dv:(hh + 1) * dv].astype(BF16), ones_col], axis=1)
                gt = gates_ref[c, hh * 8:(hh + 1) * 8, :]
                li = gt[2 * direction:2 * direction + 1]
                lf = gt[2 * direction + 1:2 * direction + 2]
                brow = gt[4 + direction:5 + direction]
                h, caug_new, m_new = _mlstm_chunk(q, k, vaug, li, lf, brow, cst_ref[idx], ms[idx], masks[direction])
                cst_ref[idx] = caug_new
                hdir_ref[direction, pl.ds(r0, L), hh * dv:(hh + 1) * dv] = h
                new_ms.append(m_new)
        return tuple(new_ms)

    m_fin = lax.fori_loop(0, n_chunks, body, tuple(m_init))

    def finish(c, _):
        r0 = pl.multiple_of(c * L, L)
        for hh in range(heads):
            cols = slice(hh * dv, (hh + 1) * dv)
            hs = hdir_ref[0, pl.ds(r0, L), cols] + hdir_ref[1, pl.ds(r0, L), cols]
            hn = hs * lax.rsqrt(jnp.mean(hs * hs, axis=-1, keepdims=True) + EPS) * hg_ref[:, cols]
            y_ref[pl.ds(r0, L), cols] = (hn * jax.nn.sigmoid(o_ref[pl.ds(r0, L), cols])).astype(y_ref.dtype)
        return 0

    lax.fori_loop(0, n_chunks, finish, 0)

    if write_state:
        for hh in range(heads):
            for direction in range(2):
                idx = hh * 2 + direction
                cf_ref[direction, hh] = cst_ref[idx, :, 0:dv]
                n_b = jnp.broadcast_to(cst_ref[idx, :, dv:dv + 1], (dk, dk))
                nf_ref[direction, hh] = jnp.sum(jnp.where(eye, n_b, 0.0), axis=0, keepdims=True)
                mf_ref[direction, hh] = m_fin[idx]


def mlstm_scan(proj, gates, head_g, state=None, *, batch, seq, n_heads, write_state, heads=2):
    m = proj.shape[0]
    dk, dv, L = MLSTM_DK, MLSTM_DV, MLSTM_CHUNK
    H = n_heads
    nb = H // heads
    n_chunks = seq // L
    zero_init = state is None
    kern = functools.partial(_mlstm_kernel, n_chunks=n_chunks, heads=heads, zero_init=zero_init,
                             write_state=write_state)
    in_specs = [
        pl.BlockSpec((seq, heads * dk), lambda b, h: (b, h)),
        pl.BlockSpec((seq, heads * dk), lambda b, h: (b, nb + h)),
        pl.BlockSpec((seq, heads * dv), lambda b, h: (b, nb + h)),
        pl.BlockSpec((seq, heads * dv), lambda b, h: (b, 2 * nb + h)),
        pl.BlockSpec((n_chunks, heads * 8, L), lambda b, h: (b, h, 0)),
        pl.BlockSpec((None, 1, heads * dv), lambda b, h: (h, 0, 0)),
    ]
    args = [proj, proj, proj, proj, gates, head_g.reshape(nb, 1, heads * dv)]
    st_specs = [pl.BlockSpec((None, 2, heads, dk, dv), lambda b, h: (b, 0, h, 0, 0)),
                pl.BlockSpec((None, 2, heads, 1, dk), lambda b, h: (b, 0, h, 0, 0)),
                pl.BlockSpec((None, 2, heads, 1, 1), lambda b, h: (b, 0, h, 0, 0))]
    if not zero_init:
        c0, n0, m0 = state
        in_specs += st_specs
        args += [c0.astype(F32), n0.astype(F32).reshape(batch, 2, H, 1, dk), m0.astype(F32).reshape(batch, 2, H, 1, 1)]
    out_shape = [jax.ShapeDtypeStruct((m, H * dv), BF16)]
    out_specs = [pl.BlockSpec((seq, heads * dv), lambda b, h: (b, h))]
    if write_state:
        out_shape += [jax.ShapeDtypeStruct((batch, 2, H, dk, dv), F32),
                      jax.ShapeDtypeStruct((batch, 2, H, 1, dk), F32),
                      jax.ShapeDtypeStruct((batch, 2, H, 1, 1), F32)]
        out_specs += st_specs
    outs = pl.pallas_call(
        kern,
        out_shape=tuple(out_shape),
        grid=(batch, nb),
        in_specs=in_specs,
        out_specs=tuple(out_specs),
        scratch_shapes=[pltpu.VMEM((2, seq, heads * dv), F32), pltpu.VMEM((2 * heads, dk, dv + 128), F32)],
        compiler_params=_params(("parallel", "parallel")),
        name="mlstm_scan",
    )(*args)
    if write_state:
        y, cf, nf, mf = outs
        return y, (cf, nf.reshape(batch, 2, H, dk), mf.reshape(batch, 2, H))
    return outs[0], None


def _gate_weights_head_major(w_gate, b_gate, n_heads):
    d = w_gate.shape[0]
    wt = jnp.transpose(w_gate.reshape(d, 4, n_heads), (2, 1, 0)).astype(F32)
    wt = jnp.concatenate([wt, wt[:, 1:2], wt[:, 3:4], jnp.zeros_like(wt[:, :2])], axis=1).reshape(8 * n_heads, d)
    bt = jnp.transpose(b_gate.reshape(4, n_heads), (1, 0)).astype(F32)
    bt = jnp.concatenate([bt, bt[:, 1:2], bt[:, 3:4], jnp.zeros_like(bt[:, :2])], axis=1).reshape(8 * n_heads, 1)
    return wt, bt


def _rope_tables(seq):
    nf = HEAD_DIM // 4
    t = jnp.arange(seq)
    inv = ROPE_THETA ** (-jnp.arange(nf, dtype=F32) / nf)
    pos = jnp.stack([t // GRID_W, t % GRID_W], axis=-1).astype(F32)
    ang = pos[:, :, None] * inv
    cos = jnp.cos(ang)
    sin = jnp.sin(ang)
    cos_full = jnp.stack([cos, cos], axis=2).reshape(seq, HEAD_DIM)
    sin_full = jnp.stack([-sin, sin], axis=2).reshape(seq, HEAD_DIM)
    return cos_full, sin_full


def kernel(x_prompt, x_sample, state_mlstm_C, state_mlstm_n, state_mlstm_m, cache_gqa_k, cache_gqa_v,
           cache_na_k, cache_na_v, c, c_ctx, norm1_g, norm2_g, w_ada, b_ada, w_mlp1, w_mlp2,
           mlstm_w_in, mlstm_w_gate, mlstm_b_gate, mlstm_head_g, mlstm_w_out,
           gqa_w_qkv, gqa_q_g, gqa_k_g, gqa_w_o, na_w_qkv, na_q_g, na_k_g, na_rpb, na_w_o):
    bp, tp, d = x_prompt.shape
    bs, ts, _ = x_sample.shape
    depth = w_ada.shape[0]
    past = cache_gqa_k.shape[2]
    mh = mlstm_w_gate.shape[-1] // 4
    gqa_kv = cache_gqa_k.shape[3]
    gqa_q = gqa_w_o.shape[1] // HEAD_DIM
    na_h = cache_na_k.shape[3]
    assert bs + 1 <= 8

    cond8 = jnp.zeros((8, d), F32).at[:bs].set(c).at[bs].set(c_ctx)
    mod = adaln_all(cond8, w_ada, b_ada).reshape(depth * 48, 1, d)
    g1 = norm1_g.reshape(depth, 1, d)
    g2 = norm2_g.reshape(depth, 1, d)
    cos, sin = _rope_tables(ts)

    groups = [
        dict(x=x_prompt.reshape(bp * tp, d), batch=bp, seq=tp, rpb_=bp * tp, row0=bs, ctx=True),
        dict(x=x_sample.reshape(bs * ts, d), batch=bs, seq=ts, rpb_=ts, row0=0, ctx=False),
    ]
    for grp in groups:
        grp["xn"] = modulate(grp["x"], g1, mod, layer=0, which_shift=0, which_scale=1,
                             rows_per_batch=grp["rpb_"], row0=grp["row0"])
    C_l, n_l, m_l, gk_l, gv_l, nk_l, nv_l = [], [], [], [], [], [], []

    for i in range(depth):
        kind, j = i % N_MIXERS, i // N_MIXERS
        last = i == depth - 1
        nxt = i if last else i + 1
        w1 = w_mlp1[i].astype(BF16)
        w2 = w_mlp2[i].astype(BF16)
        if kind == 0:
            w_in = mlstm_w_in[j].astype(BF16)
            w_o = mlstm_w_out[j].astype(BF16)
            wg_t, bg_t = _gate_weights_head_major(mlstm_w_gate[j], mlstm_b_gate[j], mh)
        elif kind == 1:
            w_in = gqa_w_qkv[j].astype(BF16)
            w_o = gqa_w_o[j].astype(BF16)
        else:
            w_in = na_w_qkv[j].astype(BF16)
            w_o = na_w_o[j].astype(BF16)

        for grp in groups:
            x, xn, batch, seq, ctx = grp["x"], grp["xn"], grp["batch"], grp["seq"], grp["ctx"]
            mk = dict(rows_per_batch=grp["rpb_"], row0=grp["row0"])
            proj = matmul(xn, w_in)
            if kind == 0:
                gates = mlstm_gates(xn, wg_t, bg_t)
                if ctx:
                    a, (cf, nf, mf) = mlstm_scan(proj, gates, mlstm_head_g[j], None, batch=batch, seq=seq,
                                                 n_heads=mh, write_state=True, heads=4)
                    C_l.append(cf)
                    n_l.append(nf)
                    m_l.append(mf)
                else:
                    st = (state_mlstm_C[:, j], state_mlstm_n[:, j], state_mlstm_m[:, j])
                    a, _ = mlstm_scan(proj, gates, mlstm_head_g[j], st, batch=batch, seq=seq,
                                      n_heads=mh, write_state=False)
            elif kind == 1:
                if ctx:
                    a, kn, v = ctx_attention(proj, gqa_q_g[j], gqa_k_g[j], batch=batch, seq=seq, n_q=gqa_q, n_kv=gqa_kv)
                    gk_l.append(kn.reshape(batch, seq, gqa_kv, HEAD_DIM))
                    gv_l.append(v.reshape(batch, seq, gqa_kv, HEAD_DIM))
                else:
                    ck = cache_gqa_k[:, j].astype(F32).reshape(batch * past, gqa_kv * HEAD_DIM)
                    cv = cache_gqa_v[:, j].astype(F32).reshape(batch * past, gqa_kv * HEAD_DIM)
                    a = gqa_attention(proj, ck, cv, gqa_q_g[j], gqa_k_g[j], cos, sin, batch=batch, seq=seq,
                                      past=past, n_q=gqa_q, n_kv=gqa_kv)
            else:
                if ctx:
                    a, kn, v = ctx_attention(proj, na_q_g[j], na_k_g[j], batch=batch, seq=seq, n_q=na_h, n_kv=na_h)
                    nk_l.append(kn.reshape(batch, seq, na_h, HEAD_DIM))
                    nv_l.append(v.reshape(batch, seq, na_h, HEAD_DIM))
                else:
                    ck = cache_na_k[:, j].astype(F32).reshape(batch * past, na_h * HEAD_DIM)
                    cv = cache_na_v[:, j].astype(F32).reshape(batch * past, na_h * HEAD_DIM)
                    a = na_attention(proj, ck, cv, na_q_g[j], na_k_g[j], na_rpb[j], batch=batch, seq=seq,
                                     past=past, n_heads=na_h)
            x, xn = mm_res_norm(a, w_o, x, mod, g2, layer=i, which_gate=2, next_layer=i, next_shift=3,
                                next_scale=4, **mk)
            x, xn = fused_mlp(xn, w1, w2, x, mod, g1, layer=i, next_layer=nxt, emit_next=not last, **mk)
            grp["x"], grp["xn"] = x, xn

    y_prompt = groups[0]["x"].reshape(bp, tp, d)
    y_sample = groups[1]["x"].reshape(bs, ts, d)
    return (y_prompt, y_sample, jnp.stack(C_l, axis=1), jnp.stack(n_l, axis=1), jnp.stack(m_l, axis=1),
            jnp.stack(gk_l, axis=1), jnp.stack(gv_l, axis=1), jnp.stack(nk_l, axis=1), jnp.stack(nv_l, axis=1))
```

```python
import functools
import math

import jax
import jax.numpy as jnp
from jax import lax
from jax.experimental import pallas as pl
from jax.experimental.pallas import tpu as pltpu

EPS = 1e-6
HEAD_DIM = 128
MLSTM_DK = 128
MLSTM_DV = 256
MLSTM_CHUNK = 128
GATE_SOFTCAP = 15.0
GRID_W = 64
WIN_ROWS = 8
WIN_COLS = 16
ROPE_THETA = 10000.0
N_MIXERS = 3
LOG2E = math.log2(math.e)

VMEM_LIMIT_BYTES = 56 * 1024 * 1024
NORM_ROW_CHUNK = 16

F32 = jnp.float32
BF16 = jnp.bfloat16
NEG_INF = float("-inf")


def _params(sem):
    return pltpu.CompilerParams(dimension_semantics=sem, vmem_limit_bytes=VMEM_LIMIT_BYTES)


def _nt(a, b):
    return lax.dot_general(a, b, (((1,), (1,)), ((), ())), preferred_element_type=F32)


def _mm(a, b):
    return jnp.dot(a, b, preferred_element_type=F32)


def _split2(x):
    hi = x.astype(BF16)
    lo = (x - hi.astype(F32)).astype(BF16)
    return hi, lo


def _split3(x):
    hi = x.astype(BF16)
    r = x - hi.astype(F32)
    mid = r.astype(BF16)
    lo = (r - mid.astype(F32)).astype(BF16)
    return hi, mid, lo


def _mod_row(layer, which, rows_per_batch, tm, row0):
    def f(i, *_):
        b = row0 + (i * tm) // rows_per_batch
        return (layer * 48 + b * 6 + which, 0, 0)
    return f


def _adaln_kernel(cond_ref, w_ref, b_ref, out_ref):
    c = cond_ref[...]
    s = c * jax.nn.sigmoid(c)
    sh, sl = _split2(s)
    wh, wl = _split2(w_ref[...])
    lhs = jnp.concatenate([sh, sl], axis=0)
    r = _mm(lhs, wh)
    out_ref[...] = r[:8] + r[8:] + _mm(sh, wl) + b_ref[...]


def adaln_all(cond8, w_ada, b_ada, tn=1024):
    nl, d, n = w_ada.shape
    return pl.pallas_call(
        _adaln_kernel,
        out_shape=jax.ShapeDtypeStruct((nl, 8, n), F32),
        grid=(nl, n // tn),
        in_specs=[
            pl.BlockSpec((8, d), lambda l, j: (0, 0)),
            pl.BlockSpec((None, d, tn), lambda l, j: (l, 0, j)),
            pl.BlockSpec((None, 1, tn), lambda l, j: (l, 0, j)),
        ],
        out_specs=pl.BlockSpec((None, 8, tn), lambda l, j: (l, 0, j)),
        compiler_params=_params(("parallel", "parallel")),
        name="adaln",
    )(cond8, w_ada, b_ada.reshape(nl, 1, n))


def _modulate_rows(src_ref, dst_ref, g, mul, sh, r0, n_rows):
    for c in range(n_rows // NORM_ROW_CHUNK):
        start = r0 + c * NORM_ROW_CHUNK
        if not isinstance(start, int):
            start = pl.multiple_of(start, NORM_ROW_CHUNK)
        rows = pl.ds(start, NORM_ROW_CHUNK)
        xf = src_ref[rows, :]
        y = xf * lax.rsqrt(jnp.mean(xf * xf, axis=-1, keepdims=True) + EPS)
        dst_ref[rows, :] = ((y * g) * mul + sh).astype(dst_ref.dtype)


def _modulate_kernel(x_ref, g_ref, sh_ref, sc_ref, out_ref, *, tm):
    g = g_ref[...]
    mul = 1.0 + sc_ref[...]
    sh = sh_ref[...]
    unroll = 4 * NORM_ROW_CHUNK

    def body(i, _):
        _modulate_rows(x_ref, out_ref, g, mul, sh, pl.multiple_of(i * unroll, unroll), unroll)
        return 0

    lax.fori_loop(0, tm // unroll, body, 0)


def modulate(x, g, mod, *, layer, which_shift, which_scale, rows_per_batch, row0, tm=256):
    m, d = x.shape
    tm = min(tm, rows_per_batch)
    return pl.pallas_call(
        functools.partial(_modulate_kernel, tm=tm),
        out_shape=jax.ShapeDtypeStruct((m, d), BF16),
        grid=(m // tm,),
        in_specs=[
            pl.BlockSpec((tm, d), lambda i: (i, 0)),
            pl.BlockSpec((None, 1, d), lambda i: (layer, 0, 0)),
            pl.BlockSpec((None, 1, d), _mod_row(layer, which_shift, rows_per_batch, tm, row0)),
            pl.BlockSpec((None, 1, d), _mod_row(layer, which_scale, rows_per_batch, tm, row0)),
        ],
        out_specs=pl.BlockSpec((tm, d), lambda i: (i, 0)),
        compiler_params=_params(("parallel",)),
        name="modulate",
    )(x, g, mod, mod)


def _mm_kernel(a_ref, w_ref, out_ref):
    out_ref[...] = _mm(a_ref[...], w_ref[...]).astype(out_ref.dtype)


def matmul(a, w, *, out_dtype=F32, tm=1024, tn=1024, name="proj"):
    m, k = a.shape
    n = w.shape[1]
    return pl.pallas_call(
        _mm_kernel,
        out_shape=jax.ShapeDtypeStruct((m, n), out_dtype),
        grid=(m // tm, n // tn),
        in_specs=[
            pl.BlockSpec((tm, k), lambda i, j: (i, 0)),
            pl.BlockSpec((k, tn), lambda i, j: (0, j)),
        ],
        out_specs=pl.BlockSpec((tm, tn), lambda i, j: (i, j)),
        compiler_params=_params(("parallel", "arbitrary")),
        name=name,
    )(a, w)


def _mmres_kernel(a_ref, w_ref, res_ref, gate_ref, g_ref, sh_ref, sc_ref, x_ref, xn_ref, *, tm, sub):
    gate = gate_ref[...]
    g = g_ref[...]
    mul = 1.0 + sc_ref[...]
    sh = sh_ref[...]
    for s in range(tm // sub):
        rows = pl.ds(s * sub, sub)
        x_ref[rows, :] = res_ref[rows, :] + gate * _mm(a_ref[rows, :], w_ref[...])
        _modulate_rows(x_ref, xn_ref, g, mul, sh, s * sub, sub)


def mm_res_norm(a, w, res, mod, g_next, *, layer, which_gate, next_layer, next_shift, next_scale,
                rows_per_batch, row0, tm=512, sub=128):
    m, k = a.shape
    d = w.shape[1]
    tm = min(tm, rows_per_batch)
    mk = (rows_per_batch, tm, row0)
    return pl.pallas_call(
        functools.partial(_mmres_kernel, tm=tm, sub=sub),
        out_shape=(jax.ShapeDtypeStruct((m, d), F32), jax.ShapeDtypeStruct((m, d), BF16)),
        grid=(m // tm,),
        in_specs=[
            pl.BlockSpec((tm, k), lambda i: (i, 0)),
            pl.BlockSpec((k, d), lambda i: (0, 0), pipeline_mode=pl.Buffered(1)),
            pl.BlockSpec((tm, d), lambda i: (i, 0)),
            pl.BlockSpec((None, 1, d), _mod_row(layer, which_gate, *mk)),
            pl.BlockSpec((None, 1, d), lambda i: (next_layer, 0, 0)),
            pl.BlockSpec((None, 1, d), _mod_row(next_layer, next_shift, *mk)),
            pl.BlockSpec((None, 1, d), _mod_row(next_layer, next_scale, *mk)),
        ],
        out_specs=(pl.BlockSpec((tm, d), lambda i: (i, 0)), pl.BlockSpec((tm, d), lambda i: (i, 0))),
        compiler_params=_params(("parallel",)),
        name="outproj_res_norm",
    )(a, w, res, mod, g_next, mod, mod)


def _mlp_kernel(xn_ref, w1_ref, w2_ref, res_ref, gate_ref, g_ref, sh_ref, sc_ref, x_ref, *rest, tm, emit_next):
    xno_ref = rest[0] if emit_next else None
    acc_ref = rest[-1]
    f = pl.program_id(1)

    @pl.when(f == 0)
    def _():
        acc_ref[...] = jnp.zeros_like(acc_ref)

    h = _mm(xn_ref[...], w1_ref[...])
    h = jnp.square(jnp.maximum(h, 0.0)).astype(BF16)
    acc_ref[...] += _mm(h, w2_ref[...])

    @pl.when(f == pl.num_programs(1) - 1)
    def _():
        gate = gate_ref[...]
        g = g_ref[...]
        mul = 1.0 + sc_ref[...]
        sh = sh_ref[...]
        unroll = 4 * NORM_ROW_CHUNK

        def body(i, _):
            r0 = pl.multiple_of(i * unroll, unroll)
            rows = pl.ds(r0, unroll)
            x_ref[rows, :] = res_ref[rows, :] + gate * acc_ref[rows, :]
            if emit_next:
                _modulate_rows(x_ref, xno_ref, g, mul, sh, r0, unroll)
            return 0

        lax.fori_loop(0, tm // unroll, body, 0)


def fused_mlp(xn, w1, w2, res, mod, g_next, *, layer, next_layer, rows_per_batch, row0, emit_next,
              tm=512, fc=1024):
    m, d = xn.shape
    ff = w1.shape[1]
    tm = min(tm, rows_per_batch)
    mk = (rows_per_batch, tm, row0)
    row_tile = pl.BlockSpec((tm, d), lambda i, f: (i, 0))
    out_shape = [jax.ShapeDtypeStruct((m, d), F32)] + ([jax.ShapeDtypeStruct((m, d), BF16)] if emit_next else [])
    outs = pl.pallas_call(
        functools.partial(_mlp_kernel, tm=tm, emit_next=emit_next),
        out_shape=tuple(out_shape),
        grid=(m // tm, ff // fc),
        in_specs=[
            pl.BlockSpec((tm, d), lambda i, f: (i, 0)),
            pl.BlockSpec((d, fc), lambda i, f: (0, f)),
            pl.BlockSpec((fc, d), lambda i, f: (f, 0)),
            row_tile,
            pl.BlockSpec((None, 1, d), _mod_row(layer, 5, *mk)),
            pl.BlockSpec((None, 1, d), lambda i, f: (next_layer, 0, 0)),
            pl.BlockSpec((None, 1, d), _mod_row(next_layer, 0, *mk)),
            pl.BlockSpec((None, 1, d), _mod_row(next_layer, 1, *mk)),
        ],
        out_specs=tuple(row_tile for _ in out_shape),
        scratch_shapes=[pltpu.VMEM((tm, d), F32)],
        compiler_params=_params(("parallel", "arbitrary")),
        name="fused_mlp",
    )(xn, w1, w2, res, mod, g_next, mod, mod)
    return (outs[0], outs[1]) if emit_next else (outs[0], None)


def _head_norm(x, g):
    return x * lax.rsqrt(jnp.mean(x * x, axis=-1, keepdims=True) + EPS) * g


_Q_SCALE = HEAD_DIM ** -0.5 * LOG2E


def _with_ones_column(v):
    lane = lax.broadcasted_iota(jnp.int32, v.shape, 1)
    return jnp.concatenate([v.astype(BF16), jnp.where(lane == 0, 1.0, 0.0).astype(BF16)], axis=1)


def _softmax2_pv(s, vaug):
    p = jnp.exp2(s - jnp.max(s, axis=-1, keepdims=True)).astype(BF16)
    pv = _mm(p, vaug)
    return pv[:, :HEAD_DIM] / pv[:, HEAD_DIM:HEAD_DIM + 1]


def _rope(x, cos, sin_signed, lane_is_first):
    swapped = jnp.where(lane_is_first, pltpu.roll(x, 96, 1), pltpu.roll(x, 32, 1))
    return x * cos + swapped * sin_signed


def _ctx_attn_kernel(q_ref, k_ref, v_ref, qg_ref, kg_ref, o_ref, kn_ref, vo_ref, *, groups, heads):
    hd = HEAD_DIM
    qg = qg_ref[...] * _Q_SCALE
    for h in range(heads):
        hs = slice(h * hd, (h + 1) * hd)
        kn = _head_norm(k_ref[:, hs], kg_ref[...])
        kn_ref[:, hs] = kn
        v = v_ref[:, hs]
        vo_ref[:, hs] = v
        kb = kn.astype(BF16)
        vb = v.astype(BF16)
        for g in range(groups):
            qs = slice((h * groups + g) * hd, (h * groups + g + 1) * hd)
            qn = _head_norm(q_ref[:, qs], qg).astype(BF16)
            s = _nt(qn, kb)
            p = jnp.exp2(s - jnp.max(s, axis=-1, keepdims=True))
            o_ref[:, qs] = (_mm(p.astype(BF16), vb) / jnp.sum(p, axis=-1, keepdims=True)).astype(o_ref.dtype)


def ctx_attention(qkv, q_g, k_g, *, batch, seq, n_q, n_kv, heads=4):
    m = qkv.shape[0]
    groups = n_q // n_kv
    hd = HEAD_DIM
    kw = heads * hd
    qw = heads * groups * hd
    assert (n_q * hd) % kw == 0 and ((n_q + n_kv) * hd) % kw == 0
    k_blk0 = n_q * hd // kw
    v_blk0 = (n_q + n_kv) * hd // kw
    kern = functools.partial(_ctx_attn_kernel, groups=groups, heads=heads)
    return pl.pallas_call(
        kern,
        out_shape=(jax.ShapeDtypeStruct((m, n_q * hd), BF16),
                   jax.ShapeDtypeStruct((m, n_kv * hd), F32),
                   jax.ShapeDtypeStruct((m, n_kv * hd), F32)),
        grid=(batch, n_kv // heads),
        in_specs=[
            pl.BlockSpec((seq, qw), lambda b, h: (b, h)),
            pl.BlockSpec((seq, kw), lambda b, h: (b, k_blk0 + h)),
            pl.BlockSpec((seq, kw), lambda b, h: (b, v_blk0 + h)),
            pl.BlockSpec((1, hd), lambda b, h: (0, 0)),
            pl.BlockSpec((1, hd), lambda b, h: (0, 0)),
        ],
        out_specs=(pl.BlockSpec((seq, qw), lambda b, h: (b, h)),
                   pl.BlockSpec((seq, kw), lambda b, h: (b, h)),
                   pl.BlockSpec((seq, kw), lambda b, h: (b, h))),
        compiler_params=_params(("parallel", "parallel")),
        name="ctx_attention",
    )(qkv, qkv, qkv, q_g.reshape(1, hd), k_g.reshape(1, hd))


def _gqa_kernel(q_ref, k_ref, v_ref, ck_ref, cv_ref, qg_ref, kg_ref, cosq_ref, sinq_ref, cosk_ref, sink_ref,
                o_ref, kall_ref, vall_ref, *, groups, seq, past):
    lane = lax.broadcasted_iota(jnp.int32, (1, HEAD_DIM), 1)
    first = (lane % 64) < 32

    @pl.when(pl.program_id(2) == 0)
    def _():
        kn = _head_norm(k_ref[...], kg_ref[...])
        kall_ref[0:seq, :] = _rope(kn, cosk_ref[...], sink_ref[...], first).astype(BF16)
        kall_ref[seq:seq + past, :] = ck_ref[...].astype(BF16)
        vall_ref[0:seq, :] = _with_ones_column(v_ref[...])
        vall_ref[seq:seq + past, :] = _with_ones_column(cv_ref[...])

    cos = cosq_ref[...]
    sin = sinq_ref[...]
    qg = qg_ref[...] * _Q_SCALE
    for g in range(groups):
        qn = _head_norm(q_ref[:, g * HEAD_DIM:(g + 1) * HEAD_DIM], qg)
        qr = _rope(qn, cos, sin, first).astype(BF16)
        o_ref[:, g * HEAD_DIM:(g + 1) * HEAD_DIM] = _softmax2_pv(_nt(qr, kall_ref[...]), vall_ref[...]).astype(o_ref.dtype)


def gqa_attention(qkv, cache_k, cache_v, q_g, k_g, cos, sin, *, batch, seq, past, n_q, n_kv, tq=256):
    m = qkv.shape[0]
    groups = n_q // n_kv
    hd = HEAD_DIM
    nqb = seq // tq
    kern = functools.partial(_gqa_kernel, groups=groups, seq=seq, past=past)
    return pl.pallas_call(
        kern,
        out_shape=jax.ShapeDtypeStruct((m, n_q * hd), BF16),
        grid=(batch, n_kv, nqb),
        in_specs=[
            pl.BlockSpec((tq, groups * hd), lambda b, h, i: (b * nqb + i, h)),
            pl.BlockSpec((seq, hd), lambda b, h, i: (b, n_q + h)),
            pl.BlockSpec((seq, hd), lambda b, h, i: (b, n_q + n_kv + h)),
            pl.BlockSpec((past, hd), lambda b, h, i: (b, h)),
            pl.BlockSpec((past, hd), lambda b, h, i: (b, h)),
            pl.BlockSpec((1, hd), lambda b, h, i: (0, 0)),
            pl.BlockSpec((1, hd), lambda b, h, i: (0, 0)),
            pl.BlockSpec((tq, hd), lambda b, h, i: (i, 0)),
            pl.BlockSpec((tq, hd), lambda b, h, i: (i, 0)),
            pl.BlockSpec((seq, hd), lambda b, h, i: (0, 0)),
            pl.BlockSpec((seq, hd), lambda b, h, i: (0, 0)),
        ],
        out_specs=pl.BlockSpec((tq, groups * hd), lambda b, h, i: (b * nqb + i, h)),
        scratch_shapes=[pltpu.VMEM((seq + past, hd), BF16), pltpu.VMEM((seq + past, 2 * hd), BF16)],
        compiler_params=_params(("parallel", "parallel", "arbitrary")),
        name="gqa_attention",
    )(qkv, qkv, qkv, cache_k, cache_v, q_g.reshape(1, hd), k_g.reshape(1, hd), cos, sin, cos, sin)


def _na_kernel(rpb_ref, q_ref, k_ref, v_ref, ck_ref, cv_ref, qg_ref, kg_ref, o_ref,
               qn_ref, kn_ref, vb_ref, ckb_ref, cvb_ref, tile_ref, pair_ref, *, rows, wr, n_dr, n_dc, unroll):
    h = pl.program_id(0)
    w = GRID_W

    qn_ref[...] = _head_norm(q_ref[...], qg_ref[...] * _Q_SCALE).astype(BF16)
    kn_ref[...] = _head_norm(k_ref[...], kg_ref[...]).astype(BF16)
    vb_ref[...] = _with_ones_column(v_ref[...])
    ckb_ref[...] = ck_ref[...].astype(BF16)
    cvb_ref[...] = _with_ones_column(cv_ref[...])

    @pl.when(pl.program_id(1) == 0)
    def _():
        qc = lax.broadcasted_iota(jnp.int32, (w, 2 * w), 0)
        lane = lax.broadcasted_iota(jnp.int32, (w, 2 * w), 1)
        kc = lane % w
        cs = jnp.clip(qc - WIN_COLS // 2, 0, w - WIN_COLS)
        col_ok = (kc >= cs) & (kc < cs + WIN_COLS)
        dc = jnp.clip(kc - qc + WIN_COLS - 1, 0, n_dc - 1)
        tiles = [jnp.zeros((w, 2 * w), F32) for _ in range(n_dr)]
        for d in range(n_dc):
            sel = dc == d
            for dr in range(n_dr):
                tiles[dr] = jnp.where(sel, rpb_ref[h, dr * n_dc + d], tiles[dr])
        for dr in range(n_dr):
            tile_ref[dr] = jnp.where(col_ok, tiles[dr] * LOG2E, NEG_INF)
        for dr in range(n_dr - 1):
            pair_ref[dr] = jnp.where(lane < w, tile_ref[dr], tile_ref[dr + 1])

    zero_bias = jnp.zeros((w, ckb_ref.shape[0]), F32)

    def body(i, _):
        rws = [i * unroll + u for u in range(unroll)]
        rss = [jnp.clip(r - wr // 2, 0, rows - wr) for r in rws]
        k0s = [pl.multiple_of(rs * w, w) for rs in rss]
        scores = []
        for r, rs, k0 in zip(rws, rss, k0s):
            dr0 = rs - r + WIN_ROWS - 1
            q_r = qn_ref[pl.ds(pl.multiple_of(r * w, w), w), :]
            kcat = jnp.concatenate([kn_ref[pl.ds(k0, wr * w), :], ckb_ref[...]], axis=0)
            bias = jnp.concatenate([pair_ref[dr0 + 2 * j] for j in range(wr // 2)] + [zero_bias], axis=1)
            scores.append(_nt(q_r, kcat) + bias)
        probs = [jnp.exp2(s - jnp.max(s, axis=-1, keepdims=True)).astype(BF16) for s in scores]
        for r, k0, p in zip(rws, k0s, probs):
            vcat = jnp.concatenate([vb_ref[pl.ds(k0, wr * w), :], cvb_ref[...]], axis=0)
            pv = _mm(p, vcat)
            o = pv[:, :HEAD_DIM] / pv[:, HEAD_DIM:HEAD_DIM + 1]
            o_ref[pl.ds(pl.multiple_of(r * w, w), w), :] = o.astype(o_ref.dtype)
        return 0

    lax.fori_loop(0, rows // unroll, body, 0)


def na_attention(qkv, cache_k, cache_v, q_g, k_g, rpb, *, batch, seq, past, n_heads, unroll=4):
    m = qkv.shape[0]
    hd = HEAD_DIM
    rows = seq // GRID_W
    wr = min(WIN_ROWS, rows)
    n_dr, n_dc = rpb.shape[1], rpb.shape[2]
    assert wr == WIN_ROWS and wr % 2 == 0 and rows % unroll == 0
    kern = functools.partial(_na_kernel, rows=rows, wr=wr, n_dr=n_dr, n_dc=n_dc, unroll=unroll)
    return pl.pallas_call(
        kern,
        out_shape=jax.ShapeDtypeStruct((m, n_heads * hd), BF16),
        grid=(n_heads, batch),
        in_specs=[
            pl.BlockSpec(memory_space=pltpu.SMEM),
            pl.BlockSpec((seq, hd), lambda h, b: (b, h)),
            pl.BlockSpec((seq, hd), lambda h, b: (b, n_heads + h)),
            pl.BlockSpec((seq, hd), lambda h, b: (b, 2 * n_heads + h)),
            pl.BlockSpec((past, hd), lambda h, b: (b, h)),
            pl.BlockSpec((past, hd), lambda h, b: (b, h)),
            pl.BlockSpec((1, hd), lambda h, b: (0, 0)),
            pl.BlockSpec((1, hd), lambda h, b: (0, 0)),
        ],
        out_specs=pl.BlockSpec((seq, hd), lambda h, b: (b, h)),
        scratch_shapes=[
            pltpu.VMEM((seq, hd), BF16), pltpu.VMEM((seq, hd), BF16), pltpu.VMEM((seq, 2 * hd), BF16),
            pltpu.VMEM((past, hd), BF16), pltpu.VMEM((past, 2 * hd), BF16),
            pltpu.VMEM((n_dr, GRID_W, 2 * GRID_W), F32),
            pltpu.VMEM((n_dr - 1, GRID_W, 2 * GRID_W), F32),
        ],
        compiler_params=_params(("parallel", "arbitrary")),
        name="na_attention",
    )(rpb.reshape(n_heads, n_dr * n_dc), qkv, qkv, qkv, cache_k, cache_v,
      q_g.reshape(1, hd), k_g.reshape(1, hd))


def _gates_kernel(xn_ref, w_ref, b_ref, out_ref, *, tm):
    L = MLSTM_CHUNK
    wh, wl = _split2(w_ref[...])
    xn = xn_ref[...]
    pre = _nt(wh, xn) + _nt(wl, xn) + b_ref[...]
    capped = GATE_SOFTCAP * jnp.tanh(pre / GATE_SOFTCAP)
    row = lax.broadcasted_iota(jnp.int32, (capped.shape[0], L), 0) % 8
    is_input = (row == 0) | (row == 2)
    ri = lax.broadcasted_iota(jnp.int32, (L, L), 0)
    ci = lax.broadcasted_iota(jnp.int32, (L, L), 1)
    upper = jnp.where(ri <= ci, 1.0, 0.0).astype(BF16)
    lower = jnp.where(ri >= ci, 1.0, 0.0).astype(BF16)
    for c in range(tm // L):
        cap = capped[:, c * L:(c + 1) * L]
        gates = jnp.where(is_input, cap, jax.nn.log_sigmoid(cap))
        pieces = _split3(gates)
        prefix = sum(_mm(p, upper) for p in pieces)
        suffix = sum(_mm(p, lower) for p in pieces)
        out_ref[c] = jnp.where(row == 4, prefix, jnp.where(row == 5, suffix, gates))


def mlstm_gates(xn, w_gate_t, b_gate_t, *, tm=512):
    m, d = xn.shape
    gh = w_gate_t.shape[0]
    return pl.pallas_call(
        functools.partial(_gates_kernel, tm=tm),
        out_shape=jax.ShapeDtypeStruct((m // MLSTM_CHUNK, gh, MLSTM_CHUNK), F32),
        grid=(m // tm,),
        in_specs=[
            pl.BlockSpec((tm, d), lambda i: (i, 0)),
            pl.BlockSpec((gh, d), lambda i: (0, 0)),
            pl.BlockSpec((gh, 1), lambda i: (0, 0)),
        ],
        out_specs=pl.BlockSpec((tm // MLSTM_CHUNK, gh, MLSTM_CHUNK), lambda i: (i, 0, 0)),
        compiler_params=_params(("parallel",)),
        name="mlstm_gates",
    )(xn, w_gate_t, b_gate_t)


def _mlstm_kernel(*refs, n_chunks, heads, zero_init, write_state, has_prev_states):
    it = iter(refs)
    q_ref, k_ref, v_ref, o_ref, gates_ref, hg_ref = (next(it) for _ in range(6))
    if not zero_init:
        c0_ref, n0_ref, m0_ref = (next(it) for _ in range(3))
    if has_prev_states:
        next(it)
    y_ref = next(it)
    if write_state:
        cf_ref, nf_ref, mf_ref = (next(it) for _ in range(3))
    hdir_ref, cst_ref = next(it), next(it)

    L, dk, dv = MLSTM_CHUNK, MLSTM_DK, MLSTM_DV
    ri = lax.broadcasted_iota(jnp.int32, (L, L), 0)
    ci = lax.broadcasted_iota(jnp.int32, (L, L), 1)
    masks = (ri >= ci, ri <= ci)
    eye = ri == ci
    eye_b = jnp.where(eye, 1.0, 0.0).astype(BF16)
    ones_blk = jnp.ones((L, 128), BF16)
    qscale = dk ** -0.5

    m_init = []
    for hh in range(heads):
        for direction in range(2):
            idx = hh * 2 + direction
            if zero_init:
                cst_ref[idx] = jnp.zeros((dk, dv + 128), F32)
                m_init.append(jnp.zeros((1, 1), F32))
            else:
                n_row = jnp.broadcast_to(n0_ref[direction, hh], (dk, dk))
                n_col = jnp.sum(jnp.where(eye, n_row, 0.0), axis=-1, keepdims=True)
                cst_ref[idx, :, 0:dv] = c0_ref[direction, hh]
                cst_ref[idx, :, dv:dv + 128] = jnp.broadcast_to(n_col, (dk, 128))
                m_init.append(m0_ref[direction, hh])

    def body(step, ms):
        chains = [(hh, direction) for hh in range(heads) for direction in range(2)]
        r0s = [pl.multiple_of((step if d == 0 else n_chunks - 1 - step) * L, L) for _, d in chains]
        cs = [step if d == 0 else n_chunks - 1 - step for _, d in chains]

        st1 = []
        for idx, (hh, d) in enumerate(chains):
            gt = gates_ref[cs[idx], hh * 8:(hh + 1) * 8, :]
            li, lf, brow = gt[2 * d:2 * d + 1], gt[2 * d + 1:2 * d + 2], gt[4 + d:5 + d]
            b3 = _nt(eye_b, jnp.concatenate(_split3(jnp.broadcast_to(brow, (L, L))), axis=0))
            bmat = b3[:, :L] + b3[:, L:2 * L] + b3[:, 2 * L:]
            log_d = jnp.where(masks[d], bmat - brow + li, NEG_INF)
            inter = bmat + ms[idx]
            m_j = jnp.maximum(inter, jnp.max(log_d, axis=-1, keepdims=True))
            tot = jnp.sum(lf, axis=-1, keepdims=True)
            log_w = tot - brow + li
            m_new = jnp.maximum(tot + ms[idx], jnp.max(log_w, axis=-1, keepdims=True))
            st1.append((log_d, inter, m_j, tot, log_w, m_new))

        st2 = []
        for idx, (hh, d) in enumerate(chains):
            log_d, inter, m_j, tot, log_w, m_new = st1[idx]
            q = (q_ref[pl.ds(r0s[idx], L), hh * dk:(hh + 1) * dk] * qscale).astype(BF16)
            k = k_ref[pl.ds(r0s[idx], L), hh * dk:(hh + 1) * dk]
            k_hi, k_lo = _split2(k)
            qk = _nt(q, k_hi)
            kt2 = _nt(eye_b, jnp.concatenate([k_hi, k_lo], axis=0))
            kt = kt2[:, :L] + kt2[:, L:]
            qc = _mm(q, cst_ref[idx].astype(BF16))
            d_mat = jnp.exp(log_d - m_j)
            w_inter = jnp.exp(inter - m_j)
            wt = jnp.exp(log_w - m_new)
            decay = jnp.exp(tot + ms[idx] - m_new)
            ktw = (kt * wt).astype(BF16)
            st2.append((qk * d_mat, qc, w_inter, decay, ktw))

        for idx, (hh, d) in enumerate(chains):
            s, qc, w_inter, decay, ktw = st2[idx]
            floor = jnp.exp(-st1[idx][2])
            vaug = jnp.concatenate([v_ref[pl.ds(r0s[idx], L), hh * dv:(hh + 1) * dv].astype(BF16), ones_blk], axis=1)
            both = _mm(jnp.concatenate([s.astype(BF16), ktw], axis=0), vaug)
            nd = jnp.concatenate([w_inter] * 3, axis=1) * qc + both[:L]
            den = jnp.maximum(jnp.abs(nd[:, dv:dv + 128]), floor)
            h = nd[:, :dv] / jnp.concatenate([den, den], axis=1)
            hdir_ref[d, pl.ds(r0s[idx], L), hh * dv:(hh + 1) * dv] = h
            cst_ref[idx] = decay * cst_ref[idx] + both[L:]
        return tuple(s1[5] for s1 in st1)

    m_fin = lax.fori_loop(0, n_chunks, body, tuple(m_init))

    def finish(c, _):
        r0 = pl.multiple_of(c * L, L)
        for hh in range(heads):
            cols = slice(hh * dv, (hh + 1) * dv)
            hs = hdir_ref[0, pl.ds(r0, L), cols] + hdir_ref[1, pl.ds(r0, L), cols]
            hn = hs * lax.rsqrt(jnp.mean(hs * hs, axis=-1, keepdims=True) + EPS) * hg_ref[:, cols]
            y_ref[pl.ds(r0, L), cols] = (hn * jax.nn.sigmoid(o_ref[pl.ds(r0, L), cols])).astype(y_ref.dtype)
        return 0

    lax.fori_loop(0, n_chunks, finish, 0)

    if write_state:
        for hh in range(heads):
            for direction in range(2):
                idx = hh * 2 + direction
                cf_ref[direction, hh] = cst_ref[idx, :, 0:dv]
                n_b = cst_ref[idx, :, dv:dv + 128]
                nf_ref[direction, hh] = jnp.sum(jnp.where(eye, n_b, 0.0), axis=0, keepdims=True)
                mf_ref[direction, hh] = m_fin[idx]


def mlstm_scan(proj, gates, head_g, state=None, *, batch, seq, n_heads, write_state, heads=2,
               state_slot=(0, 1), prev_states=None):
    m = proj.shape[0]
    dk, dv, L = MLSTM_DK, MLSTM_DV, MLSTM_CHUNK
    H = n_heads
    nb = H // heads
    n_chunks = seq // L
    zero_init = state is None
    slot, n_slots = state_slot
    kern = functools.partial(_mlstm_kernel, n_chunks=n_chunks, heads=heads, zero_init=zero_init,
                             write_state=write_state, has_prev_states=prev_states is not None)
    in_specs = [
        pl.BlockSpec((seq, heads * dk), lambda b, h: (b, h)),
        pl.BlockSpec((seq, heads * dk), lambda b, h: (b, nb + h)),
        pl.BlockSpec((seq, heads * dv), lambda b, h: (b, nb + h)),
        pl.BlockSpec((seq, heads * dv), lambda b, h: (b, 2 * nb + h)),
        pl.BlockSpec((n_chunks, heads * 8, L), lambda b, h: (b, h, 0)),
        pl.BlockSpec((None, 1, heads * dv), lambda b, h: (h, 0, 0)),
    ]
    args = [proj, proj, proj, proj, gates, head_g.reshape(nb, 1, heads * dv)]
    st_specs = [pl.BlockSpec((None, 2, heads, dk, dv), lambda b, h: (b, 0, h, 0, 0)),
                pl.BlockSpec((None, 2, heads, 1, dk), lambda b, h: (b, 0, h, 0, 0)),
                pl.BlockSpec((None, 2, heads, 1, 1), lambda b, h: (b, 0, h, 0, 0))]
    if not zero_init:
        c0, n0, m0 = state
        in_specs += st_specs
        args += [c0.astype(F32), n0.astype(F32).reshape(batch, 2, H, 1, dk), m0.astype(F32).reshape(batch, 2, H, 1, 1)]
    aliases = {}
    if prev_states is not None:
        aliases = {len(args): 1}
        in_specs.append(pl.BlockSpec(memory_space=pl.ANY))
        args.append(prev_states)
    out_shape = [jax.ShapeDtypeStruct((m, H * dv), BF16)]
    out_specs = [pl.BlockSpec((seq, heads * dv), lambda b, h: (b, h))]
    if write_state:
        out_shape += [jax.ShapeDtypeStruct((batch, n_slots, 2, H, dk, dv), F32),
                      jax.ShapeDtypeStruct((batch, 2, H, 1, dk), F32),
                      jax.ShapeDtypeStruct((batch, 2, H, 1, 1), F32)]
        out_specs += [pl.BlockSpec((None, None, 2, heads, dk, dv), lambda b, h: (b, slot, 0, h, 0, 0))] + st_specs[1:]
    outs = pl.pallas_call(
        kern,
        out_shape=tuple(out_shape),
        grid=(batch, nb),
        in_specs=in_specs,
        out_specs=tuple(out_specs),
        scratch_shapes=[pltpu.VMEM((2, seq, heads * dv), F32), pltpu.VMEM((2 * heads, dk, dv + 128), F32)],
        input_output_aliases=aliases,
        compiler_params=_params(("parallel", "parallel")),
        name="mlstm_scan",
    )(*args)
    if write_state:
        y, cf, nf, mf = outs
        return y, (cf, nf.reshape(batch, 2, H, dk), mf.reshape(batch, 2, H))
    return outs[0], None


def _gate_weights_head_major(w_gate, b_gate, n_heads):
    d = w_gate.shape[0]
    wt = jnp.transpose(w_gate.reshape(d, 4, n_heads), (2, 1, 0)).astype(F32)
    wt = jnp.concatenate([wt, wt[:, 1:2], wt[:, 3:4], jnp.zeros_like(wt[:, :2])], axis=1).reshape(8 * n_heads, d)
    bt = jnp.transpose(b_gate.reshape(4, n_heads), (1, 0)).astype(F32)
    bt = jnp.concatenate([bt, bt[:, 1:2], bt[:, 3:4], jnp.zeros_like(bt[:, :2])], axis=1).reshape(8 * n_heads, 1)
    return wt, bt


def _rope_tables(seq):
    nf = HEAD_DIM // 4
    t = jnp.arange(seq)
    inv = ROPE_THETA ** (-jnp.arange(nf, dtype=F32) / nf)
    pos = jnp.stack([t // GRID_W, t % GRID_W], axis=-1).astype(F32)
    ang = pos[:, :, None] * inv
    cos = jnp.cos(ang)
    sin = jnp.sin(ang)
    cos_full = jnp.stack([cos, cos], axis=2).reshape(seq, HEAD_DIM)
    sin_full = jnp.stack([-sin, sin], axis=2).reshape(seq, HEAD_DIM)
    return cos_full, sin_full


def kernel(x_prompt, x_sample, state_mlstm_C, state_mlstm_n, state_mlstm_m, cache_gqa_k, cache_gqa_v,
           cache_na_k, cache_na_v, c, c_ctx, norm1_g, norm2_g, w_ada, b_ada, w_mlp1, w_mlp2,
           mlstm_w_in, mlstm_w_gate, mlstm_b_gate, mlstm_head_g, mlstm_w_out,
           gqa_w_qkv, gqa_q_g, gqa_k_g, gqa_w_o, na_w_qkv, na_q_g, na_k_g, na_rpb, na_w_o):
    bp, tp, d = x_prompt.shape
    bs, ts, _ = x_sample.shape
    depth = w_ada.shape[0]
    past = cache_gqa_k.shape[2]
    mh = mlstm_w_gate.shape[-1] // 4
    gqa_kv = cache_gqa_k.shape[3]
    gqa_q = gqa_w_o.shape[1] // HEAD_DIM
    na_h = cache_na_k.shape[3]
    assert bs + 1 <= 8

    cond8 = jnp.zeros((8, d), F32).at[:bs].set(c).at[bs].set(c_ctx)
    mod = adaln_all(cond8, w_ada, b_ada).reshape(depth * 48, 1, d)
    g1 = norm1_g.reshape(depth, 1, d)
    g2 = norm2_g.reshape(depth, 1, d)
    cos, sin = _rope_tables(ts)

    groups = [
        dict(x=x_prompt.reshape(bp * tp, d), batch=bp, seq=tp, rpb_=bp * tp, row0=bs, ctx=True),
        dict(x=x_sample.reshape(bs * ts, d), batch=bs, seq=ts, rpb_=ts, row0=0, ctx=False),
    ]
    for grp in groups:
        grp["xn"] = modulate(grp["x"], g1, mod, layer=0, which_shift=0, which_scale=1,
                             rows_per_batch=grp["rpb_"], row0=grp["row0"])
    n_mlstm = state_mlstm_C.shape[1]
    c_all = None
    n_l, m_l, gk_l, gv_l, nk_l, nv_l = [], [], [], [], [], []

    for i in range(depth):
        kind, j = i % N_MIXERS, i // N_MIXERS
        last = i == depth - 1
        nxt = i if last else i + 1
        w1 = w_mlp1[i].astype(BF16)
        w2 = w_mlp2[i].astype(BF16)
        if kind == 0:
            w_in = mlstm_w_in[j].astype(BF16)
            w_o = mlstm_w_out[j].astype(BF16)
            wg_t, bg_t = _gate_weights_head_major(mlstm_w_gate[j], mlstm_b_gate[j], mh)
        elif kind == 1:
            w_in = gqa_w_qkv[j].astype(BF16)
            w_o = gqa_w_o[j].astype(BF16)
        else:
            w_in = na_w_qkv[j].astype(BF16)
            w_o = na_w_o[j].astype(BF16)

        for grp in groups:
            x, xn, batch, seq, ctx = grp["x"], grp["xn"], grp["batch"], grp["seq"], grp["ctx"]
            mk = dict(rows_per_batch=grp["rpb_"], row0=grp["row0"])
            proj = matmul(xn, w_in)
            if kind == 0:
                gates = mlstm_gates(xn, wg_t, bg_t)
                if ctx:
                    a, (c_all, nf, mf) = mlstm_scan(proj, gates, mlstm_head_g[j], None, batch=batch, seq=seq,
                                                    n_heads=mh, write_state=True, heads=4,
                                                    state_slot=(j, n_mlstm), prev_states=c_all)
                    n_l.append(nf)
                    m_l.append(mf)
                else:
                    st = (state_mlstm_C[:, j], state_mlstm_n[:, j], state_mlstm_m[:, j])
                    a, _ = mlstm_scan(proj, gates, mlstm_head_g[j], st, batch=batch, seq=seq,
                                      n_heads=mh, write_state=False)
            elif kind == 1:
                if ctx:
                    a, kn, v = ctx_attention(proj, gqa_q_g[j], gqa_k_g[j], batch=batch, seq=seq, n_q=gqa_q, n_kv=gqa_kv)
                    gk_l.append(kn.reshape(batch, seq, gqa_kv, HEAD_DIM))
                    gv_l.append(v.reshape(batch, seq, gqa_kv, HEAD_DIM))
                else:
                    ck = cache_gqa_k[:, j].astype(F32).reshape(batch * past, gqa_kv * HEAD_DIM)
                    cv = cache_gqa_v[:, j].astype(F32).reshape(batch * past, gqa_kv * HEAD_DIM)
                    a = gqa_attention(proj, ck, cv, gqa_q_g[j], gqa_k_g[j], cos, sin, batch=batch, seq=seq,
                                      past=past, n_q=gqa_q, n_kv=gqa_kv)
            else:
                if ctx:
                    a, kn, v = ctx_attention(proj, na_q_g[j], na_k_g[j], batch=batch, seq=seq, n_q=na_h, n_kv=na_h)
                    nk_l.append(kn.reshape(batch, seq, na_h, HEAD_DIM))
                    nv_l.append(v.reshape(batch, seq, na_h, HEAD_DIM))
                else:
                    ck = cache_na_k[:, j].astype(F32).reshape(batch * past, na_h * HEAD_DIM)
                    cv = cache_na_v[:, j].astype(F32).reshape(batch * past, na_h * HEAD_DIM)
                    a = na_attention(proj, ck, cv, na_q_g[j], na_k_g[j], na_rpb[j], batch=batch, seq=seq,
                                     past=past, n_heads=na_h)
            x, xn = mm_res_norm(a, w_o, x, mod, g2, layer=i, which_gate=2, next_layer=i, next_shift=3,
                                next_scale=4, **mk)
            x, xn = fused_mlp(xn, w1, w2, x, mod, g1, layer=i, next_layer=nxt, emit_next=not last, **mk)
            grp["x"], grp["xn"] = x, xn

    y_prompt = groups[0]["x"].reshape(bp, tp, d)
    y_sample = groups[1]["x"].reshape(bs, ts, d)
    return (y_prompt, y_sample, c_all, jnp.stack(n_l, axis=1), jnp.stack(m_l, axis=1),
            jnp.stack(gk_l, axis=1), jnp.stack(gv_l, axis=1), jnp.stack(nk_l, axis=1), jnp.stack(nv_l, axis=1))
```

```python
import functools
import math

import jax
import jax.numpy as jnp
from jax import lax
from jax.experimental import pallas as pl
from jax.experimental.pallas import tpu as pltpu

EPS = 1e-6
HEAD_DIM = 128
MLSTM_DK = 128
MLSTM_DV = 256
MLSTM_CHUNK = 128
GATE_SOFTCAP = 15.0
GRID_W = 64
WIN_ROWS = 8
WIN_COLS = 16
ROPE_THETA = 10000.0
N_MIXERS = 3
LOG2E = math.log2(math.e)

VMEM_LIMIT_BYTES = 56 * 1024 * 1024
NORM_ROW_CHUNK = 16

F32 = jnp.float32
BF16 = jnp.bfloat16
NEG_INF = float("-inf")


def _params(sem):
    return pltpu.CompilerParams(dimension_semantics=sem, vmem_limit_bytes=VMEM_LIMIT_BYTES)


def _nt(a, b):
    return lax.dot_general(a, b, (((1,), (1,)), ((), ())), preferred_element_type=F32)


def _mm(a, b):
    return jnp.dot(a, b, preferred_element_type=F32)


def _split2(x):
    hi = x.astype(BF16)
    lo = (x - hi.astype(F32)).astype(BF16)
    return hi, lo


def _split3(x):
    hi = x.astype(BF16)
    r = x - hi.astype(F32)
    mid = r.astype(BF16)
    lo = (r - mid.astype(F32)).astype(BF16)
    return hi, mid, lo


def _mod_row(layer, which, rows_per_batch, tm, row0):
    def f(i, *_):
        b = row0 + (i * tm) // rows_per_batch
        return (layer * 48 + b * 6 + which, 0, 0)
    return f


def _adaln_kernel(cond_ref, w_ref, b_ref, out_ref):
    c = cond_ref[...]
    s = c * jax.nn.sigmoid(c)
    sh, sl = _split2(s)
    wh, wl = _split2(w_ref[...])
    lhs = jnp.concatenate([sh, sl], axis=0)
    r = _mm(lhs, wh)
    out_ref[...] = r[:8] + r[8:] + _mm(sh, wl) + b_ref[...]


def adaln_all(cond8, w_ada, b_ada, tn=1024):
    nl, d, n = w_ada.shape
    return pl.pallas_call(
        _adaln_kernel,
        out_shape=jax.ShapeDtypeStruct((nl, 8, n), F32),
        grid=(nl, n // tn),
        in_specs=[
            pl.BlockSpec((8, d), lambda l, j: (0, 0)),
            pl.BlockSpec((None, d, tn), lambda l, j: (l, 0, j)),
            pl.BlockSpec((None, 1, tn), lambda l, j: (l, 0, j)),
        ],
        out_specs=pl.BlockSpec((None, 8, tn), lambda l, j: (l, 0, j)),
        compiler_params=_params(("parallel", "parallel")),
        name="adaln",
    )(cond8, w_ada, b_ada.reshape(nl, 1, n))


def _modulate_rows(src_ref, dst_ref, g, mul, sh, r0, n_rows):
    for c in range(n_rows // NORM_ROW_CHUNK):
        start = r0 + c * NORM_ROW_CHUNK
        if not isinstance(start, int):
            start = pl.multiple_of(start, NORM_ROW_CHUNK)
        rows = pl.ds(start, NORM_ROW_CHUNK)
        xf = src_ref[rows, :]
        y = xf * lax.rsqrt(jnp.mean(xf * xf, axis=-1, keepdims=True) + EPS)
        dst_ref[rows, :] = ((y * g) * mul + sh).astype(dst_ref.dtype)


def _modulate_kernel(x_ref, g_ref, sh_ref, sc_ref, out_ref, *, tm):
    g = g_ref[...]
    mul = 1.0 + sc_ref[...]
    sh = sh_ref[...]
    unroll = 4 * NORM_ROW_CHUNK

    def body(i, _):
        _modulate_rows(x_ref, out_ref, g, mul, sh, pl.multiple_of(i * unroll, unroll), unroll)
        return 0

    lax.fori_loop(0, tm // unroll, body, 0)


def modulate(x, g, mod, *, layer, which_shift, which_scale, rows_per_batch, row0, tm=512):
    m, d = x.shape
    tm = min(tm, rows_per_batch)
    return pl.pallas_call(
        functools.partial(_modulate_kernel, tm=tm),
        out_shape=jax.ShapeDtypeStruct((m, d), BF16),
        grid=(m // tm,),
        in_specs=[
            pl.BlockSpec((tm, d), lambda i: (i, 0)),
            pl.BlockSpec((None, 1, d), lambda i: (layer, 0, 0)),
            pl.BlockSpec((None, 1, d), _mod_row(layer, which_shift, rows_per_batch, tm, row0)),
            pl.BlockSpec((None, 1, d), _mod_row(layer, which_scale, rows_per_batch, tm, row0)),
        ],
        out_specs=pl.BlockSpec((tm, d), lambda i: (i, 0)),
        compiler_params=_params(("parallel",)),
        name="modulate",
    )(x, g, mod, mod)


def _mm_kernel(a_ref, w_ref, out_ref):
    out_ref[...] = _mm(a_ref[...], w_ref[...]).astype(out_ref.dtype)


def matmul(a, w, wl, *, out_dtype=F32, tm=1024, tn=1024, name="proj"):
    m, k = a.shape
    n = w.shape[2]
    return pl.pallas_call(
        _mm_kernel,
        out_shape=jax.ShapeDtypeStruct((m, n), out_dtype),
        grid=(m // tm, n // tn),
        in_specs=[
            pl.BlockSpec((tm, k), lambda i, j: (i, 0)),
            pl.BlockSpec((None, k, tn), lambda i, j: (wl, 0, j)),
        ],
        out_specs=pl.BlockSpec((tm, tn), lambda i, j: (i, j)),
        compiler_params=_params(("parallel", "arbitrary")),
        name=name,
    )(a, w)


def _mmres_kernel(a_ref, w_ref, res_ref, gate_ref, g_ref, sh_ref, sc_ref, x_ref, xn_ref, *, tm, sub):
    gate = gate_ref[...]
    g = g_ref[...]
    mul = 1.0 + sc_ref[...]
    sh = sh_ref[...]
    for s in range(tm // sub):
        rows = pl.ds(s * sub, sub)
        x_ref[rows, :] = res_ref[rows, :] + gate * _mm(a_ref[rows, :], w_ref[...])
        _modulate_rows(x_ref, xn_ref, g, mul, sh, s * sub, sub)


def mm_res_norm(a, w, wl, res, mod, g_next, *, layer, which_gate, next_layer, next_shift, next_scale,
                rows_per_batch, row0, tm=512, sub=128):
    m, k = a.shape
    d = w.shape[2]
    tm = min(tm, rows_per_batch)
    mk = (rows_per_batch, tm, row0)
    return pl.pallas_call(
        functools.partial(_mmres_kernel, tm=tm, sub=sub),
        out_shape=(jax.ShapeDtypeStruct((m, d), F32), jax.ShapeDtypeStruct((m, d), BF16)),
        grid=(m // tm,),
        in_specs=[
            pl.BlockSpec((tm, k), lambda i: (i, 0)),
            pl.BlockSpec((None, k, d), lambda i: (wl, 0, 0), pipeline_mode=pl.Buffered(1)),
            pl.BlockSpec((tm, d), lambda i: (i, 0)),
            pl.BlockSpec((None, 1, d), _mod_row(layer, which_gate, *mk)),
            pl.BlockSpec((None, 1, d), lambda i: (next_layer, 0, 0)),
            pl.BlockSpec((None, 1, d), _mod_row(next_layer, next_shift, *mk)),
            pl.BlockSpec((None, 1, d), _mod_row(next_layer, next_scale, *mk)),
        ],
        out_specs=(pl.BlockSpec((tm, d), lambda i: (i, 0)), pl.BlockSpec((tm, d), lambda i: (i, 0))),
        compiler_params=_params(("parallel",)),
        name="outproj_res_norm",
    )(a, w, res, mod, g_next, mod, mod)


def _mlp_kernel(xn_ref, w1_ref, w2_ref, res_ref, gate_ref, g_ref, sh_ref, sc_ref, x_ref, *rest, tm, emit_next):
    xno_ref = rest[0] if emit_next else None
    acc_ref = rest[-1]
    f = pl.program_id(1)

    @pl.when(f == 0)
    def _():
        acc_ref[...] = jnp.zeros_like(acc_ref)

    h = _mm(xn_ref[...], w1_ref[...])
    h = jnp.square(jnp.maximum(h, 0.0)).astype(BF16)
    acc_ref[...] += _mm(h, w2_ref[...])

    @pl.when(f == pl.num_programs(1) - 1)
    def _():
        gate = gate_ref[...]
        g = g_ref[...]
        mul = 1.0 + sc_ref[...]
        sh = sh_ref[...]
        unroll = 4 * NORM_ROW_CHUNK

        def body(i, _):
            r0 = pl.multiple_of(i * unroll, unroll)
            rows = pl.ds(r0, unroll)
            x_ref[rows, :] = res_ref[rows, :] + gate * acc_ref[rows, :]
            if emit_next:
                _modulate_rows(x_ref, xno_ref, g, mul, sh, r0, unroll)
            return 0

        lax.fori_loop(0, tm // unroll, body, 0)


def fused_mlp(xn, w1, w2, res, mod, g_next, *, layer, next_layer, rows_per_batch, row0, emit_next,
              tm=512, fc=1024):
    m, d = xn.shape
    ff = w1.shape[2]
    tm = min(tm, rows_per_batch)
    mk = (rows_per_batch, tm, row0)
    row_tile = pl.BlockSpec((tm, d), lambda i, f: (i, 0))
    out_shape = [jax.ShapeDtypeStruct((m, d), F32)] + ([jax.ShapeDtypeStruct((m, d), BF16)] if emit_next else [])
    outs = pl.pallas_call(
        functools.partial(_mlp_kernel, tm=tm, emit_next=emit_next),
        out_shape=tuple(out_shape),
        grid=(m // tm, ff // fc),
        in_specs=[
            pl.BlockSpec((tm, d), lambda i, f: (i, 0)),
            pl.BlockSpec((None, d, fc), lambda i, f: (layer, 0, f)),
            pl.BlockSpec((None, fc, d), lambda i, f: (layer, f, 0)),
            row_tile,
            pl.BlockSpec((None, 1, d), _mod_row(layer, 5, *mk)),
            pl.BlockSpec((None, 1, d), lambda i, f: (next_layer, 0, 0)),
            pl.BlockSpec((None, 1, d), _mod_row(next_layer, 0, *mk)),
            pl.BlockSpec((None, 1, d), _mod_row(next_layer, 1, *mk)),
        ],
        out_specs=tuple(row_tile for _ in out_shape),
        scratch_shapes=[pltpu.VMEM((tm, d), F32)],
        compiler_params=_params(("parallel", "arbitrary")),
        name="fused_mlp",
    )(xn, w1, w2, res, mod, g_next, mod, mod)
    return (outs[0], outs[1]) if emit_next else (outs[0], None)


def _head_norm(x, g):
    return x * lax.rsqrt(jnp.mean(x * x, axis=-1, keepdims=True) + EPS) * g


_Q_SCALE = HEAD_DIM ** -0.5 * LOG2E


def _with_ones_column(v):
    lane = lax.broadcasted_iota(jnp.int32, v.shape, 1)
    return jnp.concatenate([v.astype(BF16), jnp.where(lane == 0, 1.0, 0.0).astype(BF16)], axis=1)


def _softmax2_pv(s, vaug):
    p = jnp.exp2(s - jnp.max(s, axis=-1, keepdims=True)).astype(BF16)
    pv = _mm(p, vaug)
    return pv[:, :HEAD_DIM] / pv[:, HEAD_DIM:HEAD_DIM + 1]


def _rope(x, cos, sin_signed, lane_is_first):
    swapped = jnp.where(lane_is_first, pltpu.roll(x, 96, 1), pltpu.roll(x, 32, 1))
    return x * cos + swapped * sin_signed


def _ctx_attn_kernel(q_ref, k_ref, v_ref, qg_ref, kg_ref, o_ref, kn_ref, vo_ref, *, groups, heads):
    hd = HEAD_DIM
    qg = qg_ref[...] * _Q_SCALE
    for h in range(heads):
        hs = slice(h * hd, (h + 1) * hd)
        kn = _head_norm(k_ref[:, hs], kg_ref[...])
        kn_ref[:, hs] = kn
        v = v_ref[:, hs]
        vo_ref[:, hs] = v
        kb = kn.astype(BF16)
        vb = v.astype(BF16)
        for g in range(groups):
            qs = slice((h * groups + g) * hd, (h * groups + g + 1) * hd)
            qn = _head_norm(q_ref[:, qs], qg).astype(BF16)
            s = _nt(qn, kb)
            p = jnp.exp2(s - jnp.max(s, axis=-1, keepdims=True))
            o_ref[:, qs] = (_mm(p.astype(BF16), vb) / jnp.sum(p, axis=-1, keepdims=True)).astype(o_ref.dtype)


def ctx_attention(qkv, q_g, k_g, *, batch, seq, n_q, n_kv, heads=4):
    m = qkv.shape[0]
    groups = n_q // n_kv
    hd = HEAD_DIM
    kw = heads * hd
    qw = heads * groups * hd
    assert (n_q * hd) % kw == 0 and ((n_q + n_kv) * hd) % kw == 0
    k_blk0 = n_q * hd // kw
    v_blk0 = (n_q + n_kv) * hd // kw
    kern = functools.partial(_ctx_attn_kernel, groups=groups, heads=heads)
    return pl.pallas_call(
        kern,
        out_shape=(jax.ShapeDtypeStruct((m, n_q * hd), BF16),
                   jax.ShapeDtypeStruct((m, n_kv * hd), F32),
                   jax.ShapeDtypeStruct((m, n_kv * hd), F32)),
        grid=(batch, n_kv // heads),
        in_specs=[
            pl.BlockSpec((seq, qw), lambda b, h: (b, h)),
            pl.BlockSpec((seq, kw), lambda b, h: (b, k_blk0 + h)),
            pl.BlockSpec((seq, kw), lambda b, h: (b, v_blk0 + h)),
            pl.BlockSpec((1, hd), lambda b, h: (0, 0)),
            pl.BlockSpec((1, hd), lambda b, h: (0, 0)),
        ],
        out_specs=(pl.BlockSpec((seq, qw), lambda b, h: (b, h)),
                   pl.BlockSpec((seq, kw), lambda b, h: (b, h)),
                   pl.BlockSpec((seq, kw), lambda b, h: (b, h))),
        compiler_params=_params(("parallel", "parallel")),
        name="ctx_attention",
    )(qkv, qkv, qkv, q_g.reshape(1, hd), k_g.reshape(1, hd))


def _gqa_kernel(q_ref, k_ref, v_ref, ck_ref, cv_ref, qg_ref, kg_ref, cosq_ref, sinq_ref, cosk_ref, sink_ref,
                o_ref, kall_ref, vall_ref, *, groups, seq, past):
    lane = lax.broadcasted_iota(jnp.int32, (1, HEAD_DIM), 1)
    first = (lane % 64) < 32

    @pl.when(pl.program_id(2) == 0)
    def _():
        kn = _head_norm(k_ref[...], kg_ref[...])
        kall_ref[0:seq, :] = _rope(kn, cosk_ref[...], sink_ref[...], first).astype(BF16)
        kall_ref[seq:seq + past, :] = ck_ref[...].astype(BF16)
        vall_ref[0:seq, :] = _with_ones_column(v_ref[...])
        vall_ref[seq:seq + past, :] = _with_ones_column(cv_ref[...])

    cos = cosq_ref[...]
    sin = sinq_ref[...]
    qg = qg_ref[...] * _Q_SCALE
    for g in range(groups):
        qn = _head_norm(q_ref[:, g * HEAD_DIM:(g + 1) * HEAD_DIM], qg)
        qr = _rope(qn, cos, sin, first).astype(BF16)
        o_ref[:, g * HEAD_DIM:(g + 1) * HEAD_DIM] = _softmax2_pv(_nt(qr, kall_ref[...]), vall_ref[...]).astype(o_ref.dtype)


def gqa_attention(qkv, cache_k, cache_v, q_g, k_g, cos, sin, *, batch, seq, past, n_q, n_kv, tq=256):
    m = qkv.shape[0]
    groups = n_q // n_kv
    hd = HEAD_DIM
    nqb = seq // tq
    kern = functools.partial(_gqa_kernel, groups=groups, seq=seq, past=past)
    return pl.pallas_call(
        kern,
        out_shape=jax.ShapeDtypeStruct((m, n_q * hd), BF16),
        grid=(batch, n_kv, nqb),
        in_specs=[
            pl.BlockSpec((tq, groups * hd), lambda b, h, i: (b * nqb + i, h)),
            pl.BlockSpec((seq, hd), lambda b, h, i: (b, n_q + h)),
            pl.BlockSpec((seq, hd), lambda b, h, i: (b, n_q + n_kv + h)),
            pl.BlockSpec((past, hd), lambda b, h, i: (b, h)),
            pl.BlockSpec((past, hd), lambda b, h, i: (b, h)),
            pl.BlockSpec((1, hd), lambda b, h, i: (0, 0)),
            pl.BlockSpec((1, hd), lambda b, h, i: (0, 0)),
            pl.BlockSpec((tq, hd), lambda b, h, i: (i, 0)),
            pl.BlockSpec((tq, hd), lambda b, h, i: (i, 0)),
            pl.BlockSpec((seq, hd), lambda b, h, i: (0, 0)),
            pl.BlockSpec((seq, hd), lambda b, h, i: (0, 0)),
        ],
        out_specs=pl.BlockSpec((tq, groups * hd), lambda b, h, i: (b * nqb + i, h)),
        scratch_shapes=[pltpu.VMEM((seq + past, hd), BF16), pltpu.VMEM((seq + past, 2 * hd), BF16)],
        compiler_params=_params(("parallel", "parallel", "arbitrary")),
        name="gqa_attention",
    )(qkv, qkv, qkv, cache_k, cache_v, q_g.reshape(1, hd), k_g.reshape(1, hd), cos, sin, cos, sin)


def _na_kernel(rpb_ref, q_ref, k_ref, v_ref, ck_ref, cv_ref, qg_ref, kg_ref, o_ref,
               qn_ref, kn_ref, vb_ref, ckb_ref, cvb_ref, tile_ref, pair_ref, *, rows, wr, n_dr, n_dc, unroll):
    h = pl.program_id(0)
    w = GRID_W

    qn_ref[...] = _head_norm(q_ref[...], qg_ref[...] * _Q_SCALE).astype(BF16)
    kn_ref[...] = _head_norm(k_ref[...], kg_ref[...]).astype(BF16)
    vb_ref[...] = _with_ones_column(v_ref[...])
    ckb_ref[...] = ck_ref[...].astype(BF16)
    cvb_ref[...] = _with_ones_column(cv_ref[...])

    @pl.when(pl.program_id(1) == 0)
    def _():
        qc = lax.broadcasted_iota(jnp.int32, (w, 2 * w), 0)
        lane = lax.broadcasted_iota(jnp.int32, (w, 2 * w), 1)
        kc = lane % w
        cs = jnp.clip(qc - WIN_COLS // 2, 0, w - WIN_COLS)
        col_ok = (kc >= cs) & (kc < cs + WIN_COLS)
        dc = jnp.clip(kc - qc + WIN_COLS - 1, 0, n_dc - 1)
        tiles = [jnp.zeros((w, 2 * w), F32) for _ in range(n_dr)]
        for d in range(n_dc):
            sel = dc == d
            for dr in range(n_dr):
                tiles[dr] = jnp.where(sel, rpb_ref[h, dr * n_dc + d], tiles[dr])
        for dr in range(n_dr):
            tile_ref[dr] = jnp.where(col_ok, tiles[dr] * LOG2E, NEG_INF)
        for dr in range(n_dr - 1):
            pair_ref[dr] = jnp.where(lane < w, tile_ref[dr], tile_ref[dr + 1])

    zero_bias = jnp.zeros((w, ckb_ref.shape[0]), F32)

    def body(i, _):
        rws = [i * unroll + u for u in range(unroll)]
        rss = [jnp.clip(r - wr // 2, 0, rows - wr) for r in rws]
        k0s = [pl.multiple_of(rs * w, w) for rs in rss]
        scores = []
        for r, rs, k0 in zip(rws, rss, k0s):
            dr0 = rs - r + WIN_ROWS - 1
            q_r = qn_ref[pl.ds(pl.multiple_of(r * w, w), w), :]
            kcat = jnp.concatenate([kn_ref[pl.ds(k0, wr * w), :], ckb_ref[...]], axis=0)
            bias = jnp.concatenate([pair_ref[dr0 + 2 * j] for j in range(wr // 2)] + [zero_bias], axis=1)
            scores.append(_nt(q_r, kcat) + bias)
        probs = [jnp.exp2(s - jnp.max(s, axis=-1, keepdims=True)).astype(BF16) for s in scores]
        for r, k0, p in zip(rws, k0s, probs):
            vcat = jnp.concatenate([vb_ref[pl.ds(k0, wr * w), :], cvb_ref[...]], axis=0)
            pv = _mm(p, vcat)
            o = pv[:, :HEAD_DIM] / pv[:, HEAD_DIM:HEAD_DIM + 1]
            o_ref[pl.ds(pl.multiple_of(r * w, w), w), :] = o.astype(o_ref.dtype)
        return 0

    lax.fori_loop(0, rows // unroll, body, 0)


def na_attention(qkv, cache_k, cache_v, q_g, k_g, rpb, *, batch, seq, past, n_heads, unroll=16):
    m = qkv.shape[0]
    hd = HEAD_DIM
    rows = seq // GRID_W
    wr = min(WIN_ROWS, rows)
    n_dr, n_dc = rpb.shape[1], rpb.shape[2]
    assert wr == WIN_ROWS and wr % 2 == 0 and rows % unroll == 0
    kern = functools.partial(_na_kernel, rows=rows, wr=wr, n_dr=n_dr, n_dc=n_dc, unroll=unroll)
    return pl.pallas_call(
        kern,
        out_shape=jax.ShapeDtypeStruct((m, n_heads * hd), BF16),
        grid=(n_heads, batch),
        in_specs=[
            pl.BlockSpec(memory_space=pltpu.SMEM),
            pl.BlockSpec((seq, hd), lambda h, b: (b, h)),
            pl.BlockSpec((seq, hd), lambda h, b: (b, n_heads + h)),
            pl.BlockSpec((seq, hd), lambda h, b: (b, 2 * n_heads + h)),
            pl.BlockSpec((past, hd), lambda h, b: (b, h)),
            pl.BlockSpec((past, hd), lambda h, b: (b, h)),
            pl.BlockSpec((1, hd), lambda h, b: (0, 0)),
            pl.BlockSpec((1, hd), lambda h, b: (0, 0)),
        ],
        out_specs=pl.BlockSpec((seq, hd), lambda h, b: (b, h)),
        scratch_shapes=[
            pltpu.VMEM((seq, hd), BF16), pltpu.VMEM((seq, hd), BF16), pltpu.VMEM((seq, 2 * hd), BF16),
            pltpu.VMEM((past, hd), BF16), pltpu.VMEM((past, 2 * hd), BF16),
            pltpu.VMEM((n_dr, GRID_W, 2 * GRID_W), F32),
            pltpu.VMEM((n_dr - 1, GRID_W, 2 * GRID_W), F32),
        ],
        compiler_params=_params(("parallel", "arbitrary")),
        name="na_attention",
    )(rpb.reshape(n_heads, n_dr * n_dc), qkv, qkv, qkv, cache_k, cache_v,
      q_g.reshape(1, hd), k_g.reshape(1, hd))


def _gates_kernel(xn_ref, w_ref, b_ref, out_ref, *, tm):
    L = MLSTM_CHUNK
    wh, wl = _split2(w_ref[...])
    xn = xn_ref[...]
    pre = _nt(wh, xn) + _nt(wl, xn) + b_ref[...]
    capped = GATE_SOFTCAP * jnp.tanh(pre / GATE_SOFTCAP)
    row = lax.broadcasted_iota(jnp.int32, (capped.shape[0], L), 0) % 8
    is_input = (row == 0) | (row == 2)
    ri = lax.broadcasted_iota(jnp.int32, (L, L), 0)
    ci = lax.broadcasted_iota(jnp.int32, (L, L), 1)
    upper = jnp.where(ri <= ci, 1.0, 0.0).astype(BF16)
    lower = jnp.where(ri >= ci, 1.0, 0.0).astype(BF16)
    for c in range(tm // L):
        cap = capped[:, c * L:(c + 1) * L]
        gates = jnp.where(is_input, cap, jax.nn.log_sigmoid(cap))
        pieces = _split3(gates)
        prefix = sum(_mm(p, upper) for p in pieces)
        suffix = sum(_mm(p, lower) for p in pieces)
        out_ref[c] = jnp.where(row == 4, prefix, jnp.where(row == 5, suffix, gates))


def mlstm_gates(xn, w_gate_t, b_gate_t, *, tm=512):
    m, d = xn.shape
    gh = w_gate_t.shape[0]
    return pl.pallas_call(
        functools.partial(_gates_kernel, tm=tm),
        out_shape=jax.ShapeDtypeStruct((m // MLSTM_CHUNK, gh, MLSTM_CHUNK), F32),
        grid=(m // tm,),
        in_specs=[
            pl.BlockSpec((tm, d), lambda i: (i, 0)),
            pl.BlockSpec((gh, d), lambda i: (0, 0)),
            pl.BlockSpec((gh, 1), lambda i: (0, 0)),
        ],
        out_specs=pl.BlockSpec((tm // MLSTM_CHUNK, gh, MLSTM_CHUNK), lambda i: (i, 0, 0)),
        compiler_params=_params(("parallel",)),
        name="mlstm_gates",
    )(xn, w_gate_t, b_gate_t)


def _mlstm_kernel(*refs, n_chunks, heads, zero_init, write_state, has_prev_states):
    it = iter(refs)
    q_ref, k_ref, v_ref, o_ref, gates_ref, hg_ref = (next(it) for _ in range(6))
    if not zero_init:
        c0_ref, n0_ref, m0_ref = (next(it) for _ in range(3))
    if has_prev_states:
        next(it)
    y_ref = next(it)
    if write_state:
        cf_ref, nf_ref, mf_ref = (next(it) for _ in range(3))
    hdir_ref, cst_ref = next(it), next(it)

    L, dk, dv = MLSTM_CHUNK, MLSTM_DK, MLSTM_DV
    ri = lax.broadcasted_iota(jnp.int32, (L, L), 0)
    ci = lax.broadcasted_iota(jnp.int32, (L, L), 1)
    masks = (ri >= ci, ri <= ci)
    eye = ri == ci
    eye_b = jnp.where(eye, 1.0, 0.0).astype(BF16)
    ones_blk = jnp.ones((L, 128), BF16)
    qscale = dk ** -0.5

    m_init = []
    for hh in range(heads):
        for direction in range(2):
            idx = hh * 2 + direction
            if zero_init:
                cst_ref[idx] = jnp.zeros((dk, dv + 128), F32)
                m_init.append(jnp.zeros((1, 1), F32))
            else:
                n_row = jnp.broadcast_to(n0_ref[direction, hh], (dk, dk))
                n_col = jnp.sum(jnp.where(eye, n_row, 0.0), axis=-1, keepdims=True)
                cst_ref[idx, :, 0:dv] = c0_ref[direction, hh]
                cst_ref[idx, :, dv:dv + 128] = jnp.broadcast_to(n_col, (dk, 128))
                m_init.append(m0_ref[direction, hh])

    def body(step, ms):
        chains = [(hh, direction) for hh in range(heads) for direction in range(2)]
        r0s = [pl.multiple_of((step if d == 0 else n_chunks - 1 - step) * L, L) for _, d in chains]
        cs = [step if d == 0 else n_chunks - 1 - step for _, d in chains]

        st1 = []
        for idx, (hh, d) in enumerate(chains):
            gt = gates_ref[cs[idx], hh * 8:(hh + 1) * 8, :]
            li, lf, brow = gt[2 * d:2 * d + 1], gt[2 * d + 1:2 * d + 2], gt[4 + d:5 + d]
            b3 = _nt(eye_b, jnp.concatenate(_split3(jnp.broadcast_to(brow, (L, L))), axis=0))
            bmat = b3[:, :L] + b3[:, L:2 * L] + b3[:, 2 * L:]
            log_d = jnp.where(masks[d], bmat - brow + li, NEG_INF)
            inter = bmat + ms[idx]
            m_j = jnp.maximum(inter, jnp.max(log_d, axis=-1, keepdims=True))
            tot = jnp.sum(lf, axis=-1, keepdims=True)
            log_w = tot - brow + li
            m_new = jnp.maximum(tot + ms[idx], jnp.max(log_w, axis=-1, keepdims=True))
            st1.append((log_d, inter, m_j, tot, log_w, m_new))

        st2 = []
        for idx, (hh, d) in enumerate(chains):
            log_d, inter, m_j, tot, log_w, m_new = st1[idx]
            q = (q_ref[pl.ds(r0s[idx], L), hh * dk:(hh + 1) * dk] * qscale).astype(BF16)
            k = k_ref[pl.ds(r0s[idx], L), hh * dk:(hh + 1) * dk]
            k_hi, k_lo = _split2(k)
            qk = _nt(q, k_hi)
            kt2 = _nt(eye_b, jnp.concatenate([k_hi, k_lo], axis=0))
            kt = kt2[:, :L] + kt2[:, L:]
            qc = _mm(q, cst_ref[idx].astype(BF16))
            d_mat = jnp.exp(log_d - m_j)
            w_inter = jnp.exp(inter - m_j)
            wt = jnp.exp(log_w - m_new)
            decay = jnp.exp(tot + ms[idx] - m_new)
            ktw = (kt * wt).astype(BF16)
            st2.append((qk * d_mat, qc, w_inter, decay, ktw))

        for idx, (hh, d) in enumerate(chains):
            s, qc, w_inter, decay, ktw = st2[idx]
            floor = jnp.exp(-st1[idx][2])
            vaug = jnp.concatenate([v_ref[pl.ds(r0s[idx], L), hh * dv:(hh + 1) * dv].astype(BF16), ones_blk], axis=1)
            both = _mm(jnp.concatenate([s.astype(BF16), ktw], axis=0), vaug)
            nd = jnp.concatenate([w_inter] * 3, axis=1) * qc + both[:L]
            den = jnp.maximum(jnp.abs(nd[:, dv:dv + 128]), floor)
            h = nd[:, :dv] / jnp.concatenate([den, den], axis=1)
            hdir_ref[d, pl.ds(r0s[idx], L), hh * dv:(hh + 1) * dv] = h
            cst_ref[idx] = decay * cst_ref[idx] + both[L:]
        return tuple(s1[5] for s1 in st1)

    m_fin = lax.fori_loop(0, n_chunks, body, tuple(m_init))

    def finish(c, _):
        r0 = pl.multiple_of(c * L, L)
        for hh in range(heads):
            cols = slice(hh * dv, (hh + 1) * dv)
            hs = hdir_ref[0, pl.ds(r0, L), cols] + hdir_ref[1, pl.ds(r0, L), cols]
            hn = hs * lax.rsqrt(jnp.mean(hs * hs, axis=-1, keepdims=True) + EPS) * hg_ref[:, cols]
            y_ref[pl.ds(r0, L), cols] = (hn * jax.nn.sigmoid(o_ref[pl.ds(r0, L), cols])).astype(y_ref.dtype)
        return 0

    lax.fori_loop(0, n_chunks, finish, 0)

    if write_state:
        for hh in range(heads):
            for direction in range(2):
                idx = hh * 2 + direction
                cf_ref[direction, hh] = cst_ref[idx, :, 0:dv]
                n_b = cst_ref[idx, :, dv:dv + 128]
                nf_ref[direction, hh] = jnp.sum(jnp.where(eye, n_b, 0.0), axis=0, keepdims=True)
                mf_ref[direction, hh] = m_fin[idx]


def mlstm_scan(proj, gates, head_g, state=None, *, batch, seq, n_heads, write_state, heads=2,
               state_slot=(0, 1), prev_states=None):
    m = proj.shape[0]
    dk, dv, L = MLSTM_DK, MLSTM_DV, MLSTM_CHUNK
    H = n_heads
    nb = H // heads
    n_chunks = seq // L
    zero_init = state is None
    slot, n_slots = state_slot
    kern = functools.partial(_mlstm_kernel, n_chunks=n_chunks, heads=heads, zero_init=zero_init,
                             write_state=write_state, has_prev_states=prev_states is not None)
    in_specs = [
        pl.BlockSpec((seq, heads * dk), lambda b, h: (b, h)),
        pl.BlockSpec((seq, heads * dk), lambda b, h: (b, nb + h)),
        pl.BlockSpec((seq, heads * dv), lambda b, h: (b, nb + h)),
        pl.BlockSpec((seq, heads * dv), lambda b, h: (b, 2 * nb + h)),
        pl.BlockSpec((n_chunks, heads * 8, L), lambda b, h: (b, h, 0)),
        pl.BlockSpec((None, 1, heads * dv), lambda b, h: (h, 0, 0)),
    ]
    args = [proj, proj, proj, proj, gates, head_g.reshape(nb, 1, heads * dv)]
    st_specs = [pl.BlockSpec((None, 2, heads, dk, dv), lambda b, h: (b, 0, h, 0, 0)),
                pl.BlockSpec((None, 2, heads, 1, dk), lambda b, h: (b, 0, h, 0, 0)),
                pl.BlockSpec((None, 2, heads, 1, 1), lambda b, h: (b, 0, h, 0, 0))]
    if not zero_init:
        c0_all, c0_slot, n0, m0 = state
        in_specs += [pl.BlockSpec((None, None, 2, heads, dk, dv), lambda b, h: (b, c0_slot, 0, h, 0, 0))] + st_specs[1:]
        args += [c0_all.astype(F32), n0.astype(F32).reshape(batch, 2, H, 1, dk),
                 m0.astype(F32).reshape(batch, 2, H, 1, 1)]
    aliases = {}
    if prev_states is not None:
        aliases = {len(args): 1}
        in_specs.append(pl.BlockSpec(memory_space=pl.ANY))
        args.append(prev_states)
    out_shape = [jax.ShapeDtypeStruct((m, H * dv), BF16)]
    out_specs = [pl.BlockSpec((seq, heads * dv), lambda b, h: (b, h))]
    if write_state:
        out_shape += [jax.ShapeDtypeStruct((batch, n_slots, 2, H, dk, dv), F32),
                      jax.ShapeDtypeStruct((batch, 2, H, 1, dk), F32),
                      jax.ShapeDtypeStruct((batch, 2, H, 1, 1), F32)]
        out_specs += [pl.BlockSpec((None, None, 2, heads, dk, dv), lambda b, h: (b, slot, 0, h, 0, 0))] + st_specs[1:]
    outs = pl.pallas_call(
        kern,
        out_shape=tuple(out_shape),
        grid=(batch, nb),
        in_specs=in_specs,
        out_specs=tuple(out_specs),
        scratch_shapes=[pltpu.VMEM((2, seq, heads * dv), F32), pltpu.VMEM((2 * heads, dk, dv + 128), F32)],
        input_output_aliases=aliases,
        compiler_params=_params(("parallel", "parallel")),
        name="mlstm_scan",
    )(*args)
    if write_state:
        y, cf, nf, mf = outs
        return y, (cf, nf.reshape(batch, 2, H, dk), mf.reshape(batch, 2, H))
    return outs[0], None


def _gate_weights_head_major(w_gate, b_gate, n_heads):
    d = w_gate.shape[0]
    wt = jnp.transpose(w_gate.reshape(d, 4, n_heads), (2, 1, 0)).astype(F32)
    wt = jnp.concatenate([wt, wt[:, 1:2], wt[:, 3:4], jnp.zeros_like(wt[:, :2])], axis=1).reshape(8 * n_heads, d)
    bt = jnp.transpose(b_gate.reshape(4, n_heads), (1, 0)).astype(F32)
    bt = jnp.concatenate([bt, bt[:, 1:2], bt[:, 3:4], jnp.zeros_like(bt[:, :2])], axis=1).reshape(8 * n_heads, 1)
    return wt, bt


def _rope_tables(seq):
    nf = HEAD_DIM // 4
    t = jnp.arange(seq)
    inv = ROPE_THETA ** (-jnp.arange(nf, dtype=F32) / nf)
    pos = jnp.stack([t // GRID_W, t % GRID_W], axis=-1).astype(F32)
    ang = pos[:, :, None] * inv
    cos = jnp.cos(ang)
    sin = jnp.sin(ang)
    cos_full = jnp.stack([cos, cos], axis=2).reshape(seq, HEAD_DIM)
    sin_full = jnp.stack([-sin, sin], axis=2).reshape(seq, HEAD_DIM)
    return cos_full, sin_full


def kernel(x_prompt, x_sample, state_mlstm_C, state_mlstm_n, state_mlstm_m, cache_gqa_k, cache_gqa_v,
           cache_na_k, cache_na_v, c, c_ctx, norm1_g, norm2_g, w_ada, b_ada, w_mlp1, w_mlp2,
           mlstm_w_in, mlstm_w_gate, mlstm_b_gate, mlstm_head_g, mlstm_w_out,
           gqa_w_qkv, gqa_q_g, gqa_k_g, gqa_w_o, na_w_qkv, na_q_g, na_k_g, na_rpb, na_w_o):
    bp, tp, d = x_prompt.shape
    bs, ts, _ = x_sample.shape
    depth = w_ada.shape[0]
    past = cache_gqa_k.shape[2]
    mh = mlstm_w_gate.shape[-1] // 4
    gqa_kv = cache_gqa_k.shape[3]
    gqa_q = gqa_w_o.shape[1] // HEAD_DIM
    na_h = cache_na_k.shape[3]
    assert bs + 1 <= 8

    cond8 = jnp.zeros((8, d), F32).at[:bs].set(c).at[bs].set(c_ctx)
    mod = adaln_all(cond8, w_ada, b_ada).reshape(depth * 48, 1, d)
    g1 = norm1_g.reshape(depth, 1, d)
    g2 = norm2_g.reshape(depth, 1, d)
    cos, sin = _rope_tables(ts)

    groups = [
        dict(x=x_prompt.reshape(bp * tp, d), batch=bp, seq=tp, rpb_=bp * tp, row0=bs, ctx=True),
        dict(x=x_sample.reshape(bs * ts, d), batch=bs, seq=ts, rpb_=ts, row0=0, ctx=False),
    ]
    for grp in groups:
        grp["xn"] = modulate(grp["x"], g1, mod, layer=0, which_shift=0, which_scale=1,
                             rows_per_batch=grp["rpb_"], row0=grp["row0"])
    w1_all, w2_all = w_mlp1.astype(BF16), w_mlp2.astype(BF16)
    w_in_all = [mlstm_w_in.astype(BF16), gqa_w_qkv.astype(BF16), na_w_qkv.astype(BF16)]
    w_o_all = [mlstm_w_out.astype(BF16), gqa_w_o.astype(BF16), na_w_o.astype(BF16)]
    n_mlstm = state_mlstm_C.shape[1]
    c_all = None
    n_l, m_l, gk_l, gv_l, nk_l, nv_l = [], [], [], [], [], []

    for i in range(depth):
        kind, j = i % N_MIXERS, i // N_MIXERS
        last = i == depth - 1
        nxt = i if last else i + 1
        w_in, w_o = w_in_all[kind], w_o_all[kind]
        if kind == 0:
            wg_t, bg_t = _gate_weights_head_major(mlstm_w_gate[j], mlstm_b_gate[j], mh)

        for grp in groups:
            x, xn, batch, seq, ctx = grp["x"], grp["xn"], grp["batch"], grp["seq"], grp["ctx"]
            mk = dict(rows_per_batch=grp["rpb_"], row0=grp["row0"])
            proj = matmul(xn, w_in, j)
            if kind == 0:
                gates = mlstm_gates(xn, wg_t, bg_t)
                if ctx:
                    a, (c_all, nf, mf) = mlstm_scan(proj, gates, mlstm_head_g[j], None, batch=batch, seq=seq,
                                                    n_heads=mh, write_state=True, heads=4,
                                                    state_slot=(j, n_mlstm), prev_states=c_all)
                    n_l.append(nf)
                    m_l.append(mf)
                else:
                    st = (state_mlstm_C, j, state_mlstm_n[:, j], state_mlstm_m[:, j])
                    a, _ = mlstm_scan(proj, gates, mlstm_head_g[j], st, batch=batch, seq=seq,
                                      n_heads=mh, write_state=False)
            elif kind == 1:
                if ctx:
                    a, kn, v = ctx_attention(proj, gqa_q_g[j], gqa_k_g[j], batch=batch, seq=seq, n_q=gqa_q, n_kv=gqa_kv)
                    gk_l.append(kn.reshape(batch, seq, gqa_kv, HEAD_DIM))
                    gv_l.append(v.reshape(batch, seq, gqa_kv, HEAD_DIM))
                else:
                    ck = cache_gqa_k[:, j].astype(F32).reshape(batch * past, gqa_kv * HEAD_DIM)
                    cv = cache_gqa_v[:, j].astype(F32).reshape(batch * past, gqa_kv * HEAD_DIM)
                    a = gqa_attention(proj, ck, cv, gqa_q_g[j], gqa_k_g[j], cos, sin, batch=batch, seq=seq,
                                      past=past, n_q=gqa_q, n_kv=gqa_kv)
            else:
                if ctx:
                    a, kn, v = ctx_attention(proj, na_q_g[j], na_k_g[j], batch=batch, seq=seq, n_q=na_h, n_kv=na_h,
                                             heads=8)
                    nk_l.append(kn.reshape(batch, seq, na_h, HEAD_DIM))
                    nv_l.append(v.reshape(batch, seq, na_h, HEAD_DIM))
                else:
                    ck = cache_na_k[:, j].astype(F32).reshape(batch * past, na_h * HEAD_DIM)
                    cv = cache_na_v[:, j].astype(F32).reshape(batch * past, na_h * HEAD_DIM)
                    a = na_attention(proj, ck, cv, na_q_g[j], na_k_g[j], na_rpb[j], batch=batch, seq=seq,
                                     past=past, n_heads=na_h)
            x, xn = mm_res_norm(a, w_o, j, x, mod, g2, layer=i, which_gate=2, next_layer=i, next_shift=3,
                                next_scale=4, **mk)
            x, xn = fused_mlp(xn, w1_all, w2_all, x, mod, g1, layer=i, next_layer=nxt, emit_next=not last, **mk)
            grp["x"], grp["xn"] = x, xn

    y_prompt = groups[0]["x"].reshape(bp, tp, d)
    y_sample = groups[1]["x"].reshape(bs, ts, d)
    return (y_prompt, y_sample, c_all, jnp.stack(n_l, axis=1), jnp.stack(m_l, axis=1),
            jnp.stack(gk_l, axis=1), jnp.stack(gv_l, axis=1), jnp.stack(nk_l, axis=1), jnp.stack(nv_l, axis=1))
```

```python
import functools
import math

import jax
import jax.numpy as jnp
from jax import lax
from jax.experimental import pallas as pl
from jax.experimental.pallas import tpu as pltpu

EPS = 1e-6
HEAD_DIM = 128
MLSTM_DK = 128
MLSTM_DV = 256
MLSTM_CHUNK = 128
GATE_SOFTCAP = 15.0
GRID_W = 64
WIN_ROWS = 8
WIN_COLS = 16
ROPE_THETA = 10000.0
N_MIXERS = 3
LOG2E = math.log2(math.e)

VMEM_LIMIT_BYTES = 56 * 1024 * 1024
NORM_ROW_CHUNK = 16

F32 = jnp.float32
BF16 = jnp.bfloat16
NEG_INF = float("-inf")


def _params(sem):
    return pltpu.CompilerParams(dimension_semantics=sem, vmem_limit_bytes=VMEM_LIMIT_BYTES)


def _nt(a, b):
    return lax.dot_general(a, b, (((1,), (1,)), ((), ())), preferred_element_type=F32)


def _mm(a, b):
    return jnp.dot(a, b, preferred_element_type=F32)


def _split2(x):
    hi = x.astype(BF16)
    lo = (x - hi.astype(F32)).astype(BF16)
    return hi, lo


def _split3(x):
    hi = x.astype(BF16)
    r = x - hi.astype(F32)
    mid = r.astype(BF16)
    lo = (r - mid.astype(F32)).astype(BF16)
    return hi, mid, lo


def _mod_row(layer, which, rows_per_batch, tm, row0):
    def f(i, *_):
        b = row0 + (i * tm) // rows_per_batch
        return (layer * 48 + b * 6 + which, 0, 0)
    return f


def _adaln_kernel(cond_ref, w_ref, b_ref, out_ref):
    c = cond_ref[...]
    s = c * jax.nn.sigmoid(c)
    sh, sl = _split2(s)
    wh, wl = _split2(w_ref[...])
    lhs = jnp.concatenate([sh, sl], axis=0)
    r = _mm(lhs, wh)
    out_ref[...] = r[:8] + r[8:] + _mm(sh, wl) + b_ref[...]


def adaln_all(cond8, w_ada, b_ada, tn=1024):
    nl, d, n = w_ada.shape
    return pl.pallas_call(
        _adaln_kernel,
        out_shape=jax.ShapeDtypeStruct((nl, 8, n), F32),
        grid=(nl, n // tn),
        in_specs=[
            pl.BlockSpec((8, d), lambda l, j: (0, 0)),
            pl.BlockSpec((None, d, tn), lambda l, j: (l, 0, j)),
            pl.BlockSpec((None, 1, tn), lambda l, j: (l, 0, j)),
        ],
        out_specs=pl.BlockSpec((None, 8, tn), lambda l, j: (l, 0, j)),
        compiler_params=_params(("parallel", "parallel")),
        name="adaln",
    )(cond8, w_ada, b_ada.reshape(nl, 1, n))


def _modulate_rows(src_ref, dst_ref, g, mul, sh, r0, n_rows):
    for c in range(n_rows // NORM_ROW_CHUNK):
        start = r0 + c * NORM_ROW_CHUNK
        if not isinstance(start, int):
            start = pl.multiple_of(start, NORM_ROW_CHUNK)
        rows = pl.ds(start, NORM_ROW_CHUNK)
        xf = src_ref[rows, :]
        y = xf * lax.rsqrt(jnp.mean(xf * xf, axis=-1, keepdims=True) + EPS)
        dst_ref[rows, :] = ((y * g) * mul + sh).astype(dst_ref.dtype)


def _modulate_kernel(x_ref, g_ref, sh_ref, sc_ref, out_ref, *, tm):
    g = g_ref[...]
    mul = 1.0 + sc_ref[...]
    sh = sh_ref[...]
    unroll = 4 * NORM_ROW_CHUNK

    def body(i, _):
        _modulate_rows(x_ref, out_ref, g, mul, sh, pl.multiple_of(i * unroll, unroll), unroll)
        return 0

    lax.fori_loop(0, tm // unroll, body, 0)


def modulate(x, g, mod, *, layer, which_shift, which_scale, rows_per_batch, row0, tm=512):
    m, d = x.shape
    tm = min(tm, rows_per_batch)
    return pl.pallas_call(
        functools.partial(_modulate_kernel, tm=tm),
        out_shape=jax.ShapeDtypeStruct((m, d), BF16),
        grid=(m // tm,),
        in_specs=[
            pl.BlockSpec((tm, d), lambda i: (i, 0)),
            pl.BlockSpec((None, 1, d), lambda i: (layer, 0, 0)),
            pl.BlockSpec((None, 1, d), _mod_row(layer, which_shift, rows_per_batch, tm, row0)),
            pl.BlockSpec((None, 1, d), _mod_row(layer, which_scale, rows_per_batch, tm, row0)),
        ],
        out_specs=pl.BlockSpec((tm, d), lambda i: (i, 0)),
        compiler_params=_params(("parallel",)),
        name="modulate",
    )(x, g, mod, mod)


def _mm_kernel(a_ref, w_ref, out_ref):
    out_ref[...] = _mm(a_ref[...], w_ref[...]).astype(out_ref.dtype)


def matmul(a, w, wl, *, out_dtype=F32, tm=1024, tn=1024, name="proj"):
    m, k = a.shape
    n = w.shape[2]
    return pl.pallas_call(
        _mm_kernel,
        out_shape=jax.ShapeDtypeStruct((m, n), out_dtype),
        grid=(m // tm, n // tn),
        in_specs=[
            pl.BlockSpec((tm, k), lambda i, j: (i, 0)),
            pl.BlockSpec((None, k, tn), lambda i, j: (wl, 0, j)),
        ],
        out_specs=pl.BlockSpec((tm, tn), lambda i, j: (i, j)),
        compiler_params=_params(("parallel", "arbitrary")),
        name=name,
    )(a, w)


def _mmres_kernel(a_ref, w_ref, res_ref, gate_ref, g_ref, sh_ref, sc_ref, x_ref, xn_ref, *, tm, sub):
    gate = gate_ref[...]
    g = g_ref[...]
    mul = 1.0 + sc_ref[...]
    sh = sh_ref[...]
    for s in range(tm // sub):
        rows = pl.ds(s * sub, sub)
        x_ref[rows, :] = res_ref[rows, :] + gate * _mm(a_ref[rows, :], w_ref[...])
        _modulate_rows(x_ref, xn_ref, g, mul, sh, s * sub, sub)


def mm_res_norm(a, w, wl, res, mod, g_next, *, layer, which_gate, next_layer, next_shift, next_scale,
                rows_per_batch, row0, tm=512, sub=256):
    m, k = a.shape
    d = w.shape[2]
    tm = min(tm, rows_per_batch)
    mk = (rows_per_batch, tm, row0)
    return pl.pallas_call(
        functools.partial(_mmres_kernel, tm=tm, sub=sub),
        out_shape=(jax.ShapeDtypeStruct((m, d), F32), jax.ShapeDtypeStruct((m, d), BF16)),
        grid=(m // tm,),
        in_specs=[
            pl.BlockSpec((tm, k), lambda i: (i, 0)),
            pl.BlockSpec((None, k, d), lambda i: (wl, 0, 0), pipeline_mode=pl.Buffered(1)),
            pl.BlockSpec((tm, d), lambda i: (i, 0)),
            pl.BlockSpec((None, 1, d), _mod_row(layer, which_gate, *mk)),
            pl.BlockSpec((None, 1, d), lambda i: (next_layer, 0, 0)),
            pl.BlockSpec((None, 1, d), _mod_row(next_layer, next_shift, *mk)),
            pl.BlockSpec((None, 1, d), _mod_row(next_layer, next_scale, *mk)),
        ],
        out_specs=(pl.BlockSpec((tm, d), lambda i: (i, 0)), pl.BlockSpec((tm, d), lambda i: (i, 0))),
        compiler_params=_params(("parallel",)),
        name="outproj_res_norm",
    )(a, w, res, mod, g_next, mod, mod)


def _mlp_kernel(xn_ref, w1_ref, w2_ref, res_ref, gate_ref, g_ref, sh_ref, sc_ref, x_ref, *rest, tm, sub, emit_next):
    xno_ref = rest[0] if emit_next else None
    acc_ref = rest[-1]
    f = pl.program_id(1)

    @pl.when(f == 0)
    def _():
        acc_ref[...] = jnp.zeros_like(acc_ref)

    def hidden_chunk(rows):
        h = _mm(xn_ref[rows, :], w1_ref[...])
        return _mm(jnp.square(jnp.maximum(h, 0.0)).astype(BF16), w2_ref[...])

    last = pl.num_programs(1) - 1

    @pl.when(f < last)
    def _():
        acc_ref[...] += hidden_chunk(slice(None))

    @pl.when(f == last)
    def _():
        gate = gate_ref[...]
        g = g_ref[...]
        mul = 1.0 + sc_ref[...]
        sh = sh_ref[...]
        for s in range(tm // sub):
            rows = pl.ds(s * sub, sub)
            x_ref[rows, :] = res_ref[rows, :] + gate * (acc_ref[rows, :] + hidden_chunk(rows))
            if emit_next:
                _modulate_rows(x_ref, xno_ref, g, mul, sh, s * sub, sub)


def fused_mlp(xn, w1, w2, res, mod, g_next, *, layer, next_layer, rows_per_batch, row0, emit_next,
              tm=512, fc=1024, sub=128):
    m, d = xn.shape
    ff = w1.shape[2]
    tm = min(tm, rows_per_batch)
    mk = (rows_per_batch, tm, row0)
    row_tile = pl.BlockSpec((tm, d), lambda i, f: (i, 0))
    out_shape = [jax.ShapeDtypeStruct((m, d), F32)] + ([jax.ShapeDtypeStruct((m, d), BF16)] if emit_next else [])
    outs = pl.pallas_call(
        functools.partial(_mlp_kernel, tm=tm, sub=min(sub, tm), emit_next=emit_next),
        out_shape=tuple(out_shape),
        grid=(m // tm, ff // fc),
        in_specs=[
            pl.BlockSpec((tm, d), lambda i, f: (i, 0)),
            pl.BlockSpec((None, d, fc), lambda i, f: (layer, 0, f)),
            pl.BlockSpec((None, fc, d), lambda i, f: (layer, f, 0)),
            row_tile,
            pl.BlockSpec((None, 1, d), _mod_row(layer, 5, *mk)),
            pl.BlockSpec((None, 1, d), lambda i, f: (next_layer, 0, 0)),
            pl.BlockSpec((None, 1, d), _mod_row(next_layer, 0, *mk)),
            pl.BlockSpec((None, 1, d), _mod_row(next_layer, 1, *mk)),
        ],
        out_specs=tuple(row_tile for _ in out_shape),
        scratch_shapes=[pltpu.VMEM((tm, d), F32)],
        compiler_params=_params(("parallel", "arbitrary")),
        name="fused_mlp",
    )(xn, w1, w2, res, mod, g_next, mod, mod)
    return (outs[0], outs[1]) if emit_next else (outs[0], None)


def _head_norm(x, g):
    return x * lax.rsqrt(jnp.mean(x * x, axis=-1, keepdims=True) + EPS) * g


_Q_SCALE = HEAD_DIM ** -0.5 * LOG2E


def _with_ones_column(v):
    lane = lax.broadcasted_iota(jnp.int32, v.shape, 1)
    return jnp.concatenate([v.astype(BF16), jnp.where(lane == 0, 1.0, 0.0).astype(BF16)], axis=1)


def _softmax2_pv(s, vaug):
    p = jnp.exp2(s - jnp.max(s, axis=-1, keepdims=True)).astype(BF16)
    pv = _mm(p, vaug)
    return pv[:, :HEAD_DIM] / pv[:, HEAD_DIM:HEAD_DIM + 1]


def _rope(x, cos, sin_signed, lane_is_first):
    swapped = jnp.where(lane_is_first, pltpu.roll(x, 96, 1), pltpu.roll(x, 32, 1))
    return x * cos + swapped * sin_signed


def _ctx_attn_kernel(q_ref, k_ref, v_ref, qg_ref, kg_ref, o_ref, kn_ref, vo_ref, *, groups, heads):
    hd = HEAD_DIM
    qg = qg_ref[...] * _Q_SCALE
    for h in range(heads):
        hs = slice(h * hd, (h + 1) * hd)
        kn = _head_norm(k_ref[:, hs], kg_ref[...])
        kn_ref[:, hs] = kn
        v = v_ref[:, hs]
        vo_ref[:, hs] = v
        kb = kn.astype(BF16)
        vb = v.astype(BF16)
        for g in range(groups):
            qs = slice((h * groups + g) * hd, (h * groups + g + 1) * hd)
            qn = _head_norm(q_ref[:, qs], qg).astype(BF16)
            s = _nt(qn, kb)
            p = jnp.exp2(s - jnp.max(s, axis=-1, keepdims=True))
            o_ref[:, qs] = (_mm(p.astype(BF16), vb) / jnp.sum(p, axis=-1, keepdims=True)).astype(o_ref.dtype)


def ctx_attention(qkv, q_g, k_g, *, batch, seq, n_q, n_kv, heads=4):
    m = qkv.shape[0]
    groups = n_q // n_kv
    hd = HEAD_DIM
    kw = heads * hd
    qw = heads * groups * hd
    assert (n_q * hd) % kw == 0 and ((n_q + n_kv) * hd) % kw == 0
    k_blk0 = n_q * hd // kw
    v_blk0 = (n_q + n_kv) * hd // kw
    kern = functools.partial(_ctx_attn_kernel, groups=groups, heads=heads)
    return pl.pallas_call(
        kern,
        out_shape=(jax.ShapeDtypeStruct((m, n_q * hd), BF16),
                   jax.ShapeDtypeStruct((m, n_kv * hd), F32),
                   jax.ShapeDtypeStruct((m, n_kv * hd), F32)),
        grid=(batch, n_kv // heads),
        in_specs=[
            pl.BlockSpec((seq, qw), lambda b, h: (b, h)),
            pl.BlockSpec((seq, kw), lambda b, h: (b, k_blk0 + h)),
            pl.BlockSpec((seq, kw), lambda b, h: (b, v_blk0 + h)),
            pl.BlockSpec((1, hd), lambda b, h: (0, 0)),
            pl.BlockSpec((1, hd), lambda b, h: (0, 0)),
        ],
        out_specs=(pl.BlockSpec((seq, qw), lambda b, h: (b, h)),
                   pl.BlockSpec((seq, kw), lambda b, h: (b, h)),
                   pl.BlockSpec((seq, kw), lambda b, h: (b, h))),
        compiler_params=_params(("parallel", "parallel")),
        name="ctx_attention",
    )(qkv, qkv, qkv, q_g.reshape(1, hd), k_g.reshape(1, hd))


def _gqa_kernel(q_ref, k_ref, v_ref, ck_ref, cv_ref, qg_ref, kg_ref, cosq_ref, sinq_ref, cosk_ref, sink_ref,
                o_ref, kall_ref, vall_ref, *, groups, seq, past):
    lane = lax.broadcasted_iota(jnp.int32, (1, HEAD_DIM), 1)
    first = (lane % 64) < 32

    @pl.when(pl.program_id(2) == 0)
    def _():
        kn = _head_norm(k_ref[...], kg_ref[...])
        kall_ref[0:seq, :] = _rope(kn, cosk_ref[...], sink_ref[...], first).astype(BF16)
        kall_ref[seq:seq + past, :] = ck_ref[...].astype(BF16)
        vall_ref[0:seq, :] = _with_ones_column(v_ref[...])
        vall_ref[seq:seq + past, :] = _with_ones_column(cv_ref[...])

    cos = cosq_ref[...]
    sin = sinq_ref[...]
    qg = qg_ref[...] * _Q_SCALE
    for g in range(groups):
        qn = _head_norm(q_ref[:, g * HEAD_DIM:(g + 1) * HEAD_DIM], qg)
        qr = _rope(qn, cos, sin, first).astype(BF16)
        o_ref[:, g * HEAD_DIM:(g + 1) * HEAD_DIM] = _softmax2_pv(_nt(qr, kall_ref[...]), vall_ref[...]).astype(o_ref.dtype)


def gqa_attention(qkv, cache_k, cache_v, q_g, k_g, cos, sin, *, batch, seq, past, n_q, n_kv, tq=256):
    m = qkv.shape[0]
    groups = n_q // n_kv
    hd = HEAD_DIM
    nqb = seq // tq
    kern = functools.partial(_gqa_kernel, groups=groups, seq=seq, past=past)
    return pl.pallas_call(
        kern,
        out_shape=jax.ShapeDtypeStruct((m, n_q * hd), BF16),
        grid=(batch, n_kv, nqb),
        in_specs=[
            pl.BlockSpec((tq, groups * hd), lambda b, h, i: (b * nqb + i, h)),
            pl.BlockSpec((seq, hd), lambda b, h, i: (b, n_q + h)),
            pl.BlockSpec((seq, hd), lambda b, h, i: (b, n_q + n_kv + h)),
            pl.BlockSpec((past, hd), lambda b, h, i: (b, h)),
            pl.BlockSpec((past, hd), lambda b, h, i: (b, h)),
            pl.BlockSpec((1, hd), lambda b, h, i: (0, 0)),
            pl.BlockSpec((1, hd), lambda b, h, i: (0, 0)),
            pl.BlockSpec((tq, hd), lambda b, h, i: (i, 0)),
            pl.BlockSpec((tq, hd), lambda b, h, i: (i, 0)),
            pl.BlockSpec((seq, hd), lambda b, h, i: (0, 0)),
            pl.BlockSpec((seq, hd), lambda b, h, i: (0, 0)),
        ],
        out_specs=pl.BlockSpec((tq, groups * hd), lambda b, h, i: (b * nqb + i, h)),
        scratch_shapes=[pltpu.VMEM((seq + past, hd), BF16), pltpu.VMEM((seq + past, 2 * hd), BF16)],
        compiler_params=_params(("parallel", "parallel", "arbitrary")),
        name="gqa_attention",
    )(qkv, qkv, qkv, cache_k, cache_v, q_g.reshape(1, hd), k_g.reshape(1, hd), cos, sin, cos, sin)


def _na_kernel(rpb_ref, q_ref, k_ref, v_ref, ck_ref, cv_ref, qg_ref, kg_ref, o_ref,
               qn_ref, kn_ref, vb_ref, ckb_ref, cvb_ref, tile_ref, pair_ref, *, rows, wr, n_dr, n_dc, unroll):
    h = pl.program_id(0)
    w = GRID_W

    qn_ref[...] = _head_norm(q_ref[...], qg_ref[...] * _Q_SCALE).astype(BF16)
    kn_ref[...] = _head_norm(k_ref[...], kg_ref[...]).astype(BF16)
    vb_ref[...] = _with_ones_column(v_ref[...])
    ckb_ref[...] = ck_ref[...].astype(BF16)
    cvb_ref[...] = _with_ones_column(cv_ref[...])

    @pl.when(pl.program_id(1) == 0)
    def _():
        qc = lax.broadcasted_iota(jnp.int32, (w, 2 * w), 0)
        lane = lax.broadcasted_iota(jnp.int32, (w, 2 * w), 1)
        kc = lane % w
        cs = jnp.clip(qc - WIN_COLS // 2, 0, w - WIN_COLS)
        col_ok = (kc >= cs) & (kc < cs + WIN_COLS)
        dc = jnp.clip(kc - qc + WIN_COLS - 1, 0, n_dc - 1)
        tiles = [jnp.zeros((w, 2 * w), F32) for _ in range(n_dr)]
        for d in range(n_dc):
            sel = dc == d
            for dr in range(n_dr):
                tiles[dr] = jnp.where(sel, rpb_ref[h, dr * n_dc + d], tiles[dr])
        for dr in range(n_dr):
            tile_ref[dr] = jnp.where(col_ok, tiles[dr] * LOG2E, NEG_INF)
        for dr in range(n_dr - 1):
            pair_ref[dr] = jnp.where(lane < w, tile_ref[dr], tile_ref[dr + 1])

    zero_bias = jnp.zeros((w, ckb_ref.shape[0]), F32)

    def body(i, _):
        rws = [i * unroll + u for u in range(unroll)]
        rss = [jnp.clip(r - wr // 2, 0, rows - wr) for r in rws]
        k0s = [pl.multiple_of(rs * w, w) for rs in rss]
        scores = []
        for r, rs, k0 in zip(rws, rss, k0s):
            dr0 = rs - r + WIN_ROWS - 1
            q_r = qn_ref[pl.ds(pl.multiple_of(r * w, w), w), :]
            kcat = jnp.concatenate([kn_ref[pl.ds(k0, wr * w), :], ckb_ref[...]], axis=0)
            bias = jnp.concatenate([pair_ref[dr0 + 2 * j] for j in range(wr // 2)] + [zero_bias], axis=1)
            scores.append(_nt(q_r, kcat) + bias)
        probs = [jnp.exp2(s - jnp.max(s, axis=-1, keepdims=True)).astype(BF16) for s in scores]
        for r, k0, p in zip(rws, k0s, probs):
            vcat = jnp.concatenate([vb_ref[pl.ds(k0, wr * w), :], cvb_ref[...]], axis=0)
            pv = _mm(p, vcat)
            o = pv[:, :HEAD_DIM] / pv[:, HEAD_DIM:HEAD_DIM + 1]
            o_ref[pl.ds(pl.multiple_of(r * w, w), w), :] = o.astype(o_ref.dtype)
        return 0

    lax.fori_loop(0, rows // unroll, body, 0)


def na_attention(qkv, cache_k, cache_v, q_g, k_g, rpb, *, batch, seq, past, n_heads, unroll=16):
    m = qkv.shape[0]
    hd = HEAD_DIM
    rows = seq // GRID_W
    wr = min(WIN_ROWS, rows)
    n_dr, n_dc = rpb.shape[1], rpb.shape[2]
    assert wr == WIN_ROWS and wr % 2 == 0 and rows % unroll == 0
    kern = functools.partial(_na_kernel, rows=rows, wr=wr, n_dr=n_dr, n_dc=n_dc, unroll=unroll)
    return pl.pallas_call(
        kern,
        out_shape=jax.ShapeDtypeStruct((m, n_heads * hd), BF16),
        grid=(n_heads, batch),
        in_specs=[
            pl.BlockSpec(memory_space=pltpu.SMEM),
            pl.BlockSpec((seq, hd), lambda h, b: (b, h)),
            pl.BlockSpec((seq, hd), lambda h, b: (b, n_heads + h)),
            pl.BlockSpec((seq, hd), lambda h, b: (b, 2 * n_heads + h)),
            pl.BlockSpec((past, hd), lambda h, b: (b, h)),
            pl.BlockSpec((past, hd), lambda h, b: (b, h)),
            pl.BlockSpec((1, hd), lambda h, b: (0, 0)),
            pl.BlockSpec((1, hd), lambda h, b: (0, 0)),
        ],
        out_specs=pl.BlockSpec((seq, hd), lambda h, b: (b, h)),
        scratch_shapes=[
            pltpu.VMEM((seq, hd), BF16), pltpu.VMEM((seq, hd), BF16), pltpu.VMEM((seq, 2 * hd), BF16),
            pltpu.VMEM((past, hd), BF16), pltpu.VMEM((past, 2 * hd), BF16),
            pltpu.VMEM((n_dr, GRID_W, 2 * GRID_W), F32),
            pltpu.VMEM((n_dr - 1, GRID_W, 2 * GRID_W), F32),
        ],
        compiler_params=_params(("parallel", "arbitrary")),
        name="na_attention",
    )(rpb.reshape(n_heads, n_dr * n_dc), qkv, qkv, qkv, cache_k, cache_v,
      q_g.reshape(1, hd), k_g.reshape(1, hd))


def _gates_kernel(xn_ref, w_ref, b_ref, out_ref, *, tm):
    L = MLSTM_CHUNK
    wh, wl = _split2(w_ref[...])
    xn = xn_ref[...]
    pre = _nt(wh, xn) + _nt(wl, xn) + b_ref[...]
    capped = GATE_SOFTCAP * jnp.tanh(pre / GATE_SOFTCAP)
    row = lax.broadcasted_iota(jnp.int32, (capped.shape[0], L), 0) % 8
    is_input = (row == 0) | (row == 2)
    ri = lax.broadcasted_iota(jnp.int32, (L, L), 0)
    ci = lax.broadcasted_iota(jnp.int32, (L, L), 1)
    upper = jnp.where(ri <= ci, 1.0, 0.0).astype(BF16)
    lower = jnp.where(ri >= ci, 1.0, 0.0).astype(BF16)
    for c in range(tm // L):
        cap = capped[:, c * L:(c + 1) * L]
        gates = jnp.where(is_input, cap, jax.nn.log_sigmoid(cap))
        pieces = _split3(gates)
        prefix = sum(_mm(p, upper) for p in pieces)
        suffix = sum(_mm(p, lower) for p in pieces)
        out_ref[c] = jnp.where(row == 4, prefix, jnp.where(row == 5, suffix, gates))


def mlstm_gates(xn, w_gate_t, b_gate_t, *, tm=512):
    m, d = xn.shape
    gh = w_gate_t.shape[0]
    return pl.pallas_call(
        functools.partial(_gates_kernel, tm=tm),
        out_shape=jax.ShapeDtypeStruct((m // MLSTM_CHUNK, gh, MLSTM_CHUNK), F32),
        grid=(m // tm,),
        in_specs=[
            pl.BlockSpec((tm, d), lambda i: (i, 0)),
            pl.BlockSpec((gh, d), lambda i: (0, 0)),
            pl.BlockSpec((gh, 1), lambda i: (0, 0)),
        ],
        out_specs=pl.BlockSpec((tm // MLSTM_CHUNK, gh, MLSTM_CHUNK), lambda i: (i, 0, 0)),
        compiler_params=_params(("parallel",)),
        name="mlstm_gates",
    )(xn, w_gate_t, b_gate_t)


def _mlstm_kernel(*refs, n_chunks, heads, zero_init, write_state, has_prev_states):
    it = iter(refs)
    q_ref, k_ref, v_ref, o_ref, gates_ref, hg_ref = (next(it) for _ in range(6))
    if not zero_init:
        c0_ref, n0_ref, m0_ref = (next(it) for _ in range(3))
    if has_prev_states:
        next(it)
    y_ref = next(it)
    if write_state:
        cf_ref, nf_ref, mf_ref = (next(it) for _ in range(3))
    hdir_ref, cst_ref = next(it), next(it)

    L, dk, dv = MLSTM_CHUNK, MLSTM_DK, MLSTM_DV
    ri = lax.broadcasted_iota(jnp.int32, (L, L), 0)
    ci = lax.broadcasted_iota(jnp.int32, (L, L), 1)
    masks = (ri >= ci, ri <= ci)
    eye = ri == ci
    eye_b = jnp.where(eye, 1.0, 0.0).astype(BF16)
    ones_blk = jnp.ones((L, 128), BF16)
    qscale = dk ** -0.5

    m_init = []
    for hh in range(heads):
        for direction in range(2):
            idx = hh * 2 + direction
            if zero_init:
                cst_ref[idx] = jnp.zeros((dk, dv + 128), F32)
                m_init.append(jnp.zeros((1, 1), F32))
            else:
                n_row = jnp.broadcast_to(n0_ref[direction, hh], (dk, dk))
                n_col = jnp.sum(jnp.where(eye, n_row, 0.0), axis=-1, keepdims=True)
                cst_ref[idx, :, 0:dv] = c0_ref[direction, hh]
                cst_ref[idx, :, dv:dv + 128] = jnp.broadcast_to(n_col, (dk, 128))
                m_init.append(m0_ref[direction, hh])

    def body(step, ms):
        chains = [(hh, direction) for hh in range(heads) for direction in range(2)]
        r0s = [pl.multiple_of((step if d == 0 else n_chunks - 1 - step) * L, L) for _, d in chains]
        cs = [step if d == 0 else n_chunks - 1 - step for _, d in chains]

        st1 = []
        for idx, (hh, d) in enumerate(chains):
            gt = gates_ref[cs[idx], hh * 8:(hh + 1) * 8, :]
            li, lf, brow = gt[2 * d:2 * d + 1], gt[2 * d + 1:2 * d + 2], gt[4 + d:5 + d]
            b3 = _nt(eye_b, jnp.concatenate(_split3(jnp.broadcast_to(brow, (L, L))), axis=0))
            bmat = b3[:, :L] + b3[:, L:2 * L] + b3[:, 2 * L:]
            log_d = jnp.where(masks[d], bmat - brow + li, NEG_INF)
            inter = bmat + ms[idx]
            m_j = jnp.maximum(inter, jnp.max(log_d, axis=-1, keepdims=True))
            tot = jnp.sum(lf, axis=-1, keepdims=True)
            log_w = tot - brow + li
            m_new = jnp.maximum(tot + ms[idx], jnp.max(log_w, axis=-1, keepdims=True))
            st1.append((log_d, inter, m_j, tot, log_w, m_new))

        st2 = []
        for idx, (hh, d) in enumerate(chains):
            log_d, inter, m_j, tot, log_w, m_new = st1[idx]
            q = (q_ref[pl.ds(r0s[idx], L), hh * dk:(hh + 1) * dk] * qscale).astype(BF16)
            k = k_ref[pl.ds(r0s[idx], L), hh * dk:(hh + 1) * dk]
            k_hi, k_lo = _split2(k)
            qk = _nt(q, k_hi)
            kt2 = _nt(eye_b, jnp.concatenate([k_hi, k_lo], axis=0))
            kt = kt2[:, :L] + kt2[:, L:]
            qc = _mm(q, cst_ref[idx].astype(BF16))
            d_mat = jnp.exp(log_d - m_j)
            w_inter = jnp.exp(inter - m_j)
            wt = jnp.exp(log_w - m_new)
            decay = jnp.exp(tot + ms[idx] - m_new)
            ktw = (kt * wt).astype(BF16)
            st2.append((qk * d_mat, qc, w_inter, decay, ktw))

        for idx, (hh, d) in enumerate(chains):
            s, qc, w_inter, decay, ktw = st2[idx]
            floor = jnp.exp(-st1[idx][2])
            vaug = jnp.concatenate([v_ref[pl.ds(r0s[idx], L), hh * dv:(hh + 1) * dv].astype(BF16), ones_blk], axis=1)
            both = _mm(jnp.concatenate([s.astype(BF16), ktw], axis=0), vaug)
            nd = jnp.concatenate([w_inter] * 3, axis=1) * qc + both[:L]
            den = jnp.maximum(jnp.abs(nd[:, dv:dv + 128]), floor)
            h = nd[:, :dv] / jnp.concatenate([den, den], axis=1)
            hdir_ref[d, pl.ds(r0s[idx], L), hh * dv:(hh + 1) * dv] = h
            cst_ref[idx] = decay * cst_ref[idx] + both[L:]
        return tuple(s1[5] for s1 in st1)

    m_fin = lax.fori_loop(0, n_chunks, body, tuple(m_init))

    def finish(c, _):
        r0 = pl.multiple_of(c * L, L)
        for hh in range(heads):
            cols = slice(hh * dv, (hh + 1) * dv)
            hs = hdir_ref[0, pl.ds(r0, L), cols] + hdir_ref[1, pl.ds(r0, L), cols]
            hn = hs * lax.rsqrt(jnp.mean(hs * hs, axis=-1, keepdims=True) + EPS) * hg_ref[:, cols]
            y_ref[pl.ds(r0, L), cols] = (hn * jax.nn.sigmoid(o_ref[pl.ds(r0, L), cols])).astype(y_ref.dtype)
        return 0

    lax.fori_loop(0, n_chunks, finish, 0)

    if write_state:
        for hh in range(heads):
            for direction in range(2):
                idx = hh * 2 + direction
                cf_ref[direction, hh] = cst_ref[idx, :, 0:dv]
                n_b = cst_ref[idx, :, dv:dv + 128]
                nf_ref[direction, hh] = jnp.sum(jnp.where(eye, n_b, 0.0), axis=0, keepdims=True)
                mf_ref[direction, hh] = m_fin[idx]


def mlstm_scan(proj, gates, head_g, state=None, *, batch, seq, n_heads, write_state, heads=2,
               state_slot=(0, 1), prev_states=None):
    m = proj.shape[0]
    dk, dv, L = MLSTM_DK, MLSTM_DV, MLSTM_CHUNK
    H = n_heads
    nb = H // heads
    n_chunks = seq // L
    zero_init = state is None
    slot, n_slots = state_slot
    kern = functools.partial(_mlstm_kernel, n_chunks=n_chunks, heads=heads, zero_init=zero_init,
                             write_state=write_state, has_prev_states=prev_states is not None)
    in_specs = [
        pl.BlockSpec((seq, heads * dk), lambda b, h: (b, h)),
        pl.BlockSpec((seq, heads * dk), lambda b, h: (b, nb + h)),
        pl.BlockSpec((seq, heads * dv), lambda b, h: (b, nb + h)),
        pl.BlockSpec((seq, heads * dv), lambda b, h: (b, 2 * nb + h)),
        pl.BlockSpec((n_chunks, heads * 8, L), lambda b, h: (b, h, 0)),
        pl.BlockSpec((None, 1, heads * dv), lambda b, h: (h, 0, 0)),
    ]
    args = [proj, proj, proj, proj, gates, head_g.reshape(nb, 1, heads * dv)]
    st_specs = [pl.BlockSpec((None, 2, heads, dk, dv), lambda b, h: (b, 0, h, 0, 0)),
                pl.BlockSpec((None, 2, heads, 1, dk), lambda b, h: (b, 0, h, 0, 0)),
                pl.BlockSpec((None, 2, heads, 1, 1), lambda b, h: (b, 0, h, 0, 0))]
    if not zero_init:
        c0_all, c0_slot, n0, m0 = state
        in_specs += [pl.BlockSpec((None, None, 2, heads, dk, dv), lambda b, h: (b, c0_slot, 0, h, 0, 0))] + st_specs[1:]
        args += [c0_all.astype(F32), n0.astype(F32).reshape(batch, 2, H, 1, dk),
                 m0.astype(F32).reshape(batch, 2, H, 1, 1)]
    aliases = {}
    if prev_states is not None:
        aliases = {len(args): 1}
        in_specs.append(pl.BlockSpec(memory_space=pl.ANY))
        args.append(prev_states)
    out_shape = [jax.ShapeDtypeStruct((m, H * dv), BF16)]
    out_specs = [pl.BlockSpec((seq, heads * dv), lambda b, h: (b, h))]
    if write_state:
        out_shape += [jax.ShapeDtypeStruct((batch, n_slots, 2, H, dk, dv), F32),
                      jax.ShapeDtypeStruct((batch, 2, H, 1, dk), F32),
                      jax.ShapeDtypeStruct((batch, 2, H, 1, 1), F32)]
        out_specs += [pl.BlockSpec((None, None, 2, heads, dk, dv), lambda b, h: (b, slot, 0, h, 0, 0))] + st_specs[1:]
    outs = pl.pallas_call(
        kern,
        out_shape=tuple(out_shape),
        grid=(batch, nb),
        in_specs=in_specs,
        out_specs=tuple(out_specs),
        scratch_shapes=[pltpu.VMEM((2, seq, heads * dv), F32), pltpu.VMEM((2 * heads, dk, dv + 128), F32)],
        input_output_aliases=aliases,
        compiler_params=_params(("parallel", "parallel")),
        name="mlstm_scan",
    )(*args)
    if write_state:
        y, cf, nf, mf = outs
        return y, (cf, nf.reshape(batch, 2, H, dk), mf.reshape(batch, 2, H))
    return outs[0], None


def _gate_weights_head_major(w_gate, b_gate, n_heads):
    d = w_gate.shape[0]
    wt = jnp.transpose(w_gate.reshape(d, 4, n_heads), (2, 1, 0)).astype(F32)
    wt = jnp.concatenate([wt, wt[:, 1:2], wt[:, 3:4], jnp.zeros_like(wt[:, :2])], axis=1).reshape(8 * n_heads, d)
    bt = jnp.transpose(b_gate.reshape(4, n_heads), (1, 0)).astype(F32)
    bt = jnp.concatenate([bt, bt[:, 1:2], bt[:, 3:4], jnp.zeros_like(bt[:, :2])], axis=1).reshape(8 * n_heads, 1)
    return wt, bt


def _rope_tables(seq):
    nf = HEAD_DIM // 4
    t = jnp.arange(seq)
    inv = ROPE_THETA ** (-jnp.arange(nf, dtype=F32) / nf)
    pos = jnp.stack([t // GRID_W, t % GRID_W], axis=-1).astype(F32)
    ang = pos[:, :, None] * inv
    cos = jnp.cos(ang)
    sin = jnp.sin(ang)
    cos_full = jnp.stack([cos, cos], axis=2).reshape(seq, HEAD_DIM)
    sin_full = jnp.stack([-sin, sin], axis=2).reshape(seq, HEAD_DIM)
    return cos_full, sin_full


def kernel(x_prompt, x_sample, state_mlstm_C, state_mlstm_n, state_mlstm_m, cache_gqa_k, cache_gqa_v,
           cache_na_k, cache_na_v, c, c_ctx, norm1_g, norm2_g, w_ada, b_ada, w_mlp1, w_mlp2,
           mlstm_w_in, mlstm_w_gate, mlstm_b_gate, mlstm_head_g, mlstm_w_out,
           gqa_w_qkv, gqa_q_g, gqa_k_g, gqa_w_o, na_w_qkv, na_q_g, na_k_g, na_rpb, na_w_o):
    bp, tp, d = x_prompt.shape
    bs, ts, _ = x_sample.shape
    depth = w_ada.shape[0]
    past = cache_gqa_k.shape[2]
    mh = mlstm_w_gate.shape[-1] // 4
    gqa_kv = cache_gqa_k.shape[3]
    gqa_q = gqa_w_o.shape[1] // HEAD_DIM
    na_h = cache_na_k.shape[3]
    assert bs + 1 <= 8

    cond8 = jnp.zeros((8, d), F32).at[:bs].set(c).at[bs].set(c_ctx)
    mod = adaln_all(cond8, w_ada, b_ada).reshape(depth * 48, 1, d)
    g1 = norm1_g.reshape(depth, 1, d)
    g2 = norm2_g.reshape(depth, 1, d)
    cos, sin = _rope_tables(ts)

    groups = [
        dict(x=x_prompt.reshape(bp * tp, d), batch=bp, seq=tp, rpb_=bp * tp, row0=bs, ctx=True),
        dict(x=x_sample.reshape(bs * ts, d), batch=bs, seq=ts, rpb_=ts, row0=0, ctx=False),
    ]
    for grp in groups:
        grp["xn"] = modulate(grp["x"], g1, mod, layer=0, which_shift=0, which_scale=1,
                             rows_per_batch=grp["rpb_"], row0=grp["row0"])
    w1_all, w2_all = w_mlp1.astype(BF16), w_mlp2.astype(BF16)
    w_in_all = [mlstm_w_in.astype(BF16), gqa_w_qkv.astype(BF16), na_w_qkv.astype(BF16)]
    w_o_all = [mlstm_w_out.astype(BF16), gqa_w_o.astype(BF16), na_w_o.astype(BF16)]
    n_mlstm = state_mlstm_C.shape[1]
    c_all = jnp.zeros((bp, n_mlstm, 2, mh, MLSTM_DK, MLSTM_DV), F32)
    n_l, m_l, gk_l, gv_l, nk_l, nv_l = [], [], [], [], [], []

    for i in range(depth):
        kind, j = i % N_MIXERS, i // N_MIXERS
        last = i == depth - 1
        nxt = i if last else i + 1
        w_in, w_o = w_in_all[kind], w_o_all[kind]
        if kind == 0:
            wg_t, bg_t = _gate_weights_head_major(mlstm_w_gate[j], mlstm_b_gate[j], mh)

        for grp in groups:
            x, xn, batch, seq, ctx = grp["x"], grp["xn"], grp["batch"], grp["seq"], grp["ctx"]
            mk = dict(rows_per_batch=grp["rpb_"], row0=grp["row0"])
            proj = matmul(xn, w_in, j)
            if kind == 0:
                gates = mlstm_gates(xn, wg_t, bg_t)
                if ctx:
                    a, (c_all, nf, mf) = mlstm_scan(proj, gates, mlstm_head_g[j], None, batch=batch, seq=seq,
                                                    n_heads=mh, write_state=True, heads=4,
                                                    state_slot=(j, n_mlstm), prev_states=c_all)
                    n_l.append(nf)
                    m_l.append(mf)
                else:
                    st = (state_mlstm_C, j, state_mlstm_n[:, j], state_mlstm_m[:, j])
                    a, _ = mlstm_scan(proj, gates, mlstm_head_g[j], st, batch=batch, seq=seq,
                                      n_heads=mh, write_state=False)
            elif kind == 1:
                if ctx:
                    a, kn, v = ctx_attention(proj, gqa_q_g[j], gqa_k_g[j], batch=batch, seq=seq, n_q=gqa_q, n_kv=gqa_kv)
                    gk_l.append(kn.reshape(batch, seq, gqa_kv, HEAD_DIM))
                    gv_l.append(v.reshape(batch, seq, gqa_kv, HEAD_DIM))
                else:
                    ck = cache_gqa_k[:, j].astype(F32).reshape(batch * past, gqa_kv * HEAD_DIM)
                    cv = cache_gqa_v[:, j].astype(F32).reshape(batch * past, gqa_kv * HEAD_DIM)
                    a = gqa_attention(proj, ck, cv, gqa_q_g[j], gqa_k_g[j], cos, sin, batch=batch, seq=seq,
                                      past=past, n_q=gqa_q, n_kv=gqa_kv)
            else:
                if ctx:
                    a, kn, v = ctx_attention(proj, na_q_g[j], na_k_g[j], batch=batch, seq=seq, n_q=na_h, n_kv=na_h,
                                             heads=8)
                    nk_l.append(kn.reshape(batch, seq, na_h, HEAD_DIM))
                    nv_l.append(v.reshape(batch, seq, na_h, HEAD_DIM))
                else:
                    ck = cache_na_k[:, j].astype(F32).reshape(batch * past, na_h * HEAD_DIM)
                    cv = cache_na_v[:, j].astype(F32).reshape(batch * past, na_h * HEAD_DIM)
                    a = na_attention(proj, ck, cv, na_q_g[j], na_k_g[j], na_rpb[j], batch=batch, seq=seq,
                                     past=past, n_heads=na_h)
            x, xn = mm_res_norm(a, w_o, j, x, mod, g2, layer=i, which_gate=2, next_layer=i, next_shift=3,
                                next_scale=4, **mk)
            x, xn = fused_mlp(xn, w1_all, w2_all, x, mod, g1, layer=i, next_layer=nxt, emit_next=not last, **mk)
            grp["x"], grp["xn"] = x, xn

    y_prompt = groups[0]["x"].reshape(bp, tp, d)
    y_sample = groups[1]["x"].reshape(bs, ts, d)
    return (y_prompt, y_sample, c_all, jnp.stack(n_l, axis=1), jnp.stack(m_l, axis=1),
            jnp.stack(gk_l, axis=1), jnp.stack(gv_l, axis=1), jnp.stack(nk_l, axis=1), jnp.stack(nv_l, axis=1))
```

```python
import functools
import math

import jax
import jax.numpy as jnp
from jax import lax
from jax.experimental import pallas as pl
from jax.experimental.pallas import tpu as pltpu

EPS = 1e-6
HEAD_DIM = 128
MLSTM_DK = 128
MLSTM_DV = 256
MLSTM_CHUNK = 128
GATE_SOFTCAP = 15.0
GRID_W = 64
WIN_ROWS = 8
WIN_COLS = 16
ROPE_THETA = 10000.0
N_MIXERS = 3
LOG2E = math.log2(math.e)

VMEM_LIMIT_BYTES = 56 * 1024 * 1024
NORM_ROW_CHUNK = 16
MLP_HIDDEN_CHUNK = 1024
PROJ_COL_TILE = 768

F32 = jnp.float32
BF16 = jnp.bfloat16
NEG_INF = float("-inf")


def _params(sem):
    return pltpu.CompilerParams(dimension_semantics=sem, vmem_limit_bytes=VMEM_LIMIT_BYTES)


def _nt(a, b):
    return lax.dot_general(a, b, (((1,), (1,)), ((), ())), preferred_element_type=F32)


def _mm(a, b):
    return jnp.dot(a, b, preferred_element_type=F32)


def _split2(x):
    hi = x.astype(BF16)
    lo = (x - hi.astype(F32)).astype(BF16)
    return hi, lo


def _split3(x):
    hi = x.astype(BF16)
    r = x - hi.astype(F32)
    mid = r.astype(BF16)
    lo = (r - mid.astype(F32)).astype(BF16)
    return hi, mid, lo


def _mod_row(layer, which, rows_per_batch, tm, row0):
    def f(i, *_):
        b = row0 + (i * tm) // rows_per_batch
        return (layer * 48 + b * 6 + which, 0, 0)
    return f


def _adaln_kernel(cond_ref, w_ref, b_ref, out_ref):
    c = cond_ref[...]
    s = c * jax.nn.sigmoid(c)
    sh, sl = _split2(s)
    wh, wl = _split2(w_ref[...])
    lhs = jnp.concatenate([sh, sl], axis=0)
    r = _mm(lhs, wh)
    out_ref[...] = r[:8] + r[8:] + _mm(sh, wl) + b_ref[...]


def adaln_all(cond8, w_ada, b_ada, tn=1024):
    nl, d, n = w_ada.shape
    return pl.pallas_call(
        _adaln_kernel,
        out_shape=jax.ShapeDtypeStruct((nl, 8, n), F32),
        grid=(nl, n // tn),
        in_specs=[
            pl.BlockSpec((8, d), lambda l, j: (0, 0)),
            pl.BlockSpec((None, d, tn), lambda l, j: (l, 0, j)),
            pl.BlockSpec((None, 1, tn), lambda l, j: (l, 0, j)),
        ],
        out_specs=pl.BlockSpec((None, 8, tn), lambda l, j: (l, 0, j)),
        compiler_params=_params(("parallel", "parallel")),
        name="adaln",
    )(cond8, w_ada, b_ada.reshape(nl, 1, n))


def _modulate_rows(src_ref, dst_ref, g, mul, sh, r0, n_rows):
    for c in range(n_rows // NORM_ROW_CHUNK):
        start = r0 + c * NORM_ROW_CHUNK
        if not isinstance(start, int):
            start = pl.multiple_of(start, NORM_ROW_CHUNK)
        rows = pl.ds(start, NORM_ROW_CHUNK)
        xf = src_ref[rows, :]
        y = xf * lax.rsqrt(jnp.mean(xf * xf, axis=-1, keepdims=True) + EPS)
        dst_ref[rows, :] = ((y * g) * mul + sh).astype(dst_ref.dtype)


def _modulate_kernel(x_ref, g_ref, sh_ref, sc_ref, out_ref, *, tm):
    g = g_ref[...]
    mul = 1.0 + sc_ref[...]
    sh = sh_ref[...]
    unroll = 4 * NORM_ROW_CHUNK

    def body(i, _):
        _modulate_rows(x_ref, out_ref, g, mul, sh, pl.multiple_of(i * unroll, unroll), unroll)
        return 0

    lax.fori_loop(0, tm // unroll, body, 0)


def modulate(x, g, mod, *, layer, which_shift, which_scale, rows_per_batch, row0, tm=512):
    m, d = x.shape
    tm = min(tm, rows_per_batch)
    return pl.pallas_call(
        functools.partial(_modulate_kernel, tm=tm),
        out_shape=jax.ShapeDtypeStruct((m, d), BF16),
        grid=(m // tm,),
        in_specs=[
            pl.BlockSpec((tm, d), lambda i: (i, 0)),
            pl.BlockSpec((None, 1, d), lambda i: (layer, 0, 0)),
            pl.BlockSpec((None, 1, d), _mod_row(layer, which_shift, rows_per_batch, tm, row0)),
            pl.BlockSpec((None, 1, d), _mod_row(layer, which_scale, rows_per_batch, tm, row0)),
        ],
        out_specs=pl.BlockSpec((tm, d), lambda i: (i, 0)),
        compiler_params=_params(("parallel",)),
        name="modulate",
    )(x, g, mod, mod)


def _mm_kernel(a_ref, w_ref, out_ref):
    out_ref[...] = _mm(a_ref[...], w_ref[...]).astype(out_ref.dtype)


def _mm_cast_kernel(a_ref, w_ref, w1_ref, w2_ref, out_ref, w1o_ref, w2o_ref):
    out_ref[...] = _mm(a_ref[...], w_ref[...]).astype(out_ref.dtype)
    fc = w1o_ref.shape[2]
    for t in range(w1o_ref.shape[0]):
        w1o_ref[t] = w1_ref[:, t * fc:(t + 1) * fc].astype(BF16)
    w2o_ref[...] = w2_ref[...].astype(BF16)


def matmul_and_cast(a, w, wl, w1, w2, layer, *, fc, out_dtype=F32, tm=1024, name="proj_cast"):
    m, k = a.shape
    nt, tn = w.shape[1], w.shape[3]
    ni = m // tm
    _, d, ff = w1.shape
    r1 = d // ni
    c1 = ff // nt
    r2 = ff // (ni * nt)
    assert d % ni == 0 and ff % nt == 0 and c1 % fc == 0 and ff % (ni * nt) == 0 and r1 % 16 == 0 and r2 % 16 == 0
    return pl.pallas_call(
        _mm_cast_kernel,
        out_shape=(jax.ShapeDtypeStruct((m, nt * tn), out_dtype),
                   jax.ShapeDtypeStruct((ff // fc, d, fc), BF16),
                   jax.ShapeDtypeStruct((ff, d), BF16)),
        grid=(ni, nt),
        in_specs=[
            pl.BlockSpec((tm, k), lambda i, j: (i, 0)),
            pl.BlockSpec((None, None, k, tn), lambda i, j: (wl, j, 0, 0)),
            pl.BlockSpec((None, r1, c1), lambda i, j: (layer, i, j)),
            pl.BlockSpec((None, r2, d), lambda i, j: (layer, i * nt + j, 0)),
        ],
        out_specs=(pl.BlockSpec((tm, tn), lambda i, j: (i, j)),
                   pl.BlockSpec((c1 // fc, r1, fc), lambda i, j: (j, i, 0)),
                   pl.BlockSpec((r2, d), lambda i, j: (i * nt + j, 0))),
        compiler_params=_params(("parallel", "arbitrary")),
        name=name,
    )(a, w, w1, w2)


def _column_tiles(w, tn):
    nl, k, n = w.shape
    return jnp.transpose(w.astype(BF16).reshape(nl, k, n // tn, tn), (0, 2, 1, 3))


def matmul(a, w, wl, *, out_dtype=F32, tm=1024, name="proj"):
    m, k = a.shape
    nt, tn = w.shape[1], w.shape[3]
    n = nt * tn
    return pl.pallas_call(
        _mm_kernel,
        out_shape=jax.ShapeDtypeStruct((m, n), out_dtype),
        grid=(m // tm, nt),
        in_specs=[
            pl.BlockSpec((tm, k), lambda i, j: (i, 0)),
            pl.BlockSpec((None, None, k, tn), lambda i, j: (wl, j, 0, 0)),
        ],
        out_specs=pl.BlockSpec((tm, tn), lambda i, j: (i, j)),
        compiler_params=_params(("parallel", "arbitrary")),
        name=name,
    )(a, w)


def _mmres_kernel(a_ref, w_ref, res_ref, gate_ref, g_ref, sh_ref, sc_ref, x_ref, xn_ref, *, tm, sub):
    gate = gate_ref[...]
    g = g_ref[...]
    mul = 1.0 + sc_ref[...]
    sh = sh_ref[...]
    for s in range(tm // sub):
        rows = pl.ds(s * sub, sub)
        x_ref[rows, :] = res_ref[rows, :] + gate * _mm(a_ref[rows, :], w_ref[...])
        _modulate_rows(x_ref, xn_ref, g, mul, sh, s * sub, sub)


def mm_res_norm(a, w, wl, res, mod, g_next, *, layer, which_gate, next_layer, next_shift, next_scale,
                rows_per_batch, row0, tm=512, sub=256):
    m, k = a.shape
    d = w.shape[2]
    tm = min(tm, rows_per_batch)
    mk = (rows_per_batch, tm, row0)
    return pl.pallas_call(
        functools.partial(_mmres_kernel, tm=tm, sub=sub),
        out_shape=(jax.ShapeDtypeStruct((m, d), F32), jax.ShapeDtypeStruct((m, d), BF16)),
        grid=(m // tm,),
        in_specs=[
            pl.BlockSpec((tm, k), lambda i: (i, 0)),
            pl.BlockSpec((None, k, d), lambda i: (wl, 0, 0), pipeline_mode=pl.Buffered(1)),
            pl.BlockSpec((tm, d), lambda i: (i, 0)),
            pl.BlockSpec((None, 1, d), _mod_row(layer, which_gate, *mk)),
            pl.BlockSpec((None, 1, d), lambda i: (next_layer, 0, 0)),
            pl.BlockSpec((None, 1, d), _mod_row(next_layer, next_shift, *mk)),
            pl.BlockSpec((None, 1, d), _mod_row(next_layer, next_scale, *mk)),
        ],
        out_specs=(pl.BlockSpec((tm, d), lambda i: (i, 0)), pl.BlockSpec((tm, d), lambda i: (i, 0))),
        compiler_params=_params(("parallel",)),
        name="outproj_res_norm",
    )(a, w, res, mod, g_next, mod, mod)


def _mlp_kernel(xn_ref, w1_ref, w2_ref, res_ref, gate_ref, g_ref, sh_ref, sc_ref, x_ref, *rest, tm, sub, emit_next):
    xno_ref = rest[0] if emit_next else None
    acc_ref = rest[-1]
    f = pl.program_id(1)

    @pl.when(f == 0)
    def _():
        acc_ref[...] = jnp.zeros_like(acc_ref)

    def hidden_chunk(rows):
        h = _mm(xn_ref[rows, :], w1_ref[...])
        return _mm(jnp.square(jnp.maximum(h, 0.0)).astype(BF16), w2_ref[...])

    last = pl.num_programs(1) - 1

    @pl.when(f < last)
    def _():
        acc_ref[...] += hidden_chunk(slice(None))

    @pl.when(f == last)
    def _():
        gate = gate_ref[...]
        g = g_ref[...]
        mul = 1.0 + sc_ref[...]
        sh = sh_ref[...]
        for s in range(tm // sub):
            rows = pl.ds(s * sub, sub)
            x_ref[rows, :] = res_ref[rows, :] + gate * (acc_ref[rows, :] + hidden_chunk(rows))
            if emit_next:
                _modulate_rows(x_ref, xno_ref, g, mul, sh, s * sub, sub)


def fused_mlp(xn, w1, w2, res, mod, g_next, *, layer, next_layer, rows_per_batch, row0, emit_next,
              tm=512, sub=128):
    m, d = xn.shape
    fc = w1.shape[2]
    ff = w1.shape[0] * fc
    tm = min(tm, rows_per_batch)
    mk = (rows_per_batch, tm, row0)
    row_tile = pl.BlockSpec((tm, d), lambda i, f: (i, 0))
    out_shape = [jax.ShapeDtypeStruct((m, d), F32)] + ([jax.ShapeDtypeStruct((m, d), BF16)] if emit_next else [])
    outs = pl.pallas_call(
        functools.partial(_mlp_kernel, tm=tm, sub=min(sub, tm), emit_next=emit_next),
        out_shape=tuple(out_shape),
        grid=(m // tm, ff // fc),
        in_specs=[
            pl.BlockSpec((tm, d), lambda i, f: (i, 0)),
            pl.BlockSpec((None, d, fc), lambda i, f: (f, 0, 0)),
            pl.BlockSpec((fc, d), lambda i, f: (f, 0)),
            row_tile,
            pl.BlockSpec((None, 1, d), _mod_row(layer, 5, *mk)),
            pl.BlockSpec((None, 1, d), lambda i, f: (next_layer, 0, 0)),
            pl.BlockSpec((None, 1, d), _mod_row(next_layer, 0, *mk)),
            pl.BlockSpec((None, 1, d), _mod_row(next_layer, 1, *mk)),
        ],
        out_specs=tuple(row_tile for _ in out_shape),
        scratch_shapes=[pltpu.VMEM((tm, d), F32)],
        compiler_params=_params(("parallel", "arbitrary")),
        name="fused_mlp",
    )(xn, w1, w2, res, mod, g_next, mod, mod)
    return (outs[0], outs[1]) if emit_next else (outs[0], None)


def _head_norm(x, g):
    return x * lax.rsqrt(jnp.mean(x * x, axis=-1, keepdims=True) + EPS) * g


_Q_SCALE = HEAD_DIM ** -0.5 * LOG2E


def _with_ones_column(v):
    lane = lax.broadcasted_iota(jnp.int32, v.shape, 1)
    return jnp.concatenate([v.astype(BF16), jnp.where(lane == 0, 1.0, 0.0).astype(BF16)], axis=1)


def _softmax2_pv(s, vaug):
    p = jnp.exp2(s - jnp.max(s, axis=-1, keepdims=True)).astype(BF16)
    pv = _mm(p, vaug)
    return pv[:, :HEAD_DIM] / pv[:, HEAD_DIM:HEAD_DIM + 1]


def _rope(x, cos, sin_signed, lane_is_first):
    swapped = jnp.where(lane_is_first, pltpu.roll(x, 96, 1), pltpu.roll(x, 32, 1))
    return x * cos + swapped * sin_signed


def _ctx_attn_kernel(q_ref, k_ref, v_ref, qg_ref, kg_ref, o_ref, kn_ref, vo_ref, *, groups, heads):
    hd = HEAD_DIM
    qg = qg_ref[...] * _Q_SCALE
    for h in range(heads):
        hs = slice(h * hd, (h + 1) * hd)
        kn = _head_norm(k_ref[:, hs], kg_ref[...])
        kn_ref[:, hs] = kn
        v = v_ref[:, hs]
        vo_ref[:, hs] = v
        kb = kn.astype(BF16)
        vb = v.astype(BF16)
        for g in range(groups):
            qs = slice((h * groups + g) * hd, (h * groups + g + 1) * hd)
            qn = _head_norm(q_ref[:, qs], qg).astype(BF16)
            s = _nt(qn, kb)
            p = jnp.exp2(s - jnp.max(s, axis=-1, keepdims=True))
            o_ref[:, qs] = (_mm(p.astype(BF16), vb) / jnp.sum(p, axis=-1, keepdims=True)).astype(o_ref.dtype)


def ctx_attention(qkv, q_g, k_g, *, batch, seq, n_q, n_kv, heads=4):
    m = qkv.shape[0]
    groups = n_q // n_kv
    hd = HEAD_DIM
    kw = heads * hd
    qw = heads * groups * hd
    assert (n_q * hd) % kw == 0 and ((n_q + n_kv) * hd) % kw == 0
    k_blk0 = n_q * hd // kw
    v_blk0 = (n_q + n_kv) * hd // kw
    kern = functools.partial(_ctx_attn_kernel, groups=groups, heads=heads)
    return pl.pallas_call(
        kern,
        out_shape=(jax.ShapeDtypeStruct((m, n_q * hd), BF16),
                   jax.ShapeDtypeStruct((m, n_kv * hd), F32),
                   jax.ShapeDtypeStruct((m, n_kv * hd), F32)),
        grid=(batch, n_kv // heads),
        in_specs=[
            pl.BlockSpec((seq, qw), lambda b, h: (b, h)),
            pl.BlockSpec((seq, kw), lambda b, h: (b, k_blk0 + h)),
            pl.BlockSpec((seq, kw), lambda b, h: (b, v_blk0 + h)),
            pl.BlockSpec((1, hd), lambda b, h: (0, 0)),
            pl.BlockSpec((1, hd), lambda b, h: (0, 0)),
        ],
        out_specs=(pl.BlockSpec((seq, qw), lambda b, h: (b, h)),
                   pl.BlockSpec((seq, kw), lambda b, h: (b, h)),
                   pl.BlockSpec((seq, kw), lambda b, h: (b, h))),
        compiler_params=_params(("parallel", "parallel")),
        name="ctx_attention",
    )(qkv, qkv, qkv, q_g.reshape(1, hd), k_g.reshape(1, hd))


def _gqa_kernel(q_ref, k_ref, v_ref, ck_ref, cv_ref, qg_ref, kg_ref, cosq_ref, sinq_ref, cosk_ref, sink_ref,
                o_ref, kall_ref, vall_ref, *, groups, seq, past):
    lane = lax.broadcasted_iota(jnp.int32, (1, HEAD_DIM), 1)
    first = (lane % 64) < 32

    @pl.when(pl.program_id(2) == 0)
    def _():
        kn = _head_norm(k_ref[...], kg_ref[...])
        kall_ref[0:seq, :] = _rope(kn, cosk_ref[...], sink_ref[...], first).astype(BF16)
        kall_ref[seq:seq + past, :] = ck_ref[...].astype(BF16)
        vall_ref[0:seq, :] = _with_ones_column(v_ref[...])
        vall_ref[seq:seq + past, :] = _with_ones_column(cv_ref[...])

    cos = cosq_ref[...]
    sin = sinq_ref[...]
    qg = qg_ref[...] * _Q_SCALE
    for g in range(groups):
        qn = _head_norm(q_ref[:, g * HEAD_DIM:(g + 1) * HEAD_DIM], qg)
        qr = _rope(qn, cos, sin, first).astype(BF16)
        o_ref[:, g * HEAD_DIM:(g + 1) * HEAD_DIM] = _softmax2_pv(_nt(qr, kall_ref[...]), vall_ref[...]).astype(o_ref.dtype)


def gqa_attention(qkv, cache_k, cache_v, q_g, k_g, cos, sin, *, batch, seq, past, n_q, n_kv, tq=256):
    m = qkv.shape[0]
    groups = n_q // n_kv
    hd = HEAD_DIM
    nqb = seq // tq
    kern = functools.partial(_gqa_kernel, groups=groups, seq=seq, past=past)
    return pl.pallas_call(
        kern,
        out_shape=jax.ShapeDtypeStruct((m, n_q * hd), BF16),
        grid=(batch, n_kv, nqb),
        in_specs=[
            pl.BlockSpec((tq, groups * hd), lambda b, h, i: (b * nqb + i, h)),
            pl.BlockSpec((seq, hd), lambda b, h, i: (b, n_q + h)),
            pl.BlockSpec((seq, hd), lambda b, h, i: (b, n_q + n_kv + h)),
            pl.BlockSpec((past, hd), lambda b, h, i: (b, h)),
            pl.BlockSpec((past, hd), lambda b, h, i: (b, h)),
            pl.BlockSpec((1, hd), lambda b, h, i: (0, 0)),
            pl.BlockSpec((1, hd), lambda b, h, i: (0, 0)),
            pl.BlockSpec((tq, hd), lambda b, h, i: (i, 0)),
            pl.BlockSpec((tq, hd), lambda b, h, i: (i, 0)),
            pl.BlockSpec((seq, hd), lambda b, h, i: (0, 0)),
            pl.BlockSpec((seq, hd), lambda b, h, i: (0, 0)),
        ],
        out_specs=pl.BlockSpec((tq, groups * hd), lambda b, h, i: (b * nqb + i, h)),
        scratch_shapes=[pltpu.VMEM((seq + past, hd), BF16), pltpu.VMEM((seq + past, 2 * hd), BF16)],
        compiler_params=_params(("parallel", "parallel", "arbitrary")),
        name="gqa_attention",
    )(qkv, qkv, qkv, cache_k, cache_v, q_g.reshape(1, hd), k_g.reshape(1, hd), cos, sin, cos, sin)


def _na_kernel(rpb_ref, q_ref, k_ref, v_ref, ck_ref, cv_ref, qg_ref, kg_ref, o_ref,
               qn_ref, kn_ref, vb_ref, ckb_ref, cvb_ref, tile_ref, pair_ref, *, rows, wr, n_dr, n_dc, unroll):
    h = pl.program_id(0)
    w = GRID_W

    qn_ref[...] = _head_norm(q_ref[...], qg_ref[...] * _Q_SCALE).astype(BF16)
    kn_ref[...] = _head_norm(k_ref[...], kg_ref[...]).astype(BF16)
    vb_ref[...] = _with_ones_column(v_ref[...])
    ckb_ref[...] = ck_ref[...].astype(BF16)
    cvb_ref[...] = _with_ones_column(cv_ref[...])

    @pl.when(pl.program_id(1) == 0)
    def _():
        qc = lax.broadcasted_iota(jnp.int32, (w, 2 * w), 0)
        lane = lax.broadcasted_iota(jnp.int32, (w, 2 * w), 1)
        kc = lane % w
        cs = jnp.clip(qc - WIN_COLS // 2, 0, w - WIN_COLS)
        col_ok = (kc >= cs) & (kc < cs + WIN_COLS)
        dc = jnp.clip(kc - qc + WIN_COLS - 1, 0, n_dc - 1)
        tiles = [jnp.zeros((w, 2 * w), F32) for _ in range(n_dr)]
        for d in range(n_dc):
            sel = dc == d
            for dr in range(n_dr):
                tiles[dr] = jnp.where(sel, rpb_ref[h, dr * n_dc + d], tiles[dr])
        for dr in range(n_dr):
            tile_ref[dr] = jnp.where(col_ok, tiles[dr] * LOG2E, NEG_INF)
        for dr in range(n_dr - 1):
            pair_ref[dr] = jnp.where(lane < w, tile_ref[dr], tile_ref[dr + 1])

    zero_bias = jnp.zeros((w, ckb_ref.shape[0]), F32)

    def body(i, _):
        rws = [i * unroll + u for u in range(unroll)]
        rss = [jnp.clip(r - wr // 2, 0, rows - wr) for r in rws]
        k0s = [pl.multiple_of(rs * w, w) for rs in rss]
        scores = []
        for r, rs, k0 in zip(rws, rss, k0s):
            dr0 = rs - r + WIN_ROWS - 1
            q_r = qn_ref[pl.ds(pl.multiple_of(r * w, w), w), :]
            kcat = jnp.concatenate([kn_ref[pl.ds(k0, wr * w), :], ckb_ref[...]], axis=0)
            bias = jnp.concatenate([pair_ref[dr0 + 2 * j] for j in range(wr // 2)] + [zero_bias], axis=1)
            scores.append(_nt(q_r, kcat) + bias)
        probs = [jnp.exp2(s - jnp.max(s, axis=-1, keepdims=True)).astype(BF16) for s in scores]
        for r, k0, p in zip(rws, k0s, probs):
            vcat = jnp.concatenate([vb_ref[pl.ds(k0, wr * w), :], cvb_ref[...]], axis=0)
            pv = _mm(p, vcat)
            o = pv[:, :HEAD_DIM] / pv[:, HEAD_DIM:HEAD_DIM + 1]
            o_ref[pl.ds(pl.multiple_of(r * w, w), w), :] = o.astype(o_ref.dtype)
        return 0

    lax.fori_loop(0, rows // unroll, body, 0)


def na_attention(qkv, cache_k, cache_v, q_g, k_g, rpb, *, batch, seq, past, n_heads, unroll=16):
    m = qkv.shape[0]
    hd = HEAD_DIM
    rows = seq // GRID_W
    wr = min(WIN_ROWS, rows)
    n_dr, n_dc = rpb.shape[1], rpb.shape[2]
    assert wr == WIN_ROWS and wr % 2 == 0 and rows % unroll == 0
    kern = functools.partial(_na_kernel, rows=rows, wr=wr, n_dr=n_dr, n_dc=n_dc, unroll=unroll)
    return pl.pallas_call(
        kern,
        out_shape=jax.ShapeDtypeStruct((m, n_heads * hd), BF16),
        grid=(n_heads, batch),
        in_specs=[
            pl.BlockSpec(memory_space=pltpu.SMEM),
            pl.BlockSpec((seq, hd), lambda h, b: (b, h)),
            pl.BlockSpec((seq, hd), lambda h, b: (b, n_heads + h)),
            pl.BlockSpec((seq, hd), lambda h, b: (b, 2 * n_heads + h)),
            pl.BlockSpec((past, hd), lambda h, b: (b, h)),
            pl.BlockSpec((past, hd), lambda h, b: (b, h)),
            pl.BlockSpec((1, hd), lambda h, b: (0, 0)),
            pl.BlockSpec((1, hd), lambda h, b: (0, 0)),
        ],
        out_specs=pl.BlockSpec((seq, hd), lambda h, b: (b, h)),
        scratch_shapes=[
            pltpu.VMEM((seq, hd), BF16), pltpu.VMEM((seq, hd), BF16), pltpu.VMEM((seq, 2 * hd), BF16),
            pltpu.VMEM((past, hd), BF16), pltpu.VMEM((past, 2 * hd), BF16),
            pltpu.VMEM((n_dr, GRID_W, 2 * GRID_W), F32),
            pltpu.VMEM((n_dr - 1, GRID_W, 2 * GRID_W), F32),
        ],
        compiler_params=_params(("parallel", "arbitrary")),
        name="na_attention",
    )(rpb.reshape(n_heads, n_dr * n_dc), qkv, qkv, qkv, cache_k, cache_v,
      q_g.reshape(1, hd), k_g.reshape(1, hd))


def _gates_kernel(xn_ref, w_ref, b_ref, out_ref, *, tm):
    L = MLSTM_CHUNK
    wh, wl = _split2(w_ref[...])
    xn = xn_ref[...]
    pre = _nt(wh, xn) + _nt(wl, xn) + b_ref[...]
    capped = GATE_SOFTCAP * jnp.tanh(pre / GATE_SOFTCAP)
    row = lax.broadcasted_iota(jnp.int32, (capped.shape[0], L), 0) % 8
    is_input = (row == 0) | (row == 2)
    ri = lax.broadcasted_iota(jnp.int32, (L, L), 0)
    ci = lax.broadcasted_iota(jnp.int32, (L, L), 1)
    upper = jnp.where(ri <= ci, 1.0, 0.0).astype(BF16)
    lower = jnp.where(ri >= ci, 1.0, 0.0).astype(BF16)
    for c in range(tm // L):
        cap = capped[:, c * L:(c + 1) * L]
        gates = jnp.where(is_input, cap, jax.nn.log_sigmoid(cap))
        pieces = _split3(gates)
        prefix = sum(_mm(p, upper) for p in pieces)
        suffix = sum(_mm(p, lower) for p in pieces)
        out_ref[c] = jnp.where(row == 4, prefix, jnp.where(row == 5, suffix, gates))


def mlstm_gates(xn, w_gate_t, b_gate_t, *, tm=512):
    m, d = xn.shape
    gh = w_gate_t.shape[0]
    return pl.pallas_call(
        functools.partial(_gates_kernel, tm=tm),
        out_shape=jax.ShapeDtypeStruct((m // MLSTM_CHUNK, gh, MLSTM_CHUNK), F32),
        grid=(m // tm,),
        in_specs=[
            pl.BlockSpec((tm, d), lambda i: (i, 0)),
            pl.BlockSpec((gh, d), lambda i: (0, 0)),
            pl.BlockSpec((gh, 1), lambda i: (0, 0)),
        ],
        out_specs=pl.BlockSpec((tm // MLSTM_CHUNK, gh, MLSTM_CHUNK), lambda i: (i, 0, 0)),
        compiler_params=_params(("parallel",)),
        name="mlstm_gates",
    )(xn, w_gate_t, b_gate_t)


def _mlstm_kernel(*refs, n_chunks, heads, zero_init, write_state, has_prev_states):
    it = iter(refs)
    q_ref, k_ref, v_ref, o_ref, gates_ref, hg_ref = (next(it) for _ in range(6))
    if not zero_init:
        c0_ref, n0_ref, m0_ref = (next(it) for _ in range(3))
    if has_prev_states:
        next(it)
    y_ref = next(it)
    if write_state:
        cf_ref, nf_ref, mf_ref = (next(it) for _ in range(3))
    hdir_ref, cst_ref = next(it), next(it)

    L, dk, dv = MLSTM_CHUNK, MLSTM_DK, MLSTM_DV
    ri = lax.broadcasted_iota(jnp.int32, (L, L), 0)
    ci = lax.broadcasted_iota(jnp.int32, (L, L), 1)
    masks = (ri >= ci, ri <= ci)
    eye = ri == ci
    eye_b = jnp.where(eye, 1.0, 0.0).astype(BF16)
    ones_blk = jnp.ones((L, 128), BF16)
    qscale = dk ** -0.5

    m_init = []
    for hh in range(heads):
        for direction in range(2):
            idx = hh * 2 + direction
            if zero_init:
                cst_ref[idx] = jnp.zeros((dk, dv + 128), F32)
                m_init.append(jnp.zeros((1, 1), F32))
            else:
                n_row = jnp.broadcast_to(n0_ref[direction, hh], (dk, dk))
                n_col = jnp.sum(jnp.where(eye, n_row, 0.0), axis=-1, keepdims=True)
                cst_ref[idx, :, 0:dv] = c0_ref[direction, hh]
                cst_ref[idx, :, dv:dv + 128] = jnp.broadcast_to(n_col, (dk, 128))
                m_init.append(m0_ref[direction, hh])

    def body(step, ms):
        chains = [(hh, direction) for hh in range(heads) for direction in range(2)]
        r0s = [pl.multiple_of((step if d == 0 else n_chunks - 1 - step) * L, L) for _, d in chains]
        cs = [step if d == 0 else n_chunks - 1 - step for _, d in chains]

        st1 = []
        for idx, (hh, d) in enumerate(chains):
            gt = gates_ref[cs[idx], hh * 8:(hh + 1) * 8, :]
            li, lf, brow = gt[2 * d:2 * d + 1], gt[2 * d + 1:2 * d + 2], gt[4 + d:5 + d]
            b3 = _nt(eye_b, jnp.concatenate(_split3(jnp.broadcast_to(brow, (L, L))), axis=0))
            bmat = b3[:, :L] + b3[:, L:2 * L] + b3[:, 2 * L:]
            log_d = jnp.where(masks[d], bmat - brow + li, NEG_INF)
            inter = bmat + ms[idx]
            m_j = jnp.maximum(inter, jnp.max(log_d, axis=-1, keepdims=True))
            tot = jnp.sum(lf, axis=-1, keepdims=True)
            log_w = tot - brow + li
            m_new = jnp.maximum(tot + ms[idx], jnp.max(log_w, axis=-1, keepdims=True))
            st1.append((log_d, inter, m_j, tot, log_w, m_new))

        st2 = []
        for idx, (hh, d) in enumerate(chains):
            log_d, inter, m_j, tot, log_w, m_new = st1[idx]
            q = (q_ref[pl.ds(r0s[idx], L), hh * dk:(hh + 1) * dk] * qscale).astype(BF16)
            k = k_ref[pl.ds(r0s[idx], L), hh * dk:(hh + 1) * dk]
            k_hi, k_lo = _split2(k)
            qk = _nt(q, k_hi)
            kt2 = _nt(eye_b, jnp.concatenate([k_hi, k_lo], axis=0))
            kt = kt2[:, :L] + kt2[:, L:]
            qc = _mm(q, cst_ref[idx].astype(BF16))
            d_mat = jnp.exp(log_d - m_j)
            w_inter = jnp.exp(inter - m_j)
            wt = jnp.exp(log_w - m_new)
            decay = jnp.exp(tot + ms[idx] - m_new)
            ktw = (kt * wt).astype(BF16)
            st2.append((qk * d_mat, qc, w_inter, decay, ktw))

        for idx, (hh, d) in enumerate(chains):
            s, qc, w_inter, decay, ktw = st2[idx]
            floor = jnp.exp(-st1[idx][2])
            vaug = jnp.concatenate([v_ref[pl.ds(r0s[idx], L), hh * dv:(hh + 1) * dv].astype(BF16), ones_blk], axis=1)
            both = _mm(jnp.concatenate([s.astype(BF16), ktw], axis=0), vaug)
            nd = jnp.concatenate([w_inter] * 3, axis=1) * qc + both[:L]
            den = jnp.maximum(jnp.abs(nd[:, dv:dv + 128]), floor)
            h = nd[:, :dv] / jnp.concatenate([den, den], axis=1)
            hdir_ref[d, pl.ds(r0s[idx], L), hh * dv:(hh + 1) * dv] = h
            cst_ref[idx] = decay * cst_ref[idx] + both[L:]
        return tuple(s1[5] for s1 in st1)

    m_fin = lax.fori_loop(0, n_chunks, body, tuple(m_init))

    def finish(c, _):
        r0 = pl.multiple_of(c * L, L)
        for hh in range(heads):
            cols = slice(hh * dv, (hh + 1) * dv)
            hs = hdir_ref[0, pl.ds(r0, L), cols] + hdir_ref[1, pl.ds(r0, L), cols]
            hn = hs * lax.rsqrt(jnp.mean(hs * hs, axis=-1, keepdims=True) + EPS) * hg_ref[:, cols]
            y_ref[pl.ds(r0, L), cols] = (hn * jax.nn.sigmoid(o_ref[pl.ds(r0, L), cols])).astype(y_ref.dtype)
        return 0

    lax.fori_loop(0, n_chunks, finish, 0)

    if write_state:
        for hh in range(heads):
            for direction in range(2):
                idx = hh * 2 + direction
                cf_ref[direction, hh] = cst_ref[idx, :, 0:dv]
                n_b = cst_ref[idx, :, dv:dv + 128]
                nf_ref[direction, hh] = jnp.sum(jnp.where(eye, n_b, 0.0), axis=0, keepdims=True)
                mf_ref[direction, hh] = m_fin[idx]


def mlstm_scan(proj, gates, head_g, state=None, *, batch, seq, n_heads, write_state, heads=2,
               state_slot=(0, 1), prev_states=None):
    m = proj.shape[0]
    dk, dv, L = MLSTM_DK, MLSTM_DV, MLSTM_CHUNK
    H = n_heads
    nb = H // heads
    n_chunks = seq // L
    zero_init = state is None
    slot, n_slots = state_slot
    kern = functools.partial(_mlstm_kernel, n_chunks=n_chunks, heads=heads, zero_init=zero_init,
                             write_state=write_state, has_prev_states=prev_states is not None)
    in_specs = [
        pl.BlockSpec((seq, heads * dk), lambda b, h: (b, h)),
        pl.BlockSpec((seq, heads * dk), lambda b, h: (b, nb + h)),
        pl.BlockSpec((seq, heads * dv), lambda b, h: (b, nb + h)),
        pl.BlockSpec((seq, heads * dv), lambda b, h: (b, 2 * nb + h)),
        pl.BlockSpec((n_chunks, heads * 8, L), lambda b, h: (b, h, 0)),
        pl.BlockSpec((None, 1, heads * dv), lambda b, h: (h, 0, 0)),
    ]
    args = [proj, proj, proj, proj, gates, head_g.reshape(nb, 1, heads * dv)]
    st_specs = [pl.BlockSpec((None, 2, heads, dk, dv), lambda b, h: (b, 0, h, 0, 0)),
                pl.BlockSpec((None, 2, heads, 1, dk), lambda b, h: (b, 0, h, 0, 0)),
                pl.BlockSpec((None, 2, heads, 1, 1), lambda b, h: (b, 0, h, 0, 0))]
    if not zero_init:
        c0_all, c0_slot, n0, m0 = state
        in_specs += [pl.BlockSpec((None, None, 2, heads, dk, dv), lambda b, h: (b, c0_slot, 0, h, 0, 0))] + st_specs[1:]
        args += [c0_all.astype(F32), n0.astype(F32).reshape(batch, 2, H, 1, dk),
                 m0.astype(F32).reshape(batch, 2, H, 1, 1)]
    aliases = {}
    if prev_states is not None:
        aliases = {len(args): 1}
        in_specs.append(pl.BlockSpec(memory_space=pl.ANY))
        args.append(prev_states)
    out_shape = [jax.ShapeDtypeStruct((m, H * dv), BF16)]
    out_specs = [pl.BlockSpec((seq, heads * dv), lambda b, h: (b, h))]
    if write_state:
        out_shape += [jax.ShapeDtypeStruct((batch, n_slots, 2, H, dk, dv), F32),
                      jax.ShapeDtypeStruct((batch, 2, H, 1, dk), F32),
                      jax.ShapeDtypeStruct((batch, 2, H, 1, 1), F32)]
        out_specs += [pl.BlockSpec((None, None, 2, heads, dk, dv), lambda b, h: (b, slot, 0, h, 0, 0))] + st_specs[1:]
    outs = pl.pallas_call(
        kern,
        out_shape=tuple(out_shape),
        grid=(batch, nb),
        in_specs=in_specs,
        out_specs=tuple(out_specs),
        scratch_shapes=[pltpu.VMEM((2, seq, heads * dv), F32), pltpu.VMEM((2 * heads, dk, dv + 128), F32)],
        input_output_aliases=aliases,
        compiler_params=_params(("parallel", "parallel")),
        name="mlstm_scan",
    )(*args)
    if write_state:
        y, cf, nf, mf = outs
        return y, (cf, nf.reshape(batch, 2, H, dk), mf.reshape(batch, 2, H))
    return outs[0], None


def _gate_weights_head_major(w_gate, b_gate, n_heads):
    d = w_gate.shape[0]
    wt = jnp.transpose(w_gate.reshape(d, 4, n_heads), (2, 1, 0)).astype(F32)
    wt = jnp.concatenate([wt, wt[:, 1:2], wt[:, 3:4], jnp.zeros_like(wt[:, :2])], axis=1).reshape(8 * n_heads, d)
    bt = jnp.transpose(b_gate.reshape(4, n_heads), (1, 0)).astype(F32)
    bt = jnp.concatenate([bt, bt[:, 1:2], bt[:, 3:4], jnp.zeros_like(bt[:, :2])], axis=1).reshape(8 * n_heads, 1)
    return wt, bt


def _rope_tables(seq):
    nf = HEAD_DIM // 4
    t = jnp.arange(seq)
    inv = ROPE_THETA ** (-jnp.arange(nf, dtype=F32) / nf)
    pos = jnp.stack([t // GRID_W, t % GRID_W], axis=-1).astype(F32)
    ang = pos[:, :, None] * inv
    cos = jnp.cos(ang)
    sin = jnp.sin(ang)
    cos_full = jnp.stack([cos, cos], axis=2).reshape(seq, HEAD_DIM)
    sin_full = jnp.stack([-sin, sin], axis=2).reshape(seq, HEAD_DIM)
    return cos_full, sin_full


def kernel(x_prompt, x_sample, state_mlstm_C, state_mlstm_n, state_mlstm_m, cache_gqa_k, cache_gqa_v,
           cache_na_k, cache_na_v, c, c_ctx, norm1_g, norm2_g, w_ada, b_ada, w_mlp1, w_mlp2,
           mlstm_w_in, mlstm_w_gate, mlstm_b_gate, mlstm_head_g, mlstm_w_out,
           gqa_w_qkv, gqa_q_g, gqa_k_g, gqa_w_o, na_w_qkv, na_q_g, na_k_g, na_rpb, na_w_o):
    bp, tp, d = x_prompt.shape
    bs, ts, _ = x_sample.shape
    depth = w_ada.shape[0]
    past = cache_gqa_k.shape[2]
    mh = mlstm_w_gate.shape[-1] // 4
    gqa_kv = cache_gqa_k.shape[3]
    gqa_q = gqa_w_o.shape[1] // HEAD_DIM
    na_h = cache_na_k.shape[3]
    assert bs + 1 <= 8

    cond8 = jnp.zeros((8, d), F32).at[:bs].set(c).at[bs].set(c_ctx)
    mod = adaln_all(cond8, w_ada, b_ada).reshape(depth * 48, 1, d)
    g1 = norm1_g.reshape(depth, 1, d)
    g2 = norm2_g.reshape(depth, 1, d)
    cos, sin = _rope_tables(ts)

    groups = [
        dict(x=x_prompt.reshape(bp * tp, d), batch=bp, seq=tp, rpb_=bp * tp, row0=bs, ctx=True),
        dict(x=x_sample.reshape(bs * ts, d), batch=bs, seq=ts, rpb_=ts, row0=0, ctx=False),
    ]
    for grp in groups:
        grp["xn"] = modulate(grp["x"], g1, mod, layer=0, which_shift=0, which_scale=1,
                             rows_per_batch=grp["rpb_"], row0=grp["row0"])
    w_in_all =[_column_tiles(w, PROJ_COL_TILE) for w in (mlstm_w_in, gqa_w_qkv, na_w_qkv)]
    w_o_all = [mlstm_w_out.astype(BF16), gqa_w_o.astype(BF16), na_w_o.astype(BF16)]
    n_mlstm = state_mlstm_C.shape[1]
    c_all = jnp.zeros((bp, n_mlstm, 2, mh, MLSTM_DK, MLSTM_DV), F32)
    n_l, m_l, gk_l, gv_l, nk_l, nv_l = [], [], [], [], [], []

    for i in range(depth):
        kind, j = i % N_MIXERS, i // N_MIXERS
        last = i == depth - 1
        nxt = i if last else i + 1
        w_in, w_o = w_in_all[kind], w_o_all[kind]
        if kind == 0:
            wg_t, bg_t = _gate_weights_head_major(mlstm_w_gate[j], mlstm_b_gate[j], mh)

        for grp in groups:
            x, xn, batch, seq, ctx = grp["x"], grp["xn"], grp["batch"], grp["seq"], grp["ctx"]
            mk = dict(rows_per_batch=grp["rpb_"], row0=grp["row0"])
            if ctx:
                proj, w1, w2 = matmul_and_cast(xn, w_in, j, w_mlp1, w_mlp2, i, fc=MLP_HIDDEN_CHUNK)
            else:
                proj = matmul(xn, w_in, j)
            if kind == 0:
                gates = mlstm_gates(xn, wg_t, bg_t)
                if ctx:
                    a, (c_all, nf, mf) = mlstm_scan(proj, gates, mlstm_head_g[j], None, batch=batch, seq=seq,
                                                    n_heads=mh, write_state=True, heads=4,
                                                    state_slot=(j, n_mlstm), prev_states=c_all)
                    n_l.append(nf)
                    m_l.append(mf)
                else:
                    st = (state_mlstm_C, j, state_mlstm_n[:, j], state_mlstm_m[:, j])
                    a, _ = mlstm_scan(proj, gates, mlstm_head_g[j], st, batch=batch, seq=seq,
                                      n_heads=mh, write_state=False)
            elif kind == 1:
                if ctx:
                    a, kn, v = ctx_attention(proj, gqa_q_g[j], gqa_k_g[j], batch=batch, seq=seq, n_q=gqa_q, n_kv=gqa_kv)
                    gk_l.append(kn.reshape(batch, seq, gqa_kv, HEAD_DIM))
                    gv_l.append(v.reshape(batch, seq, gqa_kv, HEAD_DIM))
                else:
                    ck = cache_gqa_k[:, j].astype(F32).reshape(batch * past, gqa_kv * HEAD_DIM)
                    cv = cache_gqa_v[:, j].astype(F32).reshape(batch * past, gqa_kv * HEAD_DIM)
                    a = gqa_attention(proj, ck, cv, gqa_q_g[j], gqa_k_g[j], cos, sin, batch=batch, seq=seq,
                                      past=past, n_q=gqa_q, n_kv=gqa_kv)
            else:
                if ctx:
                    a, kn, v = ctx_attention(proj, na_q_g[j], na_k_g[j], batch=batch, seq=seq, n_q=na_h, n_kv=na_h,
                                             heads=8)
                    nk_l.append(kn.reshape(batch, seq, na_h, HEAD_DIM))
                    nv_l.append(v.reshape(batch, seq, na_h, HEAD_DIM))
                else:
                    ck = cache_na_k[:, j].astype(F32).reshape(batch * past, na_h * HEAD_DIM)
                    cv = cache_na_v[:, j].astype(F32).reshape(batch * past, na_h * HEAD_DIM)
                    a = na_attention(proj, ck, cv, na_q_g[j], na_k_g[j], na_rpb[j], batch=batch, seq=seq,
                                     past=past, n_heads=na_h)
            x, xn = mm_res_norm(a, w_o, j, x, mod, g2, layer=i, which_gate=2, next_layer=i, next_shift=3,
                                next_scale=4, **mk)
            x, xn = fused_mlp(xn, w1, w2, x, mod, g1, layer=i, next_layer=nxt, emit_next=not last, **mk)
            grp["x"], grp["xn"] = x, xn

    y_prompt = groups[0]["x"].reshape(bp, tp, d)
    y_sample = groups[1]["x"].reshape(bs, ts, d)
    return (y_prompt, y_sample, c_all, jnp.stack(n_l, axis=1), jnp.stack(m_l, axis=1),
            jnp.stack(gk_l, axis=1), jnp.stack(gv_l, axis=1), jnp.stack(nk_l, axis=1), jnp.stack(nv_l, axis=1))
```

```python
import functools
import math

import jax
import jax.numpy as jnp
from jax import lax
from jax.experimental import pallas as pl
from jax.experimental.pallas import tpu as pltpu

EPS = 1e-6
HEAD_DIM = 128
MLSTM_DK = 128
MLSTM_DV = 256
MLSTM_CHUNK = 128
GATE_SOFTCAP = 15.0
GRID_W = 64
WIN_ROWS = 8
WIN_COLS = 16
ROPE_THETA = 10000.0
N_MIXERS = 3
LOG2E = math.log2(math.e)

VMEM_LIMIT_BYTES = 56 * 1024 * 1024
NORM_ROW_CHUNK = 16
MLP_HIDDEN_CHUNK = 1024
PROJ_COL_TILE = 768

F32 = jnp.float32
BF16 = jnp.bfloat16
NEG_INF = float("-inf")


def _params(sem):
    return pltpu.CompilerParams(dimension_semantics=sem, vmem_limit_bytes=VMEM_LIMIT_BYTES)


def _nt(a, b):
    return lax.dot_general(a, b, (((1,), (1,)), ((), ())), preferred_element_type=F32)


def _mm(a, b):
    return jnp.dot(a, b, preferred_element_type=F32)


def _split2(x):
    hi = x.astype(BF16)
    lo = (x - hi.astype(F32)).astype(BF16)
    return hi, lo


def _split3(x):
    hi = x.astype(BF16)
    r = x - hi.astype(F32)
    mid = r.astype(BF16)
    lo = (r - mid.astype(F32)).astype(BF16)
    return hi, mid, lo


def _mod_row(layer, which, rows_per_batch, tm, row0):
    def f(i, *_):
        b = row0 + (i * tm) // rows_per_batch
        return (layer * 48 + b * 6 + which, 0, 0)
    return f


def _adaln_kernel(cond_ref, w_ref, b_ref, out_ref):
    c = cond_ref[...]
    s = c * jax.nn.sigmoid(c)
    sh, sl = _split2(s)
    wh, wl = _split2(w_ref[...])
    lhs = jnp.concatenate([sh, sl], axis=0)
    r = _mm(lhs, wh)
    out_ref[...] = r[:8] + r[8:] + _mm(sh, wl) + b_ref[...]


def adaln_all(cond8, w_ada, b_ada, tn=1024):
    nl, d, n = w_ada.shape
    return pl.pallas_call(
        _adaln_kernel,
        out_shape=jax.ShapeDtypeStruct((nl, 8, n), F32),
        grid=(nl, n // tn),
        in_specs=[
            pl.BlockSpec((8, d), lambda l, j: (0, 0)),
            pl.BlockSpec((None, d, tn), lambda l, j: (l, 0, j)),
            pl.BlockSpec((None, 1, tn), lambda l, j: (l, 0, j)),
        ],
        out_specs=pl.BlockSpec((None, 8, tn), lambda l, j: (l, 0, j)),
        compiler_params=_params(("parallel", "parallel")),
        name="adaln",
    )(cond8, w_ada, b_ada.reshape(nl, 1, n))


def _modulate_rows(src_ref, dst_ref, g, mul, sh, r0, n_rows):
    for c in range(n_rows // NORM_ROW_CHUNK):
        start = r0 + c * NORM_ROW_CHUNK
        if not isinstance(start, int):
            start = pl.multiple_of(start, NORM_ROW_CHUNK)
        rows = pl.ds(start, NORM_ROW_CHUNK)
        xf = src_ref[rows, :]
        y = xf * lax.rsqrt(jnp.mean(xf * xf, axis=-1, keepdims=True) + EPS)
        dst_ref[rows, :] = ((y * g) * mul + sh).astype(dst_ref.dtype)


def _modulate_kernel(x_ref, g_ref, sh_ref, sc_ref, out_ref, *, tm):
    g = g_ref[...]
    mul = 1.0 + sc_ref[...]
    sh = sh_ref[...]
    unroll = 4 * NORM_ROW_CHUNK

    def body(i, _):
        _modulate_rows(x_ref, out_ref, g, mul, sh, pl.multiple_of(i * unroll, unroll), unroll)
        return 0

    lax.fori_loop(0, tm // unroll, body, 0)


def modulate(x, g, mod, *, layer, which_shift, which_scale, rows_per_batch, row0, tm=512):
    m, d = x.shape
    tm = min(tm, rows_per_batch)
    return pl.pallas_call(
        functools.partial(_modulate_kernel, tm=tm),
        out_shape=jax.ShapeDtypeStruct((m, d), BF16),
        grid=(m // tm,),
        in_specs=[
            pl.BlockSpec((tm, d), lambda i: (i, 0)),
            pl.BlockSpec((None, 1, d), lambda i: (layer, 0, 0)),
            pl.BlockSpec((None, 1, d), _mod_row(layer, which_shift, rows_per_batch, tm, row0)),
            pl.BlockSpec((None, 1, d), _mod_row(layer, which_scale, rows_per_batch, tm, row0)),
        ],
        out_specs=pl.BlockSpec((tm, d), lambda i: (i, 0)),
        compiler_params=_params(("parallel",)),
        name="modulate",
    )(x, g, mod, mod)


def _mm_kernel(a_ref, w_ref, out_ref):
    out_ref[...] = _mm(a_ref[...], w_ref[...]).astype(out_ref.dtype)


def _mm_cast_kernel(a_ref, w_ref, w1_ref, w2_ref, out_ref, w1o_ref, w2o_ref):
    out_ref[...] = _mm(a_ref[...], w_ref[...]).astype(out_ref.dtype)
    fc = w1o_ref.shape[2]
    for t in range(w1o_ref.shape[0]):
        w1o_ref[t] = w1_ref[:, t * fc:(t + 1) * fc].astype(BF16)
    w2o_ref[...] = w2_ref[...].astype(BF16)


def matmul_and_cast(a, w, wl, w1, w2, layer, *, fc, out_dtype=F32, tm=1024, tn=PROJ_COL_TILE, name="proj_cast"):
    m, k = a.shape
    nt = w.shape[2] // tn
    ni = m // tm
    _, d, ff = w1.shape
    r1 = d // ni
    c1 = ff // nt
    r2 = ff // (ni * nt)
    assert d % ni == 0 and ff % nt == 0 and c1 % fc == 0 and ff % (ni * nt) == 0 and r1 % 16 == 0 and r2 % 16 == 0
    return pl.pallas_call(
        _mm_cast_kernel,
        out_shape=(jax.ShapeDtypeStruct((m, nt * tn), out_dtype),
                   jax.ShapeDtypeStruct((ff // fc, d, fc), BF16),
                   jax.ShapeDtypeStruct((ff, d), BF16)),
        grid=(ni, nt),
        in_specs=[
            pl.BlockSpec((tm, k), lambda i, j: (i, 0)),
            pl.BlockSpec((None, k, tn), lambda i, j: (wl, 0, j)),
            pl.BlockSpec((None, r1, c1), lambda i, j: (layer, i, j)),
            pl.BlockSpec((None, r2, d), lambda i, j: (layer, i * nt + j, 0)),
        ],
        out_specs=(pl.BlockSpec((tm, tn), lambda i, j: (i, j)),
                   pl.BlockSpec((c1 // fc, r1, fc), lambda i, j: (j, i, 0)),
                   pl.BlockSpec((r2, d), lambda i, j: (i * nt + j, 0))),
        compiler_params=_params(("parallel", "arbitrary")),
        name=name,
    )(a, w, w1, w2)


def matmul(a, w, wl, *, out_dtype=F32, tm=1024, tn=PROJ_COL_TILE, name="proj"):
    m, k = a.shape
    n = w.shape[2]
    return pl.pallas_call(
        _mm_kernel,
        out_shape=jax.ShapeDtypeStruct((m, n), out_dtype),
        grid=(m // tm, n // tn),
        in_specs=[
            pl.BlockSpec((tm, k), lambda i, j: (i, 0)),
            pl.BlockSpec((None, k, tn), lambda i, j: (wl, 0, j)),
        ],
        out_specs=pl.BlockSpec((tm, tn), lambda i, j: (i, j)),
        compiler_params=_params(("parallel", "arbitrary")),
        name=name,
    )(a, w)


def _mmres_kernel(a_ref, w_ref, res_ref, gate_ref, g_ref, sh_ref, sc_ref, x_ref, xn_ref, *, tm, sub):
    gate = gate_ref[...]
    g = g_ref[...]
    mul = 1.0 + sc_ref[...]
    sh = sh_ref[...]
    for s in range(tm // sub):
        rows = pl.ds(s * sub, sub)
        x_ref[rows, :] = res_ref[rows, :] + gate * _mm(a_ref[rows, :], w_ref[...])
        _modulate_rows(x_ref, xn_ref, g, mul, sh, s * sub, sub)


def mm_res_norm(a, w, wl, res, mod, g_next, *, layer, which_gate, next_layer, next_shift, next_scale,
                rows_per_batch, row0, tm=512, sub=256):
    m, k = a.shape
    d = w.shape[2]
    tm = min(tm, rows_per_batch)
    mk = (rows_per_batch, tm, row0)
    return pl.pallas_call(
        functools.partial(_mmres_kernel, tm=tm, sub=sub),
        out_shape=(jax.ShapeDtypeStruct((m, d), F32), jax.ShapeDtypeStruct((m, d), BF16)),
        grid=(m // tm,),
        in_specs=[
            pl.BlockSpec((tm, k), lambda i: (i, 0)),
            pl.BlockSpec((None, k, d), lambda i: (wl, 0, 0), pipeline_mode=pl.Buffered(1)),
            pl.BlockSpec((tm, d), lambda i: (i, 0)),
            pl.BlockSpec((None, 1, d), _mod_row(layer, which_gate, *mk)),
            pl.BlockSpec((None, 1, d), lambda i: (next_layer, 0, 0)),
            pl.BlockSpec((None, 1, d), _mod_row(next_layer, next_shift, *mk)),
            pl.BlockSpec((None, 1, d), _mod_row(next_layer, next_scale, *mk)),
        ],
        out_specs=(pl.BlockSpec((tm, d), lambda i: (i, 0)), pl.BlockSpec((tm, d), lambda i: (i, 0))),
        compiler_params=_params(("parallel",)),
        name="outproj_res_norm",
    )(a, w, res, mod, g_next, mod, mod)


def _mlp_kernel(xn_ref, w1_ref, w2_ref, res_ref, gate_ref, g_ref, sh_ref, sc_ref, x_ref, *rest, tm, sub, emit_next):
    xno_ref = rest[0] if emit_next else None
    acc_ref = rest[-1]
    f = pl.program_id(1)

    @pl.when(f == 0)
    def _():
        acc_ref[...] = jnp.zeros_like(acc_ref)

    def hidden_chunk(rows):
        h = _mm(xn_ref[rows, :], w1_ref[...])
        return _mm(jnp.square(jnp.maximum(h, 0.0)).astype(BF16), w2_ref[...])

    last = pl.num_programs(1) - 1

    @pl.when(f < last)
    def _():
        acc_ref[...] += hidden_chunk(slice(None))

    @pl.when(f == last)
    def _():
        gate = gate_ref[...]
        g = g_ref[...]
        mul = 1.0 + sc_ref[...]
        sh = sh_ref[...]
        for s in range(tm // sub):
            rows = pl.ds(s * sub, sub)
            x_ref[rows, :] = res_ref[rows, :] + gate * (acc_ref[rows, :] + hidden_chunk(rows))
            if emit_next:
                _modulate_rows(x_ref, xno_ref, g, mul, sh, s * sub, sub)


def fused_mlp(xn, w1, w2, res, mod, g_next, *, layer, next_layer, rows_per_batch, row0, emit_next,
              tm=512, sub=128):
    m, d = xn.shape
    fc = w1.shape[2]
    ff = w1.shape[0] * fc
    tm = min(tm, rows_per_batch)
    mk = (rows_per_batch, tm, row0)
    row_tile = pl.BlockSpec((tm, d), lambda i, f: (i, 0))
    out_shape = [jax.ShapeDtypeStruct((m, d), F32)] + ([jax.ShapeDtypeStruct((m, d), BF16)] if emit_next else [])
    outs = pl.pallas_call(
        functools.partial(_mlp_kernel, tm=tm, sub=min(sub, tm), emit_next=emit_next),
        out_shape=tuple(out_shape),
        grid=(m // tm, ff // fc),
        in_specs=[
            pl.BlockSpec((tm, d), lambda i, f: (i, 0)),
            pl.BlockSpec((None, d, fc), lambda i, f: (f, 0, 0)),
            pl.BlockSpec((fc, d), lambda i, f: (f, 0)),
            row_tile,
            pl.BlockSpec((None, 1, d), _mod_row(layer, 5, *mk)),
            pl.BlockSpec((None, 1, d), lambda i, f: (next_layer, 0, 0)),
            pl.BlockSpec((None, 1, d), _mod_row(next_layer, 0, *mk)),
            pl.BlockSpec((None, 1, d), _mod_row(next_layer, 1, *mk)),
        ],
        out_specs=tuple(row_tile for _ in out_shape),
        scratch_shapes=[pltpu.VMEM((tm, d), F32)],
        compiler_params=_params(("parallel", "arbitrary")),
        name="fused_mlp",
    )(xn, w1, w2, res, mod, g_next, mod, mod)
    return (outs[0], outs[1]) if emit_next else (outs[0], None)


def _head_norm(x, g):
    return x * lax.rsqrt(jnp.mean(x * x, axis=-1, keepdims=True) + EPS) * g


_Q_SCALE = HEAD_DIM ** -0.5 * LOG2E


def _with_ones_column(v):
    lane = lax.broadcasted_iota(jnp.int32, v.shape, 1)
    return jnp.concatenate([v.astype(BF16), jnp.where(lane == 0, 1.0, 0.0).astype(BF16)], axis=1)


def _softmax2_pv(s, vaug):
    p = jnp.exp2(s - jnp.max(s, axis=-1, keepdims=True)).astype(BF16)
    pv = _mm(p, vaug)
    return pv[:, :HEAD_DIM] / pv[:, HEAD_DIM:HEAD_DIM + 1]


def _rope(x, cos, sin_signed, lane_is_first):
    swapped = jnp.where(lane_is_first, pltpu.roll(x, 96, 1), pltpu.roll(x, 32, 1))
    return x * cos + swapped * sin_signed


def _ctx_attn_kernel(q_ref, k_ref, v_ref, qg_ref, kg_ref, o_ref, kn_ref, vo_ref, *, groups, heads):
    hd = HEAD_DIM
    qg = qg_ref[...] * _Q_SCALE
    for h in range(heads):
        hs = slice(h * hd, (h + 1) * hd)
        kn = _head_norm(k_ref[:, hs], kg_ref[...])
        kn_ref[:, hs] = kn
        v = v_ref[:, hs]
        vo_ref[:, hs] = v
        kb = kn.astype(BF16)
        vb = v.astype(BF16)
        for g in range(groups):
            qs = slice((h * groups + g) * hd, (h * groups + g + 1) * hd)
            qn = _head_norm(q_ref[:, qs], qg).astype(BF16)
            s = _nt(qn, kb)
            p = jnp.exp2(s - jnp.max(s, axis=-1, keepdims=True))
            o_ref[:, qs] = (_mm(p.astype(BF16), vb) / jnp.sum(p, axis=-1, keepdims=True)).astype(o_ref.dtype)


def ctx_attention(qkv, q_g, k_g, *, batch, seq, n_q, n_kv, heads=4):
    m = qkv.shape[0]
    groups = n_q // n_kv
    hd = HEAD_DIM
    kw = heads * hd
    qw = heads * groups * hd
    assert (n_q * hd) % kw == 0 and ((n_q + n_kv) * hd) % kw == 0
    k_blk0 = n_q * hd // kw
    v_blk0 = (n_q + n_kv) * hd // kw
    kern = functools.partial(_ctx_attn_kernel, groups=groups, heads=heads)
    return pl.pallas_call(
        kern,
        out_shape=(jax.ShapeDtypeStruct((m, n_q * hd), BF16),
                   jax.ShapeDtypeStruct((m, n_kv * hd), F32),
                   jax.ShapeDtypeStruct((m, n_kv * hd), F32)),
        grid=(batch, n_kv // heads),
        in_specs=[
            pl.BlockSpec((seq, qw), lambda b, h: (b, h)),
            pl.BlockSpec((seq, kw), lambda b, h: (b, k_blk0 + h)),
            pl.BlockSpec((seq, kw), lambda b, h: (b, v_blk0 + h)),
            pl.BlockSpec((1, hd), lambda b, h: (0, 0)),
            pl.BlockSpec((1, hd), lambda b, h: (0, 0)),
        ],
        out_specs=(pl.BlockSpec((seq, qw), lambda b, h: (b, h)),
                   pl.BlockSpec((seq, kw), lambda b, h: (b, h)),
                   pl.BlockSpec((seq, kw), lambda b, h: (b, h))),
        compiler_params=_params(("parallel", "parallel")),
        name="ctx_attention",
    )(qkv, qkv, qkv, q_g.reshape(1, hd), k_g.reshape(1, hd))


def _gqa_kernel(q_ref, k_ref, v_ref, ck_ref, cv_ref, qg_ref, kg_ref, cosq_ref, sinq_ref, cosk_ref, sink_ref,
                o_ref, kall_ref, vall_ref, *, groups, seq, past):
    lane = lax.broadcasted_iota(jnp.int32, (1, HEAD_DIM), 1)
    first = (lane % 64) < 32

    @pl.when(pl.program_id(2) == 0)
    def _():
        kn = _head_norm(k_ref[...], kg_ref[...])
        kall_ref[0:seq, :] = _rope(kn, cosk_ref[...], sink_ref[...], first).astype(BF16)
        kall_ref[seq:seq + past, :] = ck_ref[...].astype(BF16)
        vall_ref[0:seq, :] = _with_ones_column(v_ref[...])
        vall_ref[seq:seq + past, :] = _with_ones_column(cv_ref[...])

    cos = cosq_ref[...]
    sin = sinq_ref[...]
    qg = qg_ref[...] * _Q_SCALE
    for g in range(groups):
        qn = _head_norm(q_ref[:, g * HEAD_DIM:(g + 1) * HEAD_DIM], qg)
        qr = _rope(qn, cos, sin, first).astype(BF16)
        o_ref[:, g * HEAD_DIM:(g + 1) * HEAD_DIM] = _softmax2_pv(_nt(qr, kall_ref[...]), vall_ref[...]).astype(o_ref.dtype)


def gqa_attention(qkv, cache_k, cache_v, q_g, k_g, cos, sin, *, batch, seq, past, n_q, n_kv, tq=256):
    m = qkv.shape[0]
    groups = n_q // n_kv
    hd = HEAD_DIM
    nqb = seq // tq
    kern = functools.partial(_gqa_kernel, groups=groups, seq=seq, past=past)
    return pl.pallas_call(
        kern,
        out_shape=jax.ShapeDtypeStruct((m, n_q * hd), BF16),
        grid=(batch, n_kv, nqb),
        in_specs=[
            pl.BlockSpec((tq, groups * hd), lambda b, h, i: (b * nqb + i, h)),
            pl.BlockSpec((seq, hd), lambda b, h, i: (b, n_q + h)),
            pl.BlockSpec((seq, hd), lambda b, h, i: (b, n_q + n_kv + h)),
            pl.BlockSpec((past, hd), lambda b, h, i: (b, h)),
            pl.BlockSpec((past, hd), lambda b, h, i: (b, h)),
            pl.BlockSpec((1, hd), lambda b, h, i: (0, 0)),
            pl.BlockSpec((1, hd), lambda b, h, i: (0, 0)),
            pl.BlockSpec((tq, hd), lambda b, h, i: (i, 0)),
            pl.BlockSpec((tq, hd), lambda b, h, i: (i, 0)),
            pl.BlockSpec((seq, hd), lambda b, h, i: (0, 0)),
            pl.BlockSpec((seq, hd), lambda b, h, i: (0, 0)),
        ],
        out_specs=pl.BlockSpec((tq, groups * hd), lambda b, h, i: (b * nqb + i, h)),
        scratch_shapes=[pltpu.VMEM((seq + past, hd), BF16), pltpu.VMEM((seq + past, 2 * hd), BF16)],
        compiler_params=_params(("parallel", "parallel", "arbitrary")),
        name="gqa_attention",
    )(qkv, qkv, qkv, cache_k, cache_v, q_g.reshape(1, hd), k_g.reshape(1, hd), cos, sin, cos, sin)


def _na_kernel(rpb_ref, q_ref, k_ref, v_ref, ck_ref, cv_ref, qg_ref, kg_ref, o_ref,
               qn_ref, kn_ref, vb_ref, ckb_ref, cvb_ref, tile_ref, pair_ref, *, rows, wr, n_dr, n_dc, unroll):
    h = pl.program_id(0)
    w = GRID_W

    qn_ref[...] = _head_norm(q_ref[...], qg_ref[...] * _Q_SCALE).astype(BF16)
    kn_ref[...] = _head_norm(k_ref[...], kg_ref[...]).astype(BF16)
    vb_ref[...] = _with_ones_column(v_ref[...])
    ckb_ref[...] = ck_ref[...].astype(BF16)
    cvb_ref[...] = _with_ones_column(cv_ref[...])

    @pl.when(pl.program_id(1) == 0)
    def _():
        qc = lax.broadcasted_iota(jnp.int32, (w, 2 * w), 0)
        lane = lax.broadcasted_iota(jnp.int32, (w, 2 * w), 1)
        kc = lane % w
        cs = jnp.clip(qc - WIN_COLS // 2, 0, w - WIN_COLS)
        col_ok = (kc >= cs) & (kc < cs + WIN_COLS)
        dc = jnp.clip(kc - qc + WIN_COLS - 1, 0, n_dc - 1)
        tiles = [jnp.zeros((w, 2 * w), F32) for _ in range(n_dr)]
        for d in range(n_dc):
            sel = dc == d
            for dr in range(n_dr):
                tiles[dr] = jnp.where(sel, rpb_ref[h, dr * n_dc + d], tiles[dr])
        for dr in range(n_dr):
            tile_ref[dr] = jnp.where(col_ok, tiles[dr] * LOG2E, NEG_INF)
        for dr in range(n_dr - 1):
            pair_ref[dr] = jnp.where(lane < w, tile_ref[dr], tile_ref[dr + 1])

    zero_bias = jnp.zeros((w, ckb_ref.shape[0]), F32)

    def body(i, _):
        rws = [i * unroll + u for u in range(unroll)]
        rss = [jnp.clip(r - wr // 2, 0, rows - wr) for r in rws]
        k0s = [pl.multiple_of(rs * w, w) for rs in rss]
        scores = []
        for r, rs, k0 in zip(rws, rss, k0s):
            dr0 = rs - r + WIN_ROWS - 1
            q_r = qn_ref[pl.ds(pl.multiple_of(r * w, w), w), :]
            kcat = jnp.concatenate([kn_ref[pl.ds(k0, wr * w), :], ckb_ref[...]], axis=0)
            bias = jnp.concatenate([pair_ref[dr0 + 2 * j] for j in range(wr // 2)] + [zero_bias], axis=1)
            scores.append(_nt(q_r, kcat) + bias)
        probs = [jnp.exp2(s - jnp.max(s, axis=-1, keepdims=True)).astype(BF16) for s in scores]
        for r, k0, p in zip(rws, k0s, probs):
            vcat = jnp.concatenate([vb_ref[pl.ds(k0, wr * w), :], cvb_ref[...]], axis=0)
            pv = _mm(p, vcat)
            o = pv[:, :HEAD_DIM] / pv[:, HEAD_DIM:HEAD_DIM + 1]
            o_ref[pl.ds(pl.multiple_of(r * w, w), w), :] = o.astype(o_ref.dtype)
        return 0

    lax.fori_loop(0, rows // unroll, body, 0)


def na_attention(qkv, cache_k, cache_v, q_g, k_g, rpb, *, batch, seq, past, n_heads, unroll=16):
    m = qkv.shape[0]
    hd = HEAD_DIM
    rows = seq // GRID_W
    wr = min(WIN_ROWS, rows)
    n_dr, n_dc = rpb.shape[1], rpb.shape[2]
    assert wr == WIN_ROWS and wr % 2 == 0 and rows % unroll == 0
    kern = functools.partial(_na_kernel, rows=rows, wr=wr, n_dr=n_dr, n_dc=n_dc, unroll=unroll)
    return pl.pallas_call(
        kern,
        out_shape=jax.ShapeDtypeStruct((m, n_heads * hd), BF16),
        grid=(n_heads, batch),
        in_specs=[
            pl.BlockSpec(memory_space=pltpu.SMEM),
            pl.BlockSpec((seq, hd), lambda h, b: (b, h)),
            pl.BlockSpec((seq, hd), lambda h, b: (b, n_heads + h)),
            pl.BlockSpec((seq, hd), lambda h, b: (b, 2 * n_heads + h)),
            pl.BlockSpec((past, hd), lambda h, b: (b, h)),
            pl.BlockSpec((past, hd), lambda h, b: (b, h)),
            pl.BlockSpec((1, hd), lambda h, b: (0, 0)),
            pl.BlockSpec((1, hd), lambda h, b: (0, 0)),
        ],
        out_specs=pl.BlockSpec((seq, hd), lambda h, b: (b, h)),
        scratch_shapes=[
            pltpu.VMEM((seq, hd), BF16), pltpu.VMEM((seq, hd), BF16), pltpu.VMEM((seq, 2 * hd), BF16),
            pltpu.VMEM((past, hd), BF16), pltpu.VMEM((past, 2 * hd), BF16),
            pltpu.VMEM((n_dr, GRID_W, 2 * GRID_W), F32),
            pltpu.VMEM((n_dr - 1, GRID_W, 2 * GRID_W), F32),
        ],
        compiler_params=_params(("parallel", "arbitrary")),
        name="na_attention",
    )(rpb.reshape(n_heads, n_dr * n_dc), qkv, qkv, qkv, cache_k, cache_v,
      q_g.reshape(1, hd), k_g.reshape(1, hd))


def _gates_kernel(xn_ref, w_ref, b_ref, out_ref, *, tm):
    L = MLSTM_CHUNK
    wh, wl = _split2(w_ref[...])
    xn = xn_ref[...]
    pre = _nt(wh, xn) + _nt(wl, xn) + b_ref[...]
    capped = GATE_SOFTCAP * jnp.tanh(pre / GATE_SOFTCAP)
    row = lax.broadcasted_iota(jnp.int32, (capped.shape[0], L), 0) % 8
    is_input = (row == 0) | (row == 2)
    ri = lax.broadcasted_iota(jnp.int32, (L, L), 0)
    ci = lax.broadcasted_iota(jnp.int32, (L, L), 1)
    upper = jnp.where(ri <= ci, 1.0, 0.0).astype(BF16)
    lower = jnp.where(ri >= ci, 1.0, 0.0).astype(BF16)
    for c in range(tm // L):
        cap = capped[:, c * L:(c + 1) * L]
        gates = jnp.where(is_input, cap, jax.nn.log_sigmoid(cap))
        pieces = _split3(gates)
        prefix = sum(_mm(p, upper) for p in pieces)
        suffix = sum(_mm(p, lower) for p in pieces)
        out_ref[c] = jnp.where(row == 4, prefix, jnp.where(row == 5, suffix, gates))


def mlstm_gates(xn, w_gate_t, b_gate_t, *, tm=512):
    m, d = xn.shape
    gh = w_gate_t.shape[0]
    return pl.pallas_call(
        functools.partial(_gates_kernel, tm=tm),
        out_shape=jax.ShapeDtypeStruct((m // MLSTM_CHUNK, gh, MLSTM_CHUNK), F32),
        grid=(m // tm,),
        in_specs=[
            pl.BlockSpec((tm, d), lambda i: (i, 0)),
            pl.BlockSpec((gh, d), lambda i: (0, 0)),
            pl.BlockSpec((gh, 1), lambda i: (0, 0)),
        ],
        out_specs=pl.BlockSpec((tm // MLSTM_CHUNK, gh, MLSTM_CHUNK), lambda i: (i, 0, 0)),
        compiler_params=_params(("parallel",)),
        name="mlstm_gates",
    )(xn, w_gate_t, b_gate_t)


def _mlstm_kernel(*refs, n_chunks, heads, zero_init, write_state, has_prev_states):
    it = iter(refs)
    q_ref, k_ref, v_ref, o_ref, gates_ref, hg_ref = (next(it) for _ in range(6))
    if not zero_init:
        c0_ref, n0_ref, m0_ref = (next(it) for _ in range(3))
    if has_prev_states:
        next(it)
    y_ref = next(it)
    if write_state:
        cf_ref, nf_ref, mf_ref = (next(it) for _ in range(3))
    hdir_ref, cst_ref = next(it), next(it)

    L, dk, dv = MLSTM_CHUNK, MLSTM_DK, MLSTM_DV
    ri = lax.broadcasted_iota(jnp.int32, (L, L), 0)
    ci = lax.broadcasted_iota(jnp.int32, (L, L), 1)
    masks = (ri >= ci, ri <= ci)
    eye = ri == ci
    eye_b = jnp.where(eye, 1.0, 0.0).astype(BF16)
    qscale = dk ** -0.5

    m_init, n_init = [], []
    for hh in range(heads):
        for direction in range(2):
            idx = hh * 2 + direction
            if zero_init:
                cst_ref[idx] = jnp.zeros((dk, dv), F32)
                m_init.append(jnp.zeros((1, 1), F32))
                n_init.append(jnp.zeros((1, dk), F32))
            else:
                cst_ref[idx] = c0_ref[direction, hh]
                m_init.append(m0_ref[direction, hh])
                n_init.append(n0_ref[direction, hh])

    def body(step, carry):
        ms, ns = carry
        chains = [(hh, direction) for hh in range(heads) for direction in range(2)]
        r0s = [pl.multiple_of((step if d == 0 else n_chunks - 1 - step) * L, L) for _, d in chains]
        cs = [step if d == 0 else n_chunks - 1 - step for _, d in chains]

        st1 = []
        for idx, (hh, d) in enumerate(chains):
            gt = gates_ref[cs[idx], hh * 8:(hh + 1) * 8, :]
            li, lf, brow = gt[2 * d:2 * d + 1], gt[2 * d + 1:2 * d + 2], gt[4 + d:5 + d]
            b3 = _nt(eye_b, jnp.concatenate(_split3(jnp.broadcast_to(brow, (L, L))), axis=0))
            bmat = b3[:, :L] + b3[:, L:2 * L] + b3[:, 2 * L:]
            log_d = jnp.where(masks[d], bmat - brow + li, NEG_INF)
            inter = bmat + ms[idx]
            m_j = jnp.maximum(inter, jnp.max(log_d, axis=-1, keepdims=True))
            tot = jnp.sum(lf, axis=-1, keepdims=True)
            log_w = tot - brow + li
            m_new = jnp.maximum(tot + ms[idx], jnp.max(log_w, axis=-1, keepdims=True))
            st1.append((log_d, inter, m_j, tot, log_w, m_new))

        st2 = []
        for idx, (hh, d) in enumerate(chains):
            log_d, inter, m_j, tot, log_w, m_new = st1[idx]
            qf = q_ref[pl.ds(r0s[idx], L), hh * dk:(hh + 1) * dk] * qscale
            q = qf.astype(BF16)
            k = k_ref[pl.ds(r0s[idx], L), hh * dk:(hh + 1) * dk]
            k_hi, k_lo = _split2(k)
            qk = _nt(q, k_hi)
            kt2 = _nt(eye_b, jnp.concatenate([k_hi, k_lo], axis=0))
            kt = kt2[:, :L] + kt2[:, L:]
            qc = _mm(q, cst_ref[idx].astype(BF16))
            qn = jnp.sum(qf * ns[idx], axis=-1, keepdims=True)
            d_mat = jnp.exp(log_d - m_j)
            w_inter = jnp.exp(inter - m_j)
            wt = jnp.exp(log_w - m_new)
            decay = jnp.exp(tot + ms[idx] - m_new)
            ktw = (kt * wt).astype(BF16)
            wn = _mm(jnp.broadcast_to(wt, (8, L)).astype(BF16), k_hi)[:1]
            st2.append((qk * d_mat, qc, qn, w_inter, decay, ktw, decay * ns[idx] + wn))

        for idx, (hh, d) in enumerate(chains):
            s, qc, qn, w_inter, decay, ktw, _ = st2[idx]
            floor = jnp.exp(-st1[idx][2])
            v = v_ref[pl.ds(r0s[idx], L), hh * dv:(hh + 1) * dv].astype(BF16)
            both = _mm(jnp.concatenate([s.astype(BF16), ktw], axis=0), v)
            num = jnp.concatenate([w_inter] * 2, axis=1) * qc + both[:L]
            den = jnp.maximum(jnp.abs(w_inter * qn + jnp.sum(s, axis=-1, keepdims=True)), floor)
            h = num / jnp.concatenate([den, den], axis=1)
            hdir_ref[d, pl.ds(r0s[idx], L), hh * dv:(hh + 1) * dv] = h
            cst_ref[idx] = decay * cst_ref[idx] + both[L:]
        return tuple(s1[5] for s1 in st1), tuple(s2[6] for s2 in st2)

    m_fin, n_fin = lax.fori_loop(0, n_chunks, body, (tuple(m_init), tuple(n_init)))

    def finish(c, _):
        r0 = pl.multiple_of(c * L, L)
        for hh in range(heads):
            cols = slice(hh * dv, (hh + 1) * dv)
            hs = hdir_ref[0, pl.ds(r0, L), cols] + hdir_ref[1, pl.ds(r0, L), cols]
            hn = hs * lax.rsqrt(jnp.mean(hs * hs, axis=-1, keepdims=True) + EPS) * hg_ref[:, cols]
            y_ref[pl.ds(r0, L), cols] = (hn * jax.nn.sigmoid(o_ref[pl.ds(r0, L), cols])).astype(y_ref.dtype)
        return 0

    lax.fori_loop(0, n_chunks, finish, 0)

    if write_state:
        for hh in range(heads):
            for direction in range(2):
                idx = hh * 2 + direction
                cf_ref[direction, hh] = cst_ref[idx]
                nf_ref[direction, hh] = n_fin[idx]
                mf_ref[direction, hh] = m_fin[idx]


def mlstm_scan(proj, gates, head_g, state=None, *, batch, seq, n_heads, write_state, heads=2,
               state_slot=(0, 1), prev_states=None):
    m = proj.shape[0]
    dk, dv, L = MLSTM_DK, MLSTM_DV, MLSTM_CHUNK
    H = n_heads
    nb = H // heads
    n_chunks = seq // L
    zero_init = state is None
    slot, n_slots = state_slot
    kern = functools.partial(_mlstm_kernel, n_chunks=n_chunks, heads=heads, zero_init=zero_init,
                             write_state=write_state, has_prev_states=prev_states is not None)
    in_specs = [
        pl.BlockSpec((seq, heads * dk), lambda b, h: (b, h)),
        pl.BlockSpec((seq, heads * dk), lambda b, h: (b, nb + h)),
        pl.BlockSpec((seq, heads * dv), lambda b, h: (b, nb + h)),
        pl.BlockSpec((seq, heads * dv), lambda b, h: (b, 2 * nb + h)),
        pl.BlockSpec((n_chunks, heads * 8, L), lambda b, h: (b, h, 0)),
        pl.BlockSpec((None, 1, heads * dv), lambda b, h: (h, 0, 0)),
    ]
    args = [proj, proj, proj, proj, gates, head_g.reshape(nb, 1, heads * dv)]
    st_specs = [pl.BlockSpec((None, 2, heads, dk, dv), lambda b, h: (b, 0, h, 0, 0)),
                pl.BlockSpec((None, 2, heads, 1, dk), lambda b, h: (b, 0, h, 0, 0)),
                pl.BlockSpec((None, 2, heads, 1, 1), lambda b, h: (b, 0, h, 0, 0))]
    if not zero_init:
        c0_all, c0_slot, n0, m0 = state
        in_specs += [pl.BlockSpec((None, None, 2, heads, dk, dv), lambda b, h: (b, c0_slot, 0, h, 0, 0))] + st_specs[1:]
        args += [c0_all.astype(F32), n0.astype(F32).reshape(batch, 2, H, 1, dk),
                 m0.astype(F32).reshape(batch, 2, H, 1, 1)]
    aliases = {}
    if prev_states is not None:
        aliases = {len(args): 1}
        in_specs.append(pl.BlockSpec(memory_space=pl.ANY))
        args.append(prev_states)
    out_shape = [jax.ShapeDtypeStruct((m, H * dv), BF16)]
    out_specs = [pl.BlockSpec((seq, heads * dv), lambda b, h: (b, h))]
    if write_state:
        out_shape += [jax.ShapeDtypeStruct((batch, n_slots, 2, H, dk, dv), F32),
                      jax.ShapeDtypeStruct((batch, 2, H, 1, dk), F32),
                      jax.ShapeDtypeStruct((batch, 2, H, 1, 1), F32)]
        out_specs += [pl.BlockSpec((None, None, 2, heads, dk, dv), lambda b, h: (b, slot, 0, h, 0, 0))] + st_specs[1:]
    outs = pl.pallas_call(
        kern,
        out_shape=tuple(out_shape),
        grid=(batch, nb),
        in_specs=in_specs,
        out_specs=tuple(out_specs),
        scratch_shapes=[pltpu.VMEM((2, seq, heads * dv), F32), pltpu.VMEM((2 * heads, dk, dv), F32)],
        input_output_aliases=aliases,
        compiler_params=_params(("parallel", "parallel")),
        name="mlstm_scan",
    )(*args)
    if write_state:
        y, cf, nf, mf = outs
        return y, (cf, nf.reshape(batch, 2, H, dk), mf.reshape(batch, 2, H))
    return outs[0], None


def _gate_weights_head_major(w_gate, b_gate, n_heads):
    d = w_gate.shape[0]
    wt = jnp.transpose(w_gate.reshape(d, 4, n_heads), (2, 1, 0)).astype(F32)
    wt = jnp.concatenate([wt, wt[:, 1:2], wt[:, 3:4], jnp.zeros_like(wt[:, :2])], axis=1).reshape(8 * n_heads, d)
    bt = jnp.transpose(b_gate.reshape(4, n_heads), (1, 0)).astype(F32)
    bt = jnp.concatenate([bt, bt[:, 1:2], bt[:, 3:4], jnp.zeros_like(bt[:, :2])], axis=1).reshape(8 * n_heads, 1)
    return wt, bt


def _rope_tables(seq):
    nf = HEAD_DIM // 4
    t = jnp.arange(seq)
    inv = ROPE_THETA ** (-jnp.arange(nf, dtype=F32) / nf)
    pos = jnp.stack([t // GRID_W, t % GRID_W], axis=-1).astype(F32)
    ang = pos[:, :, None] * inv
    cos = jnp.cos(ang)
    sin = jnp.sin(ang)
    cos_full = jnp.stack([cos, cos], axis=2).reshape(seq, HEAD_DIM)
    sin_full = jnp.stack([-sin, sin], axis=2).reshape(seq, HEAD_DIM)
    return cos_full, sin_full


def kernel(x_prompt, x_sample, state_mlstm_C, state_mlstm_n, state_mlstm_m, cache_gqa_k, cache_gqa_v,
           cache_na_k, cache_na_v, c, c_ctx, norm1_g, norm2_g, w_ada, b_ada, w_mlp1, w_mlp2,
           mlstm_w_in, mlstm_w_gate, mlstm_b_gate, mlstm_head_g, mlstm_w_out,
           gqa_w_qkv, gqa_q_g, gqa_k_g, gqa_w_o, na_w_qkv, na_q_g, na_k_g, na_rpb, na_w_o):
    bp, tp, d = x_prompt.shape
    bs, ts, _ = x_sample.shape
    depth = w_ada.shape[0]
    past = cache_gqa_k.shape[2]
    mh = mlstm_w_gate.shape[-1] // 4
    gqa_kv = cache_gqa_k.shape[3]
    gqa_q = gqa_w_o.shape[1] // HEAD_DIM
    na_h = cache_na_k.shape[3]
    assert bs + 1 <= 8

    cond8 = jnp.zeros((8, d), F32).at[:bs].set(c).at[bs].set(c_ctx)
    mod = adaln_all(cond8, w_ada, b_ada).reshape(depth * 48, 1, d)
    g1 = norm1_g.reshape(depth, 1, d)
    g2 = norm2_g.reshape(depth, 1, d)
    cos, sin = _rope_tables(ts)

    groups = [
        dict(x=x_prompt.reshape(bp * tp, d), batch=bp, seq=tp, rpb_=bp * tp, row0=bs, ctx=True),
        dict(x=x_sample.reshape(bs * ts, d), batch=bs, seq=ts, rpb_=ts, row0=0, ctx=False),
    ]
    for grp in groups:
        grp["xn"] = modulate(grp["x"], g1, mod, layer=0, which_shift=0, which_scale=1,
                             rows_per_batch=grp["rpb_"], row0=grp["row0"])
    w_in_all = [mlstm_w_in.astype(BF16), gqa_w_qkv.astype(BF16), na_w_qkv.astype(BF16)]
    w_o_all = [mlstm_w_out.astype(BF16), gqa_w_o.astype(BF16), na_w_o.astype(BF16)]
    n_mlstm = state_mlstm_C.shape[1]
    c_all = jnp.zeros((bp, n_mlstm, 2, mh, MLSTM_DK, MLSTM_DV), F32)
    n_l, m_l, gk_l, gv_l, nk_l, nv_l = [], [], [], [], [], []

    for i in range(depth):
        kind, j = i % N_MIXERS, i // N_MIXERS
        last = i == depth - 1
        nxt = i if last else i + 1
        w_in, w_o = w_in_all[kind], w_o_all[kind]
        if kind == 0:
            wg_t, bg_t = _gate_weights_head_major(mlstm_w_gate[j], mlstm_b_gate[j], mh)

        for grp in groups:
            x, xn, batch, seq, ctx = grp["x"], grp["xn"], grp["batch"], grp["seq"], grp["ctx"]
            mk = dict(rows_per_batch=grp["rpb_"], row0=grp["row0"])
            if ctx:
                proj, w1, w2 = matmul_and_cast(xn, w_in, j, w_mlp1, w_mlp2, i, fc=MLP_HIDDEN_CHUNK)
            else:
                proj = matmul(xn, w_in, j)
            if kind == 0:
                gates = mlstm_gates(xn, wg_t, bg_t)
                if ctx:
                    a, (c_all, nf, mf) = mlstm_scan(proj, gates, mlstm_head_g[j], None, batch=batch, seq=seq,
                                                    n_heads=mh, write_state=True, heads=4,
                                                    state_slot=(j, n_mlstm), prev_states=c_all)
                    n_l.append(nf)
                    m_l.append(mf)
                else:
                    st = (state_mlstm_C, j, state_mlstm_n[:, j], state_mlstm_m[:, j])
                    a, _ = mlstm_scan(proj, gates, mlstm_head_g[j], st, batch=batch, seq=seq,
                                      n_heads=mh, write_state=False)
            elif kind == 1:
                if ctx:
                    a, kn, v = ctx_attention(proj, gqa_q_g[j], gqa_k_g[j], batch=batch, seq=seq, n_q=gqa_q, n_kv=gqa_kv)
                    gk_l.append(kn.reshape(batch, seq, gqa_kv, HEAD_DIM))
                    gv_l.append(v.reshape(batch, seq, gqa_kv, HEAD_DIM))
                else:
                    ck = cache_gqa_k[:, j].astype(F32).reshape(batch * past, gqa_kv * HEAD_DIM)
                    cv = cache_gqa_v[:, j].astype(F32).reshape(batch * past, gqa_kv * HEAD_DIM)
                    a = gqa_attention(proj, ck, cv, gqa_q_g[j], gqa_k_g[j], cos, sin, batch=batch, seq=seq,
                                      past=past, n_q=gqa_q, n_kv=gqa_kv)
            else:
                if ctx:
                    a, kn, v = ctx_attention(proj, na_q_g[j], na_k_g[j], batch=batch, seq=seq, n_q=na_h, n_kv=na_h,
                                             heads=8)
                    nk_l.append(kn.reshape(batch, seq, na_h, HEAD_DIM))
                    nv_l.append(v.reshape(batch, seq, na_h, HEAD_DIM))
                else:
                    ck = cache_na_k[:, j].astype(F32).reshape(batch * past, na_h * HEAD_DIM)
                    cv = cache_na_v[:, j].astype(F32).reshape(batch * past, na_h * HEAD_DIM)
                    a = na_attention(proj, ck, cv, na_q_g[j], na_k_g[j], na_rpb[j], batch=batch, seq=seq,
                                     past=past, n_heads=na_h)
            x, xn = mm_res_norm(a, w_o, j, x, mod, g2, layer=i, which_gate=2, next_layer=i, next_shift=3,
                                next_scale=4, **mk)
            x, xn = fused_mlp(xn, w1, w2, x, mod, g1, layer=i, next_layer=nxt, emit_next=not last, **mk)
            grp["x"], grp["xn"] = x, xn

    y_prompt = groups[0]["x"].reshape(bp, tp, d)
    y_sample = groups[1]["x"].reshape(bs, ts, d)
    return (y_prompt, y_sample, c_all, jnp.stack(n_l, axis=1), jnp.stack(m_l, axis=1),
            jnp.stack(gk_l, axis=1), jnp.stack(gv_l, axis=1), jnp.stack(nk_l, axis=1), jnp.stack(nv_l, axis=1))
```

```python
import functools
import math

import jax
import jax.numpy as jnp
from jax import lax
from jax.experimental import pallas as pl
from jax.experimental.pallas import tpu as pltpu

EPS = 1e-6
HEAD_DIM = 128
MLSTM_DK = 128
MLSTM_DV = 256
MLSTM_CHUNK = 128
GATE_SOFTCAP = 15.0
GRID_W = 64
WIN_ROWS = 8
WIN_COLS = 16
ROPE_THETA = 10000.0
N_MIXERS = 3
LOG2E = math.log2(math.e)

VMEM_LIMIT_BYTES = 56 * 1024 * 1024
NORM_ROW_CHUNK = 16
MLP_HIDDEN_CHUNK = 1024
PROJ_COL_TILE = 768

F32 = jnp.float32
BF16 = jnp.bfloat16
NEG_INF = float("-inf")


def _params(sem):
    return pltpu.CompilerParams(dimension_semantics=sem, vmem_limit_bytes=VMEM_LIMIT_BYTES)


def _nt(a, b):
    return lax.dot_general(a, b, (((1,), (1,)), ((), ())), preferred_element_type=F32)


def _mm(a, b):
    return jnp.dot(a, b, preferred_element_type=F32)


def _split2(x):
    hi = x.astype(BF16)
    lo = (x - hi.astype(F32)).astype(BF16)
    return hi, lo


def _split3(x):
    hi = x.astype(BF16)
    r = x - hi.astype(F32)
    mid = r.astype(BF16)
    lo = (r - mid.astype(F32)).astype(BF16)
    return hi, mid, lo


def _mod_row(layer, which, rows_per_batch, tm, row0):
    def f(i, *_):
        b = row0 + (i * tm) // rows_per_batch
        return (layer * 48 + b * 6 + which, 0, 0)
    return f


def _adaln_kernel(cond_ref, w_ref, b_ref, out_ref):
    c = cond_ref[...]
    s = c * jax.nn.sigmoid(c)
    sh, sl = _split2(s)
    wh, wl = _split2(w_ref[...])
    lhs = jnp.concatenate([sh, sl], axis=0)
    r = _mm(lhs, wh)
    out_ref[...] = r[:8] + r[8:] + _mm(sh, wl) + b_ref[...]


def adaln_all(cond8, w_ada, b_ada, tn=1024):
    nl, d, n = w_ada.shape
    return pl.pallas_call(
        _adaln_kernel,
        out_shape=jax.ShapeDtypeStruct((nl, 8, n), F32),
        grid=(nl, n // tn),
        in_specs=[
            pl.BlockSpec((8, d), lambda l, j: (0, 0)),
            pl.BlockSpec((None, d, tn), lambda l, j: (l, 0, j)),
            pl.BlockSpec((None, 1, tn), lambda l, j: (l, 0, j)),
        ],
        out_specs=pl.BlockSpec((None, 8, tn), lambda l, j: (l, 0, j)),
        compiler_params=_params(("parallel", "parallel")),
        name="adaln",
    )(cond8, w_ada, b_ada.reshape(nl, 1, n))


def _modulate_rows(src_ref, dst_ref, g, mul, sh, r0, n_rows):
    for c in range(n_rows // NORM_ROW_CHUNK):
        start = r0 + c * NORM_ROW_CHUNK
        if not isinstance(start, int):
            start = pl.multiple_of(start, NORM_ROW_CHUNK)
        rows = pl.ds(start, NORM_ROW_CHUNK)
        xf = src_ref[rows, :]
        y = xf * lax.rsqrt(jnp.mean(xf * xf, axis=-1, keepdims=True) + EPS)
        dst_ref[rows, :] = ((y * g) * mul + sh).astype(dst_ref.dtype)


def _modulate_kernel(x_ref, g_ref, sh_ref, sc_ref, out_ref, *, tm):
    g = g_ref[...]
    mul = 1.0 + sc_ref[...]
    sh = sh_ref[...]
    unroll = 4 * NORM_ROW_CHUNK

    def body(i, _):
        _modulate_rows(x_ref, out_ref, g, mul, sh, pl.multiple_of(i * unroll, unroll), unroll)
        return 0

    lax.fori_loop(0, tm // unroll, body, 0)


def modulate(x, g, mod, *, layer, which_shift, which_scale, rows_per_batch, row0, tm=512):
    m, d = x.shape
    tm = min(tm, rows_per_batch)
    return pl.pallas_call(
        functools.partial(_modulate_kernel, tm=tm),
        out_shape=jax.ShapeDtypeStruct((m, d), BF16),
        grid=(m // tm,),
        in_specs=[
            pl.BlockSpec((tm, d), lambda i: (i, 0)),
            pl.BlockSpec((None, 1, d), lambda i: (layer, 0, 0)),
            pl.BlockSpec((None, 1, d), _mod_row(layer, which_shift, rows_per_batch, tm, row0)),
            pl.BlockSpec((None, 1, d), _mod_row(layer, which_scale, rows_per_batch, tm, row0)),
        ],
        out_specs=pl.BlockSpec((tm, d), lambda i: (i, 0)),
        compiler_params=_params(("parallel",)),
        name="modulate",
    )(x, g, mod, mod)


def _mm_kernel(a_ref, w_ref, out_ref):
    out_ref[...] = _mm(a_ref[...], w_ref[...]).astype(out_ref.dtype)


def _mm_cast_kernel(a_ref, w_ref, w1_ref, w2_ref, out_ref, w1o_ref, w2o_ref):
    out_ref[...] = _mm(a_ref[...], w_ref[...]).astype(out_ref.dtype)
    fc = w1o_ref.shape[2]
    for t in range(w1o_ref.shape[0]):
        w1o_ref[t] = w1_ref[:, t * fc:(t + 1) * fc].astype(BF16)
    w2o_ref[...] = w2_ref[...].astype(BF16)


def matmul_and_cast(a, w, wl, w1, w2, layer, *, fc, out_dtype=F32, tm=1024, tn=PROJ_COL_TILE, name="proj_cast"):
    m, k = a.shape
    nt = w.shape[2] // tn
    ni = m // tm
    _, d, ff = w1.shape
    r1 = d // ni
    c1 = ff // nt
    r2 = ff // (ni * nt)
    assert d % ni == 0 and ff % nt == 0 and c1 % fc == 0 and ff % (ni * nt) == 0 and r1 % 16 == 0 and r2 % 16 == 0
    return pl.pallas_call(
        _mm_cast_kernel,
        out_shape=(jax.ShapeDtypeStruct((m, nt * tn), out_dtype),
                   jax.ShapeDtypeStruct((ff // fc, d, fc), BF16),
                   jax.ShapeDtypeStruct((ff, d), BF16)),
        grid=(ni, nt),
        in_specs=[
            pl.BlockSpec((tm, k), lambda i, j: (i, 0)),
            pl.BlockSpec((None, k, tn), lambda i, j: (wl, 0, j)),
            pl.BlockSpec((None, r1, c1), lambda i, j: (layer, i, j)),
            pl.BlockSpec((None, r2, d), lambda i, j: (layer, i * nt + j, 0)),
        ],
        out_specs=(pl.BlockSpec((tm, tn), lambda i, j: (i, j)),
                   pl.BlockSpec((c1 // fc, r1, fc), lambda i, j: (j, i, 0)),
                   pl.BlockSpec((r2, d), lambda i, j: (i * nt + j, 0))),
        compiler_params=_params(("parallel", "arbitrary")),
        name=name,
    )(a, w, w1, w2)


def matmul(a, w, wl, *, out_dtype=F32, tm=1024, tn=PROJ_COL_TILE, name="proj"):
    m, k = a.shape
    n = w.shape[2]
    return pl.pallas_call(
        _mm_kernel,
        out_shape=jax.ShapeDtypeStruct((m, n), out_dtype),
        grid=(m // tm, n // tn),
        in_specs=[
            pl.BlockSpec((tm, k), lambda i, j: (i, 0)),
            pl.BlockSpec((None, k, tn), lambda i, j: (wl, 0, j)),
        ],
        out_specs=pl.BlockSpec((tm, tn), lambda i, j: (i, j)),
        compiler_params=_params(("parallel", "arbitrary")),
        name=name,
    )(a, w)


def _mmres_kernel(a_ref, w_ref, res_ref, gate_ref, g_ref, sh_ref, sc_ref, x_ref, xn_ref, *, tm, sub):
    gate = gate_ref[...]
    g = g_ref[...]
    mul = 1.0 + sc_ref[...]
    sh = sh_ref[...]
    for s in range(tm // sub):
        rows = pl.ds(s * sub, sub)
        x_ref[rows, :] = res_ref[rows, :] + gate * _mm(a_ref[rows, :], w_ref[...])
        _modulate_rows(x_ref, xn_ref, g, mul, sh, s * sub, sub)


def mm_res_norm(a, w, wl, res, mod, g_next, *, layer, which_gate, next_layer, next_shift, next_scale,
                rows_per_batch, row0, tm=512, sub=256):
    m, k = a.shape
    d = w.shape[2]
    tm = min(tm, rows_per_batch)
    mk = (rows_per_batch, tm, row0)
    return pl.pallas_call(
        functools.partial(_mmres_kernel, tm=tm, sub=sub),
        out_shape=(jax.ShapeDtypeStruct((m, d), F32), jax.ShapeDtypeStruct((m, d), BF16)),
        grid=(m // tm,),
        in_specs=[
            pl.BlockSpec((tm, k), lambda i: (i, 0)),
            pl.BlockSpec((None, k, d), lambda i: (wl, 0, 0), pipeline_mode=pl.Buffered(1)),
            pl.BlockSpec((tm, d), lambda i: (i, 0)),
            pl.BlockSpec((None, 1, d), _mod_row(layer, which_gate, *mk)),
            pl.BlockSpec((None, 1, d), lambda i: (next_layer, 0, 0)),
            pl.BlockSpec((None, 1, d), _mod_row(next_layer, next_shift, *mk)),
            pl.BlockSpec((None, 1, d), _mod_row(next_layer, next_scale, *mk)),
        ],
        out_specs=(pl.BlockSpec((tm, d), lambda i: (i, 0)), pl.BlockSpec((tm, d), lambda i: (i, 0))),
        compiler_params=_params(("parallel",)),
        name="outproj_res_norm",
    )(a, w, res, mod, g_next, mod, mod)


def _mlp_kernel(xn_ref, w1_ref, w2_ref, res_hbm, gate_ref, g_ref, sh_ref, sc_ref, x_hbm, *rest, tm, sub, emit_next):
    if emit_next:
        xno_hbm, acc_ref, res_buf, x_buf, xn_buf, res_sem, x_sem, xn_sem = rest
    else:
        acc_ref, res_buf, x_buf, res_sem, x_sem = rest
    i = pl.program_id(0)
    f = pl.program_id(1)

    @pl.when(f == 0)
    def _():
        acc_ref[...] = jnp.zeros_like(acc_ref)

    def hidden_chunk(rows):
        h = _mm(xn_ref[rows, :], w1_ref[...])
        return _mm(jnp.square(jnp.maximum(h, 0.0)).astype(BF16), w2_ref[...])

    last = pl.num_programs(1) - 1

    @pl.when(f < last)
    def _():
        acc_ref[...] += hidden_chunk(slice(None))

    def hbm_rows(ref, s):
        return ref.at[pl.ds(pl.multiple_of(i * tm + s * sub, sub), sub), :]

    def res_copy(s):
        return pltpu.make_async_copy(hbm_rows(res_hbm, s), res_buf.at[s % 2], res_sem.at[s % 2])

    def x_copy(s):
        return pltpu.make_async_copy(x_buf.at[s % 2], hbm_rows(x_hbm, s), x_sem.at[s % 2])

    def xn_copy(s):
        return pltpu.make_async_copy(xn_buf.at[s % 2], hbm_rows(xno_hbm, s), xn_sem.at[s % 2])

    @pl.when(f == last)
    def _():
        gate = gate_ref[...]
        g = g_ref[...]
        mul = 1.0 + sc_ref[...]
        sh = sh_ref[...]
        n_sub = tm // sub
        res_copy(0).start()
        for s in range(n_sub):
            slot = s % 2
            if s + 1 < n_sub:
                res_copy(s + 1).start()
            rows = pl.ds(s * sub, sub)
            delta = acc_ref[rows, :] + hidden_chunk(rows)
            res_copy(s).wait()
            if s >= 2:
                x_copy(s - 2).wait()
                if emit_next:
                    xn_copy(s - 2).wait()
            x_buf[slot] = res_buf[slot] + gate * delta
            x_copy(s).start()
            if emit_next:
                _modulate_rows(x_buf.at[slot], xn_buf.at[slot], g, mul, sh, 0, sub)
                xn_copy(s).start()
        for s in range(max(n_sub - 2, 0), n_sub):
            x_copy(s).wait()
            if emit_next:
                xn_copy(s).wait()


def fused_mlp(xn, w1, w2, res, mod, g_next, *, layer, next_layer, rows_per_batch, row0, emit_next,
              tm=1024, sub=256):
    m, d = xn.shape
    fc = w1.shape[2]
    ff = w1.shape[0] * fc
    tm = min(tm, rows_per_batch)
    sub = min(sub, tm)
    mk = (rows_per_batch, tm, row0)
    in_hbm = pl.BlockSpec(memory_space=pl.ANY)
    out_shape = [jax.ShapeDtypeStruct((m, d), F32)] + ([jax.ShapeDtypeStruct((m, d), BF16)] if emit_next else [])
    slab_bufs = [pltpu.VMEM((2, sub, d), F32), pltpu.VMEM((2, sub, d), F32)] + ([pltpu.VMEM((2, sub, d), BF16)] if emit_next else [])
    outs = pl.pallas_call(
        functools.partial(_mlp_kernel, tm=tm, sub=sub, emit_next=emit_next),
        out_shape=tuple(out_shape),
        grid=(m // tm, ff // fc),
        in_specs=[
            pl.BlockSpec((tm, d), lambda i, f: (i, 0)),
            pl.BlockSpec((None, d, fc), lambda i, f: (f, 0, 0)),
            pl.BlockSpec((fc, d), lambda i, f: (f, 0)),
            in_hbm,
            pl.BlockSpec((None, 1, d), _mod_row(layer, 5, *mk)),
            pl.BlockSpec((None, 1, d), lambda i, f: (next_layer, 0, 0)),
            pl.BlockSpec((None, 1, d), _mod_row(next_layer, 0, *mk)),
            pl.BlockSpec((None, 1, d), _mod_row(next_layer, 1, *mk)),
        ],
        out_specs=tuple(in_hbm for _ in out_shape),
        scratch_shapes=[pltpu.VMEM((tm, d), F32)] + slab_bufs + [pltpu.SemaphoreType.DMA((2,)) for _ in slab_bufs],
        compiler_params=_params(("parallel", "arbitrary")),
        name="fused_mlp",
    )(xn, w1, w2, res, mod, g_next, mod, mod)
    return (outs[0], outs[1]) if emit_next else (outs[0], None)


def _head_norm(x, g):
    return x * lax.rsqrt(jnp.mean(x * x, axis=-1, keepdims=True) + EPS) * g


_Q_SCALE = HEAD_DIM ** -0.5 * LOG2E


def _with_ones_column(v):
    lane = lax.broadcasted_iota(jnp.int32, v.shape, 1)
    return jnp.concatenate([v.astype(BF16), jnp.where(lane == 0, 1.0, 0.0).astype(BF16)], axis=1)


def _softmax2_pv(s, vaug):
    p = jnp.exp2(s - jnp.max(s, axis=-1, keepdims=True)).astype(BF16)
    pv = _mm(p, vaug)
    return pv[:, :HEAD_DIM] / pv[:, HEAD_DIM:HEAD_DIM + 1]


def _rope(x, cos, sin_signed, lane_is_first):
    swapped = jnp.where(lane_is_first, pltpu.roll(x, 96, 1), pltpu.roll(x, 32, 1))
    return x * cos + swapped * sin_signed


def _ctx_attn_kernel(q_ref, k_ref, v_ref, qg_ref, kg_ref, o_ref, kn_ref, vo_ref, *, groups, heads):
    hd = HEAD_DIM
    qg = qg_ref[...] * _Q_SCALE
    for h in range(heads):
        hs = slice(h * hd, (h + 1) * hd)
        kn = _head_norm(k_ref[:, hs], kg_ref[...])
        kn_ref[:, hs] = kn
        v = v_ref[:, hs]
        vo_ref[:, hs] = v
        kb = kn.astype(BF16)
        vb = v.astype(BF16)
        for g in range(groups):
            qs = slice((h * groups + g) * hd, (h * groups + g + 1) * hd)
            qn = _head_norm(q_ref[:, qs], qg).astype(BF16)
            s = _nt(qn, kb)
            p = jnp.exp2(s - jnp.max(s, axis=-1, keepdims=True))
            o_ref[:, qs] = (_mm(p.astype(BF16), vb) / jnp.sum(p, axis=-1, keepdims=True)).astype(o_ref.dtype)


def ctx_attention(qkv, q_g, k_g, *, batch, seq, n_q, n_kv, heads=4):
    m = qkv.shape[0]
    groups = n_q // n_kv
    hd = HEAD_DIM
    kw = heads * hd
    qw = heads * groups * hd
    assert (n_q * hd) % kw == 0 and ((n_q + n_kv) * hd) % kw == 0
    k_blk0 = n_q * hd // kw
    v_blk0 = (n_q + n_kv) * hd // kw
    kern = functools.partial(_ctx_attn_kernel, groups=groups, heads=heads)
    return pl.pallas_call(
        kern,
        out_shape=(jax.ShapeDtypeStruct((m, n_q * hd), BF16),
                   jax.ShapeDtypeStruct((m, n_kv * hd), F32),
                   jax.ShapeDtypeStruct((m, n_kv * hd), F32)),
        grid=(batch, n_kv // heads),
        in_specs=[
            pl.BlockSpec((seq, qw), lambda b, h: (b, h)),
            pl.BlockSpec((seq, kw), lambda b, h: (b, k_blk0 + h)),
            pl.BlockSpec((seq, kw), lambda b, h: (b, v_blk0 + h)),
            pl.BlockSpec((1, hd), lambda b, h: (0, 0)),
            pl.BlockSpec((1, hd), lambda b, h: (0, 0)),
        ],
        out_specs=(pl.BlockSpec((seq, qw), lambda b, h: (b, h)),
                   pl.BlockSpec((seq, kw), lambda b, h: (b, h)),
                   pl.BlockSpec((seq, kw), lambda b, h: (b, h))),
        compiler_params=_params(("parallel", "parallel")),
        name="ctx_attention",
    )(qkv, qkv, qkv, q_g.reshape(1, hd), k_g.reshape(1, hd))


def _gqa_kernel(q_ref, k_ref, v_ref, ck_ref, cv_ref, qg_ref, kg_ref, cosq_ref, sinq_ref, cosk_ref, sink_ref,
                o_ref, kall_ref, vall_ref, *, groups, seq, past):
    lane = lax.broadcasted_iota(jnp.int32, (1, HEAD_DIM), 1)
    first = (lane % 64) < 32

    @pl.when(pl.program_id(2) == 0)
    def _():
        kn = _head_norm(k_ref[...], kg_ref[...])
        kall_ref[0:seq, :] = _rope(kn, cosk_ref[...], sink_ref[...], first).astype(BF16)
        kall_ref[seq:seq + past, :] = ck_ref[...].astype(BF16)
        vall_ref[0:seq, :] = _with_ones_column(v_ref[...])
        vall_ref[seq:seq + past, :] = _with_ones_column(cv_ref[...])

    cos = cosq_ref[...]
    sin = sinq_ref[...]
    qg = qg_ref[...] * _Q_SCALE
    for g in range(groups):
        qn = _head_norm(q_ref[:, g * HEAD_DIM:(g + 1) * HEAD_DIM], qg)
        qr = _rope(qn, cos, sin, first).astype(BF16)
        o_ref[:, g * HEAD_DIM:(g + 1) * HEAD_DIM] = _softmax2_pv(_nt(qr, kall_ref[...]), vall_ref[...]).astype(o_ref.dtype)


def gqa_attention(qkv, cache_k, cache_v, q_g, k_g, cos, sin, *, batch, seq, past, n_q, n_kv, tq=256):
    m = qkv.shape[0]
    groups = n_q // n_kv
    hd = HEAD_DIM
    nqb = seq // tq
    kern = functools.partial(_gqa_kernel, groups=groups, seq=seq, past=past)
    return pl.pallas_call(
        kern,
        out_shape=jax.ShapeDtypeStruct((m, n_q * hd), BF16),
        grid=(batch, n_kv, nqb),
        in_specs=[
            pl.BlockSpec((tq, groups * hd), lambda b, h, i: (b * nqb + i, h)),
            pl.BlockSpec((seq, hd), lambda b, h, i: (b, n_q + h)),
            pl.BlockSpec((seq, hd), lambda b, h, i: (b, n_q + n_kv + h)),
            pl.BlockSpec((past, hd), lambda b, h, i: (b, h)),
            pl.BlockSpec((past, hd), lambda b, h, i: (b, h)),
            pl.BlockSpec((1, hd), lambda b, h, i: (0, 0)),
            pl.BlockSpec((1, hd), lambda b, h, i: (0, 0)),
            pl.BlockSpec((tq, hd), lambda b, h, i: (i, 0)),
            pl.BlockSpec((tq, hd), lambda b, h, i: (i, 0)),
            pl.BlockSpec((seq, hd), lambda b, h, i: (0, 0)),
            pl.BlockSpec((seq, hd), lambda b, h, i: (0, 0)),
        ],
        out_specs=pl.BlockSpec((tq, groups * hd), lambda b, h, i: (b * nqb + i, h)),
        scratch_shapes=[pltpu.VMEM((seq + past, hd), BF16), pltpu.VMEM((seq + past, 2 * hd), BF16)],
        compiler_params=_params(("parallel", "parallel", "arbitrary")),
        name="gqa_attention",
    )(qkv, qkv, qkv, cache_k, cache_v, q_g.reshape(1, hd), k_g.reshape(1, hd), cos, sin, cos, sin)


def _na_kernel(rpb_ref, q_ref, k_ref, v_ref, ck_ref, cv_ref, qg_ref, kg_ref, o_ref,
               qn_ref, kn_ref, vb_ref, ckb_ref, cvb_ref, tile_ref, pair_ref, *, rows, wr, n_dr, n_dc, unroll):
    h = pl.program_id(0)
    w = GRID_W

    qn_ref[...] = _head_norm(q_ref[...], qg_ref[...] * _Q_SCALE).astype(BF16)
    kn_ref[...] = _head_norm(k_ref[...], kg_ref[...]).astype(BF16)
    vb_ref[...] = _with_ones_column(v_ref[...])
    ckb_ref[...] = ck_ref[...].astype(BF16)
    cvb_ref[...] = _with_ones_column(cv_ref[...])

    @pl.when(pl.program_id(1) == 0)
    def _():
        qc = lax.broadcasted_iota(jnp.int32, (w, 2 * w), 0)
        lane = lax.broadcasted_iota(jnp.int32, (w, 2 * w), 1)
        kc = lane % w
        cs = jnp.clip(qc - WIN_COLS // 2, 0, w - WIN_COLS)
        col_ok = (kc >= cs) & (kc < cs + WIN_COLS)
        dc = jnp.clip(kc - qc + WIN_COLS - 1, 0, n_dc - 1)
        tiles = [jnp.zeros((w, 2 * w), F32) for _ in range(n_dr)]
        for d in range(n_dc):
            sel = dc == d
            for dr in range(n_dr):
                tiles[dr] = jnp.where(sel, rpb_ref[h, dr * n_dc + d], tiles[dr])
        for dr in range(n_dr):
            tile_ref[dr] = jnp.where(col_ok, tiles[dr] * LOG2E, NEG_INF)
        for dr in range(n_dr - 1):
            pair_ref[dr] = jnp.where(lane < w, tile_ref[dr], tile_ref[dr + 1])

    zero_bias = jnp.zeros((w, ckb_ref.shape[0]), F32)

    def body(i, _):
        rws = [i * unroll + u for u in range(unroll)]
        rss = [jnp.clip(r - wr // 2, 0, rows - wr) for r in rws]
        k0s = [pl.multiple_of(rs * w, w) for rs in rss]
        scores = []
        for r, rs, k0 in zip(rws, rss, k0s):
            dr0 = rs - r + WIN_ROWS - 1
            q_r = qn_ref[pl.ds(pl.multiple_of(r * w, w), w), :]
            kcat = jnp.concatenate([kn_ref[pl.ds(k0, wr * w), :], ckb_ref[...]], axis=0)
            bias = jnp.concatenate([pair_ref[dr0 + 2 * j] for j in range(wr // 2)] + [zero_bias], axis=1)
            scores.append(_nt(q_r, kcat) + bias)
        probs = [jnp.exp2(s - jnp.max(s, axis=-1, keepdims=True)).astype(BF16) for s in scores]
        for r, k0, p in zip(rws, k0s, probs):
            vcat = jnp.concatenate([vb_ref[pl.ds(k0, wr * w), :], cvb_ref[...]], axis=0)
            pv = _mm(p, vcat)
            o = pv[:, :HEAD_DIM] / pv[:, HEAD_DIM:HEAD_DIM + 1]
            o_ref[pl.ds(pl.multiple_of(r * w, w), w), :] = o.astype(o_ref.dtype)
        return 0

    lax.fori_loop(0, rows // unroll, body, 0)


def na_attention(qkv, cache_k, cache_v, q_g, k_g, rpb, *, batch, seq, past, n_heads, unroll=16):
    m = qkv.shape[0]
    hd = HEAD_DIM
    rows = seq // GRID_W
    wr = min(WIN_ROWS, rows)
    n_dr, n_dc = rpb.shape[1], rpb.shape[2]
    assert wr == WIN_ROWS and wr % 2 == 0 and rows % unroll == 0
    kern = functools.partial(_na_kernel, rows=rows, wr=wr, n_dr=n_dr, n_dc=n_dc, unroll=unroll)
    return pl.pallas_call(
        kern,
        out_shape=jax.ShapeDtypeStruct((m, n_heads * hd), BF16),
        grid=(n_heads, batch),
        in_specs=[
            pl.BlockSpec(memory_space=pltpu.SMEM),
            pl.BlockSpec((seq, hd), lambda h, b: (b, h)),
            pl.BlockSpec((seq, hd), lambda h, b: (b, n_heads + h)),
            pl.BlockSpec((seq, hd), lambda h, b: (b, 2 * n_heads + h)),
            pl.BlockSpec((past, hd), lambda h, b: (b, h)),
            pl.BlockSpec((past, hd), lambda h, b: (b, h)),
            pl.BlockSpec((1, hd), lambda h, b: (0, 0)),
            pl.BlockSpec((1, hd), lambda h, b: (0, 0)),
        ],
        out_specs=pl.BlockSpec((seq, hd), lambda h, b: (b, h)),
        scratch_shapes=[
            pltpu.VMEM((seq, hd), BF16), pltpu.VMEM((seq, hd), BF16), pltpu.VMEM((seq, 2 * hd), BF16),
            pltpu.VMEM((past, hd), BF16), pltpu.VMEM((past, 2 * hd), BF16),
            pltpu.VMEM((n_dr, GRID_W, 2 * GRID_W), F32),
            pltpu.VMEM((n_dr - 1, GRID_W, 2 * GRID_W), F32),
        ],
        compiler_params=_params(("parallel", "arbitrary")),
        name="na_attention",
    )(rpb.reshape(n_heads, n_dr * n_dc), qkv, qkv, qkv, cache_k, cache_v,
      q_g.reshape(1, hd), k_g.reshape(1, hd))


def _gates_kernel(xn_ref, w_ref, b_ref, out_ref, *, tm):
    L = MLSTM_CHUNK
    wh, wl = _split2(w_ref[...])
    xn = xn_ref[...]
    pre = _nt(wh, xn) + _nt(wl, xn) + b_ref[...]
    capped = GATE_SOFTCAP * jnp.tanh(pre / GATE_SOFTCAP)
    row = lax.broadcasted_iota(jnp.int32, (capped.shape[0], L), 0) % 8
    is_input = (row == 0) | (row == 2)
    ri = lax.broadcasted_iota(jnp.int32, (L, L), 0)
    ci = lax.broadcasted_iota(jnp.int32, (L, L), 1)
    upper = jnp.where(ri <= ci, 1.0, 0.0).astype(BF16)
    lower = jnp.where(ri >= ci, 1.0, 0.0).astype(BF16)
    for c in range(tm // L):
        cap = capped[:, c * L:(c + 1) * L]
        gates = jnp.where(is_input, cap, jax.nn.log_sigmoid(cap))
        pieces = _split3(gates)
        prefix = sum(_mm(p, upper) for p in pieces)
        suffix = sum(_mm(p, lower) for p in pieces)
        out_ref[c] = jnp.where(row == 4, prefix, jnp.where(row == 5, suffix, gates))


def mlstm_gates(xn, w_gate_t, b_gate_t, *, tm=512):
    m, d = xn.shape
    gh = w_gate_t.shape[0]
    return pl.pallas_call(
        functools.partial(_gates_kernel, tm=tm),
        out_shape=jax.ShapeDtypeStruct((m // MLSTM_CHUNK, gh, MLSTM_CHUNK), F32),
        grid=(m // tm,),
        in_specs=[
            pl.BlockSpec((tm, d), lambda i: (i, 0)),
            pl.BlockSpec((gh, d), lambda i: (0, 0)),
            pl.BlockSpec((gh, 1), lambda i: (0, 0)),
        ],
        out_specs=pl.BlockSpec((tm // MLSTM_CHUNK, gh, MLSTM_CHUNK), lambda i: (i, 0, 0)),
        compiler_params=_params(("parallel",)),
        name="mlstm_gates",
    )(xn, w_gate_t, b_gate_t)


def _mlstm_kernel(*refs, n_chunks, heads, zero_init, write_state, has_prev_states):
    it = iter(refs)
    q_ref, k_ref, v_ref, o_ref, gates_ref, hg_ref = (next(it) for _ in range(6))
    if not zero_init:
        c0_ref, n0_ref, m0_ref = (next(it) for _ in range(3))
    if has_prev_states:
        next(it)
    y_ref = next(it)
    if write_state:
        cf_ref, nf_ref, mf_ref = (next(it) for _ in range(3))
    hdir_ref, cst_ref = next(it), next(it)

    L, dk, dv = MLSTM_CHUNK, MLSTM_DK, MLSTM_DV
    ri = lax.broadcasted_iota(jnp.int32, (L, L), 0)
    ci = lax.broadcasted_iota(jnp.int32, (L, L), 1)
    masks = (ri >= ci, ri <= ci)
    eye = ri == ci
    eye_b = jnp.where(eye, 1.0, 0.0).astype(BF16)
    qscale = dk ** -0.5

    m_init, n_init = [], []
    for hh in range(heads):
        for direction in range(2):
            idx = hh * 2 + direction
            if zero_init:
                cst_ref[idx] = jnp.zeros((dk, dv), F32)
                m_init.append(jnp.zeros((1, 1), F32))
                n_init.append(jnp.zeros((1, dk), F32))
            else:
                cst_ref[idx] = c0_ref[direction, hh]
                m_init.append(m0_ref[direction, hh])
                n_init.append(n0_ref[direction, hh])

    def body(step, carry):
        ms, ns = carry
        chains = [(hh, direction) for hh in range(heads) for direction in range(2)]
        r0s = [pl.multiple_of((step if d == 0 else n_chunks - 1 - step) * L, L) for _, d in chains]
        cs = [step if d == 0 else n_chunks - 1 - step for _, d in chains]

        st1 = []
        for idx, (hh, d) in enumerate(chains):
            gt = gates_ref[cs[idx], hh * 8:(hh + 1) * 8, :]
            li, lf, brow = gt[2 * d:2 * d + 1], gt[2 * d + 1:2 * d + 2], gt[4 + d:5 + d]
            b3 = _nt(eye_b, jnp.concatenate(_split3(jnp.broadcast_to(brow, (L, L))), axis=0))
            bmat = b3[:, :L] + b3[:, L:2 * L] + b3[:, 2 * L:]
            log_d = jnp.where(masks[d], bmat - brow + li, NEG_INF)
            inter = bmat + ms[idx]
            m_j = jnp.maximum(inter, jnp.max(log_d, axis=-1, keepdims=True))
            tot = jnp.sum(lf, axis=-1, keepdims=True)
            log_w = tot - brow + li
            m_new = jnp.maximum(tot + ms[idx], jnp.max(log_w, axis=-1, keepdims=True))
            st1.append((log_d, inter, m_j, tot, log_w, m_new))

        st2 = []
        for idx, (hh, d) in enumerate(chains):
            log_d, inter, m_j, tot, log_w, m_new = st1[idx]
            qf = q_ref[pl.ds(r0s[idx], L), hh * dk:(hh + 1) * dk] * qscale
            q = qf.astype(BF16)
            k = k_ref[pl.ds(r0s[idx], L), hh * dk:(hh + 1) * dk]
            k_hi, k_lo = _split2(k)
            qk = _nt(q, k_hi)
            kt2 = _nt(eye_b, jnp.concatenate([k_hi, k_lo], axis=0))
            kt = kt2[:, :L] + kt2[:, L:]
            qc = _mm(q, cst_ref[idx].astype(BF16))
            qn = jnp.sum(qf * ns[idx], axis=-1, keepdims=True)
            d_mat = jnp.exp(log_d - m_j)
            w_inter = jnp.exp(inter - m_j)
            wt = jnp.exp(log_w - m_new)
            decay = jnp.exp(tot + ms[idx] - m_new)
            ktw = (kt * wt).astype(BF16)
            wn = _mm(jnp.broadcast_to(wt, (8, L)).astype(BF16), k_hi)[:1]
            st2.append((qk * d_mat, qc, qn, w_inter, decay, ktw, decay * ns[idx] + wn))

        for idx, (hh, d) in enumerate(chains):
            s, qc, qn, w_inter, decay, ktw, _ = st2[idx]
            floor = jnp.exp(-st1[idx][2])
            v = v_ref[pl.ds(r0s[idx], L), hh * dv:(hh + 1) * dv].astype(BF16)
            both = _mm(jnp.concatenate([s.astype(BF16), ktw], axis=0), v)
            num = jnp.concatenate([w_inter] * 2, axis=1) * qc + both[:L]
            den = jnp.maximum(jnp.abs(w_inter * qn + jnp.sum(s, axis=-1, keepdims=True)), floor)
            h = num / jnp.concatenate([den, den], axis=1)
            hdir_ref[d, pl.ds(r0s[idx], L), hh * dv:(hh + 1) * dv] = h
            cst_ref[idx] = decay * cst_ref[idx] + both[L:]
        return tuple(s1[5] for s1 in st1), tuple(s2[6] for s2 in st2)

    m_fin, n_fin = lax.fori_loop(0, n_chunks, body, (tuple(m_init), tuple(n_init)))

    def finish(c, _):
        r0 = pl.multiple_of(c * L, L)
        for hh in range(heads):
            cols = slice(hh * dv, (hh + 1) * dv)
            hs = hdir_ref[0, pl.ds(r0, L), cols] + hdir_ref[1, pl.ds(r0, L), cols]
            hn = hs * lax.rsqrt(jnp.mean(hs * hs, axis=-1, keepdims=True) + EPS) * hg_ref[:, cols]
            y_ref[pl.ds(r0, L), cols] = (hn * jax.nn.sigmoid(o_ref[pl.ds(r0, L), cols])).astype(y_ref.dtype)
        return 0

    lax.fori_loop(0, n_chunks, finish, 0)

    if write_state:
        for hh in range(heads):
            for direction in range(2):
                idx = hh * 2 + direction
                cf_ref[direction, hh] = cst_ref[idx]
                nf_ref[direction, hh] = n_fin[idx]
                mf_ref[direction, hh] = m_fin[idx]


def mlstm_scan(proj, gates, head_g, state=None, *, batch, seq, n_heads, write_state, heads=2,
               state_slot=(0, 1), prev_states=None):
    m = proj.shape[0]
    dk, dv, L = MLSTM_DK, MLSTM_DV, MLSTM_CHUNK
    H = n_heads
    nb = H // heads
    n_chunks = seq // L
    zero_init = state is None
    slot, n_slots = state_slot
    kern = functools.partial(_mlstm_kernel, n_chunks=n_chunks, heads=heads, zero_init=zero_init,
                             write_state=write_state, has_prev_states=prev_states is not None)
    in_specs = [
        pl.BlockSpec((seq, heads * dk), lambda b, h: (b, h)),
        pl.BlockSpec((seq, heads * dk), lambda b, h: (b, nb + h)),
        pl.BlockSpec((seq, heads * dv), lambda b, h: (b, nb + h)),
        pl.BlockSpec((seq, heads * dv), lambda b, h: (b, 2 * nb + h)),
        pl.BlockSpec((n_chunks, heads * 8, L), lambda b, h: (b, h, 0)),
        pl.BlockSpec((None, 1, heads * dv), lambda b, h: (h, 0, 0)),
    ]
    args = [proj, proj, proj, proj, gates, head_g.reshape(nb, 1, heads * dv)]
    st_specs = [pl.BlockSpec((None, 2, heads, dk, dv), lambda b, h: (b, 0, h, 0, 0)),
                pl.BlockSpec((None, 2, heads, 1, dk), lambda b, h: (b, 0, h, 0, 0)),
                pl.BlockSpec((None, 2, heads, 1, 1), lambda b, h: (b, 0, h, 0, 0))]
    if not zero_init:
        c0_all, c0_slot, n0, m0 = state
        in_specs += [pl.BlockSpec((None, None, 2, heads, dk, dv), lambda b, h: (b, c0_slot, 0, h, 0, 0))] + st_specs[1:]
        args += [c0_all.astype(F32), n0.astype(F32).reshape(batch, 2, H, 1, dk),
                 m0.astype(F32).reshape(batch, 2, H, 1, 1)]
    aliases = {}
    if prev_states is not None:
        aliases = {len(args): 1}
        in_specs.append(pl.BlockSpec(memory_space=pl.ANY))
        args.append(prev_states)
    out_shape = [jax.ShapeDtypeStruct((m, H * dv), BF16)]
    out_specs = [pl.BlockSpec((seq, heads * dv), lambda b, h: (b, h))]
    if write_state:
        out_shape += [jax.ShapeDtypeStruct((batch, n_slots, 2, H, dk, dv), F32),
                      jax.ShapeDtypeStruct((batch, 2, H, 1, dk), F32),
                      jax.ShapeDtypeStruct((batch, 2, H, 1, 1), F32)]
        out_specs += [pl.BlockSpec((None, None, 2, heads, dk, dv), lambda b, h: (b, slot, 0, h, 0, 0))] + st_specs[1:]
    outs = pl.pallas_call(
        kern,
        out_shape=tuple(out_shape),
        grid=(batch, nb),
        in_specs=in_specs,
        out_specs=tuple(out_specs),
        scratch_shapes=[pltpu.VMEM((2, seq, heads * dv), F32), pltpu.VMEM((2 * heads, dk, dv), F32)],
        input_output_aliases=aliases,
        compiler_params=_params(("parallel", "parallel")),
        name="mlstm_scan",
    )(*args)
    if write_state:
        y, cf, nf, mf = outs
        return y, (cf, nf.reshape(batch, 2, H, dk), mf.reshape(batch, 2, H))
    return outs[0], None


def _gate_weights_head_major(w_gate, b_gate, n_heads):
    d = w_gate.shape[0]
    wt = jnp.transpose(w_gate.reshape(d, 4, n_heads), (2, 1, 0)).astype(F32)
    wt = jnp.concatenate([wt, wt[:, 1:2], wt[:, 3:4], jnp.zeros_like(wt[:, :2])], axis=1).reshape(8 * n_heads, d)
    bt = jnp.transpose(b_gate.reshape(4, n_heads), (1, 0)).astype(F32)
    bt = jnp.concatenate([bt, bt[:, 1:2], bt[:, 3:4], jnp.zeros_like(bt[:, :2])], axis=1).reshape(8 * n_heads, 1)
    return wt, bt


def _rope_tables(seq):
    nf = HEAD_DIM // 4
    t = jnp.arange(seq)
    inv = ROPE_THETA ** (-jnp.arange(nf, dtype=F32) / nf)
    pos = jnp.stack([t // GRID_W, t % GRID_W], axis=-1).astype(F32)
    ang = pos[:, :, None] * inv
    cos = jnp.cos(ang)
    sin = jnp.sin(ang)
    cos_full = jnp.stack([cos, cos], axis=2).reshape(seq, HEAD_DIM)
    sin_full = jnp.stack([-sin, sin], axis=2).reshape(seq, HEAD_DIM)
    return cos_full, sin_full


def kernel(x_prompt, x_sample, state_mlstm_C, state_mlstm_n, state_mlstm_m, cache_gqa_k, cache_gqa_v,
           cache_na_k, cache_na_v, c, c_ctx, norm1_g, norm2_g, w_ada, b_ada, w_mlp1, w_mlp2,
           mlstm_w_in, mlstm_w_gate, mlstm_b_gate, mlstm_head_g, mlstm_w_out,
           gqa_w_qkv, gqa_q_g, gqa_k_g, gqa_w_o, na_w_qkv, na_q_g, na_k_g, na_rpb, na_w_o):
    bp, tp, d = x_prompt.shape
    bs, ts, _ = x_sample.shape
    depth = w_ada.shape[0]
    past = cache_gqa_k.shape[2]
    mh = mlstm_w_gate.shape[-1] // 4
    gqa_kv = cache_gqa_k.shape[3]
    gqa_q = gqa_w_o.shape[1] // HEAD_DIM
    na_h = cache_na_k.shape[3]
    assert bs + 1 <= 8

    cond8 = jnp.zeros((8, d), F32).at[:bs].set(c).at[bs].set(c_ctx)
    mod = adaln_all(cond8, w_ada, b_ada).reshape(depth * 48, 1, d)
    g1 = norm1_g.reshape(depth, 1, d)
    g2 = norm2_g.reshape(depth, 1, d)
    cos, sin = _rope_tables(ts)

    groups = [
        dict(x=x_prompt.reshape(bp * tp, d), batch=bp, seq=tp, rpb_=bp * tp, row0=bs, ctx=True),
        dict(x=x_sample.reshape(bs * ts, d), batch=bs, seq=ts, rpb_=ts, row0=0, ctx=False),
    ]
    for grp in groups:
        grp["xn"] = modulate(grp["x"], g1, mod, layer=0, which_shift=0, which_scale=1,
                             rows_per_batch=grp["rpb_"], row0=grp["row0"])
    w_in_all = [mlstm_w_in.astype(BF16), gqa_w_qkv.astype(BF16), na_w_qkv.astype(BF16)]
    w_o_all = [mlstm_w_out.astype(BF16), gqa_w_o.astype(BF16), na_w_o.astype(BF16)]
    n_mlstm = state_mlstm_C.shape[1]
    c_all = jnp.zeros((bp, n_mlstm, 2, mh, MLSTM_DK, MLSTM_DV), F32)
    n_l, m_l, gk_l, gv_l, nk_l, nv_l = [], [], [], [], [], []

    for i in range(depth):
        kind, j = i % N_MIXERS, i // N_MIXERS
        last = i == depth - 1
        nxt = i if last else i + 1
        w_in, w_o = w_in_all[kind], w_o_all[kind]
        if kind == 0:
            wg_t, bg_t = _gate_weights_head_major(mlstm_w_gate[j], mlstm_b_gate[j], mh)

        for grp in groups:
            x, xn, batch, seq, ctx = grp["x"], grp["xn"], grp["batch"], grp["seq"], grp["ctx"]
            mk = dict(rows_per_batch=grp["rpb_"], row0=grp["row0"])
            if ctx:
                proj, w1, w2 = matmul_and_cast(xn, w_in, j, w_mlp1, w_mlp2, i, fc=MLP_HIDDEN_CHUNK)
            else:
                proj = matmul(xn, w_in, j)
            if kind == 0:
                gates = mlstm_gates(xn, wg_t, bg_t)
                if ctx:
                    a, (c_all, nf, mf) = mlstm_scan(proj, gates, mlstm_head_g[j], None, batch=batch, seq=seq,
                                                    n_heads=mh, write_state=True, heads=4,
                                                    state_slot=(j, n_mlstm), prev_states=c_all)
                    n_l.append(nf)
                    m_l.append(mf)
                else:
                    st = (state_mlstm_C, j, state_mlstm_n[:, j], state_mlstm_m[:, j])
                    a, _ = mlstm_scan(proj, gates, mlstm_head_g[j], st, batch=batch, seq=seq,
                                      n_heads=mh, write_state=False)
            elif kind == 1:
                if ctx:
                    a, kn, v = ctx_attention(proj, gqa_q_g[j], gqa_k_g[j], batch=batch, seq=seq, n_q=gqa_q, n_kv=gqa_kv)
                    gk_l.append(kn.reshape(batch, seq, gqa_kv, HEAD_DIM))
                    gv_l.append(v.reshape(batch, seq, gqa_kv, HEAD_DIM))
                else:
                    ck = cache_gqa_k[:, j].astype(F32).reshape(batch * past, gqa_kv * HEAD_DIM)
                    cv = cache_gqa_v[:, j].astype(F32).reshape(batch * past, gqa_kv * HEAD_DIM)
                    a = gqa_attention(proj, ck, cv, gqa_q_g[j], gqa_k_g[j], cos, sin, batch=batch, seq=seq,
                                      past=past, n_q=gqa_q, n_kv=gqa_kv)
            else:
                if ctx:
                    a, kn, v = ctx_attention(proj, na_q_g[j], na_k_g[j], batch=batch, seq=seq, n_q=na_h, n_kv=na_h,
                                             heads=8)
                    nk_l.append(kn.reshape(batch, seq, na_h, HEAD_DIM))
                    nv_l.append(v.reshape(batch, seq, na_h, HEAD_DIM))
                else:
                    ck = cache_na_k[:, j].astype(F32).reshape(batch * past, na_h * HEAD_DIM)
                    cv = cache_na_v[:, j].astype(F32).reshape(batch * past, na_h * HEAD_DIM)
                    a = na_attention(proj, ck, cv, na_q_g[j], na_k_g[j], na_rpb[j], batch=batch, seq=seq,
                                     past=past, n_heads=na_h)
            x, xn = mm_res_norm(a, w_o, j, x, mod, g2, layer=i, which_gate=2, next_layer=i, next_shift=3,
                                next_scale=4, **mk)
            x, xn = fused_mlp(xn, w1, w2, x, mod, g1, layer=i, next_layer=nxt, emit_next=not last, **mk)
            grp["x"], grp["xn"] = x, xn

    y_prompt = groups[0]["x"].reshape(bp, tp, d)
    y_sample = groups[1]["x"].reshape(bs, ts, d)
    return (y_prompt, y_sample, c_all, jnp.stack(n_l, axis=1), jnp.stack(m_l, axis=1),
            jnp.stack(gk_l, axis=1), jnp.stack(gv_l, axis=1), jnp.stack(nk_l, axis=1), jnp.stack(nv_l, axis=1))
```

```python
import functools
import math

import jax
import jax.numpy as jnp
from jax import lax
from jax.experimental import pallas as pl
from jax.experimental.pallas import tpu as pltpu

EPS = 1e-6
HEAD_DIM = 128
MLSTM_DK = 128
MLSTM_DV = 256
MLSTM_CHUNK = 128
GATE_SOFTCAP = 15.0
GRID_W = 64
WIN_ROWS = 8
WIN_COLS = 16
ROPE_THETA = 10000.0
N_MIXERS = 3
LOG2E = math.log2(math.e)

VMEM_LIMIT_BYTES = 56 * 1024 * 1024
NORM_ROW_CHUNK = 16
MLP_HIDDEN_CHUNK = 1024
PROJ_COL_TILE = 768

F32 = jnp.float32
BF16 = jnp.bfloat16
NEG_INF = float("-inf")


def _params(sem):
    return pltpu.CompilerParams(dimension_semantics=sem, vmem_limit_bytes=VMEM_LIMIT_BYTES)


def _nt(a, b):
    return lax.dot_general(a, b, (((1,), (1,)), ((), ())), preferred_element_type=F32)


def _mm(a, b):
    return jnp.dot(a, b, preferred_element_type=F32)


def _split2(x):
    hi = x.astype(BF16)
    lo = (x - hi.astype(F32)).astype(BF16)
    return hi, lo


def _split3(x):
    hi = x.astype(BF16)
    r = x - hi.astype(F32)
    mid = r.astype(BF16)
    lo = (r - mid.astype(F32)).astype(BF16)
    return hi, mid, lo


def _mod_row(layer, which, rows_per_batch, tm, row0):
    def f(i, *_):
        b = row0 + (i * tm) // rows_per_batch
        return (layer * 48 + b * 6 + which, 0, 0)
    return f


def _adaln_kernel(cond_ref, w_ref, b_ref, out_ref):
    c = cond_ref[...]
    s = c * jax.nn.sigmoid(c)
    sh, sl = _split2(s)
    wh, wl = _split2(w_ref[...])
    lhs = jnp.concatenate([sh, sl], axis=0)
    r = _mm(lhs, wh)
    out_ref[...] = r[:8] + r[8:] + _mm(sh, wl) + b_ref[...]


def adaln_all(cond8, w_ada, b_ada, tn=1024):
    nl, d, n = w_ada.shape
    return pl.pallas_call(
        _adaln_kernel,
        out_shape=jax.ShapeDtypeStruct((nl, 8, n), F32),
        grid=(nl, n // tn),
        in_specs=[
            pl.BlockSpec((8, d), lambda l, j: (0, 0)),
            pl.BlockSpec((None, d, tn), lambda l, j: (l, 0, j)),
            pl.BlockSpec((None, 1, tn), lambda l, j: (l, 0, j)),
        ],
        out_specs=pl.BlockSpec((None, 8, tn), lambda l, j: (l, 0, j)),
        compiler_params=_params(("parallel", "parallel")),
        name="adaln",
    )(cond8, w_ada, b_ada.reshape(nl, 1, n))


def _modulate_rows(src_ref, dst_ref, g, mul, sh, r0, n_rows):
    for c in range(n_rows // NORM_ROW_CHUNK):
        start = r0 + c * NORM_ROW_CHUNK
        if not isinstance(start, int):
            start = pl.multiple_of(start, NORM_ROW_CHUNK)
        rows = pl.ds(start, NORM_ROW_CHUNK)
        xf = src_ref[rows, :]
        y = xf * lax.rsqrt(jnp.mean(xf * xf, axis=-1, keepdims=True) + EPS)
        dst_ref[rows, :] = ((y * g) * mul + sh).astype(dst_ref.dtype)


def _modulate_kernel(x_ref, g_ref, sh_ref, sc_ref, out_ref, *, tm):
    g = g_ref[...]
    mul = 1.0 + sc_ref[...]
    sh = sh_ref[...]
    unroll = 4 * NORM_ROW_CHUNK

    def body(i, _):
        _modulate_rows(x_ref, out_ref, g, mul, sh, pl.multiple_of(i * unroll, unroll), unroll)
        return 0

    lax.fori_loop(0, tm // unroll, body, 0)


def modulate(x, g, mod, *, layer, which_shift, which_scale, rows_per_batch, row0, tm=512):
    m, d = x.shape
    tm = min(tm, rows_per_batch)
    return pl.pallas_call(
        functools.partial(_modulate_kernel, tm=tm),
        out_shape=jax.ShapeDtypeStruct((m, d), BF16),
        grid=(m // tm,),
        in_specs=[
            pl.BlockSpec((tm, d), lambda i: (i, 0)),
            pl.BlockSpec((None, 1, d), lambda i: (layer, 0, 0)),
            pl.BlockSpec((None, 1, d), _mod_row(layer, which_shift, rows_per_batch, tm, row0)),
            pl.BlockSpec((None, 1, d), _mod_row(layer, which_scale, rows_per_batch, tm, row0)),
        ],
        out_specs=pl.BlockSpec((tm, d), lambda i: (i, 0)),
        compiler_params=_params(("parallel",)),
        name="modulate",
    )(x, g, mod, mod)


def _mm_kernel(a_ref, w_ref, out_ref):
    out_ref[...] = _mm(a_ref[...], w_ref[...]).astype(out_ref.dtype)


def _mm_cast_kernel(a_ref, w_ref, w1_ref, w2_ref, out_ref, w1o_ref, w2o_ref):
    out_ref[...] = _mm(a_ref[...], w_ref[...]).astype(out_ref.dtype)
    fc = w1o_ref.shape[2]
    for t in range(w1o_ref.shape[0]):
        w1o_ref[t] = w1_ref[:, t * fc:(t + 1) * fc].astype(BF16)
    w2o_ref[...] = w2_ref[...].astype(BF16)


def matmul_and_cast(a, w, wl, w1, w2, layer, *, fc, out_dtype=F32, tm=1024, tn=PROJ_COL_TILE, name="proj_cast"):
    m, k = a.shape
    nt = w.shape[2] // tn
    ni = m // tm
    _, d, ff = w1.shape
    r1 = d // ni
    c1 = ff // nt
    r2 = ff // (ni * nt)
    assert d % ni == 0 and ff % nt == 0 and c1 % fc == 0 and ff % (ni * nt) == 0 and r1 % 16 == 0 and r2 % 16 == 0
    return pl.pallas_call(
        _mm_cast_kernel,
        out_shape=(jax.ShapeDtypeStruct((m, nt * tn), out_dtype),
                   jax.ShapeDtypeStruct((ff // fc, d, fc), BF16),
                   jax.ShapeDtypeStruct((ff, d), BF16)),
        grid=(ni, nt),
        in_specs=[
            pl.BlockSpec((tm, k), lambda i, j: (i, 0)),
            pl.BlockSpec((None, k, tn), lambda i, j: (wl, 0, j)),
            pl.BlockSpec((None, r1, c1), lambda i, j: (layer, i, j)),
            pl.BlockSpec((None, r2, d), lambda i, j: (layer, i * nt + j, 0)),
        ],
        out_specs=(pl.BlockSpec((tm, tn), lambda i, j: (i, j)),
                   pl.BlockSpec((c1 // fc, r1, fc), lambda i, j: (j, i, 0)),
                   pl.BlockSpec((r2, d), lambda i, j: (i * nt + j, 0))),
        compiler_params=_params(("parallel", "arbitrary")),
        name=name,
    )(a, w, w1, w2)


def matmul(a, w, wl, *, out_dtype=F32, tm=1024, tn=1024, name="proj"):
    m, k = a.shape
    n = w.shape[2]
    return pl.pallas_call(
        _mm_kernel,
        out_shape=jax.ShapeDtypeStruct((m, n), out_dtype),
        grid=(m // tm, n // tn),
        in_specs=[
            pl.BlockSpec((tm, k), lambda i, j: (i, 0)),
            pl.BlockSpec((None, k, tn), lambda i, j: (wl, 0, j)),
        ],
        out_specs=pl.BlockSpec((tm, tn), lambda i, j: (i, j)),
        compiler_params=_params(("parallel", "arbitrary")),
        name=name,
    )(a, w)


def _mmres_kernel(a_ref, w_ref, res_ref, gate_ref, g_ref, sh_ref, sc_ref, x_ref, xn_ref, *, tm, sub):
    gate = gate_ref[...]
    g = g_ref[...]
    mul = 1.0 + sc_ref[...]
    sh = sh_ref[...]
    for s in range(tm // sub):
        rows = pl.ds(s * sub, sub)
        x_ref[rows, :] = res_ref[rows, :] + gate * _mm(a_ref[rows, :], w_ref[...])
        _modulate_rows(x_ref, xn_ref, g, mul, sh, s * sub, sub)


def mm_res_norm(a, w, wl, res, mod, g_next, *, layer, which_gate, next_layer, next_shift, next_scale,
                rows_per_batch, row0, tm=512, sub=256):
    m, k = a.shape
    d = w.shape[2]
    tm = min(tm, rows_per_batch)
    mk = (rows_per_batch, tm, row0)
    return pl.pallas_call(
        functools.partial(_mmres_kernel, tm=tm, sub=sub),
        out_shape=(jax.ShapeDtypeStruct((m, d), F32), jax.ShapeDtypeStruct((m, d), BF16)),
        grid=(m // tm,),
        in_specs=[
            pl.BlockSpec((tm, k), lambda i: (i, 0)),
            pl.BlockSpec((None, k, d), lambda i: (wl, 0, 0), pipeline_mode=pl.Buffered(1)),
            pl.BlockSpec((tm, d), lambda i: (i, 0)),
            pl.BlockSpec((None, 1, d), _mod_row(layer, which_gate, *mk)),
            pl.BlockSpec((None, 1, d), lambda i: (next_layer, 0, 0)),
            pl.BlockSpec((None, 1, d), _mod_row(next_layer, next_shift, *mk)),
            pl.BlockSpec((None, 1, d), _mod_row(next_layer, next_scale, *mk)),
        ],
        out_specs=(pl.BlockSpec((tm, d), lambda i: (i, 0)), pl.BlockSpec((tm, d), lambda i: (i, 0))),
        compiler_params=_params(("parallel",)),
        name="outproj_res_norm",
    )(a, w, res, mod, g_next, mod, mod)


def _mlp_kernel(xn_ref, w1_ref, w2_ref, res_ref, gate_ref, g_ref, sh_ref, sc_ref, x_ref, *rest, tm, sub, emit_next):
    xno_ref = rest[0] if emit_next else None
    acc_ref = rest[-1]
    f = pl.program_id(1)

    @pl.when(f == 0)
    def _():
        acc_ref[...] = jnp.zeros_like(acc_ref)

    def hidden_chunk(rows):
        h = _mm(xn_ref[rows, :], w1_ref[...])
        return _mm(jnp.square(jnp.maximum(h, 0.0)).astype(BF16), w2_ref[...])

    last = pl.num_programs(1) - 1

    @pl.when(f < last)
    def _():
        acc_ref[...] += hidden_chunk(slice(None))

    @pl.when(f == last)
    def _():
        gate = gate_ref[...]
        g = g_ref[...]
        mul = 1.0 + sc_ref[...]
        sh = sh_ref[...]
        for s in range(tm // sub):
            rows = pl.ds(s * sub, sub)
            x_ref[rows, :] = res_ref[rows, :] + gate * (acc_ref[rows, :] + hidden_chunk(rows))
            if emit_next:
                _modulate_rows(x_ref, xno_ref, g, mul, sh, s * sub, sub)


def fused_mlp(xn, w1, w2, res, mod, g_next, *, layer, next_layer, rows_per_batch, row0, emit_next,
              tm=512, sub=128):
    m, d = xn.shape
    fc = w1.shape[2]
    ff = w1.shape[0] * fc
    tm = min(tm, rows_per_batch)
    mk = (rows_per_batch, tm, row0)
    row_tile = pl.BlockSpec((tm, d), lambda i, f: (i, 0))
    out_shape = [jax.ShapeDtypeStruct((m, d), F32)] + ([jax.ShapeDtypeStruct((m, d), BF16)] if emit_next else [])
    outs = pl.pallas_call(
        functools.partial(_mlp_kernel, tm=tm, sub=min(sub, tm), emit_next=emit_next),
        out_shape=tuple(out_shape),
        grid=(m // tm, ff // fc),
        in_specs=[
            pl.BlockSpec((tm, d), lambda i, f: (i, 0)),
            pl.BlockSpec((None, d, fc), lambda i, f: (f, 0, 0)),
            pl.BlockSpec((fc, d), lambda i, f: (f, 0)),
            row_tile,
            pl.BlockSpec((None, 1, d), _mod_row(layer, 5, *mk)),
            pl.BlockSpec((None, 1, d), lambda i, f: (next_layer, 0, 0)),
            pl.BlockSpec((None, 1, d), _mod_row(next_layer, 0, *mk)),
            pl.BlockSpec((None, 1, d), _mod_row(next_layer, 1, *mk)),
        ],
        out_specs=tuple(row_tile for _ in out_shape),
        scratch_shapes=[pltpu.VMEM((tm, d), F32)],
        compiler_params=_params(("parallel", "arbitrary")),
        name="fused_mlp",
    )(xn, w1, w2, res, mod, g_next, mod, mod)
    return (outs[0], outs[1]) if emit_next else (outs[0], None)


def _head_norm(x, g):
    return x * lax.rsqrt(jnp.mean(x * x, axis=-1, keepdims=True) + EPS) * g


_Q_SCALE = HEAD_DIM ** -0.5 * LOG2E


def _with_ones_column(v):
    lane = lax.broadcasted_iota(jnp.int32, v.shape, 1)
    return jnp.concatenate([v.astype(BF16), jnp.where(lane == 0, 1.0, 0.0).astype(BF16)], axis=1)


def _softmax2_pv(s, vaug):
    p = jnp.exp2(s - jnp.max(s, axis=-1, keepdims=True)).astype(BF16)
    pv = _mm(p, vaug)
    return pv[:, :HEAD_DIM] / pv[:, HEAD_DIM:HEAD_DIM + 1]


def _rope(x, cos, sin_signed, lane_is_first):
    swapped = jnp.where(lane_is_first, pltpu.roll(x, 96, 1), pltpu.roll(x, 32, 1))
    return x * cos + swapped * sin_signed


def _ctx_attn_kernel(q_ref, k_ref, v_ref, qg_ref, kg_ref, o_ref, kn_ref, vo_ref, *, groups, heads):
    hd = HEAD_DIM
    qg = qg_ref[...] * _Q_SCALE
    for h in range(heads):
        hs = slice(h * hd, (h + 1) * hd)
        kn = _head_norm(k_ref[:, hs], kg_ref[...])
        kn_ref[:, hs] = kn
        v = v_ref[:, hs]
        vo_ref[:, hs] = v
        kb = kn.astype(BF16)
        vb = v.astype(BF16)
        for g in range(groups):
            qs = slice((h * groups + g) * hd, (h * groups + g + 1) * hd)
            qn = _head_norm(q_ref[:, qs], qg).astype(BF16)
            s = _nt(qn, kb)
            p = jnp.exp2(s - jnp.max(s, axis=-1, keepdims=True))
            o_ref[:, qs] = (_mm(p.astype(BF16), vb) / jnp.sum(p, axis=-1, keepdims=True)).astype(o_ref.dtype)


def ctx_attention(qkv, q_g, k_g, *, batch, seq, n_q, n_kv, heads=4):
    m = qkv.shape[0]
    groups = n_q // n_kv
    hd = HEAD_DIM
    kw = heads * hd
    qw = heads * groups * hd
    assert (n_q * hd) % kw == 0 and ((n_q + n_kv) * hd) % kw == 0
    k_blk0 = n_q * hd // kw
    v_blk0 = (n_q + n_kv) * hd // kw
    kern = functools.partial(_ctx_attn_kernel, groups=groups, heads=heads)
    return pl.pallas_call(
        kern,
        out_shape=(jax.ShapeDtypeStruct((m, n_q * hd), BF16),
                   jax.ShapeDtypeStruct((m, n_kv * hd), F32),
                   jax.ShapeDtypeStruct((m, n_kv * hd), F32)),
        grid=(batch, n_kv // heads),
        in_specs=[
            pl.BlockSpec((seq, qw), lambda b, h: (b, h)),
            pl.BlockSpec((seq, kw), lambda b, h: (b, k_blk0 + h)),
            pl.BlockSpec((seq, kw), lambda b, h: (b, v_blk0 + h)),
            pl.BlockSpec((1, hd), lambda b, h: (0, 0)),
            pl.BlockSpec((1, hd), lambda b, h: (0, 0)),
        ],
        out_specs=(pl.BlockSpec((seq, qw), lambda b, h: (b, h)),
                   pl.BlockSpec((seq, kw), lambda b, h: (b, h)),
                   pl.BlockSpec((seq, kw), lambda b, h: (b, h))),
        compiler_params=_params(("parallel", "parallel")),
        name="ctx_attention",
    )(qkv, qkv, qkv, q_g.reshape(1, hd), k_g.reshape(1, hd))


def _gqa_kernel(q_ref, k_ref, v_ref, ck_ref, cv_ref, qg_ref, kg_ref, cosq_ref, sinq_ref, cosk_ref, sink_ref,
                o_ref, kall_ref, vall_ref, *, groups, seq, past):
    lane = lax.broadcasted_iota(jnp.int32, (1, HEAD_DIM), 1)
    first = (lane % 64) < 32

    @pl.when(pl.program_id(2) == 0)
    def _():
        kn = _head_norm(k_ref[...], kg_ref[...])
        kall_ref[0:seq, :] = _rope(kn, cosk_ref[...], sink_ref[...], first).astype(BF16)
        kall_ref[seq:seq + past, :] = ck_ref[...].astype(BF16)
        vall_ref[0:seq, :] = _with_ones_column(v_ref[...])
        vall_ref[seq:seq + past, :] = _with_ones_column(cv_ref[...])

    cos = cosq_ref[...]
    sin = sinq_ref[...]
    qg = qg_ref[...] * _Q_SCALE
    for g in range(groups):
        qn = _head_norm(q_ref[:, g * HEAD_DIM:(g + 1) * HEAD_DIM], qg)
        qr = _rope(qn, cos, sin, first).astype(BF16)
        o_ref[:, g * HEAD_DIM:(g + 1) * HEAD_DIM] = _softmax2_pv(_nt(qr, kall_ref[...]), vall_ref[...]).astype(o_ref.dtype)


def gqa_attention(qkv, cache_k, cache_v, q_g, k_g, cos, sin, *, batch, seq, past, n_q, n_kv, tq=256):
    m = qkv.shape[0]
    groups = n_q // n_kv
    hd = HEAD_DIM
    nqb = seq // tq
    kern = functools.partial(_gqa_kernel, groups=groups, seq=seq, past=past)
    return pl.pallas_call(
        kern,
        out_shape=jax.ShapeDtypeStruct((m, n_q * hd), BF16),
        grid=(batch, n_kv, nqb),
        in_specs=[
            pl.BlockSpec((tq, groups * hd), lambda b, h, i: (b * nqb + i, h)),
            pl.BlockSpec((seq, hd), lambda b, h, i: (b, n_q + h)),
            pl.BlockSpec((seq, hd), lambda b, h, i: (b, n_q + n_kv + h)),
            pl.BlockSpec((past, hd), lambda b, h, i: (b, h)),
            pl.BlockSpec((past, hd), lambda b, h, i: (b, h)),
            pl.BlockSpec((1, hd), lambda b, h, i: (0, 0)),
            pl.BlockSpec((1, hd), lambda b, h, i: (0, 0)),
            pl.BlockSpec((tq, hd), lambda b, h, i: (i, 0)),
            pl.BlockSpec((tq, hd), lambda b, h, i: (i, 0)),
            pl.BlockSpec((seq, hd), lambda b, h, i: (0, 0)),
            pl.BlockSpec((seq, hd), lambda b, h, i: (0, 0)),
        ],
        out_specs=pl.BlockSpec((tq, groups * hd), lambda b, h, i: (b * nqb + i, h)),
        scratch_shapes=[pltpu.VMEM((seq + past, hd), BF16), pltpu.VMEM((seq + past, 2 * hd), BF16)],
        compiler_params=_params(("parallel", "parallel", "arbitrary")),
        name="gqa_attention",
    )(qkv, qkv, qkv, cache_k, cache_v, q_g.reshape(1, hd), k_g.reshape(1, hd), cos, sin, cos, sin)


def _na_kernel(rpb_ref, q_ref, k_ref, v_ref, ck_ref, cv_ref, qg_ref, kg_ref, o_ref,
               qn_ref, kn_ref, vb_ref, ckb_ref, cvb_ref, tile_ref, pair_ref, *, rows, wr, n_dr, n_dc, unroll):
    h = pl.program_id(0)
    w = GRID_W

    qn_ref[...] = _head_norm(q_ref[...], qg_ref[...] * _Q_SCALE).astype(BF16)
    kn_ref[...] = _head_norm(k_ref[...], kg_ref[...]).astype(BF16)
    vb_ref[...] = _with_ones_column(v_ref[...])
    ckb_ref[...] = ck_ref[...].astype(BF16)
    cvb_ref[...] = _with_ones_column(cv_ref[...])

    @pl.when(pl.program_id(1) == 0)
    def _():
        qc = lax.broadcasted_iota(jnp.int32, (w, 2 * w), 0)
        lane = lax.broadcasted_iota(jnp.int32, (w, 2 * w), 1)
        kc = lane % w
        cs = jnp.clip(qc - WIN_COLS // 2, 0, w - WIN_COLS)
        col_ok = (kc >= cs) & (kc < cs + WIN_COLS)
        dc = jnp.clip(kc - qc + WIN_COLS - 1, 0, n_dc - 1)
        tiles = [jnp.zeros((w, 2 * w), F32) for _ in range(n_dr)]
        for d in range(n_dc):
            sel = dc == d
            for dr in range(n_dr):
                tiles[dr] = jnp.where(sel, rpb_ref[h, dr * n_dc + d], tiles[dr])
        for dr in range(n_dr):
            tile_ref[dr] = jnp.where(col_ok, tiles[dr] * LOG2E, NEG_INF)
        for dr in range(n_dr - 1):
            pair_ref[dr] = jnp.where(lane < w, tile_ref[dr], tile_ref[dr + 1])

    zero_bias = jnp.zeros((w, ckb_ref.shape[0]), F32)

    def body(i, _):
        rws = [i * unroll + u for u in range(unroll)]
        rss = [jnp.clip(r - wr // 2, 0, rows - wr) for r in rws]
        k0s = [pl.multiple_of(rs * w, w) for rs in rss]
        scores = []
        for r, rs, k0 in zip(rws, rss, k0s):
            dr0 = rs - r + WIN_ROWS - 1
            q_r = qn_ref[pl.ds(pl.multiple_of(r * w, w), w), :]
            kcat = jnp.concatenate([kn_ref[pl.ds(k0, wr * w), :], ckb_ref[...]], axis=0)
            bias = jnp.concatenate([pair_ref[dr0 + 2 * j] for j in range(wr // 2)] + [zero_bias], axis=1)
            scores.append(_nt(q_r, kcat) + bias)
        probs = [jnp.exp2(s - jnp.max(s, axis=-1, keepdims=True)).astype(BF16) for s in scores]
        for r, k0, p in zip(rws, k0s, probs):
            vcat = jnp.concatenate([vb_ref[pl.ds(k0, wr * w), :], cvb_ref[...]], axis=0)
            pv = _mm(p, vcat)
            o = pv[:, :HEAD_DIM] / pv[:, HEAD_DIM:HEAD_DIM + 1]
            o_ref[pl.ds(pl.multiple_of(r * w, w), w), :] = o.astype(o_ref.dtype)
        return 0

    lax.fori_loop(0, rows // unroll, body, 0)


def na_attention(qkv, cache_k, cache_v, q_g, k_g, rpb, *, batch, seq, past, n_heads, unroll=16):
    m = qkv.shape[0]
    hd = HEAD_DIM
    rows = seq // GRID_W
    wr = min(WIN_ROWS, rows)
    n_dr, n_dc = rpb.shape[1], rpb.shape[2]
    assert wr == WIN_ROWS and wr % 2 == 0 and rows % unroll == 0
    kern = functools.partial(_na_kernel, rows=rows, wr=wr, n_dr=n_dr, n_dc=n_dc, unroll=unroll)
    return pl.pallas_call(
        kern,
        out_shape=jax.ShapeDtypeStruct((m, n_heads * hd), BF16),
        grid=(n_heads, batch),
        in_specs=[
            pl.BlockSpec(memory_space=pltpu.SMEM),
            pl.BlockSpec((seq, hd), lambda h, b: (b, h)),
            pl.BlockSpec((seq, hd), lambda h, b: (b, n_heads + h)),
            pl.BlockSpec((seq, hd), lambda h, b: (b, 2 * n_heads + h)),
            pl.BlockSpec((past, hd), lambda h, b: (b, h)),
            pl.BlockSpec((past, hd), lambda h, b: (b, h)),
            pl.BlockSpec((1, hd), lambda h, b: (0, 0)),
            pl.BlockSpec((1, hd), lambda h, b: (0, 0)),
        ],
        out_specs=pl.BlockSpec((seq, hd), lambda h, b: (b, h)),
        scratch_shapes=[
            pltpu.VMEM((seq, hd), BF16), pltpu.VMEM((seq, hd), BF16), pltpu.VMEM((seq, 2 * hd), BF16),
            pltpu.VMEM((past, hd), BF16), pltpu.VMEM((past, 2 * hd), BF16),
            pltpu.VMEM((n_dr, GRID_W, 2 * GRID_W), F32),
            pltpu.VMEM((n_dr - 1, GRID_W, 2 * GRID_W), F32),
        ],
        compiler_params=_params(("parallel", "arbitrary")),
        name="na_attention",
    )(rpb.reshape(n_heads, n_dr * n_dc), qkv, qkv, qkv, cache_k, cache_v,
      q_g.reshape(1, hd), k_g.reshape(1, hd))


def _gates_kernel(xn_ref, w_ref, b_ref, out_ref, *, tm):
    L = MLSTM_CHUNK
    wh, wl = _split2(w_ref[...])
    xn = xn_ref[...]
    pre = _nt(wh, xn) + _nt(wl, xn) + b_ref[...]
    capped = GATE_SOFTCAP * jnp.tanh(pre / GATE_SOFTCAP)
    row = lax.broadcasted_iota(jnp.int32, (capped.shape[0], L), 0) % 8
    is_input = (row == 0) | (row == 2)
    ri = lax.broadcasted_iota(jnp.int32, (L, L), 0)
    ci = lax.broadcasted_iota(jnp.int32, (L, L), 1)
    upper = jnp.where(ri <= ci, 1.0, 0.0).astype(BF16)
    lower = jnp.where(ri >= ci, 1.0, 0.0).astype(BF16)
    for c in range(tm // L):
        cap = capped[:, c * L:(c + 1) * L]
        gates = jnp.where(is_input, cap, jax.nn.log_sigmoid(cap))
        pieces = _split3(gates)
        prefix = sum(_mm(p, upper) for p in pieces)
        suffix = sum(_mm(p, lower) for p in pieces)
        out_ref[c] = jnp.where(row == 4, prefix, jnp.where(row == 5, suffix, gates))


def mlstm_gates(xn, w_gate_t, b_gate_t, *, tm=512):
    m, d = xn.shape
    gh = w_gate_t.shape[0]
    return pl.pallas_call(
        functools.partial(_gates_kernel, tm=tm),
        out_shape=jax.ShapeDtypeStruct((m // MLSTM_CHUNK, gh, MLSTM_CHUNK), F32),
        grid=(m // tm,),
        in_specs=[
            pl.BlockSpec((tm, d), lambda i: (i, 0)),
            pl.BlockSpec((gh, d), lambda i: (0, 0)),
            pl.BlockSpec((gh, 1), lambda i: (0, 0)),
        ],
        out_specs=pl.BlockSpec((tm // MLSTM_CHUNK, gh, MLSTM_CHUNK), lambda i: (i, 0, 0)),
        compiler_params=_params(("parallel",)),
        name="mlstm_gates",
    )(xn, w_gate_t, b_gate_t)


def _mlstm_kernel(*refs, n_chunks, heads, zero_init, write_state, has_prev_states, slot):
    it = iter(refs)
    q_ref, k_ref, v_ref, o_ref, gates_ref, hg_ref = (next(it) for _ in range(6))
    if not zero_init:
        c0_ref, n0_ref, m0_ref = (next(it) for _ in range(3))
    if has_prev_states:
        next(it)
    y_ref = next(it)
    if write_state:
        cf_ref, nf_ref, mf_ref = (next(it) for _ in range(3))
    hdir_ref, cst_ref = next(it), next(it)

    L, dk, dv = MLSTM_CHUNK, MLSTM_DK, MLSTM_DV
    ri = lax.broadcasted_iota(jnp.int32, (L, L), 0)
    ci = lax.broadcasted_iota(jnp.int32, (L, L), 1)
    masks = (ri >= ci, ri <= ci)
    eye = ri == ci
    eye_b = jnp.where(eye, 1.0, 0.0).astype(BF16)
    qscale = dk ** -0.5

    m_init, n_init = [], []
    for hh in range(heads):
        for direction in range(2):
            idx = hh * 2 + direction
            if zero_init:
                cst_ref[idx] = jnp.zeros((dk, dv), F32)
                m_init.append(jnp.zeros((1, 1), F32))
                n_init.append(jnp.zeros((1, dk), F32))
            else:
                cst_ref[idx] = c0_ref[direction, hh]
                m_init.append(m0_ref[direction, hh])
                n_init.append(n0_ref[direction, hh])

    def body(step, carry, finish_now):
        ms, ns = carry
        chains = [(hh, direction) for hh in range(heads) for direction in range(2)]
        r0s = [pl.multiple_of((step if d == 0 else n_chunks - 1 - step) * L, L) for _, d in chains]
        cs = [step if d == 0 else n_chunks - 1 - step for _, d in chains]

        st1 = []
        for idx, (hh, d) in enumerate(chains):
            gt = gates_ref[cs[idx], hh * 8:(hh + 1) * 8, :]
            li, lf, brow = gt[2 * d:2 * d + 1], gt[2 * d + 1:2 * d + 2], gt[4 + d:5 + d]
            b3 = _nt(eye_b, jnp.concatenate(_split3(jnp.broadcast_to(brow, (L, L))), axis=0))
            bmat = b3[:, :L] + b3[:, L:2 * L] + b3[:, 2 * L:]
            log_d = jnp.where(masks[d], bmat - brow + li, NEG_INF)
            inter = bmat + ms[idx]
            m_j = jnp.maximum(inter, jnp.max(log_d, axis=-1, keepdims=True))
            tot = jnp.sum(lf, axis=-1, keepdims=True)
            log_w = tot - brow + li
            m_new = jnp.maximum(tot + ms[idx], jnp.max(log_w, axis=-1, keepdims=True))
            st1.append((log_d, inter, m_j, tot, log_w, m_new))

        st2 = []
        for idx, (hh, d) in enumerate(chains):
            log_d, inter, m_j, tot, log_w, m_new = st1[idx]
            qf = q_ref[pl.ds(r0s[idx], L), hh * dk:(hh + 1) * dk] * qscale
            q = qf.astype(BF16)
            k = k_ref[pl.ds(r0s[idx], L), hh * dk:(hh + 1) * dk]
            k_hi, k_lo = _split2(k)
            qk = _nt(q, k_hi)
            kt2 = _nt(eye_b, jnp.concatenate([k_hi, k_lo], axis=0))
            kt = kt2[:, :L] + kt2[:, L:]
            qc = _mm(q, cst_ref[idx].astype(BF16))
            qn = jnp.sum(qf * ns[idx], axis=-1, keepdims=True)
            d_mat = jnp.exp(log_d - m_j)
            w_inter = jnp.exp(inter - m_j)
            wt = jnp.exp(log_w - m_new)
            decay = jnp.exp(tot + ms[idx] - m_new)
            ktw = (kt * wt).astype(BF16)
            wn = _mm(jnp.broadcast_to(wt, (8, L)).astype(BF16), k_hi)[:1]
            st2.append((qk * d_mat, qc, qn, w_inter, decay, ktw, decay * ns[idx] + wn))

        for idx, (hh, d) in enumerate(chains):
            s, qc, qn, w_inter, decay, ktw, _ = st2[idx]
            floor = jnp.exp(-st1[idx][2])
            v = v_ref[pl.ds(r0s[idx], L), hh * dv:(hh + 1) * dv].astype(BF16)
            both = _mm(jnp.concatenate([s.astype(BF16), ktw], axis=0), v)
            num = jnp.concatenate([w_inter] * 2, axis=1) * qc + both[:L]
            den = jnp.maximum(jnp.abs(w_inter * qn + jnp.sum(s, axis=-1, keepdims=True)), floor)
            h = num / jnp.concatenate([den, den], axis=1)
            cst_ref[idx] = decay * cst_ref[idx] + both[L:]
            if not finish_now:
                hdir_ref[d, pl.ds(r0s[idx], L), hh * dv:(hh + 1) * dv] = h
            else:
                cols = slice(hh * dv, (hh + 1) * dv)
                hs = h + hdir_ref[1 - d, pl.ds(r0s[idx], L), cols]
                hn = hs * lax.rsqrt(jnp.mean(hs * hs, axis=-1, keepdims=True) + EPS) * hg_ref[:, cols]
                y = hn * jax.nn.sigmoid(o_ref[pl.ds(r0s[idx], L), cols])
                y_ref[pl.ds(r0s[idx], L), cols] = y.astype(y_ref.dtype)
        return tuple(s1[5] for s1 in st1), tuple(s2[6] for s2 in st2)

    half = n_chunks // 2
    carry = lax.fori_loop(0, half, functools.partial(body, finish_now=False), (tuple(m_init), tuple(n_init)))
    m_fin, n_fin = lax.fori_loop(half, n_chunks, functools.partial(body, finish_now=True), carry)

    if write_state:
        own = cf_ref if has_prev_states else cf_ref.at[slot]
        if not has_prev_states:
            for other in range(cf_ref.shape[0]):
                if other != slot:
                    cf_ref[other] = jnp.zeros(cf_ref.shape[1:], F32)
        for hh in range(heads):
            for direction in range(2):
                idx = hh * 2 + direction
                own[direction, hh] = cst_ref[idx]
                nf_ref[direction, hh] = n_fin[idx]
                mf_ref[direction, hh] = m_fin[idx]


def mlstm_scan(proj, gates, head_g, state=None, *, batch, seq, n_heads, write_state, heads=2,
               state_slot=(0, 1), prev_states=None):
    m = proj.shape[0]
    dk, dv, L = MLSTM_DK, MLSTM_DV, MLSTM_CHUNK
    H = n_heads
    nb = H // heads
    n_chunks = seq // L
    zero_init = state is None
    slot, n_slots = state_slot
    kern = functools.partial(_mlstm_kernel, n_chunks=n_chunks, heads=heads, zero_init=zero_init,
                             write_state=write_state, has_prev_states=prev_states is not None, slot=slot)
    in_specs = [
        pl.BlockSpec((seq, heads * dk), lambda b, h: (b, h)),
        pl.BlockSpec((seq, heads * dk), lambda b, h: (b, nb + h)),
        pl.BlockSpec((seq, heads * dv), lambda b, h: (b, nb + h)),
        pl.BlockSpec((seq, heads * dv), lambda b, h: (b, 2 * nb + h)),
        pl.BlockSpec((n_chunks, heads * 8, L), lambda b, h: (b, h, 0)),
        pl.BlockSpec((None, 1, heads * dv), lambda b, h: (h, 0, 0)),
    ]
    args = [proj, proj, proj, proj, gates, head_g.reshape(nb, 1, heads * dv)]
    st_specs = [pl.BlockSpec((None, 2, heads, dk, dv), lambda b, h: (b, 0, h, 0, 0)),
                pl.BlockSpec((None, 2, heads, 1, dk), lambda b, h: (b, 0, h, 0, 0)),
                pl.BlockSpec((None, 2, heads, 1, 1), lambda b, h: (b, 0, h, 0, 0))]
    if not zero_init:
        c0_all, c0_slot, n0, m0 = state
        in_specs += [pl.BlockSpec((None, None, 2, heads, dk, dv), lambda b, h: (b, c0_slot, 0, h, 0, 0))] + st_specs[1:]
        args += [c0_all.astype(F32), n0.astype(F32).reshape(batch, 2, H, 1, dk),
                 m0.astype(F32).reshape(batch, 2, H, 1, 1)]
    aliases = {}
    if prev_states is not None:
        aliases = {len(args): 1}
        in_specs.append(pl.BlockSpec(memory_space=pl.ANY))
        args.append(prev_states)
    out_shape = [jax.ShapeDtypeStruct((m, H * dv), BF16)]
    out_specs = [pl.BlockSpec((seq, heads * dv), lambda b, h: (b, h))]
    if write_state:
        out_shape += [jax.ShapeDtypeStruct((batch, n_slots, 2, H, dk, dv), F32),
                      jax.ShapeDtypeStruct((batch, 2, H, 1, dk), F32),
                      jax.ShapeDtypeStruct((batch, 2, H, 1, 1), F32)]
        if prev_states is not None:
            c_spec = pl.BlockSpec((None, None, 2, heads, dk, dv), lambda b, h: (b, slot, 0, h, 0, 0))
        else:
            c_spec = pl.BlockSpec((None, n_slots, 2, heads, dk, dv), lambda b, h: (b, 0, 0, h, 0, 0))
        out_specs += [c_spec] + st_specs[1:]
    outs = pl.pallas_call(
        kern,
        out_shape=tuple(out_shape),
        grid=(batch, nb),
        in_specs=in_specs,
        out_specs=tuple(out_specs),
        scratch_shapes=[pltpu.VMEM((2, seq, heads * dv), F32), pltpu.VMEM((2 * heads, dk, dv), F32)],
        input_output_aliases=aliases,
        compiler_params=_params(("parallel", "parallel")),
        name="mlstm_scan",
    )(*args)
    if write_state:
        y, cf, nf, mf = outs
        return y, (cf, nf.reshape(batch, 2, H, dk), mf.reshape(batch, 2, H))
    return outs[0], None


def _gate_weights_head_major(w_gate, b_gate, n_heads):
    d = w_gate.shape[0]
    wt = jnp.transpose(w_gate.reshape(d, 4, n_heads), (2, 1, 0)).astype(F32)
    wt = jnp.concatenate([wt, wt[:, 1:2], wt[:, 3:4], jnp.zeros_like(wt[:, :2])], axis=1).reshape(8 * n_heads, d)
    bt = jnp.transpose(b_gate.reshape(4, n_heads), (1, 0)).astype(F32)
    bt = jnp.concatenate([bt, bt[:, 1:2], bt[:, 3:4], jnp.zeros_like(bt[:, :2])], axis=1).reshape(8 * n_heads, 1)
    return wt, bt


def _rope_tables(seq):
    nf = HEAD_DIM // 4
    t = jnp.arange(seq)
    inv = ROPE_THETA ** (-jnp.arange(nf, dtype=F32) / nf)
    pos = jnp.stack([t // GRID_W, t % GRID_W], axis=-1).astype(F32)
    ang = pos[:, :, None] * inv
    cos = jnp.cos(ang)
    sin = jnp.sin(ang)
    cos_full = jnp.stack([cos, cos], axis=2).reshape(seq, HEAD_DIM)
    sin_full = jnp.stack([-sin, sin], axis=2).reshape(seq, HEAD_DIM)
    return cos_full, sin_full


def kernel(x_prompt, x_sample, state_mlstm_C, state_mlstm_n, state_mlstm_m, cache_gqa_k, cache_gqa_v,
           cache_na_k, cache_na_v, c, c_ctx, norm1_g, norm2_g, w_ada, b_ada, w_mlp1, w_mlp2,
           mlstm_w_in, mlstm_w_gate, mlstm_b_gate, mlstm_head_g, mlstm_w_out,
           gqa_w_qkv, gqa_q_g, gqa_k_g, gqa_w_o, na_w_qkv, na_q_g, na_k_g, na_rpb, na_w_o):
    bp, tp, d = x_prompt.shape
    bs, ts, _ = x_sample.shape
    depth = w_ada.shape[0]
    past = cache_gqa_k.shape[2]
    mh = mlstm_w_gate.shape[-1] // 4
    gqa_kv = cache_gqa_k.shape[3]
    gqa_q = gqa_w_o.shape[1] // HEAD_DIM
    na_h = cache_na_k.shape[3]
    assert bs + 1 <= 8

    cond8 = jnp.zeros((8, d), F32).at[:bs].set(c).at[bs].set(c_ctx)
    mod = adaln_all(cond8, w_ada, b_ada).reshape(depth * 48, 1, d)
    g1 = norm1_g.reshape(depth, 1, d)
    g2 = norm2_g.reshape(depth, 1, d)
    cos, sin = _rope_tables(ts)

    groups = [
        dict(x=x_prompt.reshape(bp * tp, d), batch=bp, seq=tp, rpb_=bp * tp, row0=bs, ctx=True),
        dict(x=x_sample.reshape(bs * ts, d), batch=bs, seq=ts, rpb_=ts, row0=0, ctx=False),
    ]
    for grp in groups:
        grp["xn"] = modulate(grp["x"], g1, mod, layer=0, which_shift=0, which_scale=1,
                             rows_per_batch=grp["rpb_"], row0=grp["row0"])
    w_in_all = [mlstm_w_in.astype(BF16), gqa_w_qkv.astype(BF16), na_w_qkv.astype(BF16)]
    w_o_all = [mlstm_w_out.astype(BF16), gqa_w_o.astype(BF16), na_w_o.astype(BF16)]
    n_mlstm = state_mlstm_C.shape[1]
    c_all = None
    n_l, m_l, gk_l, gv_l, nk_l, nv_l = [], [], [], [], [], []

    for i in range(depth):
        kind, j = i % N_MIXERS, i // N_MIXERS
        last = i == depth - 1
        nxt = i if last else i + 1
        w_in, w_o = w_in_all[kind], w_o_all[kind]
        if kind == 0:
            wg_t, bg_t = _gate_weights_head_major(mlstm_w_gate[j], mlstm_b_gate[j], mh)

        for grp in groups:
            x, xn, batch, seq, ctx = grp["x"], grp["xn"], grp["batch"], grp["seq"], grp["ctx"]
            mk = dict(rows_per_batch=grp["rpb_"], row0=grp["row0"])
            if ctx:
                proj, w1, w2 = matmul_and_cast(xn, w_in, j, w_mlp1, w_mlp2, i, fc=MLP_HIDDEN_CHUNK)
            else:
                proj = matmul(xn, w_in, j)
            if kind == 0:
                gates = mlstm_gates(xn, wg_t, bg_t)
                if ctx:
                    a, (c_all, nf, mf) = mlstm_scan(proj, gates, mlstm_head_g[j], None, batch=batch, seq=seq,
                                                    n_heads=mh, write_state=True, heads=4,
                                                    state_slot=(j, n_mlstm), prev_states=c_all)
                    n_l.append(nf)
                    m_l.append(mf)
                else:
                    st = (state_mlstm_C, j, state_mlstm_n[:, j], state_mlstm_m[:, j])
                    a, _ = mlstm_scan(proj, gates, mlstm_head_g[j], st, batch=batch, seq=seq,
                                      n_heads=mh, write_state=False)
            elif kind == 1:
                if ctx:
                    a, kn, v = ctx_attention(proj, gqa_q_g[j], gqa_k_g[j], batch=batch, seq=seq, n_q=gqa_q, n_kv=gqa_kv)
                    gk_l.append(kn.reshape(batch, seq, gqa_kv, HEAD_DIM))
                    gv_l.append(v.reshape(batch, seq, gqa_kv, HEAD_DIM))
                else:
                    ck = cache_gqa_k[:, j].astype(F32).reshape(batch * past, gqa_kv * HEAD_DIM)
                    cv = cache_gqa_v[:, j].astype(F32).reshape(batch * past, gqa_kv * HEAD_DIM)
                    a = gqa_attention(proj, ck, cv, gqa_q_g[j], gqa_k_g[j], cos, sin, batch=batch, seq=seq,
                                      past=past, n_q=gqa_q, n_kv=gqa_kv)
            else:
                if ctx:
                    a, kn, v = ctx_attention(proj, na_q_g[j], na_k_g[j], batch=batch, seq=seq, n_q=na_h, n_kv=na_h,
                                             heads=8)
                    nk_l.append(kn.reshape(batch, seq, na_h, HEAD_DIM))
                    nv_l.append(v.reshape(batch, seq, na_h, HEAD_DIM))
                else:
                    ck = cache_na_k[:, j].astype(F32).reshape(batch * past, na_h * HEAD_DIM)
                    cv = cache_na_v[:, j].astype(F32).reshape(batch * past, na_h * HEAD_DIM)
                    a = na_attention(proj, ck, cv, na_q_g[j], na_k_g[j], na_rpb[j], batch=batch, seq=seq,
                                     past=past, n_heads=na_h)
            x, xn = mm_res_norm(a, w_o, j, x, mod, g2, layer=i, which_gate=2, next_layer=i, next_shift=3,
                                next_scale=4, **mk)
            x, xn = fused_mlp(xn, w1, w2, x, mod, g1, layer=i, next_layer=nxt, emit_next=not last, **mk)
            grp["x"], grp["xn"] = x, xn

    y_prompt = groups[0]["x"].reshape(bp, tp, d)
    y_sample = groups[1]["x"].reshape(bs, ts, d)
    return (y_prompt, y_sample, c_all, jnp.stack(n_l, axis=1), jnp.stack(m_l, axis=1),
            jnp.stack(gk_l, axis=1), jnp.stack(gv_l, axis=1), jnp.stack(nk_l, axis=1), jnp.stack(nv_l, axis=1))
```

```python
import functools
import math

import jax
import jax.numpy as jnp
from jax import lax
from jax.experimental import pallas as pl
from jax.experimental.pallas import tpu as pltpu

EPS = 1e-6
HEAD_DIM = 128
MLSTM_DK = 128
MLSTM_DV = 256
MLSTM_CHUNK = 128
GATE_SOFTCAP = 15.0
GRID_W = 64
WIN_ROWS = 8
WIN_COLS = 16
ROPE_THETA = 10000.0
N_MIXERS = 3
LOG2E = math.log2(math.e)
MOD_ROWS = 8
MOD_VECS = 6

VMEM_LIMIT_BYTES = 56 * 1024 * 1024
NORM_ROW_CHUNK = 16
MLP_HIDDEN_CHUNK = 1024
CAST_PROJ_COL_TILE = 768

F32 = jnp.float32
BF16 = jnp.bfloat16
NEG_INF = float("-inf")


def _params(sem):
    return pltpu.CompilerParams(dimension_semantics=sem, vmem_limit_bytes=VMEM_LIMIT_BYTES)


def _nt(a, b):
    return lax.dot_general(a, b, (((1,), (1,)), ((), ())), preferred_element_type=F32)


def _mm(a, b):
    return jnp.dot(a, b, preferred_element_type=F32)


def _split2(x):
    hi = x.astype(BF16)
    lo = (x - hi.astype(F32)).astype(BF16)
    return hi, lo


def _split3(x):
    hi = x.astype(BF16)
    r = x - hi.astype(F32)
    mid = r.astype(BF16)
    lo = (r - mid.astype(F32)).astype(BF16)
    return hi, mid, lo


def _mod_row(layer, which, rows_per_batch, tm, row0):
    def f(i, *_):
        b = row0 + (i * tm) // rows_per_batch
        return ((layer * MOD_ROWS + b) * MOD_VECS + which, 0, 0)
    return f


def _adaln_kernel(cond_ref, w_ref, b_ref, out_ref):
    c = cond_ref[...]
    s = c * jax.nn.sigmoid(c)
    sh, sl = _split2(s)
    wh, wl = _split2(w_ref[...])
    lhs = jnp.concatenate([sh, sl], axis=0)
    r = _mm(lhs, wh)
    out_ref[...] = r[:8] + r[8:] + _mm(sh, wl) + b_ref[...]


def adaln_all(cond8, w_ada, b_ada, tn=1024):
    nl, d, n = w_ada.shape
    return pl.pallas_call(
        _adaln_kernel,
        out_shape=jax.ShapeDtypeStruct((nl, 8, n), F32),
        grid=(nl, n // tn),
        in_specs=[
            pl.BlockSpec((8, d), lambda l, j: (0, 0)),
            pl.BlockSpec((None, d, tn), lambda l, j: (l, 0, j)),
            pl.BlockSpec((None, 1, tn), lambda l, j: (l, 0, j)),
        ],
        out_specs=pl.BlockSpec((None, 8, tn), lambda l, j: (l, 0, j)),
        compiler_params=_params(("parallel", "parallel")),
        name="adaln",
    )(cond8, w_ada, b_ada.reshape(nl, 1, n))


def _modulate_rows(src_ref, dst_ref, g, mul, sh, r0, n_rows):
    for c in range(n_rows // NORM_ROW_CHUNK):
        start = r0 + c * NORM_ROW_CHUNK
        if not isinstance(start, int):
            start = pl.multiple_of(start, NORM_ROW_CHUNK)
        rows = pl.ds(start, NORM_ROW_CHUNK)
        xf = src_ref[rows, :]
        y = xf * lax.rsqrt(jnp.mean(xf * xf, axis=-1, keepdims=True) + EPS)
        dst_ref[rows, :] = ((y * g) * mul + sh).astype(dst_ref.dtype)


def _modulate_kernel(x_ref, g_ref, sh_ref, sc_ref, out_ref, *, tm):
    g = g_ref[...]
    mul = 1.0 + sc_ref[...]
    sh = sh_ref[...]
    unroll = 4 * NORM_ROW_CHUNK

    def body(i, _):
        _modulate_rows(x_ref, out_ref, g, mul, sh, pl.multiple_of(i * unroll, unroll), unroll)
        return 0

    lax.fori_loop(0, tm // unroll, body, 0)


def modulate(x, g, mod, *, layer, which_shift, which_scale, rows_per_batch, row0, tm=512):
    m, d = x.shape
    tm = min(tm, rows_per_batch)
    return pl.pallas_call(
        functools.partial(_modulate_kernel, tm=tm),
        out_shape=jax.ShapeDtypeStruct((m, d), BF16),
        grid=(m // tm,),
        in_specs=[
            pl.BlockSpec((tm, d), lambda i: (i, 0)),
            pl.BlockSpec((None, 1, d), lambda i: (layer, 0, 0)),
            pl.BlockSpec((None, 1, d), _mod_row(layer, which_shift, rows_per_batch, tm, row0)),
            pl.BlockSpec((None, 1, d), _mod_row(layer, which_scale, rows_per_batch, tm, row0)),
        ],
        out_specs=pl.BlockSpec((tm, d), lambda i: (i, 0)),
        compiler_params=_params(("parallel",)),
        name="modulate",
    )(x, g, mod, mod)


def _mm_kernel(a_ref, w_ref, out_ref):
    out_ref[...] = _mm(a_ref[...], w_ref[...]).astype(out_ref.dtype)


class _MlpCast:
    def __init__(self, w1, w2, layer, *, fc, n_steps, step_of):
        _, d, ff = w1.shape
        r1, r2 = d // n_steps, ff // n_steps
        assert d % n_steps == 0 and ff % n_steps == 0 and r1 % 16 == 0 and ff % fc == 0
        self.args = [w1, w2]
        self.in_specs = [pl.BlockSpec((None, r1, ff), lambda *g: (layer, step_of(*g), 0)),
                         pl.BlockSpec((None, r2, d), lambda *g: (layer, step_of(*g), 0))]
        self.out_shape = [jax.ShapeDtypeStruct((ff // fc, d, fc), BF16), jax.ShapeDtypeStruct((ff, d), BF16)]
        self.out_specs = [pl.BlockSpec((ff // fc, r1, fc), lambda *g: (0, step_of(*g), 0)),
                          pl.BlockSpec((r2, d), lambda *g: (step_of(*g), 0))]

    @staticmethod
    def wrap(kernel_fn, n_in, n_out):
        def wrapped(*refs):
            w1_ref, w2_ref = refs[n_in:n_in + 2]
            w1o_ref, w2o_ref = refs[n_in + 2 + n_out:n_in + 4 + n_out]
            kernel_fn(*refs[:n_in], *refs[n_in + 2:n_in + 2 + n_out], *refs[n_in + 4 + n_out:])
            fc = w1o_ref.shape[2]
            for t in range(w1o_ref.shape[0]):
                w1o_ref[t] = w1_ref[:, t * fc:(t + 1) * fc].astype(BF16)
            w2o_ref[...] = w2_ref[...].astype(BF16)
        return wrapped


def matmul(a, w, wl, *, out_dtype=F32, tm=1024, tn=1024, cast=None, name="proj"):
    m, k = a.shape
    n = w.shape[2]
    nt = n // tn
    kern, extra = _mm_kernel, None
    if cast is not None:
        extra = _MlpCast(*cast[:3], fc=cast[3], n_steps=(m // tm) * nt, step_of=lambda i, j: i * nt + j)
        kern = _MlpCast.wrap(kern, 2, 1)
    outs = pl.pallas_call(
        kern,
        out_shape=tuple([jax.ShapeDtypeStruct((m, n), out_dtype)] + (extra.out_shape if extra else [])),
        grid=(m // tm, nt),
        in_specs=[
            pl.BlockSpec((tm, k), lambda i, j: (i, 0)),
            pl.BlockSpec((None, k, tn), lambda i, j: (wl, 0, j)),
        ] + (extra.in_specs if extra else []),
        out_specs=tuple([pl.BlockSpec((tm, tn), lambda i, j: (i, j))] + (extra.out_specs if extra else [])),
        compiler_params=_params(("parallel", "arbitrary")),
        name=name,
    )(a, w, *(extra.args if extra else []))
    return outs if extra else outs[0]


def _mmres_kernel(a_ref, w_ref, res_ref, gate_ref, g_ref, sh_ref, sc_ref, x_ref, xn_ref, *, tm, sub):
    gate = gate_ref[...]
    g = g_ref[...]
    mul = 1.0 + sc_ref[...]
    sh = sh_ref[...]
    for s in range(tm // sub):
        rows = pl.ds(s * sub, sub)
        x_ref[rows, :] = res_ref[rows, :] + gate * _mm(a_ref[rows, :], w_ref[...])
        _modulate_rows(x_ref, xn_ref, g, mul, sh, s * sub, sub)


def mm_res_norm(a, w, wl, res, mod, g_next, *, layer, which_gate, next_layer, next_shift, next_scale,
                rows_per_batch, row0, tm=512, sub=256):
    m, k = a.shape
    d = w.shape[2]
    tm = min(tm, rows_per_batch)
    mk = (rows_per_batch, tm, row0)
    return pl.pallas_call(
        functools.partial(_mmres_kernel, tm=tm, sub=sub),
        out_shape=(jax.ShapeDtypeStruct((m, d), F32), jax.ShapeDtypeStruct((m, d), BF16)),
        grid=(m // tm,),
        in_specs=[
            pl.BlockSpec((tm, k), lambda i: (i, 0)),
            pl.BlockSpec((None, k, d), lambda i: (wl, 0, 0), pipeline_mode=pl.Buffered(1)),
            pl.BlockSpec((tm, d), lambda i: (i, 0)),
            pl.BlockSpec((None, 1, d), _mod_row(layer, which_gate, *mk)),
            pl.BlockSpec((None, 1, d), lambda i: (next_layer, 0, 0)),
            pl.BlockSpec((None, 1, d), _mod_row(next_layer, next_shift, *mk)),
            pl.BlockSpec((None, 1, d), _mod_row(next_layer, next_scale, *mk)),
        ],
        out_specs=(pl.BlockSpec((tm, d), lambda i: (i, 0)), pl.BlockSpec((tm, d), lambda i: (i, 0))),
        compiler_params=_params(("parallel",)),
        name="outproj_res_norm",
    )(a, w, res, mod, g_next, mod, mod)


def _mlp_kernel(xn_ref, w1_ref, w2_ref, res_ref, gate_ref, g_ref, sh_ref, sc_ref, x_ref, *rest, tm, sub, emit_next):
    xno_ref = rest[0] if emit_next else None
    acc_ref = rest[-1]
    f = pl.program_id(1)

    @pl.when(f == 0)
    def _():
        acc_ref[...] = jnp.zeros_like(acc_ref)

    def hidden_chunk(rows):
        h = _mm(xn_ref[rows, :], w1_ref[...])
        return _mm(jnp.square(jnp.maximum(h, 0.0)).astype(BF16), w2_ref[...])

    last = pl.num_programs(1) - 1

    @pl.when(f < last)
    def _():
        acc_ref[...] += hidden_chunk(slice(None))

    @pl.when(f == last)
    def _():
        gate = gate_ref[...]
        g = g_ref[...]
        mul = 1.0 + sc_ref[...]
        sh = sh_ref[...]
        for s in range(tm // sub):
            rows = pl.ds(s * sub, sub)
            x_ref[rows, :] = res_ref[rows, :] + gate * (acc_ref[rows, :] + hidden_chunk(rows))
            if emit_next:
                _modulate_rows(x_ref, xno_ref, g, mul, sh, s * sub, sub)


def fused_mlp(xn, w1, w2, res, mod, g_next, *, layer, next_layer, rows_per_batch, row0, emit_next,
              tm=512, sub=128):
    m, d = xn.shape
    fc = w1.shape[2]
    ff = w1.shape[0] * fc
    tm = min(tm, rows_per_batch)
    mk = (rows_per_batch, tm, row0)
    row_tile = pl.BlockSpec((tm, d), lambda i, f: (i, 0))
    out_shape = [jax.ShapeDtypeStruct((m, d), F32)] + ([jax.ShapeDtypeStruct((m, d), BF16)] if emit_next else [])
    outs = pl.pallas_call(
        functools.partial(_mlp_kernel, tm=tm, sub=min(sub, tm), emit_next=emit_next),
        out_shape=tuple(out_shape),
        grid=(m // tm, ff // fc),
        in_specs=[
            pl.BlockSpec((tm, d), lambda i, f: (i, 0)),
            pl.BlockSpec((None, d, fc), lambda i, f: (f, 0, 0)),
            pl.BlockSpec((fc, d), lambda i, f: (f, 0)),
            row_tile,
            pl.BlockSpec((None, 1, d), _mod_row(layer, 5, *mk)),
            pl.BlockSpec((None, 1, d), lambda i, f: (next_layer, 0, 0)),
            pl.BlockSpec((None, 1, d), _mod_row(next_layer, 0, *mk)),
            pl.BlockSpec((None, 1, d), _mod_row(next_layer, 1, *mk)),
        ],
        out_specs=tuple(row_tile for _ in out_shape),
        scratch_shapes=[pltpu.VMEM((tm, d), F32)],
        compiler_params=_params(("parallel", "arbitrary")),
        name="fused_mlp",
    )(xn, w1, w2, res, mod, g_next, mod, mod)
    return (outs[0], outs[1]) if emit_next else (outs[0], None)


def _head_norm(x, g):
    return x * lax.rsqrt(jnp.mean(x * x, axis=-1, keepdims=True) + EPS) * g


_Q_SCALE = HEAD_DIM ** -0.5 * LOG2E


def _with_ones_column(v):
    lane = lax.broadcasted_iota(jnp.int32, v.shape, 1)
    return jnp.concatenate([v.astype(BF16), jnp.where(lane == 0, 1.0, 0.0).astype(BF16)], axis=1)


def _softmax2_pv(s, vaug):
    p = jnp.exp2(s - jnp.max(s, axis=-1, keepdims=True)).astype(BF16)
    pv = _mm(p, vaug)
    return pv[:, :HEAD_DIM] / pv[:, HEAD_DIM:HEAD_DIM + 1]


def _rope(x, cos, sin_signed, lane_is_first):
    swapped = jnp.where(lane_is_first, pltpu.roll(x, 96, 1), pltpu.roll(x, 32, 1))
    return x * cos + swapped * sin_signed


def _ctx_attn_kernel(q_ref, k_ref, v_ref, qg_ref, kg_ref, o_ref, kn_ref, vo_ref, *, groups, heads):
    hd = HEAD_DIM
    qg = qg_ref[...] * _Q_SCALE
    for h in range(heads):
        hs = slice(h * hd, (h + 1) * hd)
        kn = _head_norm(k_ref[:, hs], kg_ref[...])
        kn_ref[:, hs] = kn
        v = v_ref[:, hs]
        vo_ref[:, hs] = v
        kb = kn.astype(BF16)
        vb = v.astype(BF16)
        for g in range(groups):
            qs = slice((h * groups + g) * hd, (h * groups + g + 1) * hd)
            qn = _head_norm(q_ref[:, qs], qg).astype(BF16)
            s = _nt(qn, kb)
            p = jnp.exp2(s - jnp.max(s, axis=-1, keepdims=True))
            o_ref[:, qs] = (_mm(p.astype(BF16), vb) / jnp.sum(p, axis=-1, keepdims=True)).astype(o_ref.dtype)


def ctx_attention(qkv, q_g, k_g, *, batch, seq, n_q, n_kv, heads=4):
    m = qkv.shape[0]
    groups = n_q // n_kv
    hd = HEAD_DIM
    kw = heads * hd
    qw = heads * groups * hd
    assert (n_q * hd) % kw == 0 and ((n_q + n_kv) * hd) % kw == 0
    k_blk0 = n_q * hd // kw
    v_blk0 = (n_q + n_kv) * hd // kw
    kern = functools.partial(_ctx_attn_kernel, groups=groups, heads=heads)
    return pl.pallas_call(
        kern,
        out_shape=(jax.ShapeDtypeStruct((m, n_q * hd), BF16),
                   jax.ShapeDtypeStruct((m, n_kv * hd), F32),
                   jax.ShapeDtypeStruct((m, n_kv * hd), F32)),
        grid=(batch, n_kv // heads),
        in_specs=[
            pl.BlockSpec((seq, qw), lambda b, h: (b, h)),
            pl.BlockSpec((seq, kw), lambda b, h: (b, k_blk0 + h)),
            pl.BlockSpec((seq, kw), lambda b, h: (b, v_blk0 + h)),
            pl.BlockSpec((1, hd), lambda b, h: (0, 0)),
            pl.BlockSpec((1, hd), lambda b, h: (0, 0)),
        ],
        out_specs=(pl.BlockSpec((seq, qw), lambda b, h: (b, h)),
                   pl.BlockSpec((seq, kw), lambda b, h: (b, h)),
                   pl.BlockSpec((seq, kw), lambda b, h: (b, h))),
        compiler_params=_params(("parallel", "parallel")),
        name="ctx_attention",
    )(qkv, qkv, qkv, q_g.reshape(1, hd), k_g.reshape(1, hd))


def _gqa_kernel(q_ref, k_ref, v_ref, ck_ref, cv_ref, qg_ref, kg_ref, cosq_ref, sinq_ref, cosk_ref, sink_ref,
                o_ref, kall_ref, vall_ref, *, groups, seq, past):
    lane = lax.broadcasted_iota(jnp.int32, (1, HEAD_DIM), 1)
    first = (lane % 64) < 32

    @pl.when(pl.program_id(2) == 0)
    def _():
        kn = _head_norm(k_ref[...], kg_ref[...])
        kall_ref[0:seq, :] = _rope(kn, cosk_ref[...], sink_ref[...], first).astype(BF16)
        kall_ref[seq:seq + past, :] = ck_ref[...].astype(BF16)
        vall_ref[0:seq, :] = _with_ones_column(v_ref[...])
        vall_ref[seq:seq + past, :] = _with_ones_column(cv_ref[...])

    cos = cosq_ref[...]
    sin = sinq_ref[...]
    qg = qg_ref[...] * _Q_SCALE
    for g in range(groups):
        qn = _head_norm(q_ref[:, g * HEAD_DIM:(g + 1) * HEAD_DIM], qg)
        qr = _rope(qn, cos, sin, first).astype(BF16)
        o_ref[:, g * HEAD_DIM:(g + 1) * HEAD_DIM] = _softmax2_pv(_nt(qr, kall_ref[...]), vall_ref[...]).astype(o_ref.dtype)


def gqa_attention(qkv, cache_k, cache_v, q_g, k_g, cos, sin, *, batch, seq, past, n_q, n_kv, tq=256, cast=None):
    m = qkv.shape[0]
    groups = n_q // n_kv
    hd = HEAD_DIM
    nqb = seq // tq
    kern = functools.partial(_gqa_kernel, groups=groups, seq=seq, past=past)
    args = [qkv, qkv, qkv, cache_k, cache_v, q_g.reshape(1, hd), k_g.reshape(1, hd), cos, sin, cos, sin]
    extra = None
    if cast is not None:
        extra = _MlpCast(*cast[:3], fc=cast[3], n_steps=batch * n_kv * nqb,
                         step_of=lambda b, h, i: (b * n_kv + h) * nqb + i)
        kern = _MlpCast.wrap(kern, len(args), 1)
    outs = pl.pallas_call(
        kern,
        out_shape=tuple([jax.ShapeDtypeStruct((m, n_q * hd), BF16)] + (extra.out_shape if extra else [])),
        grid=(batch, n_kv, nqb),
        in_specs=[
            pl.BlockSpec((tq, groups * hd), lambda b, h, i: (b * nqb + i, h)),
            pl.BlockSpec((seq, hd), lambda b, h, i: (b, n_q + h)),
            pl.BlockSpec((seq, hd), lambda b, h, i: (b, n_q + n_kv + h)),
            pl.BlockSpec((past, hd), lambda b, h, i: (b, h)),
            pl.BlockSpec((past, hd), lambda b, h, i: (b, h)),
            pl.BlockSpec((1, hd), lambda b, h, i: (0, 0)),
            pl.BlockSpec((1, hd), lambda b, h, i: (0, 0)),
            pl.BlockSpec((tq, hd), lambda b, h, i: (i, 0)),
            pl.BlockSpec((tq, hd), lambda b, h, i: (i, 0)),
            pl.BlockSpec((seq, hd), lambda b, h, i: (0, 0)),
            pl.BlockSpec((seq, hd), lambda b, h, i: (0, 0)),
        ] + (extra.in_specs if extra else []),
        out_specs=tuple([pl.BlockSpec((tq, groups * hd), lambda b, h, i: (b * nqb + i, h))]
                        + (extra.out_specs if extra else [])),
        scratch_shapes=[pltpu.VMEM((seq + past, hd), BF16), pltpu.VMEM((seq + past, 2 * hd), BF16)],
        compiler_params=_params(("parallel", "parallel", "arbitrary")),
        name="gqa_attention",
    )(*args, *(extra.args if extra else []))
    return outs if extra else outs[0]


def _na_kernel(rpb_ref, q_ref, k_ref, v_ref, ck_ref, cv_ref, qg_ref, kg_ref, o_ref,
               qn_ref, kn_ref, vb_ref, ckb_ref, cvb_ref, tile_ref, pair_ref, *, rows, wr, n_dr, n_dc, unroll):
    h = pl.program_id(0)
    w = GRID_W

    qn_ref[...] = _head_norm(q_ref[...], qg_ref[...] * _Q_SCALE).astype(BF16)
    kn_ref[...] = _head_norm(k_ref[...], kg_ref[...]).astype(BF16)
    vb_ref[...] = _with_ones_column(v_ref[...])
    ckb_ref[...] = ck_ref[...].astype(BF16)
    cvb_ref[...] = _with_ones_column(cv_ref[...])

    @pl.when(pl.program_id(1) == 0)
    def _():
        qc = lax.broadcasted_iota(jnp.int32, (w, 2 * w), 0)
        lane = lax.broadcasted_iota(jnp.int32, (w, 2 * w), 1)
        kc = lane % w
        cs = jnp.clip(qc - WIN_COLS // 2, 0, w - WIN_COLS)
        col_ok = (kc >= cs) & (kc < cs + WIN_COLS)
        dc = jnp.clip(kc - qc + WIN_COLS - 1, 0, n_dc - 1)
        tiles = [jnp.zeros((w, 2 * w), F32) for _ in range(n_dr)]
        for d in range(n_dc):
            sel = dc == d
            for dr in range(n_dr):
                tiles[dr] = jnp.where(sel, rpb_ref[h, dr * n_dc + d], tiles[dr])
        for dr in range(n_dr):
            tile_ref[dr] = jnp.where(col_ok, tiles[dr] * LOG2E, NEG_INF)
        for dr in range(n_dr - 1):
            pair_ref[dr] = jnp.where(lane < w, tile_ref[dr], tile_ref[dr + 1])

    zero_bias = jnp.zeros((w, ckb_ref.shape[0]), F32)

    def body(i, _):
        rws = [i * unroll + u for u in range(unroll)]
        rss = [jnp.clip(r - wr // 2, 0, rows - wr) for r in rws]
        k0s = [pl.multiple_of(rs * w, w) for rs in rss]
        scores = []
        for r, rs, k0 in zip(rws, rss, k0s):
            dr0 = rs - r + WIN_ROWS - 1
            q_r = qn_ref[pl.ds(pl.multiple_of(r * w, w), w), :]
            kcat = jnp.concatenate([kn_ref[pl.ds(k0, wr * w), :], ckb_ref[...]], axis=0)
            bias = jnp.concatenate([pair_ref[dr0 + 2 * j] for j in range(wr // 2)] + [zero_bias], axis=1)
            scores.append(_nt(q_r, kcat) + bias)
        probs = [jnp.exp2(s - jnp.max(s, axis=-1, keepdims=True)).astype(BF16) for s in scores]
        for r, k0, p in zip(rws, k0s, probs):
            vcat = jnp.concatenate([vb_ref[pl.ds(k0, wr * w), :], cvb_ref[...]], axis=0)
            pv = _mm(p, vcat)
            o = pv[:, :HEAD_DIM] / pv[:, HEAD_DIM:HEAD_DIM + 1]
            o_ref[pl.ds(pl.multiple_of(r * w, w), w), :] = o.astype(o_ref.dtype)
        return 0

    lax.fori_loop(0, rows // unroll, body, 0)


def na_attention(qkv, cache_k, cache_v, q_g, k_g, rpb, *, batch, seq, past, n_heads, unroll=32, cast=None):
    m = qkv.shape[0]
    hd = HEAD_DIM
    rows = seq // GRID_W
    wr = min(WIN_ROWS, rows)
    n_dr, n_dc = rpb.shape[1], rpb.shape[2]
    unroll = min(unroll, rows)
    assert wr == WIN_ROWS and wr % 2 == 0 and rows % unroll == 0
    kern = functools.partial(_na_kernel, rows=rows, wr=wr, n_dr=n_dr, n_dc=n_dc, unroll=unroll)
    args = [rpb.reshape(n_heads, n_dr * n_dc), qkv, qkv, qkv, cache_k, cache_v, q_g.reshape(1, hd), k_g.reshape(1, hd)]
    extra = None
    if cast is not None:
        extra = _MlpCast(*cast[:3], fc=cast[3], n_steps=n_heads * batch, step_of=lambda h, b: h * batch + b)
        kern = _MlpCast.wrap(kern, len(args), 1)
    outs = pl.pallas_call(
        kern,
        out_shape=tuple([jax.ShapeDtypeStruct((m, n_heads * hd), BF16)] + (extra.out_shape if extra else [])),
        grid=(n_heads, batch),
        in_specs=[
            pl.BlockSpec(memory_space=pltpu.SMEM),
            pl.BlockSpec((seq, hd), lambda h, b: (b, h)),
            pl.BlockSpec((seq, hd), lambda h, b: (b, n_heads + h)),
            pl.BlockSpec((seq, hd), lambda h, b: (b, 2 * n_heads + h)),
            pl.BlockSpec((past, hd), lambda h, b: (b, h)),
            pl.BlockSpec((past, hd), lambda h, b: (b, h)),
            pl.BlockSpec((1, hd), lambda h, b: (0, 0)),
            pl.BlockSpec((1, hd), lambda h, b: (0, 0)),
        ] + (extra.in_specs if extra else []),
        out_specs=tuple([pl.BlockSpec((seq, hd), lambda h, b: (b, h))] + (extra.out_specs if extra else [])),
        scratch_shapes=[
            pltpu.VMEM((seq, hd), BF16), pltpu.VMEM((seq, hd), BF16), pltpu.VMEM((seq, 2 * hd), BF16),
            pltpu.VMEM((past, hd), BF16), pltpu.VMEM((past, 2 * hd), BF16),
            pltpu.VMEM((n_dr, GRID_W, 2 * GRID_W), F32),
            pltpu.VMEM((n_dr - 1, GRID_W, 2 * GRID_W), F32),
        ],
        compiler_params=_params(("parallel", "arbitrary")),
        name="na_attention",
    )(*args, *(extra.args if extra else []))
    return outs if extra else outs[0]


def _gates_kernel(xn_ref, w_ref, b_ref, out_ref, *, tm):
    L = MLSTM_CHUNK
    wh, wl = _split2(w_ref[...])
    xn = xn_ref[...]
    pre = _nt(wh, xn) + _nt(wl, xn) + b_ref[...]
    capped = GATE_SOFTCAP * jnp.tanh(pre / GATE_SOFTCAP)
    row = lax.broadcasted_iota(jnp.int32, (capped.shape[0], L), 0) % 8
    is_input = (row == 0) | (row == 2)
    ri = lax.broadcasted_iota(jnp.int32, (L, L), 0)
    ci = lax.broadcasted_iota(jnp.int32, (L, L), 1)
    upper = jnp.where(ri <= ci, 1.0, 0.0).astype(BF16)
    lower = jnp.where(ri >= ci, 1.0, 0.0).astype(BF16)
    for c in range(tm // L):
        cap = capped[:, c * L:(c + 1) * L]
        gates = jnp.where(is_input, cap, jax.nn.log_sigmoid(cap))
        pieces = _split3(gates)
        prefix = sum(_mm(p, upper) for p in pieces)
        suffix = sum(_mm(p, lower) for p in pieces)
        out_ref[c] = jnp.where(row == 4, prefix, jnp.where(row == 5, suffix, gates))


def mlstm_gates(xn, w_gate_t, b_gate_t, *, tm=512):
    m, d = xn.shape
    gh = w_gate_t.shape[0]
    return pl.pallas_call(
        functools.partial(_gates_kernel, tm=tm),
        out_shape=jax.ShapeDtypeStruct((m // MLSTM_CHUNK, gh, MLSTM_CHUNK), F32),
        grid=(m // tm,),
        in_specs=[
            pl.BlockSpec((tm, d), lambda i: (i, 0)),
            pl.BlockSpec((gh, d), lambda i: (0, 0)),
            pl.BlockSpec((gh, 1), lambda i: (0, 0)),
        ],
        out_specs=pl.BlockSpec((tm // MLSTM_CHUNK, gh, MLSTM_CHUNK), lambda i: (i, 0, 0)),
        compiler_params=_params(("parallel",)),
        name="mlstm_gates",
    )(xn, w_gate_t, b_gate_t)


def _mlstm_kernel(*refs, n_chunks, heads, zero_init, write_state, has_prev_states, slot):
    it = iter(refs)
    q_ref, k_ref, v_ref, o_ref, gates_ref, hg_ref = (next(it) for _ in range(6))
    if not zero_init:
        c0_ref, n0_ref, m0_ref = (next(it) for _ in range(3))
    if has_prev_states:
        next(it)
    y_ref = next(it)
    if write_state:
        cf_ref, nf_ref, mf_ref = (next(it) for _ in range(3))
    hdir_ref, cst_ref = next(it), next(it)

    L, dk, dv = MLSTM_CHUNK, MLSTM_DK, MLSTM_DV
    ri = lax.broadcasted_iota(jnp.int32, (L, L), 0)
    ci = lax.broadcasted_iota(jnp.int32, (L, L), 1)
    masks = (ri >= ci, ri <= ci)
    eye = ri == ci
    eye_b = jnp.where(eye, 1.0, 0.0).astype(BF16)
    qscale = dk ** -0.5

    m_init, n_init = [], []
    for hh in range(heads):
        for direction in range(2):
            idx = hh * 2 + direction
            if zero_init:
                cst_ref[idx] = jnp.zeros((dk, dv), F32)
                m_init.append(jnp.zeros((1, 1), F32))
                n_init.append(jnp.zeros((1, dk), F32))
            else:
                cst_ref[idx] = c0_ref[direction, hh]
                m_init.append(m0_ref[direction, hh])
                n_init.append(n0_ref[direction, hh])

    def body(step, carry, finish_now):
        ms, ns = carry
        chains = [(hh, direction) for hh in range(heads) for direction in range(2)]
        r0s = [pl.multiple_of((step if d == 0 else n_chunks - 1 - step) * L, L) for _, d in chains]
        cs = [step if d == 0 else n_chunks - 1 - step for _, d in chains]

        st1 = []
        for idx, (hh, d) in enumerate(chains):
            gt = gates_ref[cs[idx], hh * 8:(hh + 1) * 8, :]
            li, lf, brow = gt[2 * d:2 * d + 1], gt[2 * d + 1:2 * d + 2], gt[4 + d:5 + d]
            b3 = _nt(eye_b, jnp.concatenate(_split3(jnp.broadcast_to(brow, (L, L))), axis=0))
            bmat = b3[:, :L] + b3[:, L:2 * L] + b3[:, 2 * L:]
            log_d = jnp.where(masks[d], bmat - brow + li, NEG_INF)
            inter = bmat + ms[idx]
            m_j = jnp.maximum(inter, jnp.max(log_d, axis=-1, keepdims=True))
            tot = jnp.sum(lf, axis=-1, keepdims=True)
            log_w = tot - brow + li
            m_new = jnp.maximum(tot + ms[idx], jnp.max(log_w, axis=-1, keepdims=True))
            st1.append((log_d, inter, m_j, tot, log_w, m_new))

        st2 = []
        for idx, (hh, d) in enumerate(chains):
            log_d, inter, m_j, tot, log_w, m_new = st1[idx]
            qf = q_ref[pl.ds(r0s[idx], L), hh * dk:(hh + 1) * dk] * qscale
            q = qf.astype(BF16)
            k = k_ref[pl.ds(r0s[idx], L), hh * dk:(hh + 1) * dk]
            k_hi, k_lo = _split2(k)
            qk = _nt(q, k_hi)
            kt2 = _nt(eye_b, jnp.concatenate([k_hi, k_lo], axis=0))
            kt = kt2[:, :L] + kt2[:, L:]
            qc = _mm(q, cst_ref[idx].astype(BF16))
            qn = jnp.sum(qf * ns[idx], axis=-1, keepdims=True)
            d_mat = jnp.exp(log_d - m_j)
            w_inter = jnp.exp(inter - m_j)
            wt = jnp.exp(log_w - m_new)
            decay = jnp.exp(tot + ms[idx] - m_new)
            ktw = (kt * wt).astype(BF16)
            wn = _mm(jnp.broadcast_to(wt, (8, L)).astype(BF16), k_hi)[:1]
            st2.append((qk * d_mat, qc, qn, w_inter, decay, ktw, decay * ns[idx] + wn))

        for idx, (hh, d) in enumerate(chains):
            s, qc, qn, w_inter, decay, ktw, _ = st2[idx]
            floor = jnp.exp(-st1[idx][2])
            v = v_ref[pl.ds(r0s[idx], L), hh * dv:(hh + 1) * dv].astype(BF16)
            both = _mm(jnp.concatenate([s.astype(BF16), ktw], axis=0), v)
            num = jnp.concatenate([w_inter] * 2, axis=1) * qc + both[:L]
            den = jnp.maximum(jnp.abs(w_inter * qn + jnp.sum(s, axis=-1, keepdims=True)), floor)
            h = num / jnp.concatenate([den, den], axis=1)
            cst_ref[idx] = decay * cst_ref[idx] + both[L:]
            if not finish_now:
                hdir_ref[d, pl.ds(r0s[idx], L), hh * dv:(hh + 1) * dv] = h
            else:
                cols = slice(hh * dv, (hh + 1) * dv)
                hs = h + hdir_ref[1 - d, pl.ds(r0s[idx], L), cols]
                hn = hs * lax.rsqrt(jnp.mean(hs * hs, axis=-1, keepdims=True) + EPS) * hg_ref[:, cols]
                y = hn * jax.nn.sigmoid(o_ref[pl.ds(r0s[idx], L), cols])
                y_ref[pl.ds(r0s[idx], L), cols] = y.astype(y_ref.dtype)
        return tuple(s1[5] for s1 in st1), tuple(s2[6] for s2 in st2)

    half = n_chunks // 2
    carry = lax.fori_loop(0, half, functools.partial(body, finish_now=False), (tuple(m_init), tuple(n_init)))
    m_fin, n_fin = lax.fori_loop(half, n_chunks, functools.partial(body, finish_now=True), carry)

    if write_state:
        own = cf_ref if has_prev_states else cf_ref.at[slot]
        if not has_prev_states:
            for other in range(cf_ref.shape[0]):
                if other != slot:
                    cf_ref[other] = jnp.zeros(cf_ref.shape[1:], F32)
        for hh in range(heads):
            for direction in range(2):
                idx = hh * 2 + direction
                own[direction, hh] = cst_ref[idx]
                nf_ref[direction, hh] = n_fin[idx]
                mf_ref[direction, hh] = m_fin[idx]


def mlstm_scan(proj, gates, head_g, state=None, *, batch, seq, n_heads, write_state, heads=2,
               state_slot=(0, 1), prev_states=None, cast=None):
    m = proj.shape[0]
    dk, dv, L = MLSTM_DK, MLSTM_DV, MLSTM_CHUNK
    H = n_heads
    nb = H // heads
    n_chunks = seq // L
    assert seq % L == 0 and n_chunks % 2 == 0 and dk == L
    zero_init = state is None
    slot, n_slots = state_slot
    kern = functools.partial(_mlstm_kernel, n_chunks=n_chunks, heads=heads, zero_init=zero_init,
                             write_state=write_state, has_prev_states=prev_states is not None, slot=slot)
    in_specs = [
        pl.BlockSpec((seq, heads * dk), lambda b, h: (b, h)),
        pl.BlockSpec((seq, heads * dk), lambda b, h: (b, nb + h)),
        pl.BlockSpec((seq, heads * dv), lambda b, h: (b, nb + h)),
        pl.BlockSpec((seq, heads * dv), lambda b, h: (b, 2 * nb + h)),
        pl.BlockSpec((n_chunks, heads * 8, L), lambda b, h: (b, h, 0)),
        pl.BlockSpec((None, 1, heads * dv), lambda b, h: (h, 0, 0)),
    ]
    args = [proj, proj, proj, proj, gates, head_g.reshape(nb, 1, heads * dv)]
    st_specs = [pl.BlockSpec((None, 2, heads, dk, dv), lambda b, h: (b, 0, h, 0, 0)),
                pl.BlockSpec((None, 2, heads, 1, dk), lambda b, h: (b, 0, h, 0, 0)),
                pl.BlockSpec((None, 2, heads, 1, 1), lambda b, h: (b, 0, h, 0, 0))]
    if not zero_init:
        c0_all, c0_slot, n0, m0 = state
        in_specs += [pl.BlockSpec((None, None, 2, heads, dk, dv), lambda b, h: (b, c0_slot, 0, h, 0, 0))] + st_specs[1:]
        args += [c0_all.astype(F32), n0.astype(F32).reshape(batch, 2, H, 1, dk),
                 m0.astype(F32).reshape(batch, 2, H, 1, 1)]
    aliases = {}
    if prev_states is not None:
        aliases = {len(args): 1}
        in_specs.append(pl.BlockSpec(memory_space=pl.ANY))
        args.append(prev_states)
    out_shape = [jax.ShapeDtypeStruct((m, H * dv), BF16)]
    out_specs = [pl.BlockSpec((seq, heads * dv), lambda b, h: (b, h))]
    if write_state:
        out_shape += [jax.ShapeDtypeStruct((batch, n_slots, 2, H, dk, dv), F32),
                      jax.ShapeDtypeStruct((batch, 2, H, 1, dk), F32),
                      jax.ShapeDtypeStruct((batch, 2, H, 1, 1), F32)]
        if prev_states is not None:
            c_spec = pl.BlockSpec((None, None, 2, heads, dk, dv), lambda b, h: (b, slot, 0, h, 0, 0))
        else:
            c_spec = pl.BlockSpec((None, n_slots, 2, heads, dk, dv), lambda b, h: (b, 0, 0, h, 0, 0))
        out_specs += [c_spec] + st_specs[1:]
    cast_out = None
    if cast is not None:
        extra = _MlpCast(*cast[:3], fc=cast[3], n_steps=batch * nb, step_of=lambda b, h: b * nb + h)
        kern = _MlpCast.wrap(kern, len(args), len(out_shape))
        in_specs, args = in_specs + extra.in_specs, args + extra.args
        out_shape, out_specs = out_shape + extra.out_shape, out_specs + extra.out_specs
    outs = pl.pallas_call(
        kern,
        out_shape=tuple(out_shape),
        grid=(batch, nb),
        in_specs=in_specs,
        out_specs=tuple(out_specs),
        scratch_shapes=[pltpu.VMEM((2, seq, heads * dv), F32), pltpu.VMEM((2 * heads, dk, dv), F32)],
        input_output_aliases=aliases,
        compiler_params=_params(("parallel", "parallel")),
        name="mlstm_scan",
    )(*args)
    if cast is not None:
        outs, cast_out = outs[:-2], tuple(outs[-2:])
    if write_state:
        y, cf, nf, mf = outs
        return y, (cf, nf.reshape(batch, 2, H, dk), mf.reshape(batch, 2, H)), cast_out
    return outs[0], None, cast_out


def _gate_weights_head_major(w_gate, b_gate, n_heads):
    d = w_gate.shape[0]
    wt = jnp.transpose(w_gate.reshape(d, 4, n_heads), (2, 1, 0)).astype(F32)
    wt = jnp.concatenate([wt, wt[:, 1:2], wt[:, 3:4], jnp.zeros_like(wt[:, :2])], axis=1).reshape(8 * n_heads, d)
    bt = jnp.transpose(b_gate.reshape(4, n_heads), (1, 0)).astype(F32)
    bt = jnp.concatenate([bt, bt[:, 1:2], bt[:, 3:4], jnp.zeros_like(bt[:, :2])], axis=1).reshape(8 * n_heads, 1)
    return wt, bt


def _rope_tables(seq):
    nf = HEAD_DIM // 4
    t = jnp.arange(seq)
    inv = ROPE_THETA ** (-jnp.arange(nf, dtype=F32) / nf)
    pos = jnp.stack([t // GRID_W, t % GRID_W], axis=-1).astype(F32)
    ang = pos[:, :, None] * inv
    cos = jnp.cos(ang)
    sin = jnp.sin(ang)
    cos_full = jnp.stack([cos, cos], axis=2).reshape(seq, HEAD_DIM)
    sin_full = jnp.stack([-sin, sin], axis=2).reshape(seq, HEAD_DIM)
    return cos_full, sin_full


def kernel(x_prompt, x_sample, state_mlstm_C, state_mlstm_n, state_mlstm_m, cache_gqa_k, cache_gqa_v,
           cache_na_k, cache_na_v, c, c_ctx, norm1_g, norm2_g, w_ada, b_ada, w_mlp1, w_mlp2,
           mlstm_w_in, mlstm_w_gate, mlstm_b_gate, mlstm_head_g, mlstm_w_out,
           gqa_w_qkv, gqa_q_g, gqa_k_g, gqa_w_o, na_w_qkv, na_q_g, na_k_g, na_rpb, na_w_o):
    bp, tp, d = x_prompt.shape
    bs, ts, _ = x_sample.shape
    depth = w_ada.shape[0]
    past = cache_gqa_k.shape[2]
    mh = mlstm_w_gate.shape[-1] // 4
    gqa_kv = cache_gqa_k.shape[3]
    gqa_q = gqa_w_o.shape[1] // HEAD_DIM
    na_h = cache_na_k.shape[3]
    assert bs + 1 <= MOD_ROWS

    cond8 = jnp.zeros((MOD_ROWS, d), F32).at[:bs].set(c).at[bs].set(c_ctx)
    mod = adaln_all(cond8, w_ada, b_ada).reshape(depth * MOD_ROWS * MOD_VECS, 1, d)
    g1 = norm1_g.reshape(depth, 1, d)
    g2 = norm2_g.reshape(depth, 1, d)
    cos, sin = _rope_tables(ts)

    groups = [
        dict(x=x_prompt.reshape(bp * tp, d), batch=bp, seq=tp, rpb_=bp * tp, row0=bs, ctx=True),
        dict(x=x_sample.reshape(bs * ts, d), batch=bs, seq=ts, rpb_=ts, row0=0, ctx=False),
    ]
    for grp in groups:
        grp["xn"] = modulate(grp["x"], g1, mod, layer=0, which_shift=0, which_scale=1,
                             rows_per_batch=grp["rpb_"], row0=grp["row0"])
    w_in_all = [mlstm_w_in.astype(BF16), gqa_w_qkv.astype(BF16), na_w_qkv.astype(BF16)]
    w_o_all = [mlstm_w_out.astype(BF16), gqa_w_o.astype(BF16), na_w_o.astype(BF16)]
    n_mlstm = state_mlstm_C.shape[1]
    c_all = None
    n_l, m_l, gk_l, gv_l, nk_l, nv_l = [], [], [], [], [], []
    mlp_w = {}

    for i in range(depth):
        kind, j = i % N_MIXERS, i // N_MIXERS
        last = i == depth - 1
        nxt = i if last else i + 1
        w_in, w_o = w_in_all[kind], w_o_all[kind]
        if kind == 0:
            wg_t, bg_t = _gate_weights_head_major(mlstm_w_gate[j], mlstm_b_gate[j], mh)

        for grp in groups:
            x, xn, batch, seq, ctx = grp["x"], grp["xn"], grp["batch"], grp["seq"], grp["ctx"]
            mk = dict(rows_per_batch=grp["rpb_"], row0=grp["row0"])
            cast_layer = 0 if (ctx and i == 0) else (i + 1 if (not ctx and not last) else None)
            cast = None if cast_layer is None else (w_mlp1, w_mlp2, cast_layer, MLP_HIDDEN_CHUNK)
            if cast is not None and kind == 0 and not ctx:
                proj, *mlp_w[cast_layer] = matmul(xn, w_in, j, tn=CAST_PROJ_COL_TILE, cast=cast)
                cast = None
            else:
                proj = matmul(xn, w_in, j)
            if kind == 0:
                gates = mlstm_gates(xn, wg_t, bg_t)
                if ctx:
                    a, (c_all, nf, mf), cast_out = mlstm_scan(
                        proj, gates, mlstm_head_g[j], None, batch=batch, seq=seq, n_heads=mh, write_state=True,
                        heads=4, state_slot=(j, n_mlstm), prev_states=c_all, cast=cast)
                    n_l.append(nf)
                    m_l.append(mf)
                else:
                    st = (state_mlstm_C, j, state_mlstm_n[:, j], state_mlstm_m[:, j])
                    a, _, cast_out = mlstm_scan(proj, gates, mlstm_head_g[j], st, batch=batch, seq=seq,
                                                n_heads=mh, write_state=False, cast=cast)
                if cast is not None:
                    mlp_w[cast_layer] = cast_out
            elif kind == 1:
                if ctx:
                    a, kn, v = ctx_attention(proj, gqa_q_g[j], gqa_k_g[j], batch=batch, seq=seq, n_q=gqa_q, n_kv=gqa_kv)
                    gk_l.append(kn.reshape(batch, seq, gqa_kv, HEAD_DIM))
                    gv_l.append(v.reshape(batch, seq, gqa_kv, HEAD_DIM))
                else:
                    ck = cache_gqa_k[:, j].astype(F32).reshape(batch * past, gqa_kv * HEAD_DIM)
                    cv = cache_gqa_v[:, j].astype(F32).reshape(batch * past, gqa_kv * HEAD_DIM)
                    a = gqa_attention(proj, ck, cv, gqa_q_g[j], gqa_k_g[j], cos, sin, batch=batch, seq=seq,
                                      past=past, n_q=gqa_q, n_kv=gqa_kv, cast=cast)
                    if cast is not None:
                        a, *mlp_w[cast_layer] = a
            else:
                if ctx:
                    a, kn, v = ctx_attention(proj, na_q_g[j], na_k_g[j], batch=batch, seq=seq, n_q=na_h, n_kv=na_h,
                                             heads=8)
                    nk_l.append(kn.reshape(batch, seq, na_h, HEAD_DIM))
                    nv_l.append(v.reshape(batch, seq, na_h, HEAD_DIM))
                else:
                    ck = cache_na_k[:, j].astype(F32).reshape(batch * past, na_h * HEAD_DIM)
                    cv = cache_na_v[:, j].astype(F32).reshape(batch * past, na_h * HEAD_DIM)
                    a = na_attention(proj, ck, cv, na_q_g[j], na_k_g[j], na_rpb[j], batch=batch, seq=seq,
                                     past=past, n_heads=na_h, cast=cast)
                    if cast is not None:
                        a, *mlp_w[cast_layer] = a
            x, xn = mm_res_norm(a, w_o, j, x, mod, g2, layer=i, which_gate=2, next_layer=i, next_shift=3,
                                next_scale=4, **mk)
            x, xn = fused_mlp(xn, *mlp_w[i], x, mod, g1, layer=i, next_layer=nxt, emit_next=not last, **mk)
            grp["x"], grp["xn"] = x, xn

    y_prompt = groups[0]["x"].reshape(bp, tp, d)
    y_sample = groups[1]["x"].reshape(bs, ts, d)
    return (y_prompt, y_sample, c_all, jnp.stack(n_l, axis=1), jnp.stack(m_l, axis=1),
            jnp.stack(gk_l, axis=1), jnp.stack(gv_l, axis=1), jnp.stack(nk_l, axis=1), jnp.stack(nv_l, axis=1))
```

```python
import functools
import math

import jax
import jax.numpy as jnp
from jax import lax
from jax.experimental import pallas as pl
from jax.experimental.pallas import tpu as pltpu

EPS = 1e-6
HEAD_DIM = 128
MLSTM_DK = 128
MLSTM_DV = 256
MLSTM_CHUNK = 128
GATE_SOFTCAP = 15.0
GRID_W = 64
WIN_ROWS = 8
WIN_COLS = 16
ROPE_THETA = 10000.0
N_MIXERS = 3
LOG2E = math.log2(math.e)
MOD_ROWS = 8
MOD_VECS = 6

VMEM_LIMIT_BYTES = 56 * 1024 * 1024
NORM_ROW_CHUNK = 16
MLP_HIDDEN_CHUNK = 1024
CAST_PROJ_COL_TILE = 768

F32 = jnp.float32
BF16 = jnp.bfloat16
NEG_INF = float("-inf")


def _params(sem):
    return pltpu.CompilerParams(dimension_semantics=sem, vmem_limit_bytes=VMEM_LIMIT_BYTES)


def _nt(a, b):
    return lax.dot_general(a, b, (((1,), (1,)), ((), ())), preferred_element_type=F32)


def _mm(a, b):
    return jnp.dot(a, b, preferred_element_type=F32)


def _split2(x):
    hi = x.astype(BF16)
    lo = (x - hi.astype(F32)).astype(BF16)
    return hi, lo


def _split3(x):
    hi = x.astype(BF16)
    r = x - hi.astype(F32)
    mid = r.astype(BF16)
    lo = (r - mid.astype(F32)).astype(BF16)
    return hi, mid, lo


def _mod_row(layer, which, rows_per_batch, tm, row0):
    def f(i, *_):
        b = row0 + (i * tm) // rows_per_batch
        return ((layer * MOD_ROWS + b) * MOD_VECS + which, 0, 0)
    return f


def _adaln_kernel(cond_ref, w_ref, b_ref, out_ref):
    c = cond_ref[...]
    s = c * jax.nn.sigmoid(c)
    sh, sl = _split2(s)
    wh, wl = _split2(w_ref[...])
    lhs = jnp.concatenate([sh, sl], axis=0)
    r = _mm(lhs, wh)
    out_ref[...] = r[:8] + r[8:] + _mm(sh, wl) + b_ref[...]


def adaln_all(cond8, w_ada, b_ada, tn=1024):
    nl, d, n = w_ada.shape
    return pl.pallas_call(
        _adaln_kernel,
        out_shape=jax.ShapeDtypeStruct((nl, 8, n), F32),
        grid=(nl, n // tn),
        in_specs=[
            pl.BlockSpec((8, d), lambda l, j: (0, 0)),
            pl.BlockSpec((None, d, tn), lambda l, j: (l, 0, j)),
            pl.BlockSpec((None, 1, tn), lambda l, j: (l, 0, j)),
        ],
        out_specs=pl.BlockSpec((None, 8, tn), lambda l, j: (l, 0, j)),
        compiler_params=_params(("parallel", "parallel")),
        name="adaln",
    )(cond8, w_ada, b_ada.reshape(nl, 1, n))


def _modulate_rows(src_ref, dst_ref, g, mul, sh, r0, n_rows):
    for c in range(n_rows // NORM_ROW_CHUNK):
        start = r0 + c * NORM_ROW_CHUNK
        if not isinstance(start, int):
            start = pl.multiple_of(start, NORM_ROW_CHUNK)
        rows = pl.ds(start, NORM_ROW_CHUNK)
        xf = src_ref[rows, :]
        y = xf * lax.rsqrt(jnp.mean(xf * xf, axis=-1, keepdims=True) + EPS)
        dst_ref[rows, :] = ((y * g) * mul + sh).astype(dst_ref.dtype)


def _modulate_kernel(x_ref, g_ref, sh_ref, sc_ref, out_ref, *, tm):
    g = g_ref[...]
    mul = 1.0 + sc_ref[...]
    sh = sh_ref[...]
    unroll = 4 * NORM_ROW_CHUNK

    def body(i, _):
        _modulate_rows(x_ref, out_ref, g, mul, sh, pl.multiple_of(i * unroll, unroll), unroll)
        return 0

    lax.fori_loop(0, tm // unroll, body, 0)


def modulate(x, g, mod, *, layer, which_shift, which_scale, rows_per_batch, row0, tm=1024):
    m, d = x.shape
    tm = min(tm, rows_per_batch)
    return pl.pallas_call(
        functools.partial(_modulate_kernel, tm=tm),
        out_shape=jax.ShapeDtypeStruct((m, d), BF16),
        grid=(m // tm,),
        in_specs=[
            pl.BlockSpec((tm, d), lambda i: (i, 0)),
            pl.BlockSpec((None, 1, d), lambda i: (layer, 0, 0)),
            pl.BlockSpec((None, 1, d), _mod_row(layer, which_shift, rows_per_batch, tm, row0)),
            pl.BlockSpec((None, 1, d), _mod_row(layer, which_scale, rows_per_batch, tm, row0)),
        ],
        out_specs=pl.BlockSpec((tm, d), lambda i: (i, 0)),
        compiler_params=_params(("parallel",)),
        name="modulate",
    )(x, g, mod, mod)


def _mm_kernel(a_ref, w_ref, out_ref):
    out_ref[...] = _mm(a_ref[...], w_ref[...]).astype(out_ref.dtype)


class _MlpCast:
    def __init__(self, w1, w2, layer, *, fc, n_steps, step_of):
        _, d, ff = w1.shape
        r1, r2 = d // n_steps, ff // n_steps
        assert d % n_steps == 0 and ff % n_steps == 0 and r1 % 16 == 0 and ff % fc == 0
        self.args = [w1, w2]
        self.in_specs = [pl.BlockSpec((None, r1, ff), lambda *g: (layer, step_of(*g), 0)),
                         pl.BlockSpec((None, r2, d), lambda *g: (layer, step_of(*g), 0))]
        self.out_shape = [jax.ShapeDtypeStruct((ff // fc, d, fc), BF16), jax.ShapeDtypeStruct((ff, d), BF16)]
        self.out_specs = [pl.BlockSpec((ff // fc, r1, fc), lambda *g: (0, step_of(*g), 0)),
                          pl.BlockSpec((r2, d), lambda *g: (step_of(*g), 0))]

    @staticmethod
    def wrap(kernel_fn, n_in, n_out):
        def wrapped(*refs):
            w1_ref, w2_ref = refs[n_in:n_in + 2]
            w1o_ref, w2o_ref = refs[n_in + 2 + n_out:n_in + 4 + n_out]
            kernel_fn(*refs[:n_in], *refs[n_in + 2:n_in + 2 + n_out], *refs[n_in + 4 + n_out:])
            fc = w1o_ref.shape[2]
            for t in range(w1o_ref.shape[0]):
                w1o_ref[t] = w1_ref[:, t * fc:(t + 1) * fc].astype(BF16)
            w2o_ref[...] = w2_ref[...].astype(BF16)
        return wrapped


def matmul(a, w, wl, *, out_dtype=F32, tm=1024, tn=1024, cast=None, name="proj"):
    m, k = a.shape
    n = w.shape[2]
    nt = n // tn
    kern, extra = _mm_kernel, None
    if cast is not None:
        extra = _MlpCast(*cast[:3], fc=cast[3], n_steps=(m // tm) * nt, step_of=lambda i, j: i * nt + j)
        kern = _MlpCast.wrap(kern, 2, 1)
    outs = pl.pallas_call(
        kern,
        out_shape=tuple([jax.ShapeDtypeStruct((m, n), out_dtype)] + (extra.out_shape if extra else [])),
        grid=(m // tm, nt),
        in_specs=[
            pl.BlockSpec((tm, k), lambda i, j: (i, 0)),
            pl.BlockSpec((None, k, tn), lambda i, j: (wl, 0, j)),
        ] + (extra.in_specs if extra else []),
        out_specs=tuple([pl.BlockSpec((tm, tn), lambda i, j: (i, j))] + (extra.out_specs if extra else [])),
        compiler_params=_params(("parallel", "arbitrary")),
        name=name,
    )(a, w, *(extra.args if extra else []))
    return outs if extra else outs[0]


def _mmres_kernel(a_ref, w_ref, res_ref, gate_ref, g_ref, sh_ref, sc_ref, x_ref, xn_ref, *, tm, sub):
    gate = gate_ref[...]
    g = g_ref[...]
    mul = 1.0 + sc_ref[...]
    sh = sh_ref[...]
    for s in range(tm // sub):
        rows = pl.ds(s * sub, sub)
        x_ref[rows, :] = res_ref[rows, :] + gate * _mm(a_ref[rows, :], w_ref[...])
        _modulate_rows(x_ref, xn_ref, g, mul, sh, s * sub, sub)


def mm_res_norm(a, w, wl, res, mod, g_next, *, layer, which_gate, next_layer, next_shift, next_scale,
                rows_per_batch, row0, tm=512, sub=256):
    m, k = a.shape
    d = w.shape[2]
    tm = min(tm, rows_per_batch)
    mk = (rows_per_batch, tm, row0)
    return pl.pallas_call(
        functools.partial(_mmres_kernel, tm=tm, sub=sub),
        out_shape=(jax.ShapeDtypeStruct((m, d), F32), jax.ShapeDtypeStruct((m, d), BF16)),
        grid=(m // tm,),
        in_specs=[
            pl.BlockSpec((tm, k), lambda i: (i, 0)),
            pl.BlockSpec((None, k, d), lambda i: (wl, 0, 0), pipeline_mode=pl.Buffered(1)),
            pl.BlockSpec((tm, d), lambda i: (i, 0)),
            pl.BlockSpec((None, 1, d), _mod_row(layer, which_gate, *mk)),
            pl.BlockSpec((None, 1, d), lambda i: (next_layer, 0, 0)),
            pl.BlockSpec((None, 1, d), _mod_row(next_layer, next_shift, *mk)),
            pl.BlockSpec((None, 1, d), _mod_row(next_layer, next_scale, *mk)),
        ],
        out_specs=(pl.BlockSpec((tm, d), lambda i: (i, 0)), pl.BlockSpec((tm, d), lambda i: (i, 0))),
        compiler_params=_params(("parallel",)),
        name="outproj_res_norm",
    )(a, w, res, mod, g_next, mod, mod)


def _mlp_kernel(xn_ref, w1_ref, w2_ref, res_ref, gate_ref, g_ref, sh_ref, sc_ref, x_ref, *rest, tm, sub, emit_next):
    xno_ref = rest[0] if emit_next else None
    acc_ref = rest[-1]
    f = pl.program_id(1)

    @pl.when(f == 0)
    def _():
        acc_ref[...] = jnp.zeros_like(acc_ref)

    def hidden_chunk(rows):
        h = _mm(xn_ref[rows, :], w1_ref[...])
        return _mm(jnp.square(jnp.maximum(h, 0.0)).astype(BF16), w2_ref[...])

    last = pl.num_programs(1) - 1

    @pl.when(f < last)
    def _():
        acc_ref[...] += hidden_chunk(slice(None))

    @pl.when(f == last)
    def _():
        gate = gate_ref[...]
        g = g_ref[...]
        mul = 1.0 + sc_ref[...]
        sh = sh_ref[...]
        for s in range(tm // sub):
            rows = pl.ds(s * sub, sub)
            x_ref[rows, :] = res_ref[rows, :] + gate * (acc_ref[rows, :] + hidden_chunk(rows))
            if emit_next:
                _modulate_rows(x_ref, xno_ref, g, mul, sh, s * sub, sub)


def fused_mlp(xn, w1, w2, res, mod, g_next, *, layer, next_layer, rows_per_batch, row0, emit_next,
              tm=512, sub=128):
    m, d = xn.shape
    fc = w1.shape[2]
    ff = w1.shape[0] * fc
    tm = min(tm, rows_per_batch)
    mk = (rows_per_batch, tm, row0)
    row_tile = pl.BlockSpec((tm, d), lambda i, f: (i, 0))
    out_shape = [jax.ShapeDtypeStruct((m, d), F32)] + ([jax.ShapeDtypeStruct((m, d), BF16)] if emit_next else [])
    outs = pl.pallas_call(
        functools.partial(_mlp_kernel, tm=tm, sub=min(sub, tm), emit_next=emit_next),
        out_shape=tuple(out_shape),
        grid=(m // tm, ff // fc),
        in_specs=[
            pl.BlockSpec((tm, d), lambda i, f: (i, 0)),
            pl.BlockSpec((None, d, fc), lambda i, f: (f, 0, 0)),
            pl.BlockSpec((fc, d), lambda i, f: (f, 0)),
            row_tile,
            pl.BlockSpec((None, 1, d), _mod_row(layer, 5, *mk)),
            pl.BlockSpec((None, 1, d), lambda i, f: (next_layer, 0, 0)),
            pl.BlockSpec((None, 1, d), _mod_row(next_layer, 0, *mk)),
            pl.BlockSpec((None, 1, d), _mod_row(next_layer, 1, *mk)),
        ],
        out_specs=tuple(row_tile for _ in out_shape),
        scratch_shapes=[pltpu.VMEM((tm, d), F32)],
        compiler_params=_params(("parallel", "arbitrary")),
        name="fused_mlp",
    )(xn, w1, w2, res, mod, g_next, mod, mod)
    return (outs[0], outs[1]) if emit_next else (outs[0], None)


def _head_norm(x, g):
    return x * lax.rsqrt(jnp.mean(x * x, axis=-1, keepdims=True) + EPS) * g


_Q_SCALE = HEAD_DIM ** -0.5 * LOG2E


def _with_ones_column(v):
    lane = lax.broadcasted_iota(jnp.int32, v.shape, 1)
    return jnp.concatenate([v.astype(BF16), jnp.where(lane == 0, 1.0, 0.0).astype(BF16)], axis=1)


def _softmax2_pv(s, vaug):
    p = jnp.exp2(s - jnp.max(s, axis=-1, keepdims=True)).astype(BF16)
    pv = _mm(p, vaug)
    return pv[:, :HEAD_DIM] / pv[:, HEAD_DIM:HEAD_DIM + 1]


def _rope(x, cos, sin_signed, lane_is_first):
    swapped = jnp.where(lane_is_first, pltpu.roll(x, 96, 1), pltpu.roll(x, 32, 1))
    return x * cos + swapped * sin_signed


def _ctx_attn_kernel(q_ref, k_ref, v_ref, qg_ref, kg_ref, o_ref, kn_ref, vo_ref, *, groups, heads):
    hd = HEAD_DIM
    qg = qg_ref[...] * _Q_SCALE
    for h in range(heads):
        hs = slice(h * hd, (h + 1) * hd)
        kn = _head_norm(k_ref[:, hs], kg_ref[...])
        kn_ref[:, hs] = kn
        v = v_ref[:, hs]
        vo_ref[:, hs] = v
        kb = kn.astype(BF16)
        vb = v.astype(BF16)
        for g in range(groups):
            qs = slice((h * groups + g) * hd, (h * groups + g + 1) * hd)
            qn = _head_norm(q_ref[:, qs], qg).astype(BF16)
            s = _nt(qn, kb)
            p = jnp.exp2(s - jnp.max(s, axis=-1, keepdims=True))
            o_ref[:, qs] = (_mm(p.astype(BF16), vb) / jnp.sum(p, axis=-1, keepdims=True)).astype(o_ref.dtype)


def ctx_attention(qkv, q_g, k_g, *, batch, seq, n_q, n_kv, heads=4):
    m = qkv.shape[0]
    groups = n_q // n_kv
    hd = HEAD_DIM
    kw = heads * hd
    qw = heads * groups * hd
    assert (n_q * hd) % kw == 0 and ((n_q + n_kv) * hd) % kw == 0
    k_blk0 = n_q * hd // kw
    v_blk0 = (n_q + n_kv) * hd // kw
    kern = functools.partial(_ctx_attn_kernel, groups=groups, heads=heads)
    return pl.pallas_call(
        kern,
        out_shape=(jax.ShapeDtypeStruct((m, n_q * hd), BF16),
                   jax.ShapeDtypeStruct((m, n_kv * hd), F32),
                   jax.ShapeDtypeStruct((m, n_kv * hd), F32)),
        grid=(batch, n_kv // heads),
        in_specs=[
            pl.BlockSpec((seq, qw), lambda b, h: (b, h)),
            pl.BlockSpec((seq, kw), lambda b, h: (b, k_blk0 + h)),
            pl.BlockSpec((seq, kw), lambda b, h: (b, v_blk0 + h)),
            pl.BlockSpec((1, hd), lambda b, h: (0, 0)),
            pl.BlockSpec((1, hd), lambda b, h: (0, 0)),
        ],
        out_specs=(pl.BlockSpec((seq, qw), lambda b, h: (b, h)),
                   pl.BlockSpec((seq, kw), lambda b, h: (b, h)),
                   pl.BlockSpec((seq, kw), lambda b, h: (b, h))),
        compiler_params=_params(("parallel", "parallel")),
        name="ctx_attention",
    )(qkv, qkv, qkv, q_g.reshape(1, hd), k_g.reshape(1, hd))


def _gqa_kernel(q_ref, k_ref, v_ref, ck_ref, cv_ref, qg_ref, kg_ref, cosq_ref, sinq_ref, cosk_ref, sink_ref,
                o_ref, kall_ref, vall_ref, *, groups, seq, past):
    lane = lax.broadcasted_iota(jnp.int32, (1, HEAD_DIM), 1)
    first = (lane % 64) < 32

    @pl.when(pl.program_id(2) == 0)
    def _():
        kn = _head_norm(k_ref[...], kg_ref[...])
        kall_ref[0:seq, :] = _rope(kn, cosk_ref[...], sink_ref[...], first).astype(BF16)
        kall_ref[seq:seq + past, :] = ck_ref[...].astype(BF16)
        vall_ref[0:seq, :] = _with_ones_column(v_ref[...])
        vall_ref[seq:seq + past, :] = _with_ones_column(cv_ref[...])

    cos = cosq_ref[...]
    sin = sinq_ref[...]
    qg = qg_ref[...] * _Q_SCALE
    for g in range(groups):
        qn = _head_norm(q_ref[:, g * HEAD_DIM:(g + 1) * HEAD_DIM], qg)
        qr = _rope(qn, cos, sin, first).astype(BF16)
        o_ref[:, g * HEAD_DIM:(g + 1) * HEAD_DIM] = _softmax2_pv(_nt(qr, kall_ref[...]), vall_ref[...]).astype(o_ref.dtype)


def gqa_attention(qkv, cache_k, cache_v, q_g, k_g, cos, sin, *, batch, seq, past, n_q, n_kv, tq=256, cast=None):
    m = qkv.shape[0]
    groups = n_q // n_kv
    hd = HEAD_DIM
    nqb = seq // tq
    kern = functools.partial(_gqa_kernel, groups=groups, seq=seq, past=past)
    args = [qkv, qkv, qkv, cache_k, cache_v, q_g.reshape(1, hd), k_g.reshape(1, hd), cos, sin, cos, sin]
    extra = None
    if cast is not None:
        extra = _MlpCast(*cast[:3], fc=cast[3], n_steps=batch * n_kv * nqb,
                         step_of=lambda b, h, i: (b * n_kv + h) * nqb + i)
        kern = _MlpCast.wrap(kern, len(args), 1)
    outs = pl.pallas_call(
        kern,
        out_shape=tuple([jax.ShapeDtypeStruct((m, n_q * hd), BF16)] + (extra.out_shape if extra else [])),
        grid=(batch, n_kv, nqb),
        in_specs=[
            pl.BlockSpec((tq, groups * hd), lambda b, h, i: (b * nqb + i, h)),
            pl.BlockSpec((seq, hd), lambda b, h, i: (b, n_q + h)),
            pl.BlockSpec((seq, hd), lambda b, h, i: (b, n_q + n_kv + h)),
            pl.BlockSpec((past, hd), lambda b, h, i: (b, h)),
            pl.BlockSpec((past, hd), lambda b, h, i: (b, h)),
            pl.BlockSpec((1, hd), lambda b, h, i: (0, 0)),
            pl.BlockSpec((1, hd), lambda b, h, i: (0, 0)),
            pl.BlockSpec((tq, hd), lambda b, h, i: (i, 0)),
            pl.BlockSpec((tq, hd), lambda b, h, i: (i, 0)),
            pl.BlockSpec((seq, hd), lambda b, h, i: (0, 0)),
            pl.BlockSpec((seq, hd), lambda b, h, i: (0, 0)),
        ] + (extra.in_specs if extra else []),
        out_specs=tuple([pl.BlockSpec((tq, groups * hd), lambda b, h, i: (b * nqb + i, h))]
                        + (extra.out_specs if extra else [])),
        scratch_shapes=[pltpu.VMEM((seq + past, hd), BF16), pltpu.VMEM((seq + past, 2 * hd), BF16)],
        compiler_params=_params(("parallel", "parallel", "arbitrary")),
        name="gqa_attention",
    )(*args, *(extra.args if extra else []))
    return outs if extra else outs[0]


def _na_kernel(rpb_ref, q_ref, k_ref, v_ref, ck_ref, cv_ref, qg_ref, kg_ref, o_ref,
               qn_ref, kn_ref, vb_ref, ckb_ref, cvb_ref, tile_ref, pair_ref, *, rows, wr, n_dr, n_dc, unroll):
    h = pl.program_id(0)
    w = GRID_W

    qn_ref[...] = _head_norm(q_ref[...], qg_ref[...] * _Q_SCALE).astype(BF16)
    kn_ref[...] = _head_norm(k_ref[...], kg_ref[...]).astype(BF16)
    vb_ref[...] = _with_ones_column(v_ref[...])
    ckb_ref[...] = ck_ref[...].astype(BF16)
    cvb_ref[...] = _with_ones_column(cv_ref[...])

    @pl.when(pl.program_id(1) == 0)
    def _():
        qc = lax.broadcasted_iota(jnp.int32, (w, 2 * w), 0)
        lane = lax.broadcasted_iota(jnp.int32, (w, 2 * w), 1)
        kc = lane % w
        cs = jnp.clip(qc - WIN_COLS // 2, 0, w - WIN_COLS)
        col_ok = (kc >= cs) & (kc < cs + WIN_COLS)
        dc = jnp.clip(kc - qc + WIN_COLS - 1, 0, n_dc - 1)
        tiles = [jnp.zeros((w, 2 * w), F32) for _ in range(n_dr)]
        for d in range(n_dc):
            sel = dc == d
            for dr in range(n_dr):
                tiles[dr] = jnp.where(sel, rpb_ref[h, dr * n_dc + d], tiles[dr])
        for dr in range(n_dr):
            tile_ref[dr] = jnp.where(col_ok, tiles[dr] * LOG2E, NEG_INF)
        for dr in range(n_dr - 1):
            pair_ref[dr] = jnp.where(lane < w, tile_ref[dr], tile_ref[dr + 1])

    zero_bias = jnp.zeros((w, ckb_ref.shape[0]), F32)

    def body(i, _):
        rws = [i * unroll + u for u in range(unroll)]
        rss = [jnp.clip(r - wr // 2, 0, rows - wr) for r in rws]
        k0s = [pl.multiple_of(rs * w, w) for rs in rss]
        scores = []
        for r, rs, k0 in zip(rws, rss, k0s):
            dr0 = rs - r + WIN_ROWS - 1
            q_r = qn_ref[pl.ds(pl.multiple_of(r * w, w), w), :]
            kcat = jnp.concatenate([kn_ref[pl.ds(k0, wr * w), :], ckb_ref[...]], axis=0)
            bias = jnp.concatenate([pair_ref[dr0 + 2 * j] for j in range(wr // 2)] + [zero_bias], axis=1)
            scores.append(_nt(q_r, kcat) + bias)
        probs = [jnp.exp2(s - jnp.max(s, axis=-1, keepdims=True)).astype(BF16) for s in scores]
        for r, k0, p in zip(rws, k0s, probs):
            vcat = jnp.concatenate([vb_ref[pl.ds(k0, wr * w), :], cvb_ref[...]], axis=0)
            pv = _mm(p, vcat)
            o = pv[:, :HEAD_DIM] / pv[:, HEAD_DIM:HEAD_DIM + 1]
            o_ref[pl.ds(pl.multiple_of(r * w, w), w), :] = o.astype(o_ref.dtype)
        return 0

    lax.fori_loop(0, rows // unroll, body, 0)


def na_attention(qkv, cache_k, cache_v, q_g, k_g, rpb, *, batch, seq, past, n_heads, unroll=32, cast=None):
    m = qkv.shape[0]
    hd = HEAD_DIM
    rows = seq // GRID_W
    wr = min(WIN_ROWS, rows)
    n_dr, n_dc = rpb.shape[1], rpb.shape[2]
    unroll = min(unroll, rows)
    assert wr == WIN_ROWS and wr % 2 == 0 and rows % unroll == 0
    kern = functools.partial(_na_kernel, rows=rows, wr=wr, n_dr=n_dr, n_dc=n_dc, unroll=unroll)
    args = [rpb.reshape(n_heads, n_dr * n_dc), qkv, qkv, qkv, cache_k, cache_v, q_g.reshape(1, hd), k_g.reshape(1, hd)]
    extra = None
    if cast is not None:
        extra = _MlpCast(*cast[:3], fc=cast[3], n_steps=n_heads * batch, step_of=lambda h, b: h * batch + b)
        kern = _MlpCast.wrap(kern, len(args), 1)
    outs = pl.pallas_call(
        kern,
        out_shape=tuple([jax.ShapeDtypeStruct((m, n_heads * hd), BF16)] + (extra.out_shape if extra else [])),
        grid=(n_heads, batch),
        in_specs=[
            pl.BlockSpec(memory_space=pltpu.SMEM),
            pl.BlockSpec((seq, hd), lambda h, b: (b, h)),
            pl.BlockSpec((seq, hd), lambda h, b: (b, n_heads + h)),
            pl.BlockSpec((seq, hd), lambda h, b: (b, 2 * n_heads + h)),
            pl.BlockSpec((past, hd), lambda h, b: (b, h)),
            pl.BlockSpec((past, hd), lambda h, b: (b, h)),
            pl.BlockSpec((1, hd), lambda h, b: (0, 0)),
            pl.BlockSpec((1, hd), lambda h, b: (0, 0)),
        ] + (extra.in_specs if extra else []),
        out_specs=tuple([pl.BlockSpec((seq, hd), lambda h, b: (b, h))] + (extra.out_specs if extra else [])),
        scratch_shapes=[
            pltpu.VMEM((seq, hd), BF16), pltpu.VMEM((seq, hd), BF16), pltpu.VMEM((seq, 2 * hd), BF16),
            pltpu.VMEM((past, hd), BF16), pltpu.VMEM((past, 2 * hd), BF16),
            pltpu.VMEM((n_dr, GRID_W, 2 * GRID_W), F32),
            pltpu.VMEM((n_dr - 1, GRID_W, 2 * GRID_W), F32),
        ],
        compiler_params=_params(("parallel", "arbitrary")),
        name="na_attention",
    )(*args, *(extra.args if extra else []))
    return outs if extra else outs[0]


def _gates_kernel(xn_ref, w_ref, b_ref, out_ref, *, tm):
    L = MLSTM_CHUNK
    wh, wl = _split2(w_ref[...])
    xn = xn_ref[...]
    pre = _nt(wh, xn) + _nt(wl, xn) + b_ref[...]
    capped = GATE_SOFTCAP * jnp.tanh(pre / GATE_SOFTCAP)
    row = lax.broadcasted_iota(jnp.int32, (capped.shape[0], L), 0) % 8
    is_input = (row == 0) | (row == 2)
    ri = lax.broadcasted_iota(jnp.int32, (L, L), 0)
    ci = lax.broadcasted_iota(jnp.int32, (L, L), 1)
    upper = jnp.where(ri <= ci, 1.0, 0.0).astype(BF16)
    lower = jnp.where(ri >= ci, 1.0, 0.0).astype(BF16)
    for c in range(tm // L):
        cap = capped[:, c * L:(c + 1) * L]
        gates = jnp.where(is_input, cap, jax.nn.log_sigmoid(cap))
        pieces = _split3(gates)
        prefix = sum(_mm(p, upper) for p in pieces)
        suffix = sum(_mm(p, lower) for p in pieces)
        out_ref[c] = jnp.where(row == 4, prefix, jnp.where(row == 5, suffix, gates))


def mlstm_gates(xn, w_gate_t, b_gate_t, *, tm=512):
    m, d = xn.shape
    gh = w_gate_t.shape[0]
    return pl.pallas_call(
        functools.partial(_gates_kernel, tm=tm),
        out_shape=jax.ShapeDtypeStruct((m // MLSTM_CHUNK, gh, MLSTM_CHUNK), F32),
        grid=(m // tm,),
        in_specs=[
            pl.BlockSpec((tm, d), lambda i: (i, 0)),
            pl.BlockSpec((gh, d), lambda i: (0, 0)),
            pl.BlockSpec((gh, 1), lambda i: (0, 0)),
        ],
        out_specs=pl.BlockSpec((tm // MLSTM_CHUNK, gh, MLSTM_CHUNK), lambda i: (i, 0, 0)),
        compiler_params=_params(("parallel",)),
        name="mlstm_gates",
    )(xn, w_gate_t, b_gate_t)


def _mlstm_kernel(*refs, n_chunks, heads, zero_init, write_state, has_prev_states, slot):
    it = iter(refs)
    q_ref, k_ref, v_ref, o_ref, gates_ref, hg_ref = (next(it) for _ in range(6))
    if not zero_init:
        c0_ref, n0_ref, m0_ref = (next(it) for _ in range(3))
    if has_prev_states:
        next(it)
    y_ref = next(it)
    if write_state:
        cf_ref, nf_ref, mf_ref = (next(it) for _ in range(3))
    hdir_ref, cst_ref = next(it), next(it)

    L, dk, dv = MLSTM_CHUNK, MLSTM_DK, MLSTM_DV
    ri = lax.broadcasted_iota(jnp.int32, (L, L), 0)
    ci = lax.broadcasted_iota(jnp.int32, (L, L), 1)
    masks = (ri >= ci, ri <= ci)
    eye = ri == ci
    eye_b = jnp.where(eye, 1.0, 0.0).astype(BF16)
    qscale = dk ** -0.5

    m_init, n_init = [], []
    for hh in range(heads):
        for direction in range(2):
            idx = hh * 2 + direction
            if zero_init:
                cst_ref[idx] = jnp.zeros((dk, dv), F32)
                m_init.append(jnp.zeros((1, 1), F32))
                n_init.append(jnp.zeros((1, dk), F32))
            else:
                cst_ref[idx] = c0_ref[direction, hh]
                m_init.append(m0_ref[direction, hh])
                n_init.append(n0_ref[direction, hh])

    def body(step, carry, finish_now):
        ms, ns = carry
        chains = [(hh, direction) for hh in range(heads) for direction in range(2)]
        r0s = [pl.multiple_of((step if d == 0 else n_chunks - 1 - step) * L, L) for _, d in chains]
        cs = [step if d == 0 else n_chunks - 1 - step for _, d in chains]

        st1 = []
        for idx, (hh, d) in enumerate(chains):
            gt = gates_ref[cs[idx], hh * 8:(hh + 1) * 8, :]
            li, lf, brow = gt[2 * d:2 * d + 1], gt[2 * d + 1:2 * d + 2], gt[4 + d:5 + d]
            b3 = _nt(eye_b, jnp.concatenate(_split3(jnp.broadcast_to(brow, (L, L))), axis=0))
            bmat = b3[:, :L] + b3[:, L:2 * L] + b3[:, 2 * L:]
            log_d = jnp.where(masks[d], bmat - brow + li, NEG_INF)
            inter = bmat + ms[idx]
            m_j = jnp.maximum(inter, jnp.max(log_d, axis=-1, keepdims=True))
            tot = jnp.sum(lf, axis=-1, keepdims=True)
            log_w = tot - brow + li
            m_new = jnp.maximum(tot + ms[idx], jnp.max(log_w, axis=-1, keepdims=True))
            st1.append((log_d, inter, m_j, tot, log_w, m_new))

        st2 = []
        for idx, (hh, d) in enumerate(chains):
            log_d, inter, m_j, tot, log_w, m_new = st1[idx]
            qf = q_ref[pl.ds(r0s[idx], L), hh * dk:(hh + 1) * dk] * qscale
            q = qf.astype(BF16)
            k = k_ref[pl.ds(r0s[idx], L), hh * dk:(hh + 1) * dk]
            k_hi, k_lo = _split2(k)
            qk = _nt(q, k_hi)
            kt2 = _nt(eye_b, jnp.concatenate([k_hi, k_lo], axis=0))
            kt = kt2[:, :L] + kt2[:, L:]
            qc = _mm(q, cst_ref[idx].astype(BF16))
            qn = jnp.sum(qf * ns[idx], axis=-1, keepdims=True)
            d_mat = jnp.exp(log_d - m_j)
            w_inter = jnp.exp(inter - m_j)
            wt = jnp.exp(log_w - m_new)
            decay = jnp.exp(tot + ms[idx] - m_new)
            ktw = (kt * wt).astype(BF16)
            wn = _mm(jnp.broadcast_to(wt, (8, L)).astype(BF16), k_hi)[:1]
            st2.append((qk * d_mat, qc, qn, w_inter, decay, ktw, decay * ns[idx] + wn))

        for idx, (hh, d) in enumerate(chains):
            s, qc, qn, w_inter, decay, ktw, _ = st2[idx]
            floor = jnp.exp(-st1[idx][2])
            v = v_ref[pl.ds(r0s[idx], L), hh * dv:(hh + 1) * dv].astype(BF16)
            both = _mm(jnp.concatenate([s.astype(BF16), ktw], axis=0), v)
            num = jnp.concatenate([w_inter] * 2, axis=1) * qc + both[:L]
            den = jnp.maximum(jnp.abs(w_inter * qn + jnp.sum(s, axis=-1, keepdims=True)), floor)
            h = num / jnp.concatenate([den, den], axis=1)
            cst_ref[idx] = decay * cst_ref[idx] + both[L:]
            if not finish_now:
                hdir_ref[d, pl.ds(r0s[idx], L), hh * dv:(hh + 1) * dv] = h
            else:
                cols = slice(hh * dv, (hh + 1) * dv)
                hs = h + hdir_ref[1 - d, pl.ds(r0s[idx], L), cols]
                hn = hs * lax.rsqrt(jnp.mean(hs * hs, axis=-1, keepdims=True) + EPS) * hg_ref[:, cols]
                y = hn * jax.nn.sigmoid(o_ref[pl.ds(r0s[idx], L), cols])
                y_ref[pl.ds(r0s[idx], L), cols] = y.astype(y_ref.dtype)
        return tuple(s1[5] for s1 in st1), tuple(s2[6] for s2 in st2)

    half = n_chunks // 2
    carry = lax.fori_loop(0, half, functools.partial(body, finish_now=False), (tuple(m_init), tuple(n_init)))
    m_fin, n_fin = lax.fori_loop(half, n_chunks, functools.partial(body, finish_now=True), carry)

    if write_state:
        own = cf_ref if has_prev_states else cf_ref.at[slot]
        if not has_prev_states:
            for other in range(cf_ref.shape[0]):
                if other != slot:
                    cf_ref[other] = jnp.zeros(cf_ref.shape[1:], F32)
        for hh in range(heads):
            for direction in range(2):
                idx = hh * 2 + direction
                own[direction, hh] = cst_ref[idx]
                nf_ref[direction, hh] = n_fin[idx]
                mf_ref[direction, hh] = m_fin[idx]


def mlstm_scan(proj, gates, head_g, state=None, *, batch, seq, n_heads, write_state, heads=2,
               state_slot=(0, 1), prev_states=None, cast=None):
    m = proj.shape[0]
    dk, dv, L = MLSTM_DK, MLSTM_DV, MLSTM_CHUNK
    H = n_heads
    nb = H // heads
    n_chunks = seq // L
    assert seq % L == 0 and n_chunks % 2 == 0 and dk == L
    zero_init = state is None
    slot, n_slots = state_slot
    kern = functools.partial(_mlstm_kernel, n_chunks=n_chunks, heads=heads, zero_init=zero_init,
                             write_state=write_state, has_prev_states=prev_states is not None, slot=slot)
    in_specs = [
        pl.BlockSpec((seq, heads * dk), lambda b, h: (b, h)),
        pl.BlockSpec((seq, heads * dk), lambda b, h: (b, nb + h)),
        pl.BlockSpec((seq, heads * dv), lambda b, h: (b, nb + h)),
        pl.BlockSpec((seq, heads * dv), lambda b, h: (b, 2 * nb + h)),
        pl.BlockSpec((n_chunks, heads * 8, L), lambda b, h: (b, h, 0)),
        pl.BlockSpec((None, 1, heads * dv), lambda b, h: (h, 0, 0)),
    ]
    args = [proj, proj, proj, proj, gates, head_g.reshape(nb, 1, heads * dv)]
    st_specs = [pl.BlockSpec((None, 2, heads, dk, dv), lambda b, h: (b, 0, h, 0, 0)),
                pl.BlockSpec((None, 2, heads, 1, dk), lambda b, h: (b, 0, h, 0, 0)),
                pl.BlockSpec((None, 2, heads, 1, 1), lambda b, h: (b, 0, h, 0, 0))]
    if not zero_init:
        c0_all, c0_slot, n0, m0 = state
        in_specs += [pl.BlockSpec((None, None, 2, heads, dk, dv), lambda b, h: (b, c0_slot, 0, h, 0, 0))] + st_specs[1:]
        args += [c0_all.astype(F32), n0.astype(F32).reshape(batch, 2, H, 1, dk),
                 m0.astype(F32).reshape(batch, 2, H, 1, 1)]
    aliases = {}
    if prev_states is not None:
        aliases = {len(args): 1}
        in_specs.append(pl.BlockSpec(memory_space=pl.ANY))
        args.append(prev_states)
    out_shape = [jax.ShapeDtypeStruct((m, H * dv), BF16)]
    out_specs = [pl.BlockSpec((seq, heads * dv), lambda b, h: (b, h))]
    if write_state:
        out_shape += [jax.ShapeDtypeStruct((batch, n_slots, 2, H, dk, dv), F32),
                      jax.ShapeDtypeStruct((batch, 2, H, 1, dk), F32),
                      jax.ShapeDtypeStruct((batch, 2, H, 1, 1), F32)]
        if prev_states is not None:
            c_spec = pl.BlockSpec((None, None, 2, heads, dk, dv), lambda b, h: (b, slot, 0, h, 0, 0))
        else:
            c_spec = pl.BlockSpec((None, n_slots, 2, heads, dk, dv), lambda b, h: (b, 0, 0, h, 0, 0))
        out_specs += [c_spec] + st_specs[1:]
    cast_out = None
    if cast is not None:
        extra = _MlpCast(*cast[:3], fc=cast[3], n_steps=batch * nb, step_of=lambda b, h: b * nb + h)
        kern = _MlpCast.wrap(kern, len(args), len(out_shape))
        in_specs, args = in_specs + extra.in_specs, args + extra.args
        out_shape, out_specs = out_shape + extra.out_shape, out_specs + extra.out_specs
    outs = pl.pallas_call(
        kern,
        out_shape=tuple(out_shape),
        grid=(batch, nb),
        in_specs=in_specs,
        out_specs=tuple(out_specs),
        scratch_shapes=[pltpu.VMEM((2, seq, heads * dv), F32), pltpu.VMEM((2 * heads, dk, dv), F32)],
        input_output_aliases=aliases,
        compiler_params=_params(("parallel", "parallel")),
        name="mlstm_scan",
    )(*args)
    if cast is not None:
        outs, cast_out = outs[:-2], tuple(outs[-2:])
    if write_state:
        y, cf, nf, mf = outs
        return y, (cf, nf.reshape(batch, 2, H, dk), mf.reshape(batch, 2, H)), cast_out
    return outs[0], None, cast_out


def _gate_weights_head_major(w_gate, b_gate, n_heads):
    d = w_gate.shape[0]
    wt = jnp.transpose(w_gate.reshape(d, 4, n_heads), (2, 1, 0)).astype(F32)
    wt = jnp.concatenate([wt, wt[:, 1:2], wt[:, 3:4], jnp.zeros_like(wt[:, :2])], axis=1).reshape(8 * n_heads, d)
    bt = jnp.transpose(b_gate.reshape(4, n_heads), (1, 0)).astype(F32)
    bt = jnp.concatenate([bt, bt[:, 1:2], bt[:, 3:4], jnp.zeros_like(bt[:, :2])], axis=1).reshape(8 * n_heads, 1)
    return wt, bt


def _rope_tables(seq):
    nf = HEAD_DIM // 4
    t = jnp.arange(seq)
    inv = ROPE_THETA ** (-jnp.arange(nf, dtype=F32) / nf)
    pos = jnp.stack([t // GRID_W, t % GRID_W], axis=-1).astype(F32)
    ang = pos[:, :, None] * inv
    cos = jnp.cos(ang)
    sin = jnp.sin(ang)
    cos_full = jnp.stack([cos, cos], axis=2).reshape(seq, HEAD_DIM)
    sin_full = jnp.stack([-sin, sin], axis=2).reshape(seq, HEAD_DIM)
    return cos_full, sin_full


def kernel(x_prompt, x_sample, state_mlstm_C, state_mlstm_n, state_mlstm_m, cache_gqa_k, cache_gqa_v,
           cache_na_k, cache_na_v, c, c_ctx, norm1_g, norm2_g, w_ada, b_ada, w_mlp1, w_mlp2,
           mlstm_w_in, mlstm_w_gate, mlstm_b_gate, mlstm_head_g, mlstm_w_out,
           gqa_w_qkv, gqa_q_g, gqa_k_g, gqa_w_o, na_w_qkv, na_q_g, na_k_g, na_rpb, na_w_o):
    bp, tp, d = x_prompt.shape
    bs, ts, _ = x_sample.shape
    depth = w_ada.shape[0]
    past = cache_gqa_k.shape[2]
    mh = mlstm_w_gate.shape[-1] // 4
    gqa_kv = cache_gqa_k.shape[3]
    gqa_q = gqa_w_o.shape[1] // HEAD_DIM
    na_h = cache_na_k.shape[3]
    assert bs + 1 <= MOD_ROWS

    cond8 = jnp.zeros((MOD_ROWS, d), F32).at[:bs].set(c).at[bs].set(c_ctx)
    mod = adaln_all(cond8, w_ada, b_ada).reshape(depth * MOD_ROWS * MOD_VECS, 1, d)
    g1 = norm1_g.reshape(depth, 1, d)
    g2 = norm2_g.reshape(depth, 1, d)
    cos, sin = _rope_tables(ts)

    groups = [
        dict(x=x_prompt.reshape(bp * tp, d), batch=bp, seq=tp, rpb_=bp * tp, row0=bs, ctx=True),
        dict(x=x_sample.reshape(bs * ts, d), batch=bs, seq=ts, rpb_=ts, row0=0, ctx=False),
    ]
    for grp in groups:
        grp["xn"] = modulate(grp["x"], g1, mod, layer=0, which_shift=0, which_scale=1,
                             rows_per_batch=grp["rpb_"], row0=grp["row0"])
    w_in_all = [mlstm_w_in.astype(BF16), gqa_w_qkv.astype(BF16), na_w_qkv.astype(BF16)]
    w_o_all = [mlstm_w_out.astype(BF16), gqa_w_o.astype(BF16), na_w_o.astype(BF16)]
    n_mlstm = state_mlstm_C.shape[1]
    c_all = None
    n_l, m_l, gk_l, gv_l, nk_l, nv_l = [], [], [], [], [], []
    mlp_w = {}

    for i in range(depth):
        kind, j = i % N_MIXERS, i // N_MIXERS
        last = i == depth - 1
        nxt = i if last else i + 1
        w_in, w_o = w_in_all[kind], w_o_all[kind]
        if kind == 0:
            wg_t, bg_t = _gate_weights_head_major(mlstm_w_gate[j], mlstm_b_gate[j], mh)

        for grp in groups:
            x, xn, batch, seq, ctx = grp["x"], grp["xn"], grp["batch"], grp["seq"], grp["ctx"]
            mk = dict(rows_per_batch=grp["rpb_"], row0=grp["row0"])
            cast_layer = 0 if (ctx and i == 0) else (i + 1 if (not ctx and not last) else None)
            cast = None if cast_layer is None else (w_mlp1, w_mlp2, cast_layer, MLP_HIDDEN_CHUNK)
            if cast is not None and kind == 0 and not ctx:
                proj, *mlp_w[cast_layer] = matmul(xn, w_in, j, tn=CAST_PROJ_COL_TILE, cast=cast)
                cast = None
            else:
                proj = matmul(xn, w_in, j)
            if kind == 0:
                gates = mlstm_gates(xn, wg_t, bg_t)
                if ctx:
                    a, (c_all, nf, mf), cast_out = mlstm_scan(
                        proj, gates, mlstm_head_g[j], None, batch=batch, seq=seq, n_heads=mh, write_state=True,
                        heads=4, state_slot=(j, n_mlstm), prev_states=c_all, cast=cast)
                    n_l.append(nf)
                    m_l.append(mf)
                else:
                    st = (state_mlstm_C, j, state_mlstm_n[:, j], state_mlstm_m[:, j])
                    a, _, cast_out = mlstm_scan(proj, gates, mlstm_head_g[j], st, batch=batch, seq=seq,
                                                n_heads=mh, write_state=False, cast=cast)
                if cast is not None:
                    mlp_w[cast_layer] = cast_out
            elif kind == 1:
                if ctx:
                    a, kn, v = ctx_attention(proj, gqa_q_g[j], gqa_k_g[j], batch=batch, seq=seq, n_q=gqa_q, n_kv=gqa_kv)
                    gk_l.append(kn.reshape(batch, seq, gqa_kv, HEAD_DIM))
                    gv_l.append(v.reshape(batch, seq, gqa_kv, HEAD_DIM))
                else:
                    ck = cache_gqa_k[:, j].astype(F32).reshape(batch * past, gqa_kv * HEAD_DIM)
                    cv = cache_gqa_v[:, j].astype(F32).reshape(batch * past, gqa_kv * HEAD_DIM)
                    a = gqa_attention(proj, ck, cv, gqa_q_g[j], gqa_k_g[j], cos, sin, batch=batch, seq=seq,
                                      past=past, n_q=gqa_q, n_kv=gqa_kv, cast=cast)
                    if cast is not None:
                        a, *mlp_w[cast_layer] = a
            else:
                if ctx:
                    a, kn, v = ctx_attention(proj, na_q_g[j], na_k_g[j], batch=batch, seq=seq, n_q=na_h, n_kv=na_h,
                                             heads=8)
                    nk_l.append(kn.reshape(batch, seq, na_h, HEAD_DIM))
                    nv_l.append(v.reshape(batch, seq, na_h, HEAD_DIM))
                else:
                    ck = cache_na_k[:, j].astype(F32).reshape(batch * past, na_h * HEAD_DIM)
                    cv = cache_na_v[:, j].astype(F32).reshape(batch * past, na_h * HEAD_DIM)
                    a = na_attention(proj, ck, cv, na_q_g[j], na_k_g[j], na_rpb[j], batch=batch, seq=seq,
                                     past=past, n_heads=na_h, cast=cast)
                    if cast is not None:
                        a, *mlp_w[cast_layer] = a
            x, xn = mm_res_norm(a, w_o, j, x, mod, g2, layer=i, which_gate=2, next_layer=i, next_shift=3,
                                next_scale=4, **mk)
            x, xn = fused_mlp(xn, *mlp_w[i], x, mod, g1, layer=i, next_layer=nxt, emit_next=not last, **mk)
            grp["x"], grp["xn"] = x, xn

    y_prompt = groups[0]["x"].reshape(bp, tp, d)
    y_sample = groups[1]["x"].reshape(bs, ts, d)
    return (y_prompt, y_sample, c_all, jnp.stack(n_l, axis=1), jnp.stack(m_l, axis=1),
            jnp.stack(gk_l, axis=1), jnp.stack(gv_l, axis=1), jnp.stack(nk_l, axis=1), jnp.stack(nv_l, axis=1))
```

```python
import functools
import math

import jax
import jax.numpy as jnp
from jax import lax
from jax.experimental import pallas as pl
from jax.experimental.pallas import tpu as pltpu

EPS = 1e-6
HEAD_DIM = 128
MLSTM_DK = 128
MLSTM_DV = 256
MLSTM_CHUNK = 128
GATE_SOFTCAP = 15.0
GRID_W = 64
WIN_ROWS = 8
WIN_COLS = 16
ROPE_THETA = 10000.0
N_MIXERS = 3
LOG2E = math.log2(math.e)
MOD_ROWS = 8
MOD_VECS = 6

VMEM_LIMIT_BYTES = 56 * 1024 * 1024
NORM_ROW_CHUNK = 16
MLP_HIDDEN_CHUNK = 1024
CAST_PROJ_COL_TILE = 768

F32 = jnp.float32
BF16 = jnp.bfloat16
NEG_INF = float("-inf")


def _params(sem):
    return pltpu.CompilerParams(dimension_semantics=sem, vmem_limit_bytes=VMEM_LIMIT_BYTES)


def _nt(a, b):
    return lax.dot_general(a, b, (((1,), (1,)), ((), ())), preferred_element_type=F32)


def _mm(a, b):
    return jnp.dot(a, b, preferred_element_type=F32)


def _split2(x):
    hi = x.astype(BF16)
    lo = (x - hi.astype(F32)).astype(BF16)
    return hi, lo


def _split3(x):
    hi = x.astype(BF16)
    r = x - hi.astype(F32)
    mid = r.astype(BF16)
    lo = (r - mid.astype(F32)).astype(BF16)
    return hi, mid, lo


def _mod_row(layer, which, rows_per_batch, tm, row0):
    def f(i, *_):
        b = row0 + (i * tm) // rows_per_batch
        return ((layer * MOD_ROWS + b) * MOD_VECS + which, 0, 0)
    return f


def _adaln_kernel(cond_ref, w_ref, b_ref, out_ref):
    c = cond_ref[...]
    s = c * jax.nn.sigmoid(c)
    sh, sl = _split2(s)
    wh, wl = _split2(w_ref[...])
    lhs = jnp.concatenate([sh, sl], axis=0)
    r = _mm(lhs, wh)
    out_ref[...] = r[:8] + r[8:] + _mm(sh, wl) + b_ref[...]


def adaln_all(cond8, w_ada, b_ada, tn=1024):
    nl, d, n = w_ada.shape
    return pl.pallas_call(
        _adaln_kernel,
        out_shape=jax.ShapeDtypeStruct((nl, 8, n), F32),
        grid=(nl, n // tn),
        in_specs=[
            pl.BlockSpec((8, d), lambda l, j: (0, 0)),
            pl.BlockSpec((None, d, tn), lambda l, j: (l, 0, j)),
            pl.BlockSpec((None, 1, tn), lambda l, j: (l, 0, j)),
        ],
        out_specs=pl.BlockSpec((None, 8, tn), lambda l, j: (l, 0, j)),
        compiler_params=_params(("parallel", "parallel")),
        name="adaln",
    )(cond8, w_ada, b_ada.reshape(nl, 1, n))


def _modulate_rows(src_ref, dst_ref, g, mul, sh, r0, n_rows):
    for c in range(n_rows // NORM_ROW_CHUNK):
        start = r0 + c * NORM_ROW_CHUNK
        if not isinstance(start, int):
            start = pl.multiple_of(start, NORM_ROW_CHUNK)
        rows = pl.ds(start, NORM_ROW_CHUNK)
        xf = src_ref[rows, :]
        y = xf * lax.rsqrt(jnp.mean(xf * xf, axis=-1, keepdims=True) + EPS)
        dst_ref[rows, :] = ((y * g) * mul + sh).astype(dst_ref.dtype)


def _modulate_kernel(x_ref, g_ref, sh_ref, sc_ref, out_ref, *, tm):
    g = g_ref[...]
    mul = 1.0 + sc_ref[...]
    sh = sh_ref[...]
    unroll = 4 * NORM_ROW_CHUNK

    def body(i, _):
        _modulate_rows(x_ref, out_ref, g, mul, sh, pl.multiple_of(i * unroll, unroll), unroll)
        return 0

    lax.fori_loop(0, tm // unroll, body, 0)


def modulate(x, g, mod, *, layer, which_shift, which_scale, rows_per_batch, row0, tm=1024):
    m, d = x.shape
    tm = min(tm, rows_per_batch)
    return pl.pallas_call(
        functools.partial(_modulate_kernel, tm=tm),
        out_shape=jax.ShapeDtypeStruct((m, d), BF16),
        grid=(m // tm,),
        in_specs=[
            pl.BlockSpec((tm, d), lambda i: (i, 0)),
            pl.BlockSpec((None, 1, d), lambda i: (layer, 0, 0)),
            pl.BlockSpec((None, 1, d), _mod_row(layer, which_shift, rows_per_batch, tm, row0)),
            pl.BlockSpec((None, 1, d), _mod_row(layer, which_scale, rows_per_batch, tm, row0)),
        ],
        out_specs=pl.BlockSpec((tm, d), lambda i: (i, 0)),
        compiler_params=_params(("parallel",)),
        name="modulate",
    )(x, g, mod, mod)


def _mm_kernel(a_ref, w_ref, out_ref):
    out_ref[...] = _mm(a_ref[...], w_ref[...]).astype(out_ref.dtype)


class _MlpCast:
    def __init__(self, w1, w2, layer, *, fc, n_steps, step_of, plain=()):
        _, d, ff = w1.shape
        r1 = d // n_steps
        assert d % n_steps == 0 and r1 % 16 == 0 and ff % fc == 0
        self.args = [w1, w2] + [arr for arr, _ in plain]
        self.in_specs = [pl.BlockSpec((None, r1, ff), lambda *g: (layer, step_of(*g), 0))]
        self.out_shape = [jax.ShapeDtypeStruct((ff // fc, d, fc), BF16)]
        self.out_specs = [pl.BlockSpec((ff // fc, r1, fc), lambda *g: (0, step_of(*g), 0))]
        for arr, idx in [(w2, layer)] + list(plain):
            _, rows, cols = arr.shape
            r = rows // n_steps
            assert rows % n_steps == 0 and r % 16 == 0
            self.in_specs.append(pl.BlockSpec((None, r, cols), lambda *g, idx=idx: (idx, step_of(*g), 0)))
            self.out_shape.append(jax.ShapeDtypeStruct((rows, cols), BF16))
            self.out_specs.append(pl.BlockSpec((r, cols), lambda *g: (step_of(*g), 0)))

    def wrap(self, kernel_fn, n_in, n_out):
        n = len(self.args)

        def wrapped(*refs):
            src = refs[n_in:n_in + n]
            dst = refs[n_in + n + n_out:n_in + 2 * n + n_out]
            kernel_fn(*refs[:n_in], *refs[n_in + n:n_in + n + n_out], *refs[n_in + 2 * n + n_out:])
            fc = dst[0].shape[2]
            for t in range(dst[0].shape[0]):
                dst[0][t] = src[0][:, t * fc:(t + 1) * fc].astype(BF16)
            for s_ref, d_ref in zip(src[1:], dst[1:]):
                d_ref[...] = s_ref[...].astype(BF16)
        return wrapped

    @staticmethod
    def build(cast, **kw):
        return _MlpCast(*cast[:3], fc=cast[3], plain=cast[4] if len(cast) > 4 else (), **kw)


def matmul(a, w, *, out_dtype=F32, tm=1024, tn=1024, cast=None, name="proj"):
    m, k = a.shape
    n = w.shape[1]
    nt = n // tn
    kern, extra = _mm_kernel, None
    if cast is not None:
        extra = _MlpCast.build(cast, n_steps=(m // tm) * nt, step_of=lambda i, j: i * nt + j)
        kern = extra.wrap(kern, 2, 1)
    outs = pl.pallas_call(
        kern,
        out_shape=tuple([jax.ShapeDtypeStruct((m, n), out_dtype)] + (extra.out_shape if extra else [])),
        grid=(m // tm, nt),
        in_specs=[
            pl.BlockSpec((tm, k), lambda i, j: (i, 0)),
            pl.BlockSpec((k, tn), lambda i, j: (0, j)),
        ] + (extra.in_specs if extra else []),
        out_specs=tuple([pl.BlockSpec((tm, tn), lambda i, j: (i, j))] + (extra.out_specs if extra else [])),
        compiler_params=_params(("parallel", "arbitrary")),
        name=name,
    )(a, w, *(extra.args if extra else []))
    return outs if extra else outs[0]


def _mmres_kernel(a_ref, w_ref, res_ref, gate_ref, g_ref, sh_ref, sc_ref, x_ref, xn_ref, *, tm, sub):
    gate = gate_ref[...]
    g = g_ref[...]
    mul = 1.0 + sc_ref[...]
    sh = sh_ref[...]
    for s in range(tm // sub):
        rows = pl.ds(s * sub, sub)
        x_ref[rows, :] = res_ref[rows, :] + gate * _mm(a_ref[rows, :], w_ref[...])
        _modulate_rows(x_ref, xn_ref, g, mul, sh, s * sub, sub)


def mm_res_norm(a, w, res, mod, g_next, *, layer, which_gate, next_layer, next_shift, next_scale,
                rows_per_batch, row0, tm=512, sub=256):
    m, k = a.shape
    d = w.shape[1]
    tm = min(tm, rows_per_batch)
    mk = (rows_per_batch, tm, row0)
    return pl.pallas_call(
        functools.partial(_mmres_kernel, tm=tm, sub=sub),
        out_shape=(jax.ShapeDtypeStruct((m, d), F32), jax.ShapeDtypeStruct((m, d), BF16)),
        grid=(m // tm,),
        in_specs=[
            pl.BlockSpec((tm, k), lambda i: (i, 0)),
            pl.BlockSpec((k, d), lambda i: (0, 0), pipeline_mode=pl.Buffered(1)),
            pl.BlockSpec((tm, d), lambda i: (i, 0)),
            pl.BlockSpec((None, 1, d), _mod_row(layer, which_gate, *mk)),
            pl.BlockSpec((None, 1, d), lambda i: (next_layer, 0, 0)),
            pl.BlockSpec((None, 1, d), _mod_row(next_layer, next_shift, *mk)),
            pl.BlockSpec((None, 1, d), _mod_row(next_layer, next_scale, *mk)),
        ],
        out_specs=(pl.BlockSpec((tm, d), lambda i: (i, 0)), pl.BlockSpec((tm, d), lambda i: (i, 0))),
        compiler_params=_params(("parallel",)),
        name="outproj_res_norm",
    )(a, w, res, mod, g_next, mod, mod)


def _mlp_kernel(xn_ref, w1_ref, w2_ref, res_ref, gate_ref, g_ref, sh_ref, sc_ref, x_ref, *rest, tm, sub, emit_next):
    xno_ref = rest[0] if emit_next else None
    acc_ref = rest[-1]
    f = pl.program_id(1)

    @pl.when(f == 0)
    def _():
        acc_ref[...] = jnp.zeros_like(acc_ref)

    def hidden_chunk(rows):
        h = _mm(xn_ref[rows, :], w1_ref[...])
        return _mm(jnp.square(jnp.maximum(h, 0.0)).astype(BF16), w2_ref[...])

    last = pl.num_programs(1) - 1

    @pl.when(f < last)
    def _():
        acc_ref[...] += hidden_chunk(slice(None))

    @pl.when(f == last)
    def _():
        gate = gate_ref[...]
        g = g_ref[...]
        mul = 1.0 + sc_ref[...]
        sh = sh_ref[...]
        for s in range(tm // sub):
            rows = pl.ds(s * sub, sub)
            x_ref[rows, :] = res_ref[rows, :] + gate * (acc_ref[rows, :] + hidden_chunk(rows))
            if emit_next:
                _modulate_rows(x_ref, xno_ref, g, mul, sh, s * sub, sub)


def fused_mlp(xn, w1, w2, res, mod, g_next, *, layer, next_layer, rows_per_batch, row0, emit_next,
              tm=512, sub=128):
    m, d = xn.shape
    fc = w1.shape[2]
    ff = w1.shape[0] * fc
    tm = min(tm, rows_per_batch)
    mk = (rows_per_batch, tm, row0)
    row_tile = pl.BlockSpec((tm, d), lambda i, f: (i, 0))
    out_shape = [jax.ShapeDtypeStruct((m, d), F32)] + ([jax.ShapeDtypeStruct((m, d), BF16)] if emit_next else [])
    outs = pl.pallas_call(
        functools.partial(_mlp_kernel, tm=tm, sub=min(sub, tm), emit_next=emit_next),
        out_shape=tuple(out_shape),
        grid=(m // tm, ff // fc),
        in_specs=[
            pl.BlockSpec((tm, d), lambda i, f: (i, 0)),
            pl.BlockSpec((None, d, fc), lambda i, f: (f, 0, 0)),
            pl.BlockSpec((fc, d), lambda i, f: (f, 0)),
            row_tile,
            pl.BlockSpec((None, 1, d), _mod_row(layer, 5, *mk)),
            pl.BlockSpec((None, 1, d), lambda i, f: (next_layer, 0, 0)),
            pl.BlockSpec((None, 1, d), _mod_row(next_layer, 0, *mk)),
            pl.BlockSpec((None, 1, d), _mod_row(next_layer, 1, *mk)),
        ],
        out_specs=tuple(row_tile for _ in out_shape),
        scratch_shapes=[pltpu.VMEM((tm, d), F32)],
        compiler_params=_params(("parallel", "arbitrary")),
        name="fused_mlp",
    )(xn, w1, w2, res, mod, g_next, mod, mod)
    return (outs[0], outs[1]) if emit_next else (outs[0], None)


def _head_norm(x, g):
    return x * lax.rsqrt(jnp.mean(x * x, axis=-1, keepdims=True) + EPS) * g


_Q_SCALE = HEAD_DIM ** -0.5 * LOG2E


def _with_ones_column(v):
    lane = lax.broadcasted_iota(jnp.int32, v.shape, 1)
    return jnp.concatenate([v.astype(BF16), jnp.where(lane == 0, 1.0, 0.0).astype(BF16)], axis=1)


def _softmax2_pv(s, vaug):
    p = jnp.exp2(s - jnp.max(s, axis=-1, keepdims=True)).astype(BF16)
    pv = _mm(p, vaug)
    return pv[:, :HEAD_DIM] / pv[:, HEAD_DIM:HEAD_DIM + 1]


def _rope(x, cos, sin_signed, lane_is_first):
    swapped = jnp.where(lane_is_first, pltpu.roll(x, 96, 1), pltpu.roll(x, 32, 1))
    return x * cos + swapped * sin_signed


def _ctx_attn_kernel(q_ref, k_ref, v_ref, qg_ref, kg_ref, o_ref, kn_ref, vo_ref, *, groups, heads):
    hd = HEAD_DIM
    qg = qg_ref[...] * _Q_SCALE
    for h in range(heads):
        hs = slice(h * hd, (h + 1) * hd)
        kn = _head_norm(k_ref[:, hs], kg_ref[...])
        kn_ref[:, hs] = kn
        v = v_ref[:, hs]
        vo_ref[:, hs] = v
        kb = kn.astype(BF16)
        vb = v.astype(BF16)
        for g in range(groups):
            qs = slice((h * groups + g) * hd, (h * groups + g + 1) * hd)
            qn = _head_norm(q_ref[:, qs], qg).astype(BF16)
            s = _nt(qn, kb)
            p = jnp.exp2(s - jnp.max(s, axis=-1, keepdims=True))
            o_ref[:, qs] = (_mm(p.astype(BF16), vb) / jnp.sum(p, axis=-1, keepdims=True)).astype(o_ref.dtype)


def ctx_attention(qkv, q_g, k_g, *, batch, seq, n_q, n_kv, heads=4):
    m = qkv.shape[0]
    groups = n_q // n_kv
    hd = HEAD_DIM
    kw = heads * hd
    qw = heads * groups * hd
    assert (n_q * hd) % kw == 0 and ((n_q + n_kv) * hd) % kw == 0
    k_blk0 = n_q * hd // kw
    v_blk0 = (n_q + n_kv) * hd // kw
    kern = functools.partial(_ctx_attn_kernel, groups=groups, heads=heads)
    return pl.pallas_call(
        kern,
        out_shape=(jax.ShapeDtypeStruct((m, n_q * hd), BF16),
                   jax.ShapeDtypeStruct((m, n_kv * hd), F32),
                   jax.ShapeDtypeStruct((m, n_kv * hd), F32)),
        grid=(batch, n_kv // heads),
        in_specs=[
            pl.BlockSpec((seq, qw), lambda b, h: (b, h)),
            pl.BlockSpec((seq, kw), lambda b, h: (b, k_blk0 + h)),
            pl.BlockSpec((seq, kw), lambda b, h: (b, v_blk0 + h)),
            pl.BlockSpec((1, hd), lambda b, h: (0, 0)),
            pl.BlockSpec((1, hd), lambda b, h: (0, 0)),
        ],
        out_specs=(pl.BlockSpec((seq, qw), lambda b, h: (b, h)),
                   pl.BlockSpec((seq, kw), lambda b, h: (b, h)),
                   pl.BlockSpec((seq, kw), lambda b, h: (b, h))),
        compiler_params=_params(("parallel", "parallel")),
        name="ctx_attention",
    )(qkv, qkv, qkv, q_g.reshape(1, hd), k_g.reshape(1, hd))


def _gqa_kernel(q_ref, k_ref, v_ref, ck_ref, cv_ref, qg_ref, kg_ref, cosq_ref, sinq_ref, cosk_ref, sink_ref,
                o_ref, kall_ref, vall_ref, *, groups, seq, past):
    lane = lax.broadcasted_iota(jnp.int32, (1, HEAD_DIM), 1)
    first = (lane % 64) < 32

    @pl.when(pl.program_id(2) == 0)
    def _():
        kn = _head_norm(k_ref[...], kg_ref[...])
        kall_ref[0:seq, :] = _rope(kn, cosk_ref[...], sink_ref[...], first).astype(BF16)
        kall_ref[seq:seq + past, :] = ck_ref[...].astype(BF16)
        vall_ref[0:seq, :] = _with_ones_column(v_ref[...])
        vall_ref[seq:seq + past, :] = _with_ones_column(cv_ref[...])

    cos = cosq_ref[...]
    sin = sinq_ref[...]
    qg = qg_ref[...] * _Q_SCALE
    for g in range(groups):
        qn = _head_norm(q_ref[:, g * HEAD_DIM:(g + 1) * HEAD_DIM], qg)
        qr = _rope(qn, cos, sin, first).astype(BF16)
        o_ref[:, g * HEAD_DIM:(g + 1) * HEAD_DIM] = _softmax2_pv(_nt(qr, kall_ref[...]), vall_ref[...]).astype(o_ref.dtype)


def gqa_attention(qkv, cache_k, cache_v, q_g, k_g, cos, sin, *, batch, seq, past, n_q, n_kv, tq=256, cast=None):
    m = qkv.shape[0]
    groups = n_q // n_kv
    hd = HEAD_DIM
    nqb = seq // tq
    kern = functools.partial(_gqa_kernel, groups=groups, seq=seq, past=past)
    args = [qkv, qkv, qkv, cache_k, cache_v, q_g.reshape(1, hd), k_g.reshape(1, hd), cos, sin, cos, sin]
    extra = None
    if cast is not None:
        extra = _MlpCast.build(cast, n_steps=batch * n_kv * nqb, step_of=lambda b, h, i: (b * n_kv + h) * nqb + i)
        kern = extra.wrap(kern, len(args), 1)
    outs = pl.pallas_call(
        kern,
        out_shape=tuple([jax.ShapeDtypeStruct((m, n_q * hd), BF16)] + (extra.out_shape if extra else [])),
        grid=(batch, n_kv, nqb),
        in_specs=[
            pl.BlockSpec((tq, groups * hd), lambda b, h, i: (b * nqb + i, h)),
            pl.BlockSpec((seq, hd), lambda b, h, i: (b, n_q + h)),
            pl.BlockSpec((seq, hd), lambda b, h, i: (b, n_q + n_kv + h)),
            pl.BlockSpec((past, hd), lambda b, h, i: (b, h)),
            pl.BlockSpec((past, hd), lambda b, h, i: (b, h)),
            pl.BlockSpec((1, hd), lambda b, h, i: (0, 0)),
            pl.BlockSpec((1, hd), lambda b, h, i: (0, 0)),
            pl.BlockSpec((tq, hd), lambda b, h, i: (i, 0)),
            pl.BlockSpec((tq, hd), lambda b, h, i: (i, 0)),
            pl.BlockSpec((seq, hd), lambda b, h, i: (0, 0)),
            pl.BlockSpec((seq, hd), lambda b, h, i: (0, 0)),
        ] + (extra.in_specs if extra else []),
        out_specs=tuple([pl.BlockSpec((tq, groups * hd), lambda b, h, i: (b * nqb + i, h))]
                        + (extra.out_specs if extra else [])),
        scratch_shapes=[pltpu.VMEM((seq + past, hd), BF16), pltpu.VMEM((seq + past, 2 * hd), BF16)],
        compiler_params=_params(("parallel", "parallel", "arbitrary")),
        name="gqa_attention",
    )(*args, *(extra.args if extra else []))
    return outs if extra else outs[0]


def _na_kernel(rpb_ref, q_ref, k_ref, v_ref, ck_ref, cv_ref, qg_ref, kg_ref, o_ref,
               qn_ref, kn_ref, vb_ref, ckb_ref, cvb_ref, tile_ref, pair_ref, *, rows, wr, n_dr, n_dc, unroll):
    h = pl.program_id(0)
    w = GRID_W

    qn_ref[...] = _head_norm(q_ref[...], qg_ref[...] * _Q_SCALE).astype(BF16)
    kn_ref[...] = _head_norm(k_ref[...], kg_ref[...]).astype(BF16)
    vb_ref[...] = _with_ones_column(v_ref[...])
    ckb_ref[...] = ck_ref[...].astype(BF16)
    cvb_ref[...] = _with_ones_column(cv_ref[...])

    @pl.when(pl.program_id(1) == 0)
    def _():
        qc = lax.broadcasted_iota(jnp.int32, (w, 2 * w), 0)
        lane = lax.broadcasted_iota(jnp.int32, (w, 2 * w), 1)
        kc = lane % w
        cs = jnp.clip(qc - WIN_COLS // 2, 0, w - WIN_COLS)
        col_ok = (kc >= cs) & (kc < cs + WIN_COLS)
        dc = jnp.clip(kc - qc + WIN_COLS - 1, 0, n_dc - 1)
        tiles = [jnp.zeros((w, 2 * w), F32) for _ in range(n_dr)]
        for d in range(n_dc):
            sel = dc == d
            for dr in range(n_dr):
                tiles[dr] = jnp.where(sel, rpb_ref[h, dr * n_dc + d], tiles[dr])
        for dr in range(n_dr):
            tile_ref[dr] = jnp.where(col_ok, tiles[dr] * LOG2E, NEG_INF)
        for dr in range(n_dr - 1):
            pair_ref[dr] = jnp.where(lane < w, tile_ref[dr], tile_ref[dr + 1])

    zero_bias = jnp.zeros((w, ckb_ref.shape[0]), F32)

    def body(i, _):
        rws = [i * unroll + u for u in range(unroll)]
        rss = [jnp.clip(r - wr // 2, 0, rows - wr) for r in rws]
        k0s = [pl.multiple_of(rs * w, w) for rs in rss]
        scores = []
        for r, rs, k0 in zip(rws, rss, k0s):
            dr0 = rs - r + WIN_ROWS - 1
            q_r = qn_ref[pl.ds(pl.multiple_of(r * w, w), w), :]
            kcat = jnp.concatenate([kn_ref[pl.ds(k0, wr * w), :], ckb_ref[...]], axis=0)
            bias = jnp.concatenate([pair_ref[dr0 + 2 * j] for j in range(wr // 2)] + [zero_bias], axis=1)
            scores.append(_nt(q_r, kcat) + bias)
        probs = [jnp.exp2(s - jnp.max(s, axis=-1, keepdims=True)).astype(BF16) for s in scores]
        for r, k0, p in zip(rws, k0s, probs):
            vcat = jnp.concatenate([vb_ref[pl.ds(k0, wr * w), :], cvb_ref[...]], axis=0)
            pv = _mm(p, vcat)
            o = pv[:, :HEAD_DIM] / pv[:, HEAD_DIM:HEAD_DIM + 1]
            o_ref[pl.ds(pl.multiple_of(r * w, w), w), :] = o.astype(o_ref.dtype)
        return 0

    lax.fori_loop(0, rows // unroll, body, 0)


def na_attention(qkv, cache_k, cache_v, q_g, k_g, rpb, *, batch, seq, past, n_heads, unroll=32, cast=None):
    m = qkv.shape[0]
    hd = HEAD_DIM
    rows = seq // GRID_W
    wr = min(WIN_ROWS, rows)
    n_dr, n_dc = rpb.shape[1], rpb.shape[2]
    unroll = min(unroll, rows)
    assert wr == WIN_ROWS and wr % 2 == 0 and rows % unroll == 0
    kern = functools.partial(_na_kernel, rows=rows, wr=wr, n_dr=n_dr, n_dc=n_dc, unroll=unroll)
    args = [rpb.reshape(n_heads, n_dr * n_dc), qkv, qkv, qkv, cache_k, cache_v, q_g.reshape(1, hd), k_g.reshape(1, hd)]
    extra = None
    if cast is not None:
        extra = _MlpCast.build(cast, n_steps=n_heads * batch, step_of=lambda h, b: h * batch + b)
        kern = extra.wrap(kern, len(args), 1)
    outs = pl.pallas_call(
        kern,
        out_shape=tuple([jax.ShapeDtypeStruct((m, n_heads * hd), BF16)] + (extra.out_shape if extra else [])),
        grid=(n_heads, batch),
        in_specs=[
            pl.BlockSpec(memory_space=pltpu.SMEM),
            pl.BlockSpec((seq, hd), lambda h, b: (b, h)),
            pl.BlockSpec((seq, hd), lambda h, b: (b, n_heads + h)),
            pl.BlockSpec((seq, hd), lambda h, b: (b, 2 * n_heads + h)),
            pl.BlockSpec((past, hd), lambda h, b: (b, h)),
            pl.BlockSpec((past, hd), lambda h, b: (b, h)),
            pl.BlockSpec((1, hd), lambda h, b: (0, 0)),
            pl.BlockSpec((1, hd), lambda h, b: (0, 0)),
        ] + (extra.in_specs if extra else []),
        out_specs=tuple([pl.BlockSpec((seq, hd), lambda h, b: (b, h))] + (extra.out_specs if extra else [])),
        scratch_shapes=[
            pltpu.VMEM((seq, hd), BF16), pltpu.VMEM((seq, hd), BF16), pltpu.VMEM((seq, 2 * hd), BF16),
            pltpu.VMEM((past, hd), BF16), pltpu.VMEM((past, 2 * hd), BF16),
            pltpu.VMEM((n_dr, GRID_W, 2 * GRID_W), F32),
            pltpu.VMEM((n_dr - 1, GRID_W, 2 * GRID_W), F32),
        ],
        compiler_params=_params(("parallel", "arbitrary")),
        name="na_attention",
    )(*args, *(extra.args if extra else []))
    return outs if extra else outs[0]


def _gates_kernel(xn_ref, w_ref, b_ref, out_ref, *, tm):
    L = MLSTM_CHUNK
    wh, wl = _split2(w_ref[...])
    xn = xn_ref[...]
    pre = _nt(wh, xn) + _nt(wl, xn) + b_ref[...]
    capped = GATE_SOFTCAP * jnp.tanh(pre / GATE_SOFTCAP)
    row = lax.broadcasted_iota(jnp.int32, (capped.shape[0], L), 0) % 8
    is_input = (row == 0) | (row == 2)
    ri = lax.broadcasted_iota(jnp.int32, (L, L), 0)
    ci = lax.broadcasted_iota(jnp.int32, (L, L), 1)
    upper = jnp.where(ri <= ci, 1.0, 0.0).astype(BF16)
    lower = jnp.where(ri >= ci, 1.0, 0.0).astype(BF16)
    for c in range(tm // L):
        cap = capped[:, c * L:(c + 1) * L]
        gates = jnp.where(is_input, cap, jax.nn.log_sigmoid(cap))
        pieces = _split3(gates)
        prefix = sum(_mm(p, upper) for p in pieces)
        suffix = sum(_mm(p, lower) for p in pieces)
        out_ref[c] = jnp.where(row == 4, prefix, jnp.where(row == 5, suffix, gates))


def mlstm_gates(xn, w_gate_t, b_gate_t, *, tm=512):
    m, d = xn.shape
    gh = w_gate_t.shape[0]
    return pl.pallas_call(
        functools.partial(_gates_kernel, tm=tm),
        out_shape=jax.ShapeDtypeStruct((m // MLSTM_CHUNK, gh, MLSTM_CHUNK), F32),
        grid=(m // tm,),
        in_specs=[
            pl.BlockSpec((tm, d), lambda i: (i, 0)),
            pl.BlockSpec((gh, d), lambda i: (0, 0)),
            pl.BlockSpec((gh, 1), lambda i: (0, 0)),
        ],
        out_specs=pl.BlockSpec((tm // MLSTM_CHUNK, gh, MLSTM_CHUNK), lambda i: (i, 0, 0)),
        compiler_params=_params(("parallel",)),
        name="mlstm_gates",
    )(xn, w_gate_t, b_gate_t)


def _mlstm_kernel(*refs, n_chunks, heads, zero_init, write_state, has_prev_states, slot):
    it = iter(refs)
    q_ref, k_ref, v_ref, o_ref, gates_ref, hg_ref = (next(it) for _ in range(6))
    if not zero_init:
        c0_ref, n0_ref, m0_ref = (next(it) for _ in range(3))
    if has_prev_states:
        next(it)
    y_ref = next(it)
    if write_state:
        cf_ref, nf_ref, mf_ref = (next(it) for _ in range(3))
    hdir_ref, cst_ref = next(it), next(it)

    L, dk, dv = MLSTM_CHUNK, MLSTM_DK, MLSTM_DV
    ri = lax.broadcasted_iota(jnp.int32, (L, L), 0)
    ci = lax.broadcasted_iota(jnp.int32, (L, L), 1)
    masks = (ri >= ci, ri <= ci)
    eye = ri == ci
    eye_b = jnp.where(eye, 1.0, 0.0).astype(BF16)
    qscale = dk ** -0.5

    m_init, n_init = [], []
    for hh in range(heads):
        for direction in range(2):
            idx = hh * 2 + direction
            if zero_init:
                cst_ref[idx] = jnp.zeros((dk, dv), F32)
                m_init.append(jnp.zeros((1, 1), F32))
                n_init.append(jnp.zeros((1, dk), F32))
            else:
                cst_ref[idx] = c0_ref[direction, hh]
                m_init.append(m0_ref[direction, hh])
                n_init.append(n0_ref[direction, hh])

    def body(step, carry, finish_now):
        ms, ns = carry
        chains = [(hh, direction) for hh in range(heads) for direction in range(2)]
        r0s = [pl.multiple_of((step if d == 0 else n_chunks - 1 - step) * L, L) for _, d in chains]
        cs = [step if d == 0 else n_chunks - 1 - step for _, d in chains]

        st1 = []
        for idx, (hh, d) in enumerate(chains):
            gt = gates_ref[cs[idx], hh * 8:(hh + 1) * 8, :]
            li, lf, brow = gt[2 * d:2 * d + 1], gt[2 * d + 1:2 * d + 2], gt[4 + d:5 + d]
            b3 = _nt(eye_b, jnp.concatenate(_split3(jnp.broadcast_to(brow, (L, L))), axis=0))
            bmat = b3[:, :L] + b3[:, L:2 * L] + b3[:, 2 * L:]
            log_d = jnp.where(masks[d], bmat - brow + li, NEG_INF)
            inter = bmat + ms[idx]
            m_j = jnp.maximum(inter, jnp.max(log_d, axis=-1, keepdims=True))
            tot = jnp.sum(lf, axis=-1, keepdims=True)
            log_w = tot - brow + li
            m_new = jnp.maximum(tot + ms[idx], jnp.max(log_w, axis=-1, keepdims=True))
            st1.append((log_d, inter, m_j, tot, log_w, m_new))

        st2 = []
        for idx, (hh, d) in enumerate(chains):
            log_d, inter, m_j, tot, log_w, m_new = st1[idx]
            qf = q_ref[pl.ds(r0s[idx], L), hh * dk:(hh + 1) * dk] * qscale
            q = qf.astype(BF16)
            k = k_ref[pl.ds(r0s[idx], L), hh * dk:(hh + 1) * dk]
            k_hi, k_lo = _split2(k)
            qk = _nt(q, k_hi)
            kt2 = _nt(eye_b, jnp.concatenate([k_hi, k_lo], axis=0))
            kt = kt2[:, :L] + kt2[:, L:]
            qc = _mm(q, cst_ref[idx].astype(BF16))
            qn = jnp.sum(qf * ns[idx], axis=-1, keepdims=True)
            d_mat = jnp.exp(log_d - m_j)
            w_inter = jnp.exp(inter - m_j)
            wt = jnp.exp(log_w - m_new)
            decay = jnp.exp(tot + ms[idx] - m_new)
            ktw = (kt * wt).astype(BF16)
            wn = _mm(jnp.broadcast_to(wt, (8, L)).astype(BF16), k_hi)[:1]
            st2.append((qk * d_mat, qc, qn, w_inter, decay, ktw, decay * ns[idx] + wn))

        for idx, (hh, d) in enumerate(chains):
            s, qc, qn, w_inter, decay, ktw, _ = st2[idx]
            floor = jnp.exp(-st1[idx][2])
            v = v_ref[pl.ds(r0s[idx], L), hh * dv:(hh + 1) * dv].astype(BF16)
            both = _mm(jnp.concatenate([s.astype(BF16), ktw], axis=0), v)
            num = jnp.concatenate([w_inter] * 2, axis=1) * qc + both[:L]
            den = jnp.maximum(jnp.abs(w_inter * qn + jnp.sum(s, axis=-1, keepdims=True)), floor)
            h = num / jnp.concatenate([den, den], axis=1)
            cst_ref[idx] = decay * cst_ref[idx] + both[L:]
            if not finish_now:
                hdir_ref[d, pl.ds(r0s[idx], L), hh * dv:(hh + 1) * dv] = h
            else:
                cols = slice(hh * dv, (hh + 1) * dv)
                hs = h + hdir_ref[1 - d, pl.ds(r0s[idx], L), cols]
                hn = hs * lax.rsqrt(jnp.mean(hs * hs, axis=-1, keepdims=True) + EPS) * hg_ref[:, cols]
                y = hn * jax.nn.sigmoid(o_ref[pl.ds(r0s[idx], L), cols])
                y_ref[pl.ds(r0s[idx], L), cols] = y.astype(y_ref.dtype)
        return tuple(s1[5] for s1 in st1), tuple(s2[6] for s2 in st2)

    half = n_chunks // 2
    carry = lax.fori_loop(0, half, functools.partial(body, finish_now=False), (tuple(m_init), tuple(n_init)))
    m_fin, n_fin = lax.fori_loop(half, n_chunks, functools.partial(body, finish_now=True), carry)

    if write_state:
        own = cf_ref if has_prev_states else cf_ref.at[slot]
        if not has_prev_states:
            for other in range(cf_ref.shape[0]):
                if other != slot:
                    cf_ref[other] = jnp.zeros(cf_ref.shape[1:], F32)
        for hh in range(heads):
            for direction in range(2):
                idx = hh * 2 + direction
                own[direction, hh] = cst_ref[idx]
                nf_ref[direction, hh] = n_fin[idx]
                mf_ref[direction, hh] = m_fin[idx]


def mlstm_scan(proj, gates, head_g, state=None, *, batch, seq, n_heads, write_state, heads=2,
               state_slot=(0, 1), prev_states=None, cast=None):
    m = proj.shape[0]
    dk, dv, L = MLSTM_DK, MLSTM_DV, MLSTM_CHUNK
    H = n_heads
    nb = H // heads
    n_chunks = seq // L
    assert seq % L == 0 and n_chunks % 2 == 0 and dk == L
    zero_init = state is None
    slot, n_slots = state_slot
    kern = functools.partial(_mlstm_kernel, n_chunks=n_chunks, heads=heads, zero_init=zero_init,
                             write_state=write_state, has_prev_states=prev_states is not None, slot=slot)
    in_specs = [
        pl.BlockSpec((seq, heads * dk), lambda b, h: (b, h)),
        pl.BlockSpec((seq, heads * dk), lambda b, h: (b, nb + h)),
        pl.BlockSpec((seq, heads * dv), lambda b, h: (b, nb + h)),
        pl.BlockSpec((seq, heads * dv), lambda b, h: (b, 2 * nb + h)),
        pl.BlockSpec((n_chunks, heads * 8, L), lambda b, h: (b, h, 0)),
        pl.BlockSpec((None, 1, heads * dv), lambda b, h: (h, 0, 0)),
    ]
    args = [proj, proj, proj, proj, gates, head_g.reshape(nb, 1, heads * dv)]
    st_specs = [pl.BlockSpec((None, 2, heads, dk, dv), lambda b, h: (b, 0, h, 0, 0)),
                pl.BlockSpec((None, 2, heads, 1, dk), lambda b, h: (b, 0, h, 0, 0)),
                pl.BlockSpec((None, 2, heads, 1, 1), lambda b, h: (b, 0, h, 0, 0))]
    if not zero_init:
        c0_all, c0_slot, n0, m0 = state
        in_specs += [pl.BlockSpec((None, None, 2, heads, dk, dv), lambda b, h: (b, c0_slot, 0, h, 0, 0))] + st_specs[1:]
        args += [c0_all.astype(F32), n0.astype(F32).reshape(batch, 2, H, 1, dk),
                 m0.astype(F32).reshape(batch, 2, H, 1, 1)]
    aliases = {}
    if prev_states is not None:
        aliases = {len(args): 1}
        in_specs.append(pl.BlockSpec(memory_space=pl.ANY))
        args.append(prev_states)
    out_shape = [jax.ShapeDtypeStruct((m, H * dv), BF16)]
    out_specs = [pl.BlockSpec((seq, heads * dv), lambda b, h: (b, h))]
    if write_state:
        out_shape += [jax.ShapeDtypeStruct((batch, n_slots, 2, H, dk, dv), F32),
                      jax.ShapeDtypeStruct((batch, 2, H, 1, dk), F32),
                      jax.ShapeDtypeStruct((batch, 2, H, 1, 1), F32)]
        if prev_states is not None:
            c_spec = pl.BlockSpec((None, None, 2, heads, dk, dv), lambda b, h: (b, slot, 0, h, 0, 0))
        else:
            c_spec = pl.BlockSpec((None, n_slots, 2, heads, dk, dv), lambda b, h: (b, 0, 0, h, 0, 0))
        out_specs += [c_spec] + st_specs[1:]
    cast_out = None
    if cast is not None:
        extra = _MlpCast.build(cast, n_steps=batch * nb, step_of=lambda b, h: b * nb + h)
        kern = extra.wrap(kern, len(args), len(out_shape))
        in_specs, args = in_specs + extra.in_specs, args + extra.args
        out_shape, out_specs = out_shape + extra.out_shape, out_specs + extra.out_specs
    outs = pl.pallas_call(
        kern,
        out_shape=tuple(out_shape),
        grid=(batch, nb),
        in_specs=in_specs,
        out_specs=tuple(out_specs),
        scratch_shapes=[pltpu.VMEM((2, seq, heads * dv), F32), pltpu.VMEM((2 * heads, dk, dv), F32)],
        input_output_aliases=aliases,
        compiler_params=_params(("parallel", "parallel")),
        name="mlstm_scan",
    )(*args)
    if cast is not None:
        n_cast = len(extra.args)
        outs, cast_out = outs[:-n_cast], tuple(outs[-n_cast:])
    if write_state:
        y, cf, nf, mf = outs
        return y, (cf, nf.reshape(batch, 2, H, dk), mf.reshape(batch, 2, H)), cast_out
    return outs[0], None, cast_out


def _gate_weights_head_major(w_gate, b_gate, n_heads):
    d = w_gate.shape[0]
    wt = jnp.transpose(w_gate.reshape(d, 4, n_heads), (2, 1, 0)).astype(F32)
    wt = jnp.concatenate([wt, wt[:, 1:2], wt[:, 3:4], jnp.zeros_like(wt[:, :2])], axis=1).reshape(8 * n_heads, d)
    bt = jnp.transpose(b_gate.reshape(4, n_heads), (1, 0)).astype(F32)
    bt = jnp.concatenate([bt, bt[:, 1:2], bt[:, 3:4], jnp.zeros_like(bt[:, :2])], axis=1).reshape(8 * n_heads, 1)
    return wt, bt


def _rope_tables(seq):
    nf = HEAD_DIM // 4
    t = jnp.arange(seq)
    inv = ROPE_THETA ** (-jnp.arange(nf, dtype=F32) / nf)
    pos = jnp.stack([t // GRID_W, t % GRID_W], axis=-1).astype(F32)
    ang = pos[:, :, None] * inv
    cos = jnp.cos(ang)
    sin = jnp.sin(ang)
    cos_full = jnp.stack([cos, cos], axis=2).reshape(seq, HEAD_DIM)
    sin_full = jnp.stack([-sin, sin], axis=2).reshape(seq, HEAD_DIM)
    return cos_full, sin_full


def kernel(x_prompt, x_sample, state_mlstm_C, state_mlstm_n, state_mlstm_m, cache_gqa_k, cache_gqa_v,
           cache_na_k, cache_na_v, c, c_ctx, norm1_g, norm2_g, w_ada, b_ada, w_mlp1, w_mlp2,
           mlstm_w_in, mlstm_w_gate, mlstm_b_gate, mlstm_head_g, mlstm_w_out,
           gqa_w_qkv, gqa_q_g, gqa_k_g, gqa_w_o, na_w_qkv, na_q_g, na_k_g, na_rpb, na_w_o):
    bp, tp, d = x_prompt.shape
    bs, ts, _ = x_sample.shape
    depth = w_ada.shape[0]
    past = cache_gqa_k.shape[2]
    mh = mlstm_w_gate.shape[-1] // 4
    gqa_kv = cache_gqa_k.shape[3]
    gqa_q = gqa_w_o.shape[1] // HEAD_DIM
    na_h = cache_na_k.shape[3]
    assert bs + 1 <= MOD_ROWS

    cond8 = jnp.zeros((MOD_ROWS, d), F32).at[:bs].set(c).at[bs].set(c_ctx)
    mod = adaln_all(cond8, w_ada, b_ada).reshape(depth * MOD_ROWS * MOD_VECS, 1, d)
    g1 = norm1_g.reshape(depth, 1, d)
    g2 = norm2_g.reshape(depth, 1, d)
    cos, sin = _rope_tables(ts)

    groups = [
        dict(x=x_prompt.reshape(bp * tp, d), batch=bp, seq=tp, rpb_=bp * tp, row0=bs, ctx=True),
        dict(x=x_sample.reshape(bs * ts, d), batch=bs, seq=ts, rpb_=ts, row0=0, ctx=False),
    ]
    for grp in groups:
        grp["xn"] = modulate(grp["x"], g1, mod, layer=0, which_shift=0, which_scale=1,
                             rows_per_batch=grp["rpb_"], row0=grp["row0"])
    proj_src = [(mlstm_w_in, mlstm_w_out), (gqa_w_qkv, gqa_w_o), (na_w_qkv, na_w_o)]
    proj_w = {0: (mlstm_w_in[0].astype(BF16), mlstm_w_out[0].astype(BF16))}
    n_mlstm = state_mlstm_C.shape[1]
    c_all = None
    n_l, m_l, gk_l, gv_l, nk_l, nv_l = [], [], [], [], [], []
    mlp_w = {}

    for i in range(depth):
        kind, j = i % N_MIXERS, i // N_MIXERS
        last = i == depth - 1
        nxt = i if last else i + 1
        w_in, w_o = proj_w[i]
        if kind == 0:
            wg_t, bg_t = _gate_weights_head_major(mlstm_w_gate[j], mlstm_b_gate[j], mh)

        for grp in groups:
            x, xn, batch, seq, ctx = grp["x"], grp["xn"], grp["batch"], grp["seq"], grp["ctx"]
            mk = dict(rows_per_batch=grp["rpb_"], row0=grp["row0"])
            cast_layer = 0 if (ctx and i == 0) else (i + 1 if (not ctx and not last) else None)
            cast, casted = None, None
            if cast_layer is not None:
                kl, jl = cast_layer % N_MIXERS, cast_layer // N_MIXERS
                plain = tuple((w, jl) for w in proj_src[kl]) if cast_layer > 0 else ()
                cast = (w_mlp1, w_mlp2, cast_layer, MLP_HIDDEN_CHUNK, plain)
            if cast is not None and kind == 0 and not ctx:
                proj, *casted = matmul(xn, w_in, tn=CAST_PROJ_COL_TILE, cast=cast)
                cast = None
            else:
                proj = matmul(xn, w_in)
            if kind == 0:
                gates = mlstm_gates(xn, wg_t, bg_t)
                if ctx:
                    a, (c_all, nf, mf), cast_out = mlstm_scan(
                        proj, gates, mlstm_head_g[j], None, batch=batch, seq=seq, n_heads=mh, write_state=True,
                        heads=4, state_slot=(j, n_mlstm), prev_states=c_all, cast=cast)
                    n_l.append(nf)
                    m_l.append(mf)
                else:
                    st = (state_mlstm_C, j, state_mlstm_n[:, j], state_mlstm_m[:, j])
                    a, _, cast_out = mlstm_scan(proj, gates, mlstm_head_g[j], st, batch=batch, seq=seq,
                                                n_heads=mh, write_state=False, cast=cast)
                if cast is not None:
                    casted = cast_out
            elif kind == 1:
                if ctx:
                    a, kn, v = ctx_attention(proj, gqa_q_g[j], gqa_k_g[j], batch=batch, seq=seq, n_q=gqa_q, n_kv=gqa_kv)
                    gk_l.append(kn.reshape(batch, seq, gqa_kv, HEAD_DIM))
                    gv_l.append(v.reshape(batch, seq, gqa_kv, HEAD_DIM))
                else:
                    ck = cache_gqa_k[:, j].astype(F32).reshape(batch * past, gqa_kv * HEAD_DIM)
                    cv = cache_gqa_v[:, j].astype(F32).reshape(batch * past, gqa_kv * HEAD_DIM)
                    a = gqa_attention(proj, ck, cv, gqa_q_g[j], gqa_k_g[j], cos, sin, batch=batch, seq=seq,
                                      past=past, n_q=gqa_q, n_kv=gqa_kv, cast=cast)
                    if cast is not None:
                        a, *casted = a
            else:
                if ctx:
                    a, kn, v = ctx_attention(proj, na_q_g[j], na_k_g[j], batch=batch, seq=seq, n_q=na_h, n_kv=na_h,
                                             heads=8)
                    nk_l.append(kn.reshape(batch, seq, na_h, HEAD_DIM))
                    nv_l.append(v.reshape(batch, seq, na_h, HEAD_DIM))
                else:
                    ck = cache_na_k[:, j].astype(F32).reshape(batch * past, na_h * HEAD_DIM)
                    cv = cache_na_v[:, j].astype(F32).reshape(batch * past, na_h * HEAD_DIM)
                    a = na_attention(proj, ck, cv, na_q_g[j], na_k_g[j], na_rpb[j], batch=batch, seq=seq,
                                     past=past, n_heads=na_h, cast=cast)
                    if cast is not None:
                        a, *casted = a
            if casted is not None:
                mlp_w[cast_layer] = tuple(casted[:2])
                if len(casted) > 2:
                    proj_w[cast_layer] = tuple(casted[2:4])
            x, xn = mm_res_norm(a, w_o, x, mod, g2, layer=i, which_gate=2, next_layer=i, next_shift=3,
                                next_scale=4, **mk)
            x, xn = fused_mlp(xn, *mlp_w[i], x, mod, g1, layer=i, next_layer=nxt, emit_next=not last, **mk)
            grp["x"], grp["xn"] = x, xn

    y_prompt = groups[0]["x"].reshape(bp, tp, d)
    y_sample = groups[1]["x"].reshape(bs, ts, d)
    return (y_prompt, y_sample, c_all, jnp.stack(n_l, axis=1), jnp.stack(m_l, axis=1),
            jnp.stack(gk_l, axis=1), jnp.stack(gv_l, axis=1), jnp.stack(nk_l, axis=1), jnp.stack(nv_l, axis=1))
```

```python
import functools
import math

import jax
import jax.numpy as jnp
from jax import lax
from jax.experimental import pallas as pl
from jax.experimental.pallas import tpu as pltpu

EPS = 1e-6
HEAD_DIM = 128
MLSTM_DK = 128
MLSTM_DV = 256
MLSTM_CHUNK = 128
GATE_SOFTCAP = 15.0
GRID_W = 64
WIN_ROWS = 8
WIN_COLS = 16
ROPE_THETA = 10000.0
N_MIXERS = 3
LOG2E = math.log2(math.e)
MOD_ROWS = 8
MOD_VECS = 6

VMEM_LIMIT_BYTES = 56 * 1024 * 1024
NORM_ROW_CHUNK = 16
MLP_HIDDEN_CHUNK = 1024
CAST_PROJ_COL_TILE = 768

F32 = jnp.float32
BF16 = jnp.bfloat16
NEG_INF = float("-inf")


def _params(sem):
    return pltpu.CompilerParams(dimension_semantics=sem, vmem_limit_bytes=VMEM_LIMIT_BYTES)


def _nt(a, b):
    return lax.dot_general(a, b, (((1,), (1,)), ((), ())), preferred_element_type=F32)


def _mm(a, b):
    return jnp.dot(a, b, preferred_element_type=F32)


def _split2(x):
    hi = x.astype(BF16)
    lo = (x - hi.astype(F32)).astype(BF16)
    return hi, lo


def _split3(x):
    hi = x.astype(BF16)
    r = x - hi.astype(F32)
    mid = r.astype(BF16)
    lo = (r - mid.astype(F32)).astype(BF16)
    return hi, mid, lo


def _mod_row(layer, which, rows_per_batch, tm, row0):
    def f(i, *_):
        b = row0 + (i * tm) // rows_per_batch
        return ((layer * MOD_ROWS + b) * MOD_VECS + which, 0, 0)
    return f


def _adaln_kernel(cond_ref, w_ref, b_ref, out_ref):
    c = cond_ref[...]
    s = c * jax.nn.sigmoid(c)
    sh, sl = _split2(s)
    wh, wl = _split2(w_ref[...])
    lhs = jnp.concatenate([sh, sl], axis=0)
    r = _mm(lhs, wh)
    out_ref[...] = r[:8] + r[8:] + _mm(sh, wl) + b_ref[...]


def adaln_all(cond8, w_ada, b_ada, tn=1024):
    nl, d, n = w_ada.shape
    return pl.pallas_call(
        _adaln_kernel,
        out_shape=jax.ShapeDtypeStruct((nl, 8, n), F32),
        grid=(nl, n // tn),
        in_specs=[
            pl.BlockSpec((8, d), lambda l, j: (0, 0)),
            pl.BlockSpec((None, d, tn), lambda l, j: (l, 0, j)),
            pl.BlockSpec((None, 1, tn), lambda l, j: (l, 0, j)),
        ],
        out_specs=pl.BlockSpec((None, 8, tn), lambda l, j: (l, 0, j)),
        compiler_params=_params(("parallel", "parallel")),
        name="adaln",
    )(cond8, w_ada, b_ada.reshape(nl, 1, n))


def _modulate_rows(src_ref, dst_ref, g, mul, sh, r0, n_rows):
    for c in range(n_rows // NORM_ROW_CHUNK):
        start = r0 + c * NORM_ROW_CHUNK
        if not isinstance(start, int):
            start = pl.multiple_of(start, NORM_ROW_CHUNK)
        rows = pl.ds(start, NORM_ROW_CHUNK)
        xf = src_ref[rows, :]
        y = xf * lax.rsqrt(jnp.mean(xf * xf, axis=-1, keepdims=True) + EPS)
        dst_ref[rows, :] = ((y * g) * mul + sh).astype(dst_ref.dtype)


def _modulate_kernel(x_ref, g_ref, sh_ref, sc_ref, out_ref, *, tm):
    g = g_ref[...]
    mul = 1.0 + sc_ref[...]
    sh = sh_ref[...]
    unroll = 4 * NORM_ROW_CHUNK

    def body(i, _):
        _modulate_rows(x_ref, out_ref, g, mul, sh, pl.multiple_of(i * unroll, unroll), unroll)
        return 0

    lax.fori_loop(0, tm // unroll, body, 0)


def modulate(x, g, mod, *, layer, which_shift, which_scale, rows_per_batch, row0, tm=1024):
    m, d = x.shape
    tm = min(tm, rows_per_batch)
    return pl.pallas_call(
        functools.partial(_modulate_kernel, tm=tm),
        out_shape=jax.ShapeDtypeStruct((m, d), BF16),
        grid=(m // tm,),
        in_specs=[
            pl.BlockSpec((tm, d), lambda i: (i, 0)),
            pl.BlockSpec((None, 1, d), lambda i: (layer, 0, 0)),
            pl.BlockSpec((None, 1, d), _mod_row(layer, which_shift, rows_per_batch, tm, row0)),
            pl.BlockSpec((None, 1, d), _mod_row(layer, which_scale, rows_per_batch, tm, row0)),
        ],
        out_specs=pl.BlockSpec((tm, d), lambda i: (i, 0)),
        compiler_params=_params(("parallel",)),
        name="modulate",
    )(x, g, mod, mod)


def _mm_kernel(a_ref, w_ref, out_ref):
    out_ref[...] = _mm(a_ref[...], w_ref[...]).astype(out_ref.dtype)


class _MlpCast:
    MIN_SLAB_ROWS = 32

    def __init__(self, w1, w2, layer, *, fc, n_steps, step_of, plain=()):
        _, d, ff = w1.shape
        assert ff % fc == 0
        self.args = [w1, w2] + [arr for arr, _ in plain]
        self.in_specs, self.out_shape, self.out_specs = [], [], []
        for pos, (arr, idx) in enumerate([(w1, layer), (w2, layer)] + list(plain)):
            _, rows, cols = arr.shape
            r = max(rows // n_steps, self.MIN_SLAB_ROWS)
            share = n_steps // (rows // r)
            assert rows % r == 0 and n_steps % (rows // r) == 0
            slab = lambda *g, share=share: step_of(*g) // share
            self.in_specs.append(pl.BlockSpec((None, r, cols), lambda *g, idx=idx, slab=slab: (idx, slab(*g), 0)))
            if pos == 0:
                self.out_shape.append(jax.ShapeDtypeStruct((cols // fc, rows, fc), BF16))
                self.out_specs.append(pl.BlockSpec((cols // fc, r, fc), lambda *g, slab=slab: (0, slab(*g), 0)))
            else:
                self.out_shape.append(jax.ShapeDtypeStruct((rows, cols), BF16))
                self.out_specs.append(pl.BlockSpec((r, cols), lambda *g, slab=slab: (slab(*g), 0)))

    def wrap(self, kernel_fn, n_in, n_out):
        n = len(self.args)

        def wrapped(*refs):
            src = refs[n_in:n_in + n]
            dst = refs[n_in + n + n_out:n_in + 2 * n + n_out]
            kernel_fn(*refs[:n_in], *refs[n_in + n:n_in + n + n_out], *refs[n_in + 2 * n + n_out:])
            fc = dst[0].shape[2]
            for t in range(dst[0].shape[0]):
                dst[0][t] = src[0][:, t * fc:(t + 1) * fc].astype(BF16)
            for s_ref, d_ref in zip(src[1:], dst[1:]):
                d_ref[...] = s_ref[...].astype(BF16)
        return wrapped

    @staticmethod
    def build(cast, **kw):
        return _MlpCast(*cast[:3], fc=cast[3], plain=cast[4] if len(cast) > 4 else (), **kw)


def matmul(a, w, *, out_dtype=F32, tm=1024, tn=1024, cast=None, name="proj"):
    m, k = a.shape
    n = w.shape[1]
    nt = n // tn
    kern, extra = _mm_kernel, None
    if cast is not None:
        extra = _MlpCast.build(cast, n_steps=(m // tm) * nt, step_of=lambda i, j: i * nt + j)
        kern = extra.wrap(kern, 2, 1)
    outs = pl.pallas_call(
        kern,
        out_shape=tuple([jax.ShapeDtypeStruct((m, n), out_dtype)] + (extra.out_shape if extra else [])),
        grid=(m // tm, nt),
        in_specs=[
            pl.BlockSpec((tm, k), lambda i, j: (i, 0)),
            pl.BlockSpec((k, tn), lambda i, j: (0, j)),
        ] + (extra.in_specs if extra else []),
        out_specs=tuple([pl.BlockSpec((tm, tn), lambda i, j: (i, j))] + (extra.out_specs if extra else [])),
        compiler_params=_params(("parallel", "arbitrary")),
        name=name,
    )(a, w, *(extra.args if extra else []))
    return outs if extra else outs[0]


def _mmres_kernel(a_ref, w_ref, res_ref, gate_ref, g_ref, sh_ref, sc_ref, x_ref, xn_ref, *, tm, sub):
    gate = gate_ref[...]
    g = g_ref[...]
    mul = 1.0 + sc_ref[...]
    sh = sh_ref[...]
    for s in range(tm // sub):
        rows = pl.ds(s * sub, sub)
        x_ref[rows, :] = res_ref[rows, :] + gate * _mm(a_ref[rows, :], w_ref[...])
        _modulate_rows(x_ref, xn_ref, g, mul, sh, s * sub, sub)


def mm_res_norm(a, w, res, mod, g_next, *, layer, which_gate, next_layer, next_shift, next_scale,
                rows_per_batch, row0, tm=512, sub=256):
    m, k = a.shape
    d = w.shape[1]
    tm = min(tm, rows_per_batch)
    mk = (rows_per_batch, tm, row0)
    return pl.pallas_call(
        functools.partial(_mmres_kernel, tm=tm, sub=sub),
        out_shape=(jax.ShapeDtypeStruct((m, d), F32), jax.ShapeDtypeStruct((m, d), BF16)),
        grid=(m // tm,),
        in_specs=[
            pl.BlockSpec((tm, k), lambda i: (i, 0)),
            pl.BlockSpec((k, d), lambda i: (0, 0), pipeline_mode=pl.Buffered(1)),
            pl.BlockSpec((tm, d), lambda i: (i, 0)),
            pl.BlockSpec((None, 1, d), _mod_row(layer, which_gate, *mk)),
            pl.BlockSpec((None, 1, d), lambda i: (next_layer, 0, 0)),
            pl.BlockSpec((None, 1, d), _mod_row(next_layer, next_shift, *mk)),
            pl.BlockSpec((None, 1, d), _mod_row(next_layer, next_scale, *mk)),
        ],
        out_specs=(pl.BlockSpec((tm, d), lambda i: (i, 0)), pl.BlockSpec((tm, d), lambda i: (i, 0))),
        compiler_params=_params(("parallel",)),
        name="outproj_res_norm",
    )(a, w, res, mod, g_next, mod, mod)


def _mlp_kernel(xn_ref, w1_ref, w2_ref, res_ref, gate_ref, g_ref, sh_ref, sc_ref, x_ref, *rest, tm, sub, emit_next):
    xno_ref = rest[0] if emit_next else None
    acc_ref = rest[-1]
    f = pl.program_id(1)

    @pl.when(f == 0)
    def _():
        acc_ref[...] = jnp.zeros_like(acc_ref)

    def hidden_chunk(rows):
        h = _mm(xn_ref[rows, :], w1_ref[...])
        return _mm(jnp.square(jnp.maximum(h, 0.0)).astype(BF16), w2_ref[...])

    last = pl.num_programs(1) - 1

    @pl.when(f < last)
    def _():
        acc_ref[...] += hidden_chunk(slice(None))

    @pl.when(f == last)
    def _():
        gate = gate_ref[...]
        g = g_ref[...]
        mul = 1.0 + sc_ref[...]
        sh = sh_ref[...]
        for s in range(tm // sub):
            rows = pl.ds(s * sub, sub)
            x_ref[rows, :] = res_ref[rows, :] + gate * (acc_ref[rows, :] + hidden_chunk(rows))
            if emit_next:
                _modulate_rows(x_ref, xno_ref, g, mul, sh, s * sub, sub)


def fused_mlp(xn, w1, w2, res, mod, g_next, *, layer, next_layer, rows_per_batch, row0, emit_next,
              tm=512, sub=128):
    m, d = xn.shape
    fc = w1.shape[2]
    ff = w1.shape[0] * fc
    tm = min(tm, rows_per_batch)
    mk = (rows_per_batch, tm, row0)
    row_tile = pl.BlockSpec((tm, d), lambda i, f: (i, 0))
    out_shape = [jax.ShapeDtypeStruct((m, d), F32)] + ([jax.ShapeDtypeStruct((m, d), BF16)] if emit_next else [])
    outs = pl.pallas_call(
        functools.partial(_mlp_kernel, tm=tm, sub=min(sub, tm), emit_next=emit_next),
        out_shape=tuple(out_shape),
        grid=(m // tm, ff // fc),
        in_specs=[
            pl.BlockSpec((tm, d), lambda i, f: (i, 0)),
            pl.BlockSpec((None, d, fc), lambda i, f: (f, 0, 0)),
            pl.BlockSpec((fc, d), lambda i, f: (f, 0)),
            row_tile,
            pl.BlockSpec((None, 1, d), _mod_row(layer, 5, *mk)),
            pl.BlockSpec((None, 1, d), lambda i, f: (next_layer, 0, 0)),
            pl.BlockSpec((None, 1, d), _mod_row(next_layer, 0, *mk)),
            pl.BlockSpec((None, 1, d), _mod_row(next_layer, 1, *mk)),
        ],
        out_specs=tuple(row_tile for _ in out_shape),
        scratch_shapes=[pltpu.VMEM((tm, d), F32)],
        compiler_params=_params(("parallel", "arbitrary")),
        name="fused_mlp",
    )(xn, w1, w2, res, mod, g_next, mod, mod)
    return (outs[0], outs[1]) if emit_next else (outs[0], None)


def _head_norm(x, g):
    return x * lax.rsqrt(jnp.mean(x * x, axis=-1, keepdims=True) + EPS) * g


_Q_SCALE = HEAD_DIM ** -0.5 * LOG2E


def _with_ones_column(v):
    lane = lax.broadcasted_iota(jnp.int32, v.shape, 1)
    return jnp.concatenate([v.astype(BF16), jnp.where(lane == 0, 1.0, 0.0).astype(BF16)], axis=1)


def _softmax2_pv(s, vaug):
    p = jnp.exp2(s - jnp.max(s, axis=-1, keepdims=True)).astype(BF16)
    pv = _mm(p, vaug)
    return pv[:, :HEAD_DIM] / pv[:, HEAD_DIM:HEAD_DIM + 1]


def _rope(x, cos, sin_signed, lane_is_first):
    swapped = jnp.where(lane_is_first, pltpu.roll(x, 96, 1), pltpu.roll(x, 32, 1))
    return x * cos + swapped * sin_signed


def _ctx_attn_kernel(q_ref, k_ref, v_ref, qg_ref, kg_ref, o_ref, kn_ref, vo_ref, *, groups, heads):
    hd = HEAD_DIM
    qg = qg_ref[...] * _Q_SCALE
    for h in range(heads):
        hs = slice(h * hd, (h + 1) * hd)
        kn = _head_norm(k_ref[:, hs], kg_ref[...])
        kn_ref[:, hs] = kn
        v = v_ref[:, hs]
        vo_ref[:, hs] = v
        kb = kn.astype(BF16)
        vb = v.astype(BF16)
        for g in range(groups):
            qs = slice((h * groups + g) * hd, (h * groups + g + 1) * hd)
            qn = _head_norm(q_ref[:, qs], qg).astype(BF16)
            s = _nt(qn, kb)
            p = jnp.exp2(s - jnp.max(s, axis=-1, keepdims=True))
            o_ref[:, qs] = (_mm(p.astype(BF16), vb) / jnp.sum(p, axis=-1, keepdims=True)).astype(o_ref.dtype)


def ctx_attention(qkv, q_g, k_g, *, batch, seq, n_q, n_kv, heads=4):
    m = qkv.shape[0]
    groups = n_q // n_kv
    hd = HEAD_DIM
    kw = heads * hd
    qw = heads * groups * hd
    assert (n_q * hd) % kw == 0 and ((n_q + n_kv) * hd) % kw == 0
    k_blk0 = n_q * hd // kw
    v_blk0 = (n_q + n_kv) * hd // kw
    kern = functools.partial(_ctx_attn_kernel, groups=groups, heads=heads)
    return pl.pallas_call(
        kern,
        out_shape=(jax.ShapeDtypeStruct((m, n_q * hd), BF16),
                   jax.ShapeDtypeStruct((m, n_kv * hd), F32),
                   jax.ShapeDtypeStruct((m, n_kv * hd), F32)),
        grid=(batch, n_kv // heads),
        in_specs=[
            pl.BlockSpec((seq, qw), lambda b, h: (b, h)),
            pl.BlockSpec((seq, kw), lambda b, h: (b, k_blk0 + h)),
            pl.BlockSpec((seq, kw), lambda b, h: (b, v_blk0 + h)),
            pl.BlockSpec((1, hd), lambda b, h: (0, 0)),
            pl.BlockSpec((1, hd), lambda b, h: (0, 0)),
        ],
        out_specs=(pl.BlockSpec((seq, qw), lambda b, h: (b, h)),
                   pl.BlockSpec((seq, kw), lambda b, h: (b, h)),
                   pl.BlockSpec((seq, kw), lambda b, h: (b, h))),
        compiler_params=_params(("parallel", "parallel")),
        name="ctx_attention",
    )(qkv, qkv, qkv, q_g.reshape(1, hd), k_g.reshape(1, hd))


def _gqa_kernel(q_ref, k_ref, v_ref, ck_ref, cv_ref, qg_ref, kg_ref, cosq_ref, sinq_ref, cosk_ref, sink_ref,
                o_ref, kall_ref, vall_ref, *, groups, seq, past):
    lane = lax.broadcasted_iota(jnp.int32, (1, HEAD_DIM), 1)
    first = (lane % 64) < 32

    @pl.when(pl.program_id(2) == 0)
    def _():
        kn = _head_norm(k_ref[...], kg_ref[...])
        kall_ref[0:seq, :] = _rope(kn, cosk_ref[...], sink_ref[...], first).astype(BF16)
        kall_ref[seq:seq + past, :] = ck_ref[...].astype(BF16)
        vall_ref[0:seq, :] = _with_ones_column(v_ref[...])
        vall_ref[seq:seq + past, :] = _with_ones_column(cv_ref[...])

    cos = cosq_ref[...]
    sin = sinq_ref[...]
    qg = qg_ref[...] * _Q_SCALE
    for g in range(groups):
        qn = _head_norm(q_ref[:, g * HEAD_DIM:(g + 1) * HEAD_DIM], qg)
        qr = _rope(qn, cos, sin, first).astype(BF16)
        o_ref[:, g * HEAD_DIM:(g + 1) * HEAD_DIM] = _softmax2_pv(_nt(qr, kall_ref[...]), vall_ref[...]).astype(o_ref.dtype)


def gqa_attention(qkv, cache_k, cache_v, q_g, k_g, cos, sin, *, batch, seq, past, n_q, n_kv, tq=256, cast=None):
    m = qkv.shape[0]
    groups = n_q // n_kv
    hd = HEAD_DIM
    nqb = seq // tq
    kern = functools.partial(_gqa_kernel, groups=groups, seq=seq, past=past)
    args = [qkv, qkv, qkv, cache_k, cache_v, q_g.reshape(1, hd), k_g.reshape(1, hd), cos, sin, cos, sin]
    extra = None
    if cast is not None:
        extra = _MlpCast.build(cast, n_steps=batch * n_kv * nqb, step_of=lambda b, h, i: (b * n_kv + h) * nqb + i)
        kern = extra.wrap(kern, len(args), 1)
    outs = pl.pallas_call(
        kern,
        out_shape=tuple([jax.ShapeDtypeStruct((m, n_q * hd), BF16)] + (extra.out_shape if extra else [])),
        grid=(batch, n_kv, nqb),
        in_specs=[
            pl.BlockSpec((tq, groups * hd), lambda b, h, i: (b * nqb + i, h)),
            pl.BlockSpec((seq, hd), lambda b, h, i: (b, n_q + h)),
            pl.BlockSpec((seq, hd), lambda b, h, i: (b, n_q + n_kv + h)),
            pl.BlockSpec((past, hd), lambda b, h, i: (b, h)),
            pl.BlockSpec((past, hd), lambda b, h, i: (b, h)),
            pl.BlockSpec((1, hd), lambda b, h, i: (0, 0)),
            pl.BlockSpec((1, hd), lambda b, h, i: (0, 0)),
            pl.BlockSpec((tq, hd), lambda b, h, i: (i, 0)),
            pl.BlockSpec((tq, hd), lambda b, h, i: (i, 0)),
            pl.BlockSpec((seq, hd), lambda b, h, i: (0, 0)),
            pl.BlockSpec((seq, hd), lambda b, h, i: (0, 0)),
        ] + (extra.in_specs if extra else []),
        out_specs=tuple([pl.BlockSpec((tq, groups * hd), lambda b, h, i: (b * nqb + i, h))]
                        + (extra.out_specs if extra else [])),
        scratch_shapes=[pltpu.VMEM((seq + past, hd), BF16), pltpu.VMEM((seq + past, 2 * hd), BF16)],
        compiler_params=_params(("parallel", "parallel", "arbitrary")),
        name="gqa_attention",
    )(*args, *(extra.args if extra else []))
    return outs if extra else outs[0]


def _na_kernel(rpb_ref, q_ref, k_ref, v_ref, ck_ref, cv_ref, qg_ref, kg_ref, o_ref,
               qn_ref, kn_ref, vb_ref, ckb_ref, cvb_ref, tile_ref, pair_ref, *, rows, wr, n_dr, n_dc, unroll):
    h = pl.program_id(0)
    w = GRID_W

    qn_ref[...] = _head_norm(q_ref[...], qg_ref[...] * _Q_SCALE).astype(BF16)
    kn_ref[...] = _head_norm(k_ref[...], kg_ref[...]).astype(BF16)
    vb_ref[...] = _with_ones_column(v_ref[...])
    ckb_ref[...] = ck_ref[...].astype(BF16)
    cvb_ref[...] = _with_ones_column(cv_ref[...])

    @pl.when(pl.program_id(1) == 0)
    def _():
        qc = lax.broadcasted_iota(jnp.int32, (w, 2 * w), 0)
        lane = lax.broadcasted_iota(jnp.int32, (w, 2 * w), 1)
        kc = lane % w
        cs = jnp.clip(qc - WIN_COLS // 2, 0, w - WIN_COLS)
        col_ok = (kc >= cs) & (kc < cs + WIN_COLS)
        dc = jnp.clip(kc - qc + WIN_COLS - 1, 0, n_dc - 1)
        tiles = [jnp.zeros((w, 2 * w), F32) for _ in range(n_dr)]
        for d in range(n_dc):
            sel = dc == d
            for dr in range(n_dr):
                tiles[dr] = jnp.where(sel, rpb_ref[h, dr * n_dc + d], tiles[dr])
        for dr in range(n_dr):
            tile_ref[dr] = jnp.where(col_ok, tiles[dr] * LOG2E, NEG_INF)
        for dr in range(n_dr - 1):
            pair_ref[dr] = jnp.where(lane < w, tile_ref[dr], tile_ref[dr + 1])

    zero_bias = jnp.zeros((w, ckb_ref.shape[0]), F32)

    def body(i, _):
        rws = [i * unroll + u for u in range(unroll)]
        rss = [jnp.clip(r - wr // 2, 0, rows - wr) for r in rws]
        k0s = [pl.multiple_of(rs * w, w) for rs in rss]
        scores = []
        for r, rs, k0 in zip(rws, rss, k0s):
            dr0 = rs - r + WIN_ROWS - 1
            q_r = qn_ref[pl.ds(pl.multiple_of(r * w, w), w), :]
            kcat = jnp.concatenate([kn_ref[pl.ds(k0, wr * w), :], ckb_ref[...]], axis=0)
            bias = jnp.concatenate([pair_ref[dr0 + 2 * j] for j in range(wr // 2)] + [zero_bias], axis=1)
            scores.append(_nt(q_r, kcat) + bias)
        probs = [jnp.exp2(s - jnp.max(s, axis=-1, keepdims=True)).astype(BF16) for s in scores]
        for r, k0, p in zip(rws, k0s, probs):
            vcat = jnp.concatenate([vb_ref[pl.ds(k0, wr * w), :], cvb_ref[...]], axis=0)
            pv = _mm(p, vcat)
            o = pv[:, :HEAD_DIM] / pv[:, HEAD_DIM:HEAD_DIM + 1]
            o_ref[pl.ds(pl.multiple_of(r * w, w), w), :] = o.astype(o_ref.dtype)
        return 0

    lax.fori_loop(0, rows // unroll, body, 0)


def na_attention(qkv, cache_k, cache_v, q_g, k_g, rpb, *, batch, seq, past, n_heads, unroll=32, cast=None):
    m = qkv.shape[0]
    hd = HEAD_DIM
    rows = seq // GRID_W
    wr = min(WIN_ROWS, rows)
    n_dr, n_dc = rpb.shape[1], rpb.shape[2]
    unroll = min(unroll, rows)
    assert wr == WIN_ROWS and wr % 2 == 0 and rows % unroll == 0
    kern = functools.partial(_na_kernel, rows=rows, wr=wr, n_dr=n_dr, n_dc=n_dc, unroll=unroll)
    args = [rpb.reshape(n_heads, n_dr * n_dc), qkv, qkv, qkv, cache_k, cache_v, q_g.reshape(1, hd), k_g.reshape(1, hd)]
    extra = None
    if cast is not None:
        extra = _MlpCast.build(cast, n_steps=n_heads * batch, step_of=lambda h, b: h * batch + b)
        kern = extra.wrap(kern, len(args), 1)
    outs = pl.pallas_call(
        kern,
        out_shape=tuple([jax.ShapeDtypeStruct((m, n_heads * hd), BF16)] + (extra.out_shape if extra else [])),
        grid=(n_heads, batch),
        in_specs=[
            pl.BlockSpec(memory_space=pltpu.SMEM),
            pl.BlockSpec((seq, hd), lambda h, b: (b, h)),
            pl.BlockSpec((seq, hd), lambda h, b: (b, n_heads + h)),
            pl.BlockSpec((seq, hd), lambda h, b: (b, 2 * n_heads + h)),
            pl.BlockSpec((past, hd), lambda h, b: (b, h)),
            pl.BlockSpec((past, hd), lambda h, b: (b, h)),
            pl.BlockSpec((1, hd), lambda h, b: (0, 0)),
            pl.BlockSpec((1, hd), lambda h, b: (0, 0)),
        ] + (extra.in_specs if extra else []),
        out_specs=tuple([pl.BlockSpec((seq, hd), lambda h, b: (b, h))] + (extra.out_specs if extra else [])),
        scratch_shapes=[
            pltpu.VMEM((seq, hd), BF16), pltpu.VMEM((seq, hd), BF16), pltpu.VMEM((seq, 2 * hd), BF16),
            pltpu.VMEM((past, hd), BF16), pltpu.VMEM((past, 2 * hd), BF16),
            pltpu.VMEM((n_dr, GRID_W, 2 * GRID_W), F32),
            pltpu.VMEM((n_dr - 1, GRID_W, 2 * GRID_W), F32),
        ],
        compiler_params=_params(("parallel", "arbitrary")),
        name="na_attention",
    )(*args, *(extra.args if extra else []))
    return outs if extra else outs[0]


def _gates_kernel(xn_ref, w_ref, b_ref, out_ref, *, tm):
    L = MLSTM_CHUNK
    wh, wl = _split2(w_ref[...])
    xn = xn_ref[...]
    pre = _nt(wh, xn) + _nt(wl, xn) + b_ref[...]
    capped = GATE_SOFTCAP * jnp.tanh(pre / GATE_SOFTCAP)
    row = lax.broadcasted_iota(jnp.int32, (capped.shape[0], L), 0) % 8
    is_input = (row == 0) | (row == 2)
    ri = lax.broadcasted_iota(jnp.int32, (L, L), 0)
    ci = lax.broadcasted_iota(jnp.int32, (L, L), 1)
    upper = jnp.where(ri <= ci, 1.0, 0.0).astype(BF16)
    lower = jnp.where(ri >= ci, 1.0, 0.0).astype(BF16)
    for c in range(tm // L):
        cap = capped[:, c * L:(c + 1) * L]
        gates = jnp.where(is_input, cap, jax.nn.log_sigmoid(cap))
        pieces = _split3(gates)
        prefix = sum(_mm(p, upper) for p in pieces)
        suffix = sum(_mm(p, lower) for p in pieces)
        out_ref[c] = jnp.where(row == 4, prefix, jnp.where(row == 5, suffix, gates))


def mlstm_gates(xn, w_gate_t, b_gate_t, *, tm=512):
    m, d = xn.shape
    gh = w_gate_t.shape[0]
    return pl.pallas_call(
        functools.partial(_gates_kernel, tm=tm),
        out_shape=jax.ShapeDtypeStruct((m // MLSTM_CHUNK, gh, MLSTM_CHUNK), F32),
        grid=(m // tm,),
        in_specs=[
            pl.BlockSpec((tm, d), lambda i: (i, 0)),
            pl.BlockSpec((gh, d), lambda i: (0, 0)),
            pl.BlockSpec((gh, 1), lambda i: (0, 0)),
        ],
        out_specs=pl.BlockSpec((tm // MLSTM_CHUNK, gh, MLSTM_CHUNK), lambda i: (i, 0, 0)),
        compiler_params=_params(("parallel",)),
        name="mlstm_gates",
    )(xn, w_gate_t, b_gate_t)


def _mlstm_kernel(*refs, n_chunks, heads, zero_init, write_state, has_prev_states, slot):
    it = iter(refs)
    q_ref, k_ref, v_ref, o_ref, gates_ref, hg_ref = (next(it) for _ in range(6))
    if not zero_init:
        c0_ref, n0_ref, m0_ref = (next(it) for _ in range(3))
    if has_prev_states:
        next(it)
    y_ref = next(it)
    if write_state:
        cf_ref, nf_ref, mf_ref = (next(it) for _ in range(3))
    hdir_ref, cst_ref = next(it), next(it)

    L, dk, dv = MLSTM_CHUNK, MLSTM_DK, MLSTM_DV
    ri = lax.broadcasted_iota(jnp.int32, (L, L), 0)
    ci = lax.broadcasted_iota(jnp.int32, (L, L), 1)
    masks = (ri >= ci, ri <= ci)
    eye = ri == ci
    eye_b = jnp.where(eye, 1.0, 0.0).astype(BF16)
    qscale = dk ** -0.5

    m_init, n_init = [], []
    for hh in range(heads):
        for direction in range(2):
            idx = hh * 2 + direction
            if zero_init:
                cst_ref[idx] = jnp.zeros((dk, dv), F32)
                m_init.append(jnp.zeros((1, 1), F32))
                n_init.append(jnp.zeros((1, dk), F32))
            else:
                cst_ref[idx] = c0_ref[direction, hh]
                m_init.append(m0_ref[direction, hh])
                n_init.append(n0_ref[direction, hh])

    def body(step, carry, finish_now):
        ms, ns = carry
        chains = [(hh, direction) for hh in range(heads) for direction in range(2)]
        r0s = [pl.multiple_of((step if d == 0 else n_chunks - 1 - step) * L, L) for _, d in chains]
        cs = [step if d == 0 else n_chunks - 1 - step for _, d in chains]

        st1 = []
        for idx, (hh, d) in enumerate(chains):
            gt = gates_ref[cs[idx], hh * 8:(hh + 1) * 8, :]
            li, lf, brow = gt[2 * d:2 * d + 1], gt[2 * d + 1:2 * d + 2], gt[4 + d:5 + d]
            b3 = _nt(eye_b, jnp.concatenate(_split3(jnp.broadcast_to(brow, (L, L))), axis=0))
            bmat = b3[:, :L] + b3[:, L:2 * L] + b3[:, 2 * L:]
            log_d = jnp.where(masks[d], bmat - brow + li, NEG_INF)
            inter = bmat + ms[idx]
            m_j = jnp.maximum(inter, jnp.max(log_d, axis=-1, keepdims=True))
            tot = jnp.sum(lf, axis=-1, keepdims=True)
            log_w = tot - brow + li
            m_new = jnp.maximum(tot + ms[idx], jnp.max(log_w, axis=-1, keepdims=True))
            st1.append((log_d, inter, m_j, tot, log_w, m_new))

        st2 = []
        for idx, (hh, d) in enumerate(chains):
            log_d, inter, m_j, tot, log_w, m_new = st1[idx]
            qf = q_ref[pl.ds(r0s[idx], L), hh * dk:(hh + 1) * dk] * qscale
            q = qf.astype(BF16)
            k = k_ref[pl.ds(r0s[idx], L), hh * dk:(hh + 1) * dk]
            k_hi, k_lo = _split2(k)
            qk = _nt(q, k_hi)
            kt2 = _nt(eye_b, jnp.concatenate([k_hi, k_lo], axis=0))
            kt = kt2[:, :L] + kt2[:, L:]
            qc = _mm(q, cst_ref[idx].astype(BF16))
            qn = jnp.sum(qf * ns[idx], axis=-1, keepdims=True)
            d_mat = jnp.exp(log_d - m_j)
            w_inter = jnp.exp(inter - m_j)
            wt = jnp.exp(log_w - m_new)
            decay = jnp.exp(tot + ms[idx] - m_new)
            ktw = (kt * wt).astype(BF16)
            wn = _mm(jnp.broadcast_to(wt, (8, L)).astype(BF16), k_hi)[:1]
            st2.append((qk * d_mat, qc, qn, w_inter, decay, ktw, decay * ns[idx] + wn))

        for idx, (hh, d) in enumerate(chains):
            s, qc, qn, w_inter, decay, ktw, _ = st2[idx]
            floor = jnp.exp(-st1[idx][2])
            v = v_ref[pl.ds(r0s[idx], L), hh * dv:(hh + 1) * dv].astype(BF16)
            both = _mm(jnp.concatenate([s.astype(BF16), ktw], axis=0), v)
            num = jnp.concatenate([w_inter] * 2, axis=1) * qc + both[:L]
            den = jnp.maximum(jnp.abs(w_inter * qn + jnp.sum(s, axis=-1, keepdims=True)), floor)
            h = num / jnp.concatenate([den, den], axis=1)
            cst_ref[idx] = decay * cst_ref[idx] + both[L:]
            if not finish_now:
                hdir_ref[d, pl.ds(r0s[idx], L), hh * dv:(hh + 1) * dv] = h
            else:
                cols = slice(hh * dv, (hh + 1) * dv)
                hs = h + hdir_ref[1 - d, pl.ds(r0s[idx], L), cols]
                hn = hs * lax.rsqrt(jnp.mean(hs * hs, axis=-1, keepdims=True) + EPS) * hg_ref[:, cols]
                y = hn * jax.nn.sigmoid(o_ref[pl.ds(r0s[idx], L), cols])
                y_ref[pl.ds(r0s[idx], L), cols] = y.astype(y_ref.dtype)
        return tuple(s1[5] for s1 in st1), tuple(s2[6] for s2 in st2)

    half = n_chunks // 2
    carry = lax.fori_loop(0, half, functools.partial(body, finish_now=False), (tuple(m_init), tuple(n_init)))
    m_fin, n_fin = lax.fori_loop(half, n_chunks, functools.partial(body, finish_now=True), carry)

    if write_state:
        own = cf_ref if has_prev_states else cf_ref.at[slot]
        if not has_prev_states:
            for other in range(cf_ref.shape[0]):
                if other != slot:
                    cf_ref[other] = jnp.zeros(cf_ref.shape[1:], F32)
        for hh in range(heads):
            for direction in range(2):
                idx = hh * 2 + direction
                own[direction, hh] = cst_ref[idx]
                nf_ref[direction, hh] = n_fin[idx]
                mf_ref[direction, hh] = m_fin[idx]


def mlstm_scan(proj, gates, head_g, state=None, *, batch, seq, n_heads, write_state, heads=2,
               state_slot=(0, 1), prev_states=None, cast=None):
    m = proj.shape[0]
    dk, dv, L = MLSTM_DK, MLSTM_DV, MLSTM_CHUNK
    H = n_heads
    nb = H // heads
    n_chunks = seq // L
    assert seq % L == 0 and n_chunks % 2 == 0 and dk == L
    zero_init = state is None
    slot, n_slots = state_slot
    kern = functools.partial(_mlstm_kernel, n_chunks=n_chunks, heads=heads, zero_init=zero_init,
                             write_state=write_state, has_prev_states=prev_states is not None, slot=slot)
    in_specs = [
        pl.BlockSpec((seq, heads * dk), lambda b, h: (b, h)),
        pl.BlockSpec((seq, heads * dk), lambda b, h: (b, nb + h)),
        pl.BlockSpec((seq, heads * dv), lambda b, h: (b, nb + h)),
        pl.BlockSpec((seq, heads * dv), lambda b, h: (b, 2 * nb + h)),
        pl.BlockSpec((n_chunks, heads * 8, L), lambda b, h: (b, h, 0)),
        pl.BlockSpec((None, 1, heads * dv), lambda b, h: (h, 0, 0)),
    ]
    args = [proj, proj, proj, proj, gates, head_g.reshape(nb, 1, heads * dv)]
    st_specs = [pl.BlockSpec((None, 2, heads, dk, dv), lambda b, h: (b, 0, h, 0, 0)),
                pl.BlockSpec((None, 2, heads, 1, dk), lambda b, h: (b, 0, h, 0, 0)),
                pl.BlockSpec((None, 2, heads, 1, 1), lambda b, h: (b, 0, h, 0, 0))]
    if not zero_init:
        c0_all, c0_slot, n0, m0 = state
        in_specs += [pl.BlockSpec((None, None, 2, heads, dk, dv), lambda b, h: (b, c0_slot, 0, h, 0, 0))] + st_specs[1:]
        args += [c0_all.astype(F32), n0.astype(F32).reshape(batch, 2, H, 1, dk),
                 m0.astype(F32).reshape(batch, 2, H, 1, 1)]
    aliases = {}
    if prev_states is not None:
        aliases = {len(args): 1}
        in_specs.append(pl.BlockSpec(memory_space=pl.ANY))
        args.append(prev_states)
    out_shape = [jax.ShapeDtypeStruct((m, H * dv), BF16)]
    out_specs = [pl.BlockSpec((seq, heads * dv), lambda b, h: (b, h))]
    if write_state:
        out_shape += [jax.ShapeDtypeStruct((batch, n_slots, 2, H, dk, dv), F32),
                      jax.ShapeDtypeStruct((batch, 2, H, 1, dk), F32),
                      jax.ShapeDtypeStruct((batch, 2, H, 1, 1), F32)]
        if prev_states is not None:
            c_spec = pl.BlockSpec((None, None, 2, heads, dk, dv), lambda b, h: (b, slot, 0, h, 0, 0))
        else:
            c_spec = pl.BlockSpec((None, n_slots, 2, heads, dk, dv), lambda b, h: (b, 0, 0, h, 0, 0))
        out_specs += [c_spec] + st_specs[1:]
    cast_out = None
    if cast is not None:
        extra = _MlpCast.build(cast, n_steps=batch * nb, step_of=lambda b, h: b * nb + h)
        kern = extra.wrap(kern, len(args), len(out_shape))
        in_specs, args = in_specs + extra.in_specs, args + extra.args
        out_shape, out_specs = out_shape + extra.out_shape, out_specs + extra.out_specs
    outs = pl.pallas_call(
        kern,
        out_shape=tuple(out_shape),
        grid=(batch, nb),
        in_specs=in_specs,
        out_specs=tuple(out_specs),
        scratch_shapes=[pltpu.VMEM((2, seq, heads * dv), F32), pltpu.VMEM((2 * heads, dk, dv), F32)],
        input_output_aliases=aliases,
        compiler_params=_params(("parallel", "parallel")),
        name="mlstm_scan",
    )(*args)
    if cast is not None:
        n_cast = len(extra.args)
        outs, cast_out = outs[:-n_cast], tuple(outs[-n_cast:])
    if write_state:
        y, cf, nf, mf = outs
        return y, (cf, nf.reshape(batch, 2, H, dk), mf.reshape(batch, 2, H)), cast_out
    return outs[0], None, cast_out


def _gate_weights_head_major(w_gate, b_gate, n_heads):
    d = w_gate.shape[0]
    wt = jnp.transpose(w_gate.reshape(d, 4, n_heads), (2, 1, 0)).astype(F32)
    wt = jnp.concatenate([wt, wt[:, 1:2], wt[:, 3:4], jnp.zeros_like(wt[:, :2])], axis=1).reshape(8 * n_heads, d)
    bt = jnp.transpose(b_gate.reshape(4, n_heads), (1, 0)).astype(F32)
    bt = jnp.concatenate([bt, bt[:, 1:2], bt[:, 3:4], jnp.zeros_like(bt[:, :2])], axis=1).reshape(8 * n_heads, 1)
    return wt, bt


def _rope_tables(seq):
    nf = HEAD_DIM // 4
    t = jnp.arange(seq)
    inv = ROPE_THETA ** (-jnp.arange(nf, dtype=F32) / nf)
    pos = jnp.stack([t // GRID_W, t % GRID_W], axis=-1).astype(F32)
    ang = pos[:, :, None] * inv
    cos = jnp.cos(ang)
    sin = jnp.sin(ang)
    cos_full = jnp.stack([cos, cos], axis=2).reshape(seq, HEAD_DIM)
    sin_full = jnp.stack([-sin, sin], axis=2).reshape(seq, HEAD_DIM)
    return cos_full, sin_full


def kernel(x_prompt, x_sample, state_mlstm_C, state_mlstm_n, state_mlstm_m, cache_gqa_k, cache_gqa_v,
           cache_na_k, cache_na_v, c, c_ctx, norm1_g, norm2_g, w_ada, b_ada, w_mlp1, w_mlp2,
           mlstm_w_in, mlstm_w_gate, mlstm_b_gate, mlstm_head_g, mlstm_w_out,
           gqa_w_qkv, gqa_q_g, gqa_k_g, gqa_w_o, na_w_qkv, na_q_g, na_k_g, na_rpb, na_w_o):
    bp, tp, d = x_prompt.shape
    bs, ts, _ = x_sample.shape
    depth = w_ada.shape[0]
    past = cache_gqa_k.shape[2]
    mh = mlstm_w_gate.shape[-1] // 4
    gqa_kv = cache_gqa_k.shape[3]
    gqa_q = gqa_w_o.shape[1] // HEAD_DIM
    na_h = cache_na_k.shape[3]
    assert bs + 1 <= MOD_ROWS

    cond8 = jnp.zeros((MOD_ROWS, d), F32).at[:bs].set(c).at[bs].set(c_ctx)
    mod = adaln_all(cond8, w_ada, b_ada).reshape(depth * MOD_ROWS * MOD_VECS, 1, d)
    g1 = norm1_g.reshape(depth, 1, d)
    g2 = norm2_g.reshape(depth, 1, d)
    cos, sin = _rope_tables(ts)

    groups = [
        dict(x=x_prompt.reshape(bp * tp, d), batch=bp, seq=tp, rpb_=bp * tp, row0=bs, ctx=True),
        dict(x=x_sample.reshape(bs * ts, d), batch=bs, seq=ts, rpb_=ts, row0=0, ctx=False),
    ]
    for grp in groups:
        grp["xn"] = modulate(grp["x"], g1, mod, layer=0, which_shift=0, which_scale=1,
                             rows_per_batch=grp["rpb_"], row0=grp["row0"])
    proj_src = [(mlstm_w_in, mlstm_w_out), (gqa_w_qkv, gqa_w_o), (na_w_qkv, na_w_o)]
    proj_w = {0: (mlstm_w_in[0].astype(BF16), mlstm_w_out[0].astype(BF16))}
    n_mlstm = state_mlstm_C.shape[1]
    c_all = None
    n_l, m_l, gk_l, gv_l, nk_l, nv_l = [], [], [], [], [], []
    mlp_w = {}

    for i in range(depth):
        kind, j = i % N_MIXERS, i // N_MIXERS
        last = i == depth - 1
        nxt = i if last else i + 1
        w_in, w_o = proj_w[i]
        if kind == 0:
            wg_t, bg_t = _gate_weights_head_major(mlstm_w_gate[j], mlstm_b_gate[j], mh)

        for grp in groups:
            x, xn, batch, seq, ctx = grp["x"], grp["xn"], grp["batch"], grp["seq"], grp["ctx"]
            mk = dict(rows_per_batch=grp["rpb_"], row0=grp["row0"])
            cast_layer = 0 if (ctx and i == 0) else (i + 1 if (not ctx and not last) else None)
            cast, casted = None, None
            if cast_layer is not None:
                kl, jl = cast_layer % N_MIXERS, cast_layer // N_MIXERS
                plain = tuple((w, jl) for w in proj_src[kl]) if cast_layer > 0 else ()
                cast = (w_mlp1, w_mlp2, cast_layer, MLP_HIDDEN_CHUNK, plain)
            if cast is not None and kind == 0 and not ctx:
                proj, *casted = matmul(xn, w_in, tn=CAST_PROJ_COL_TILE, cast=cast)
                cast = None
            else:
                proj = matmul(xn, w_in)
            if kind == 0:
                gates = mlstm_gates(xn, wg_t, bg_t)
                if ctx:
                    a, (c_all, nf, mf), cast_out = mlstm_scan(
                        proj, gates, mlstm_head_g[j], None, batch=batch, seq=seq, n_heads=mh, write_state=True,
                        heads=4, state_slot=(j, n_mlstm), prev_states=c_all, cast=cast)
                    n_l.append(nf)
                    m_l.append(mf)
                else:
                    st = (state_mlstm_C, j, state_mlstm_n[:, j], state_mlstm_m[:, j])
                    a, _, cast_out = mlstm_scan(proj, gates, mlstm_head_g[j], st, batch=batch, seq=seq,
                                                n_heads=mh, write_state=False, cast=cast)
                if cast is not None:
                    casted = cast_out
            elif kind == 1:
                if ctx:
                    a, kn, v = ctx_attention(proj, gqa_q_g[j], gqa_k_g[j], batch=batch, seq=seq, n_q=gqa_q, n_kv=gqa_kv)
                    gk_l.append(kn.reshape(batch, seq, gqa_kv, HEAD_DIM))
                    gv_l.append(v.reshape(batch, seq, gqa_kv, HEAD_DIM))
                else:
                    ck = cache_gqa_k[:, j].astype(F32).reshape(batch * past, gqa_kv * HEAD_DIM)
                    cv = cache_gqa_v[:, j].astype(F32).reshape(batch * past, gqa_kv * HEAD_DIM)
                    a = gqa_attention(proj, ck, cv, gqa_q_g[j], gqa_k_g[j], cos, sin, batch=batch, seq=seq,
                                      past=past, n_q=gqa_q, n_kv=gqa_kv, cast=cast)
                    if cast is not None:
                        a, *casted = a
            else:
                if ctx:
                    a, kn, v = ctx_attention(proj, na_q_g[j], na_k_g[j], batch=batch, seq=seq, n_q=na_h, n_kv=na_h,
                                             heads=8)
                    nk_l.append(kn.reshape(batch, seq, na_h, HEAD_DIM))
                    nv_l.append(v.reshape(batch, seq, na_h, HEAD_DIM))
                else:
                    ck = cache_na_k[:, j].astype(F32).reshape(batch * past, na_h * HEAD_DIM)
                    cv = cache_na_v[:, j].astype(F32).reshape(batch * past, na_h * HEAD_DIM)
                    a = na_attention(proj, ck, cv, na_q_g[j], na_k_g[j], na_rpb[j], batch=batch, seq=seq,
                                     past=past, n_heads=na_h, cast=cast)
                    if cast is not None:
                        a, *casted = a
            if casted is not None:
                mlp_w[cast_layer] = tuple(casted[:2])
                if len(casted) > 2:
                    proj_w[cast_layer] = tuple(casted[2:4])
            x, xn = mm_res_norm(a, w_o, x, mod, g2, layer=i, which_gate=2, next_layer=i, next_shift=3,
                                next_scale=4, **mk)
            x, xn = fused_mlp(xn, *mlp_w[i], x, mod, g1, layer=i, next_layer=nxt, emit_next=not last, **mk)
            grp["x"], grp["xn"] = x, xn

    y_prompt = groups[0]["x"].reshape(bp, tp, d)
    y_sample = groups[1]["x"].reshape(bs, ts, d)
    return (y_prompt, y_sample, c_all, jnp.stack(n_l, axis=1), jnp.stack(m_l, axis=1),
            jnp.stack(gk_l, axis=1), jnp.stack(gv_l, axis=1), jnp.stack(nk_l, axis=1), jnp.stack(nv_l, axis=1))
```

```python
import functools
import math

import jax
import jax.numpy as jnp
from jax import lax
from jax.experimental import pallas as pl
from jax.experimental.pallas import tpu as pltpu

EPS = 1e-6
HEAD_DIM = 128
MLSTM_DK = 128
MLSTM_DV = 256
MLSTM_CHUNK = 128
GATE_SOFTCAP = 15.0
GRID_W = 64
WIN_ROWS = 8
WIN_COLS = 16
ROPE_THETA = 10000.0
N_MIXERS = 3
LOG2E = math.log2(math.e)
MOD_ROWS = 8
MOD_VECS = 6

VMEM_LIMIT_BYTES = 56 * 1024 * 1024
NORM_ROW_CHUNK = 16
MLP_HIDDEN_CHUNK = 1024
CAST_PROJ_COL_TILE = 768

F32 = jnp.float32
BF16 = jnp.bfloat16
NEG_INF = float("-inf")


def _params(sem):
    return pltpu.CompilerParams(dimension_semantics=sem, vmem_limit_bytes=VMEM_LIMIT_BYTES)


def _nt(a, b):
    return lax.dot_general(a, b, (((1,), (1,)), ((), ())), preferred_element_type=F32)


def _mm(a, b):
    return jnp.dot(a, b, preferred_element_type=F32)


def _split2(x):
    hi = x.astype(BF16)
    lo = (x - hi.astype(F32)).astype(BF16)
    return hi, lo


def _split3(x):
    hi = x.astype(BF16)
    r = x - hi.astype(F32)
    mid = r.astype(BF16)
    lo = (r - mid.astype(F32)).astype(BF16)
    return hi, mid, lo


def _mod_row(layer, which, rows_per_batch, tm, row0):
    def f(i, *_):
        b = row0 + (i * tm) // rows_per_batch
        return ((layer * MOD_ROWS + b) * MOD_VECS + which, 0, 0)
    return f


def _adaln_kernel(cond_ref, w_ref, b_ref, out_ref):
    c = cond_ref[...]
    s = c * jax.nn.sigmoid(c)
    sh, sl = _split2(s)
    wh, wl = _split2(w_ref[...])
    lhs = jnp.concatenate([sh, sl], axis=0)
    r = _mm(lhs, wh)
    out_ref[...] = r[:8] + r[8:] + _mm(sh, wl) + b_ref[...]


def adaln_all(cond8, w_ada, b_ada, tn=1024):
    nl, d, n = w_ada.shape
    return pl.pallas_call(
        _adaln_kernel,
        out_shape=jax.ShapeDtypeStruct((nl, 8, n), F32),
        grid=(nl, n // tn),
        in_specs=[
            pl.BlockSpec((8, d), lambda l, j: (0, 0)),
            pl.BlockSpec((None, d, tn), lambda l, j: (l, 0, j)),
            pl.BlockSpec((None, 1, tn), lambda l, j: (l, 0, j)),
        ],
        out_specs=pl.BlockSpec((None, 8, tn), lambda l, j: (l, 0, j)),
        compiler_params=_params(("parallel", "parallel")),
        name="adaln",
    )(cond8, w_ada, b_ada.reshape(nl, 1, n))


def _modulate_rows(src_ref, dst_ref, g, mul, sh, r0, n_rows):
    for c in range(n_rows // NORM_ROW_CHUNK):
        start = r0 + c * NORM_ROW_CHUNK
        if not isinstance(start, int):
            start = pl.multiple_of(start, NORM_ROW_CHUNK)
        rows = pl.ds(start, NORM_ROW_CHUNK)
        xf = src_ref[rows, :]
        y = xf * lax.rsqrt(jnp.mean(xf * xf, axis=-1, keepdims=True) + EPS)
        dst_ref[rows, :] = ((y * g) * mul + sh).astype(dst_ref.dtype)


def _modulate_kernel(x_ref, g_ref, sh_ref, sc_ref, out_ref, *, tm):
    g = g_ref[...]
    mul = 1.0 + sc_ref[...]
    sh = sh_ref[...]
    unroll = 4 * NORM_ROW_CHUNK

    def body(i, _):
        _modulate_rows(x_ref, out_ref, g, mul, sh, pl.multiple_of(i * unroll, unroll), unroll)
        return 0

    lax.fori_loop(0, tm // unroll, body, 0)


def modulate(x, g, mod, *, layer, which_shift, which_scale, rows_per_batch, row0, tm=1024):
    m, d = x.shape
    tm = min(tm, rows_per_batch)
    return pl.pallas_call(
        functools.partial(_modulate_kernel, tm=tm),
        out_shape=jax.ShapeDtypeStruct((m, d), BF16),
        grid=(m // tm,),
        in_specs=[
            pl.BlockSpec((tm, d), lambda i: (i, 0)),
            pl.BlockSpec((None, 1, d), lambda i: (layer, 0, 0)),
            pl.BlockSpec((None, 1, d), _mod_row(layer, which_shift, rows_per_batch, tm, row0)),
            pl.BlockSpec((None, 1, d), _mod_row(layer, which_scale, rows_per_batch, tm, row0)),
        ],
        out_specs=pl.BlockSpec((tm, d), lambda i: (i, 0)),
        compiler_params=_params(("parallel",)),
        name="modulate",
    )(x, g, mod, mod)


def _mm_kernel(a_ref, w_ref, out_ref):
    out_ref[...] = _mm(a_ref[...], w_ref[...]).astype(out_ref.dtype)


class _MlpCast:
    MIN_SLAB_ROWS = 32

    def __init__(self, w1, w2, layer, *, fc, n_steps, step_of, plain=()):
        _, d, ff = w1.shape
        assert ff % fc == 0
        self.args = [w1, w2] + [arr for arr, _ in plain]
        self.in_specs, self.out_shape, self.out_specs = [], [], []
        for pos, (arr, idx) in enumerate([(w1, layer), (w2, layer)] + list(plain)):
            _, rows, cols = arr.shape
            r = max(rows // n_steps, self.MIN_SLAB_ROWS)
            share = n_steps // (rows // r)
            assert rows % r == 0 and n_steps % (rows // r) == 0
            slab = lambda *g, share=share: step_of(*g) // share
            self.in_specs.append(pl.BlockSpec((None, r, cols), lambda *g, idx=idx, slab=slab: (idx, slab(*g), 0)))
            if pos == 0:
                self.out_shape.append(jax.ShapeDtypeStruct((cols // fc, rows, fc), BF16))
                self.out_specs.append(pl.BlockSpec((cols // fc, r, fc), lambda *g, slab=slab: (0, slab(*g), 0)))
            else:
                self.out_shape.append(jax.ShapeDtypeStruct((rows, cols), BF16))
                self.out_specs.append(pl.BlockSpec((r, cols), lambda *g, slab=slab: (slab(*g), 0)))

    def wrap(self, kernel_fn, n_in, n_out):
        n = len(self.args)

        def wrapped(*refs):
            src = refs[n_in:n_in + n]
            dst = refs[n_in + n + n_out:n_in + 2 * n + n_out]
            kernel_fn(*refs[:n_in], *refs[n_in + n:n_in + n + n_out], *refs[n_in + 2 * n + n_out:])
            fc = dst[0].shape[2]
            for t in range(dst[0].shape[0]):
                dst[0][t] = src[0][:, t * fc:(t + 1) * fc].astype(BF16)
            for s_ref, d_ref in zip(src[1:], dst[1:]):
                d_ref[...] = s_ref[...].astype(BF16)
        return wrapped

    @staticmethod
    def build(cast, **kw):
        return _MlpCast(*cast[:3], fc=cast[3], plain=cast[4] if len(cast) > 4 else (), **kw)


def matmul(a, w, *, out_dtype=F32, tm=1024, tn=1024, cast=None, name="proj"):
    m, k = a.shape
    n = w.shape[1]
    nt = n // tn
    kern, extra = _mm_kernel, None
    if cast is not None:
        extra = _MlpCast.build(cast, n_steps=(m // tm) * nt, step_of=lambda i, j: i * nt + j)
        kern = extra.wrap(kern, 2, 1)
    outs = pl.pallas_call(
        kern,
        out_shape=tuple([jax.ShapeDtypeStruct((m, n), out_dtype)] + (extra.out_shape if extra else [])),
        grid=(m // tm, nt),
        in_specs=[
            pl.BlockSpec((tm, k), lambda i, j: (i, 0)),
            pl.BlockSpec((k, tn), lambda i, j: (0, j)),
        ] + (extra.in_specs if extra else []),
        out_specs=tuple([pl.BlockSpec((tm, tn), lambda i, j: (i, j))] + (extra.out_specs if extra else [])),
        compiler_params=_params(("parallel", "arbitrary")),
        name=name,
    )(a, w, *(extra.args if extra else []))
    return outs if extra else outs[0]


def _mmres_kernel(a_ref, w_ref, res_ref, gate_ref, g_ref, sh_ref, sc_ref, x_ref, xn_ref, *, tm, sub):
    gate = gate_ref[...]
    g = g_ref[...]
    mul = 1.0 + sc_ref[...]
    sh = sh_ref[...]
    for s in range(tm // sub):
        rows = pl.ds(s * sub, sub)
        x_ref[rows, :] = res_ref[rows, :] + gate * _mm(a_ref[rows, :], w_ref[...])
        _modulate_rows(x_ref, xn_ref, g, mul, sh, s * sub, sub)


def mm_res_norm(a, w, res, mod, g_next, *, layer, which_gate, next_layer, next_shift, next_scale,
                rows_per_batch, row0, tm=512, sub=256):
    m, k = a.shape
    d = w.shape[1]
    tm = min(tm, rows_per_batch)
    mk = (rows_per_batch, tm, row0)
    return pl.pallas_call(
        functools.partial(_mmres_kernel, tm=tm, sub=sub),
        out_shape=(jax.ShapeDtypeStruct((m, d), F32), jax.ShapeDtypeStruct((m, d), BF16)),
        grid=(m // tm,),
        in_specs=[
            pl.BlockSpec((tm, k), lambda i: (i, 0)),
            pl.BlockSpec((k, d), lambda i: (0, 0), pipeline_mode=pl.Buffered(1)),
            pl.BlockSpec((tm, d), lambda i: (i, 0)),
            pl.BlockSpec((None, 1, d), _mod_row(layer, which_gate, *mk)),
            pl.BlockSpec((None, 1, d), lambda i: (next_layer, 0, 0)),
            pl.BlockSpec((None, 1, d), _mod_row(next_layer, next_shift, *mk)),
            pl.BlockSpec((None, 1, d), _mod_row(next_layer, next_scale, *mk)),
        ],
        out_specs=(pl.BlockSpec((tm, d), lambda i: (i, 0)), pl.BlockSpec((tm, d), lambda i: (i, 0))),
        compiler_params=_params(("parallel",)),
        name="outproj_res_norm",
    )(a, w, res, mod, g_next, mod, mod)


def _mlp_kernel(xn_ref, w1_ref, w2_ref, res_ref, gate_ref, g_ref, sh_ref, sc_ref, x_ref, *rest, tm, sub, emit_next):
    xno_ref = rest[0] if emit_next else None
    acc_ref = rest[-1]
    f = pl.program_id(1)

    @pl.when(f == 0)
    def _():
        acc_ref[...] = jnp.zeros_like(acc_ref)

    def hidden_chunk(rows):
        h = _mm(xn_ref[rows, :], w1_ref[...])
        return _mm(jnp.square(jnp.maximum(h, 0.0)).astype(BF16), w2_ref[...])

    last = pl.num_programs(1) - 1

    @pl.when(f < last)
    def _():
        acc_ref[...] += hidden_chunk(slice(None))

    @pl.when(f == last)
    def _():
        gate = gate_ref[...]
        g = g_ref[...]
        mul = 1.0 + sc_ref[...]
        sh = sh_ref[...]
        for s in range(tm // sub):
            rows = pl.ds(s * sub, sub)
            x_ref[rows, :] = res_ref[rows, :] + gate * (acc_ref[rows, :] + hidden_chunk(rows))
            if emit_next:
                _modulate_rows(x_ref, xno_ref, g, mul, sh, s * sub, sub)


def fused_mlp(xn, w1, w2, res, mod, g_next, *, layer, next_layer, rows_per_batch, row0, emit_next,
              tm=512, sub=128):
    m, d = xn.shape
    fc = w1.shape[2]
    ff = w1.shape[0] * fc
    tm = min(tm, rows_per_batch)
    mk = (rows_per_batch, tm, row0)
    row_tile = pl.BlockSpec((tm, d), lambda i, f: (i, 0))
    out_shape = [jax.ShapeDtypeStruct((m, d), F32)] + ([jax.ShapeDtypeStruct((m, d), BF16)] if emit_next else [])
    outs = pl.pallas_call(
        functools.partial(_mlp_kernel, tm=tm, sub=min(sub, tm), emit_next=emit_next),
        out_shape=tuple(out_shape),
        grid=(m // tm, ff // fc),
        in_specs=[
            pl.BlockSpec((tm, d), lambda i, f: (i, 0)),
            pl.BlockSpec((None, d, fc), lambda i, f: (f, 0, 0)),
            pl.BlockSpec((fc, d), lambda i, f: (f, 0)),
            row_tile,
            pl.BlockSpec((None, 1, d), _mod_row(layer, 5, *mk)),
            pl.BlockSpec((None, 1, d), lambda i, f: (next_layer, 0, 0)),
            pl.BlockSpec((None, 1, d), _mod_row(next_layer, 0, *mk)),
            pl.BlockSpec((None, 1, d), _mod_row(next_layer, 1, *mk)),
        ],
        out_specs=tuple(row_tile for _ in out_shape),
        scratch_shapes=[pltpu.VMEM((tm, d), F32)],
        compiler_params=_params(("parallel", "arbitrary")),
        name="fused_mlp",
    )(xn, w1, w2, res, mod, g_next, mod, mod)
    return (outs[0], outs[1]) if emit_next else (outs[0], None)


def _head_norm(x, g):
    return x * lax.rsqrt(jnp.mean(x * x, axis=-1, keepdims=True) + EPS) * g


_Q_SCALE = HEAD_DIM ** -0.5 * LOG2E


def _with_ones_column(v):
    lane = lax.broadcasted_iota(jnp.int32, v.shape, 1)
    return jnp.concatenate([v.astype(BF16), jnp.where(lane == 0, 1.0, 0.0).astype(BF16)], axis=1)


def _softmax2_pv(s, vaug):
    p = jnp.exp2(s - jnp.max(s, axis=-1, keepdims=True)).astype(BF16)
    pv = _mm(p, vaug)
    return pv[:, :HEAD_DIM] / pv[:, HEAD_DIM:HEAD_DIM + 1]


def _rope(x, cos, sin_signed, lane_is_first):
    swapped = jnp.where(lane_is_first, pltpu.roll(x, 96, 1), pltpu.roll(x, 32, 1))
    return x * cos + swapped * sin_signed


def _ctx_attn_kernel(q_ref, k_ref, v_ref, qg_ref, kg_ref, o_ref, kn_ref, vo_ref, *, groups, heads):
    hd = HEAD_DIM
    qg = qg_ref[...] * _Q_SCALE
    for h in range(heads):
        hs = slice(h * hd, (h + 1) * hd)
        kn = _head_norm(k_ref[:, hs], kg_ref[...])
        kn_ref[:, hs] = kn
        v = v_ref[:, hs]
        vo_ref[:, hs] = v
        kb = kn.astype(BF16)
        vb = v.astype(BF16)
        for g in range(groups):
            qs = slice((h * groups + g) * hd, (h * groups + g + 1) * hd)
            qn = _head_norm(q_ref[:, qs], qg).astype(BF16)
            s = _nt(qn, kb)
            p = jnp.exp2(s - jnp.max(s, axis=-1, keepdims=True))
            o_ref[:, qs] = (_mm(p.astype(BF16), vb) / jnp.sum(p, axis=-1, keepdims=True)).astype(o_ref.dtype)


def ctx_attention(qkv, q_g, k_g, *, batch, seq, n_q, n_kv, heads=4):
    m = qkv.shape[0]
    groups = n_q // n_kv
    hd = HEAD_DIM
    kw = heads * hd
    qw = heads * groups * hd
    assert (n_q * hd) % kw == 0 and ((n_q + n_kv) * hd) % kw == 0
    k_blk0 = n_q * hd // kw
    v_blk0 = (n_q + n_kv) * hd // kw
    kern = functools.partial(_ctx_attn_kernel, groups=groups, heads=heads)
    return pl.pallas_call(
        kern,
        out_shape=(jax.ShapeDtypeStruct((m, n_q * hd), BF16),
                   jax.ShapeDtypeStruct((m, n_kv * hd), F32),
                   jax.ShapeDtypeStruct((m, n_kv * hd), F32)),
        grid=(batch, n_kv // heads),
        in_specs=[
            pl.BlockSpec((seq, qw), lambda b, h: (b, h)),
            pl.BlockSpec((seq, kw), lambda b, h: (b, k_blk0 + h)),
            pl.BlockSpec((seq, kw), lambda b, h: (b, v_blk0 + h)),
            pl.BlockSpec((1, hd), lambda b, h: (0, 0)),
            pl.BlockSpec((1, hd), lambda b, h: (0, 0)),
        ],
        out_specs=(pl.BlockSpec((seq, qw), lambda b, h: (b, h)),
                   pl.BlockSpec((seq, kw), lambda b, h: (b, h)),
                   pl.BlockSpec((seq, kw), lambda b, h: (b, h))),
        compiler_params=_params(("parallel", "parallel")),
        name="ctx_attention",
    )(qkv, qkv, qkv, q_g.reshape(1, hd), k_g.reshape(1, hd))


def _gqa_kernel(q_ref, k_ref, v_ref, ck_ref, cv_ref, qg_ref, kg_ref, cosq_ref, sinq_ref, cosk_ref, sink_ref,
                o_ref, kall_ref, vall_ref, *, groups, seq, past):
    lane = lax.broadcasted_iota(jnp.int32, (1, HEAD_DIM), 1)
    first = (lane % 64) < 32

    @pl.when(pl.program_id(2) == 0)
    def _():
        kn = _head_norm(k_ref[...], kg_ref[...])
        kall_ref[0:seq, :] = _rope(kn, cosk_ref[...], sink_ref[...], first).astype(BF16)
        kall_ref[seq:seq + past, :] = ck_ref[...].astype(BF16)
        vall_ref[0:seq, :] = _with_ones_column(v_ref[...])
        vall_ref[seq:seq + past, :] = _with_ones_column(cv_ref[...])

    cos = cosq_ref[...]
    sin = sinq_ref[...]
    qg = qg_ref[...] * _Q_SCALE
    for g in range(groups):
        qn = _head_norm(q_ref[:, g * HEAD_DIM:(g + 1) * HEAD_DIM], qg)
        qr = _rope(qn, cos, sin, first).astype(BF16)
        o_ref[:, g * HEAD_DIM:(g + 1) * HEAD_DIM] = _softmax2_pv(_nt(qr, kall_ref[...]), vall_ref[...]).astype(o_ref.dtype)


def gqa_attention(qkv, cache_k, cache_v, q_g, k_g, cos, sin, *, batch, seq, past, n_q, n_kv, tq=256, cast=None):
    m = qkv.shape[0]
    groups = n_q // n_kv
    hd = HEAD_DIM
    nqb = seq // tq
    kern = functools.partial(_gqa_kernel, groups=groups, seq=seq, past=past)
    args = [qkv, qkv, qkv, cache_k, cache_v, q_g.reshape(1, hd), k_g.reshape(1, hd), cos, sin, cos, sin]
    extra = None
    if cast is not None:
        extra = _MlpCast.build(cast, n_steps=batch * n_kv * nqb, step_of=lambda b, h, i: (b * n_kv + h) * nqb + i)
        kern = extra.wrap(kern, len(args), 1)
    outs = pl.pallas_call(
        kern,
        out_shape=tuple([jax.ShapeDtypeStruct((m, n_q * hd), BF16)] + (extra.out_shape if extra else [])),
        grid=(batch, n_kv, nqb),
        in_specs=[
            pl.BlockSpec((tq, groups * hd), lambda b, h, i: (b * nqb + i, h)),
            pl.BlockSpec((seq, hd), lambda b, h, i: (b, n_q + h)),
            pl.BlockSpec((seq, hd), lambda b, h, i: (b, n_q + n_kv + h)),
            pl.BlockSpec((past, hd), lambda b, h, i: (b, h)),
            pl.BlockSpec((past, hd), lambda b, h, i: (b, h)),
            pl.BlockSpec((1, hd), lambda b, h, i: (0, 0)),
            pl.BlockSpec((1, hd), lambda b, h, i: (0, 0)),
            pl.BlockSpec((tq, hd), lambda b, h, i: (i, 0)),
            pl.BlockSpec((tq, hd), lambda b, h, i: (i, 0)),
            pl.BlockSpec((seq, hd), lambda b, h, i: (0, 0)),
            pl.BlockSpec((seq, hd), lambda b, h, i: (0, 0)),
        ] + (extra.in_specs if extra else []),
        out_specs=tuple([pl.BlockSpec((tq, groups * hd), lambda b, h, i: (b * nqb + i, h))]
                        + (extra.out_specs if extra else [])),
        scratch_shapes=[pltpu.VMEM((seq + past, hd), BF16), pltpu.VMEM((seq + past, 2 * hd), BF16)],
        compiler_params=_params(("parallel", "parallel", "arbitrary")),
        name="gqa_attention",
    )(*args, *(extra.args if extra else []))
    return outs if extra else outs[0]


def _na_kernel(rpb_ref, q_ref, k_ref, v_ref, ck_ref, cv_ref, qg_ref, kg_ref, o_ref,
               qn_ref, kn_ref, vb_ref, ckb_ref, cvb_ref, tile_ref, pair_ref, *, rows, wr, n_dr, n_dc, unroll):
    h = pl.program_id(0)
    w = GRID_W

    qn_ref[...] = _head_norm(q_ref[...], qg_ref[...] * _Q_SCALE).astype(BF16)
    kn_ref[...] = _head_norm(k_ref[...], kg_ref[...]).astype(BF16)
    vb_ref[...] = _with_ones_column(v_ref[...])
    ckb_ref[...] = ck_ref[...].astype(BF16)
    cvb_ref[...] = _with_ones_column(cv_ref[...])

    @pl.when(pl.program_id(1) == 0)
    def _():
        qc = lax.broadcasted_iota(jnp.int32, (w, 2 * w), 0)
        lane = lax.broadcasted_iota(jnp.int32, (w, 2 * w), 1)
        kc = lane % w
        cs = jnp.clip(qc - WIN_COLS // 2, 0, w - WIN_COLS)
        col_ok = (kc >= cs) & (kc < cs + WIN_COLS)
        dc = jnp.clip(kc - qc + WIN_COLS - 1, 0, n_dc - 1)
        tiles = [jnp.zeros((w, 2 * w), F32) for _ in range(n_dr)]
        for d in range(n_dc):
            sel = dc == d
            for dr in range(n_dr):
                tiles[dr] = jnp.where(sel, rpb_ref[h, dr * n_dc + d], tiles[dr])
        for dr in range(n_dr):
            tile_ref[dr] = jnp.where(col_ok, tiles[dr] * LOG2E, NEG_INF)
        for dr in range(n_dr - 1):
            pair_ref[dr] = jnp.where(lane < w, tile_ref[dr], tile_ref[dr + 1])

    zero_bias = jnp.zeros((w, ckb_ref.shape[0]), F32)

    def body(i, _):
        rws = [i * unroll + u for u in range(unroll)]
        rss = [jnp.clip(r - wr // 2, 0, rows - wr) for r in rws]
        k0s = [pl.multiple_of(rs * w, w) for rs in rss]
        scores = []
        for r, rs, k0 in zip(rws, rss, k0s):
            dr0 = rs - r + WIN_ROWS - 1
            q_r = qn_ref[pl.ds(pl.multiple_of(r * w, w), w), :]
            kcat = jnp.concatenate([kn_ref[pl.ds(k0, wr * w), :], ckb_ref[...]], axis=0)
            bias = jnp.concatenate([pair_ref[dr0 + 2 * j] for j in range(wr // 2)] + [zero_bias], axis=1)
            scores.append(_nt(q_r, kcat) + bias)
        probs = [jnp.exp2(s - jnp.max(s, axis=-1, keepdims=True)).astype(BF16) for s in scores]
        for r, k0, p in zip(rws, k0s, probs):
            vcat = jnp.concatenate([vb_ref[pl.ds(k0, wr * w), :], cvb_ref[...]], axis=0)
            pv = _mm(p, vcat)
            o = pv[:, :HEAD_DIM] / pv[:, HEAD_DIM:HEAD_DIM + 1]
            o_ref[pl.ds(pl.multiple_of(r * w, w), w), :] = o.astype(o_ref.dtype)
        return 0

    lax.fori_loop(0, rows // unroll, body, 0)


def na_attention(qkv, cache_k, cache_v, q_g, k_g, rpb, *, batch, seq, past, n_heads, unroll=32, cast=None):
    m = qkv.shape[0]
    hd = HEAD_DIM
    rows = seq // GRID_W
    wr = min(WIN_ROWS, rows)
    n_dr, n_dc = rpb.shape[1], rpb.shape[2]
    unroll = min(unroll, rows)
    assert wr == WIN_ROWS and wr % 2 == 0 and rows % unroll == 0
    kern = functools.partial(_na_kernel, rows=rows, wr=wr, n_dr=n_dr, n_dc=n_dc, unroll=unroll)
    args = [rpb.reshape(n_heads, n_dr * n_dc), qkv, qkv, qkv, cache_k, cache_v, q_g.reshape(1, hd), k_g.reshape(1, hd)]
    extra = None
    if cast is not None:
        extra = _MlpCast.build(cast, n_steps=n_heads * batch, step_of=lambda h, b: h * batch + b)
        kern = extra.wrap(kern, len(args), 1)
    outs = pl.pallas_call(
        kern,
        out_shape=tuple([jax.ShapeDtypeStruct((m, n_heads * hd), BF16)] + (extra.out_shape if extra else [])),
        grid=(n_heads, batch),
        in_specs=[
            pl.BlockSpec(memory_space=pltpu.SMEM),
            pl.BlockSpec((seq, hd), lambda h, b: (b, h)),
            pl.BlockSpec((seq, hd), lambda h, b: (b, n_heads + h)),
            pl.BlockSpec((seq, hd), lambda h, b: (b, 2 * n_heads + h)),
            pl.BlockSpec((past, hd), lambda h, b: (b, h)),
            pl.BlockSpec((past, hd), lambda h, b: (b, h)),
            pl.BlockSpec((1, hd), lambda h, b: (0, 0)),
            pl.BlockSpec((1, hd), lambda h, b: (0, 0)),
        ] + (extra.in_specs if extra else []),
        out_specs=tuple([pl.BlockSpec((seq, hd), lambda h, b: (b, h))] + (extra.out_specs if extra else [])),
        scratch_shapes=[
            pltpu.VMEM((seq, hd), BF16), pltpu.VMEM((seq, hd), BF16), pltpu.VMEM((seq, 2 * hd), BF16),
            pltpu.VMEM((past, hd), BF16), pltpu.VMEM((past, 2 * hd), BF16),
            pltpu.VMEM((n_dr, GRID_W, 2 * GRID_W), F32),
            pltpu.VMEM((n_dr - 1, GRID_W, 2 * GRID_W), F32),
        ],
        compiler_params=_params(("parallel", "arbitrary")),
        name="na_attention",
    )(*args, *(extra.args if extra else []))
    return outs if extra else outs[0]


def _gates_kernel(xn_ref, w_ref, b_ref, out_ref, *, tm):
    L = MLSTM_CHUNK
    wh, wl = _split2(w_ref[...])
    xn = xn_ref[...]
    pre = _nt(wh, xn) + _nt(wl, xn) + b_ref[...]
    capped = GATE_SOFTCAP * jnp.tanh(pre / GATE_SOFTCAP)
    row = lax.broadcasted_iota(jnp.int32, (capped.shape[0], L), 0) % 8
    is_input = (row == 0) | (row == 2)
    ri = lax.broadcasted_iota(jnp.int32, (L, L), 0)
    ci = lax.broadcasted_iota(jnp.int32, (L, L), 1)
    upper = jnp.where(ri <= ci, 1.0, 0.0).astype(BF16)
    lower = jnp.where(ri >= ci, 1.0, 0.0).astype(BF16)
    for c in range(tm // L):
        cap = capped[:, c * L:(c + 1) * L]
        gates = jnp.where(is_input, cap, jax.nn.log_sigmoid(cap))
        pieces = _split3(gates)
        prefix = sum(_mm(p, upper) for p in pieces)
        suffix = sum(_mm(p, lower) for p in pieces)
        out_ref[c] = jnp.where(row == 4, prefix, jnp.where(row == 5, suffix, gates))


def mlstm_gates(xn, w_gate_t, b_gate_t, *, tm=512):
    m, d = xn.shape
    gh = w_gate_t.shape[0]
    return pl.pallas_call(
        functools.partial(_gates_kernel, tm=tm),
        out_shape=jax.ShapeDtypeStruct((m // MLSTM_CHUNK, gh, MLSTM_CHUNK), F32),
        grid=(m // tm,),
        in_specs=[
            pl.BlockSpec((tm, d), lambda i: (i, 0)),
            pl.BlockSpec((gh, d), lambda i: (0, 0)),
            pl.BlockSpec((gh, 1), lambda i: (0, 0)),
        ],
        out_specs=pl.BlockSpec((tm // MLSTM_CHUNK, gh, MLSTM_CHUNK), lambda i: (i, 0, 0)),
        compiler_params=_params(("parallel",)),
        name="mlstm_gates",
    )(xn, w_gate_t, b_gate_t)


def _mlstm_kernel(*refs, n_chunks, heads, zero_init, write_state, has_prev_states, slot):
    it = iter(refs)
    q_ref, k_ref, v_ref, o_ref, gates_ref, hg_ref = (next(it) for _ in range(6))
    if not zero_init:
        c0_ref, n0_ref, m0_ref = (next(it) for _ in range(3))
    if has_prev_states:
        next(it)
    y_ref = next(it)
    if write_state:
        cf_ref, nf_ref, mf_ref = (next(it) for _ in range(3))
    hdir_ref, cst_ref = next(it), next(it)

    L, dk, dv = MLSTM_CHUNK, MLSTM_DK, MLSTM_DV
    ri = lax.broadcasted_iota(jnp.int32, (L, L), 0)
    ci = lax.broadcasted_iota(jnp.int32, (L, L), 1)
    masks = (ri >= ci, ri <= ci)
    eye = ri == ci
    eye_b = jnp.where(eye, 1.0, 0.0).astype(BF16)
    qscale = dk ** -0.5

    m_init, n_init = [], []
    for hh in range(heads):
        for direction in range(2):
            idx = hh * 2 + direction
            if zero_init:
                cst_ref[idx] = jnp.zeros((dk, dv), F32)
                m_init.append(jnp.zeros((1, 1), F32))
                n_init.append(jnp.zeros((1, dk), F32))
            else:
                cst_ref[idx] = c0_ref[direction, hh]
                m_init.append(m0_ref[direction, hh])
                n_init.append(n0_ref[direction, hh])

    def body(step, carry, finish_now):
        ms, ns = carry
        chains = [(hh, direction) for hh in range(heads) for direction in range(2)]
        r0s = [pl.multiple_of((step if d == 0 else n_chunks - 1 - step) * L, L) for _, d in chains]
        cs = [step if d == 0 else n_chunks - 1 - step for _, d in chains]

        st1 = []
        for idx, (hh, d) in enumerate(chains):
            gt = gates_ref[cs[idx], hh * 8:(hh + 1) * 8, :]
            li, lf, brow = gt[2 * d:2 * d + 1], gt[2 * d + 1:2 * d + 2], gt[4 + d:5 + d]
            b3 = _nt(eye_b, jnp.concatenate(_split3(jnp.broadcast_to(brow, (L, L))), axis=0))
            bmat = b3[:, :L] + b3[:, L:2 * L] + b3[:, 2 * L:]
            log_d = jnp.where(masks[d], bmat - brow + li, NEG_INF)
            inter = bmat + ms[idx]
            m_j = jnp.maximum(inter, jnp.max(log_d, axis=-1, keepdims=True))
            tot = jnp.sum(lf, axis=-1, keepdims=True)
            log_w = tot - brow + li
            m_new = jnp.maximum(tot + ms[idx], jnp.max(log_w, axis=-1, keepdims=True))
            st1.append((log_d, inter, m_j, tot, log_w, m_new))

        st2 = []
        for idx, (hh, d) in enumerate(chains):
            log_d, inter, m_j, tot, log_w, m_new = st1[idx]
            qf = q_ref[pl.ds(r0s[idx], L), hh * dk:(hh + 1) * dk] * qscale
            q = qf.astype(BF16)
            k = k_ref[pl.ds(r0s[idx], L), hh * dk:(hh + 1) * dk]
            k_hi, k_lo = _split2(k)
            qk = _nt(q, k_hi)
            kt2 = _nt(eye_b, jnp.concatenate([k_hi, k_lo], axis=0))
            kt = kt2[:, :L] + kt2[:, L:]
            qc = _mm(q, cst_ref[idx].astype(BF16))
            qn = jnp.sum(qf * ns[idx], axis=-1, keepdims=True)
            d_mat = jnp.exp(log_d - m_j)
            w_inter = jnp.exp(inter - m_j)
            wt = jnp.exp(log_w - m_new)
            decay = jnp.exp(tot + ms[idx] - m_new)
            ktw = (kt * wt).astype(BF16)
            wn = _mm(jnp.broadcast_to(wt, (8, L)).astype(BF16), k_hi)[:1]
            st2.append((qk * d_mat, qc, qn, w_inter, decay, ktw, decay * ns[idx] + wn))

        for idx, (hh, d) in enumerate(chains):
            s, qc, qn, w_inter, decay, ktw, _ = st2[idx]
            floor = jnp.exp(-st1[idx][2])
            v = v_ref[pl.ds(r0s[idx], L), hh * dv:(hh + 1) * dv].astype(BF16)
            both = _mm(jnp.concatenate([s.astype(BF16), ktw], axis=0), v)
            num = jnp.concatenate([w_inter] * 2, axis=1) * qc + both[:L]
            den = jnp.maximum(jnp.abs(w_inter * qn + jnp.sum(s, axis=-1, keepdims=True)), floor)
            h = num / jnp.concatenate([den, den], axis=1)
            cst_ref[idx] = decay * cst_ref[idx] + both[L:]
            if not finish_now:
                hdir_ref[d, pl.ds(r0s[idx], L), hh * dv:(hh + 1) * dv] = h
            else:
                cols = slice(hh * dv, (hh + 1) * dv)
                hs = h + hdir_ref[1 - d, pl.ds(r0s[idx], L), cols]
                hn = hs * lax.rsqrt(jnp.mean(hs * hs, axis=-1, keepdims=True) + EPS) * hg_ref[:, cols]
                y = hn * jax.nn.sigmoid(o_ref[pl.ds(r0s[idx], L), cols])
                y_ref[pl.ds(r0s[idx], L), cols] = y.astype(y_ref.dtype)
        return tuple(s1[5] for s1 in st1), tuple(s2[6] for s2 in st2)

    half = n_chunks // 2
    carry = lax.fori_loop(0, half, functools.partial(body, finish_now=False), (tuple(m_init), tuple(n_init)))
    m_fin, n_fin = lax.fori_loop(half, n_chunks, functools.partial(body, finish_now=True), carry)

    if write_state:
        own = cf_ref if has_prev_states else cf_ref.at[slot]
        if not has_prev_states:
            for other in range(cf_ref.shape[0]):
                if other != slot:
                    cf_ref[other] = jnp.zeros(cf_ref.shape[1:], F32)
        for hh in range(heads):
            for direction in range(2):
                idx = hh * 2 + direction
                own[direction, hh] = cst_ref[idx]
                nf_ref[direction, hh] = n_fin[idx]
                mf_ref[direction, hh] = m_fin[idx]


def mlstm_scan(proj, gates, head_g, state=None, *, batch, seq, n_heads, write_state, heads=2,
               state_slot=(0, 1), prev_states=None, cast=None):
    m = proj.shape[0]
    dk, dv, L = MLSTM_DK, MLSTM_DV, MLSTM_CHUNK
    H = n_heads
    nb = H // heads
    n_chunks = seq // L
    assert seq % L == 0 and n_chunks % 2 == 0 and dk == L
    zero_init = state is None
    slot, n_slots = state_slot
    kern = functools.partial(_mlstm_kernel, n_chunks=n_chunks, heads=heads, zero_init=zero_init,
                             write_state=write_state, has_prev_states=prev_states is not None, slot=slot)
    in_specs = [
        pl.BlockSpec((seq, heads * dk), lambda b, h: (b, h)),
        pl.BlockSpec((seq, heads * dk), lambda b, h: (b, nb + h)),
        pl.BlockSpec((seq, heads * dv), lambda b, h: (b, nb + h)),
        pl.BlockSpec((seq, heads * dv), lambda b, h: (b, 2 * nb + h)),
        pl.BlockSpec((n_chunks, heads * 8, L), lambda b, h: (b, h, 0)),
        pl.BlockSpec((None, 1, heads * dv), lambda b, h: (h, 0, 0)),
    ]
    args = [proj, proj, proj, proj, gates, head_g.reshape(nb, 1, heads * dv)]
    st_specs = [pl.BlockSpec((None, 2, heads, dk, dv), lambda b, h: (b, 0, h, 0, 0)),
                pl.BlockSpec((None, 2, heads, 1, dk), lambda b, h: (b, 0, h, 0, 0)),
                pl.BlockSpec((None, 2, heads, 1, 1), lambda b, h: (b, 0, h, 0, 0))]
    if not zero_init:
        c0_all, c0_slot, n0, m0 = state
        in_specs += [pl.BlockSpec((None, None, 2, heads, dk, dv), lambda b, h: (b, c0_slot, 0, h, 0, 0))] + st_specs[1:]
        args += [c0_all.astype(F32), n0.astype(F32).reshape(batch, 2, H, 1, dk),
                 m0.astype(F32).reshape(batch, 2, H, 1, 1)]
    aliases = {}
    if prev_states is not None:
        aliases = {len(args): 1}
        in_specs.append(pl.BlockSpec(memory_space=pl.ANY))
        args.append(prev_states)
    out_shape = [jax.ShapeDtypeStruct((m, H * dv), BF16)]
    out_specs = [pl.BlockSpec((seq, heads * dv), lambda b, h: (b, h))]
    if write_state:
        out_shape += [jax.ShapeDtypeStruct((batch, n_slots, 2, H, dk, dv), F32),
                      jax.ShapeDtypeStruct((batch, 2, H, 1, dk), F32),
                      jax.ShapeDtypeStruct((batch, 2, H, 1, 1), F32)]
        if prev_states is not None:
            c_spec = pl.BlockSpec((None, None, 2, heads, dk, dv), lambda b, h: (b, slot, 0, h, 0, 0))
        else:
            c_spec = pl.BlockSpec((None, n_slots, 2, heads, dk, dv), lambda b, h: (b, 0, 0, h, 0, 0))
        out_specs += [c_spec] + st_specs[1:]
    cast_out = None
    if cast is not None:
        extra = _MlpCast.build(cast, n_steps=batch * nb, step_of=lambda b, h: b * nb + h)
        kern = extra.wrap(kern, len(args), len(out_shape))
        in_specs, args = in_specs + extra.in_specs, args + extra.args
        out_shape, out_specs = out_shape + extra.out_shape, out_specs + extra.out_specs
    outs = pl.pallas_call(
        kern,
        out_shape=tuple(out_shape),
        grid=(batch, nb),
        in_specs=in_specs,
        out_specs=tuple(out_specs),
        scratch_shapes=[pltpu.VMEM((2, seq, heads * dv), F32), pltpu.VMEM((2 * heads, dk, dv), F32)],
        input_output_aliases=aliases,
        compiler_params=_params(("parallel", "parallel")),
        name="mlstm_scan",
    )(*args)
    if cast is not None:
        n_cast = len(extra.args)
        outs, cast_out = outs[:-n_cast], tuple(outs[-n_cast:])
    if write_state:
        y, cf, nf, mf = outs
        return y, (cf, nf.reshape(batch, 2, H, dk), mf.reshape(batch, 2, H)), cast_out
    return outs[0], None, cast_out


def _gate_weights_head_major(w_gate, b_gate, n_heads):
    d = w_gate.shape[0]
    wt = jnp.transpose(w_gate.reshape(d, 4, n_heads), (2, 1, 0)).astype(F32)
    wt = jnp.concatenate([wt, wt[:, 1:2], wt[:, 3:4], jnp.zeros_like(wt[:, :2])], axis=1).reshape(8 * n_heads, d)
    bt = jnp.transpose(b_gate.reshape(4, n_heads), (1, 0)).astype(F32)
    bt = jnp.concatenate([bt, bt[:, 1:2], bt[:, 3:4], jnp.zeros_like(bt[:, :2])], axis=1).reshape(8 * n_heads, 1)
    return wt, bt


def _rope_tables(seq):
    nf = HEAD_DIM // 4
    t = jnp.arange(seq)
    inv = ROPE_THETA ** (-jnp.arange(nf, dtype=F32) / nf)
    pos = jnp.stack([t // GRID_W, t % GRID_W], axis=-1).astype(F32)
    ang = pos[:, :, None] * inv
    cos = jnp.cos(ang)
    sin = jnp.sin(ang)
    cos_full = jnp.stack([cos, cos], axis=2).reshape(seq, HEAD_DIM)
    sin_full = jnp.stack([-sin, sin], axis=2).reshape(seq, HEAD_DIM)
    return cos_full, sin_full


def kernel(x_prompt, x_sample, state_mlstm_C, state_mlstm_n, state_mlstm_m, cache_gqa_k, cache_gqa_v,
           cache_na_k, cache_na_v, c, c_ctx, norm1_g, norm2_g, w_ada, b_ada, w_mlp1, w_mlp2,
           mlstm_w_in, mlstm_w_gate, mlstm_b_gate, mlstm_head_g, mlstm_w_out,
           gqa_w_qkv, gqa_q_g, gqa_k_g, gqa_w_o, na_w_qkv, na_q_g, na_k_g, na_rpb, na_w_o):
    bp, tp, d = x_prompt.shape
    bs, ts, _ = x_sample.shape
    depth = w_ada.shape[0]
    past = cache_gqa_k.shape[2]
    mh = mlstm_w_gate.shape[-1] // 4
    gqa_kv = cache_gqa_k.shape[3]
    gqa_q = gqa_w_o.shape[1] // HEAD_DIM
    na_h = cache_na_k.shape[3]
    assert bs + 1 <= MOD_ROWS

    cond8 = jnp.zeros((MOD_ROWS, d), F32).at[:bs].set(c).at[bs].set(c_ctx)
    mod = adaln_all(cond8, w_ada, b_ada).reshape(depth * MOD_ROWS * MOD_VECS, 1, d)
    g1 = norm1_g.reshape(depth, 1, d)
    g2 = norm2_g.reshape(depth, 1, d)
    cos, sin = _rope_tables(ts)

    groups = [
        dict(x=x_prompt.reshape(bp * tp, d), batch=bp, seq=tp, rpb_=bp * tp, row0=bs, ctx=True),
        dict(x=x_sample.reshape(bs * ts, d), batch=bs, seq=ts, rpb_=ts, row0=0, ctx=False),
    ]
    for grp in groups:
        grp["xn"] = modulate(grp["x"], g1, mod, layer=0, which_shift=0, which_scale=1,
                             rows_per_batch=grp["rpb_"], row0=grp["row0"])
    proj_src = [(mlstm_w_in, mlstm_w_out), (gqa_w_qkv, gqa_w_o), (na_w_qkv, na_w_o)]
    proj_w = {0: (mlstm_w_in[0].astype(BF16), mlstm_w_out[0].astype(BF16))}
    n_mlstm = state_mlstm_C.shape[1]
    c_all = None
    n_l, m_l, gk_l, gv_l, nk_l, nv_l = [], [], [], [], [], []
    mlp_w = {}

    for i in range(depth):
        kind, j = i % N_MIXERS, i // N_MIXERS
        last = i == depth - 1
        nxt = i if last else i + 1
        w_in, w_o = proj_w[i]
        if kind == 0:
            wg_t, bg_t = _gate_weights_head_major(mlstm_w_gate[j], mlstm_b_gate[j], mh)

        for grp in groups:
            x, xn, batch, seq, ctx = grp["x"], grp["xn"], grp["batch"], grp["seq"], grp["ctx"]
            mk = dict(rows_per_batch=grp["rpb_"], row0=grp["row0"])
            cast_layer = 0 if (ctx and i == 0) else (i + 1 if (not ctx and not last) else None)
            cast, casted = None, None
            if cast_layer is not None:
                kl, jl = cast_layer % N_MIXERS, cast_layer // N_MIXERS
                plain = tuple((w, jl) for w in proj_src[kl]) if (cast_layer > 0 and kind != 1) else ()
                if cast_layer > 0 and not plain:
                    proj_w[cast_layer] = tuple(w[jl].astype(BF16) for w in proj_src[kl])
                cast = (w_mlp1, w_mlp2, cast_layer, MLP_HIDDEN_CHUNK, plain)
            if cast is not None and kind == 0 and not ctx:
                proj, *casted = matmul(xn, w_in, tn=CAST_PROJ_COL_TILE, cast=cast)
                cast = None
            else:
                proj = matmul(xn, w_in)
            if kind == 0:
                gates = mlstm_gates(xn, wg_t, bg_t)
                if ctx:
                    a, (c_all, nf, mf), cast_out = mlstm_scan(
                        proj, gates, mlstm_head_g[j], None, batch=batch, seq=seq, n_heads=mh, write_state=True,
                        heads=4, state_slot=(j, n_mlstm), prev_states=c_all, cast=cast)
                    n_l.append(nf)
                    m_l.append(mf)
                else:
                    st = (state_mlstm_C, j, state_mlstm_n[:, j], state_mlstm_m[:, j])
                    a, _, cast_out = mlstm_scan(proj, gates, mlstm_head_g[j], st, batch=batch, seq=seq,
                                                n_heads=mh, write_state=False, cast=cast)
                if cast is not None:
                    casted = cast_out
            elif kind == 1:
                if ctx:
                    a, kn, v = ctx_attention(proj, gqa_q_g[j], gqa_k_g[j], batch=batch, seq=seq, n_q=gqa_q, n_kv=gqa_kv)
                    gk_l.append(kn.reshape(batch, seq, gqa_kv, HEAD_DIM))
                    gv_l.append(v.reshape(batch, seq, gqa_kv, HEAD_DIM))
                else:
                    ck = cache_gqa_k[:, j].astype(F32).reshape(batch * past, gqa_kv * HEAD_DIM)
                    cv = cache_gqa_v[:, j].astype(F32).reshape(batch * past, gqa_kv * HEAD_DIM)
                    a = gqa_attention(proj, ck, cv, gqa_q_g[j], gqa_k_g[j], cos, sin, batch=batch, seq=seq,
                                      past=past, n_q=gqa_q, n_kv=gqa_kv, cast=cast)
                    if cast is not None:
                        a, *casted = a
            else:
                if ctx:
                    a, kn, v = ctx_attention(proj, na_q_g[j], na_k_g[j], batch=batch, seq=seq, n_q=na_h, n_kv=na_h,
                                             heads=8)
                    nk_l.append(kn.reshape(batch, seq, na_h, HEAD_DIM))
                    nv_l.append(v.reshape(batch, seq, na_h, HEAD_DIM))
                else:
                    ck = cache_na_k[:, j].astype(F32).reshape(batch * past, na_h * HEAD_DIM)
                    cv = cache_na_v[:, j].astype(F32).reshape(batch * past, na_h * HEAD_DIM)
                    a = na_attention(proj, ck, cv, na_q_g[j], na_k_g[j], na_rpb[j], batch=batch, seq=seq,
                                     past=past, n_heads=na_h, cast=cast)
                    if cast is not None:
                        a, *casted = a
            if casted is not None:
                mlp_w[cast_layer] = tuple(casted[:2])
                if len(casted) > 2:
                    proj_w[cast_layer] = tuple(casted[2:4])
            x, xn = mm_res_norm(a, w_o, x, mod, g2, layer=i, which_gate=2, next_layer=i, next_shift=3,
                                next_scale=4, **mk)
            x, xn = fused_mlp(xn, *mlp_w[i], x, mod, g1, layer=i, next_layer=nxt, emit_next=not last, **mk)
            grp["x"], grp["xn"] = x, xn

    y_prompt = groups[0]["x"].reshape(bp, tp, d)
    y_sample = groups[1]["x"].reshape(bs, ts, d)
    return (y_prompt, y_sample, c_all, jnp.stack(n_l, axis=1), jnp.stack(m_l, axis=1),
            jnp.stack(gk_l, axis=1), jnp.stack(gv_l, axis=1), jnp.stack(nk_l, axis=1), jnp.stack(nv_l, axis=1))
```
